```python
import jax
import jax.numpy as jnp
from jax import lax
import numpy as np

D_MODEL = 2048
BATCH = 4
SEQ = 4096
DEPTH = 2

GRID_W = 64
CTX_LEN = 256
D_FF = 5632
N_MOD = 9
EPS = 1e-6
ROPE_BASE = 10000.0
BLOCK = 128

CONV_WIDTH = 1024
CONV_K = 3

MLA_HEADS = 8
MLA_Q_LORA = 512
MLA_KV_LORA = 256
MLA_NOPE = 128
MLA_ROPE = 64
MLA_V = 128
MLA_SCALE = (MLA_NOPE + MLA_ROPE) ** -0.5

GQA_HEADS = 8
GQA_KV_HEADS = 2
GQA_GROUP = GQA_HEADS // GQA_KV_HEADS
GQA_HEAD_DIM = 128
GQA_SCALE = GQA_HEAD_DIM ** -0.5
WINDOW = 128

N_BRANCH = 3
COLS_CONV = 3 * CONV_WIDTH
COLS_MLA = MLA_Q_LORA + MLA_KV_LORA + MLA_ROPE
COLS_GQA = (GQA_HEADS + 2 * GQA_KV_HEADS) * GQA_HEAD_DIM
COLS_GATE = N_BRANCH * D_MODEL
IN_COLS = COLS_CONV + COLS_MLA + COLS_GQA + COLS_GATE
IN_SPLITS = [COLS_CONV, COLS_CONV + COLS_MLA, COLS_CONV + COLS_MLA + COLS_GQA]

kernel_name = "hybrid_dit_conv_mla_swa_macaron"


def rmsnorm(x, g):
    xf = x.astype(jnp.float32)
    y = xf * lax.rsqrt(jnp.mean(xf * xf, axis=-1, keepdims=True) + EPS)
    return (y * g.astype(jnp.float32)).astype(x.dtype)


def modulate(x, shift, scale):
    return x * (1 + scale) + shift


def swiglu(x, w_gu, w_down):
    g, u = jnp.split(x @ w_gu, 2, axis=-1)
    return (jax.nn.silu(g) * u) @ w_down


def rope_1d(x, pos, dim):
    inv = ROPE_BASE ** (-jnp.arange(0, dim, 2, dtype=jnp.float32) / dim)
    ang = pos.astype(jnp.float32)[:, None] * inv[None, :]
    cos = jnp.cos(ang)[:, None, :].astype(x.dtype)
    sin = jnp.sin(ang)[:, None, :].astype(x.dtype)
    x1, x2 = jnp.split(x, 2, axis=-1)
    return jnp.concatenate([x1 * cos - x2 * sin, x1 * sin + x2 * cos], axis=-1)


def rope_2d(x, row, col):
    half = x.shape[-1] // 2
    return jnp.concatenate([rope_1d(x[..., :half], row, half),
                            rope_1d(x[..., half:], col, half)], axis=-1)


def short_conv(u, w):
    s = u.shape[1]
    half = CONV_K // 2
    up = jnp.pad(u, ((0, 0), (half, half), (0, 0)))
    return sum(up[:, k:k + s] * w[k] for k in range(CONV_K))


def gated_short_conv(p, w_conv):
    gb, gc, v = jnp.split(p, 3, axis=-1)
    return gb * short_conv(gc * v, w_conv)


def mla_project(p, q_norm, w_qb, kv_norm, w_kvb):
    b, s = p.shape[:2]
    cq, ckv, k_rope = jnp.split(p, [MLA_Q_LORA, MLA_Q_LORA + MLA_KV_LORA], axis=-1)
    q = (rmsnorm(cq, q_norm) @ w_qb).reshape(b, s, MLA_HEADS, MLA_NOPE + MLA_ROPE)
    kv = (rmsnorm(ckv, kv_norm) @ w_kvb).reshape(b, s, MLA_HEADS, MLA_NOPE + MLA_V)
    return q[..., :MLA_NOPE], q[..., MLA_NOPE:], kv[..., :MLA_NOPE], k_rope, kv[..., MLA_NOPE:]


def mla_core(qn, qr, kn, kr, v):
    s = (jnp.einsum("bqhd,bkhd->bhqk", qn, kn)
         + jnp.einsum("bqhd,bkd->bhqk", qr, kr)).astype(jnp.float32) * MLA_SCALE
    pr = jax.nn.softmax(s, axis=-1).astype(v.dtype)
    return jnp.einsum("bhqk,bkhd->bqhd", pr, v)


def mla_latent(qn, qr, kn, kr, v):
    b, s = qn.shape[:2]
    nblk = s // BLOCK

    def to_blocks(t):
        return jnp.moveaxis(t.reshape(b, nblk, BLOCK, *t.shape[2:]), 1, 0)

    out = lax.map(lambda qb: mla_core(qb[0], qb[1], kn, kr, v), (to_blocks(qn), to_blocks(qr)))
    return jnp.moveaxis(out, 0, 1).reshape(b, s, MLA_HEADS * MLA_V)


def gqa_project(p):
    b, s = p.shape[:2]
    q, k, v = jnp.split(p, [GQA_HEADS * GQA_HEAD_DIM, (GQA_HEADS + GQA_KV_HEADS) * GQA_HEAD_DIM], axis=-1)
    return (q.reshape(b, s, GQA_HEADS, GQA_HEAD_DIM),
            k.reshape(b, s, GQA_KV_HEADS, GQA_HEAD_DIM),
            v.reshape(b, s, GQA_KV_HEADS, GQA_HEAD_DIM))


def sink_logits(sink, shape):
    return jnp.broadcast_to(sink.astype(jnp.float32).reshape(GQA_KV_HEADS, GQA_GROUP, 1, 1), shape)


def window_attend(q, k, v, ck, cv, sink):
    b, s = q.shape[:2]
    nblk = s // BLOCK
    qb = q.reshape(b, nblk, BLOCK, GQA_KV_HEADS, GQA_GROUP, GQA_HEAD_DIM)

    def band(t):
        tp = jnp.pad(t, ((0, 0), (BLOCK, BLOCK), (0, 0), (0, 0)))
        tp = tp.reshape(b, nblk + 2, BLOCK, GQA_KV_HEADS, GQA_HEAD_DIM)
        return jnp.concatenate([tp[:, :-2], tp[:, 1:-1], tp[:, 2:]], axis=2)

    kb, vb = band(k), band(v)
    blk = jnp.arange(nblk)[:, None, None]
    qi = blk * BLOCK + jnp.arange(BLOCK)[None, :, None]
    kj = (blk - 1) * BLOCK + jnp.arange(3 * BLOCK)[None, None, :]
    valid = (jnp.abs(kj - qi) <= WINDOW) & (kj >= 0) & (kj < s)
    s_loc = jnp.einsum("bnqkgd,bnpkd->bnkgqp", qb, kb).astype(jnp.float32) * GQA_SCALE
    s_loc = jnp.where(valid[None, :, None, None], s_loc, -jnp.inf)
    s_ctx = jnp.einsum("bnqkgd,bpkd->bnkgqp", qb, ck).astype(jnp.float32) * GQA_SCALE
    s_snk = sink_logits(sink, s_loc.shape[:-1] + (1,))
    pr = jax.nn.softmax(jnp.concatenate([s_loc, s_ctx, s_snk], axis=-1), axis=-1).astype(v.dtype)
    n_loc = 3 * BLOCK
    o = (jnp.einsum("bnkgqp,bnpkd->bnqkgd", pr[..., :n_loc], vb)
         + jnp.einsum("bnkgqp,bpkd->bnqkgd", pr[..., n_loc:n_loc + ck.shape[1]], cv))
    return o.reshape(b, s, GQA_HEADS * GQA_HEAD_DIM)


def ctx_window_attend(cq, ck, cv, sink):
    b, n = cq.shape[:2]
    qg = cq.reshape(b, n, GQA_KV_HEADS, GQA_GROUP, GQA_HEAD_DIM)
    sc = jnp.einsum("bqkgd,bpkd->bkgqp", qg, ck).astype(jnp.float32) * GQA_SCALE
    s_snk = sink_logits(sink, sc.shape[:-1] + (1,))
    pr = jax.nn.softmax(jnp.concatenate([sc, s_snk], axis=-1), axis=-1)[..., :n].astype(cv.dtype)
    o = jnp.einsum("bkgqp,bpkd->bqkgd", pr, cv)
    return o.reshape(b, n, GQA_HEADS * GQA_HEAD_DIM)


def merge_branches(y_conv, y_mla, y_gqa, p_gate, w_bc, w_bm, w_bg, w_out):
    g_c, g_m, g_g = jnp.split(p_gate, N_BRANCH, axis=-1)
    merged = (jax.nn.sigmoid(g_c) * (y_conv @ w_bc)
              + jax.nn.sigmoid(g_m) * (y_mla @ w_bm)
              + jax.nn.sigmoid(g_g) * (y_gqa @ w_bg))
    return merged @ w_out


def token_mixing(px, pc, row, col, conv_w, mla_q_norm, mla_w_qb, mla_kv_norm, mla_w_kvb,
                 gqa_sink, w_bc, w_bm, w_bg, w_out, with_ctx):
    x_conv, x_mla, x_gqa, x_gate = jnp.split(px, IN_SPLITS, axis=-1)
    c_conv, c_mla, c_gqa, c_gate = jnp.split(pc, IN_SPLITS, axis=-1)

    ya_x = gated_short_conv(x_conv, conv_w)

    qn, qr, kn, kr, v = mla_project(x_mla, mla_q_norm, mla_w_qb, mla_kv_norm, mla_w_kvb)
    qr = rope_2d(qr, row, col)
    kr = rope_2d(kr[:, :, None, :], row, col)[:, :, 0, :]
    cqn, cqr, ckn, ckr, cv = mla_project(c_mla, mla_q_norm, mla_w_qb, mla_kv_norm, mla_w_kvb)
    yb_x = mla_latent(qn, qr,
                      jnp.concatenate([kn, ckn], axis=1),
                      jnp.concatenate([kr, ckr], axis=1),
                      jnp.concatenate([v, cv], axis=1))

    q, k, vg = gqa_project(x_gqa)
    q, k = rope_2d(q, row, col), rope_2d(k, row, col)
    cq, ck, cvg = gqa_project(c_gqa)
    yc_x = window_attend(q, k, vg, ck, cvg, gqa_sink)

    out_x = merge_branches(ya_x, yb_x, yc_x, x_gate, w_bc, w_bm, w_bg, w_out)
    if not with_ctx:
        return out_x, None

    b, n = pc.shape[:2]
    ya_c = gated_short_conv(c_conv, conv_w)
    yb_c = mla_core(cqn, cqr, ckn, ckr, cv).reshape(b, n, MLA_HEADS * MLA_V)
    yc_c = ctx_window_attend(cq, ck, cvg, gqa_sink)
    out_c = merge_branches(ya_c, yb_c, yc_c, c_gate, w_bc, w_bm, w_bg, w_out)
    return out_x, out_c


def setup_inputs(seed: int = 0) -> dict:
    key = jax.random.key(seed)
    ks = jax.random.split(key, 25)

    def nrm(i, shape, scale):
        return jax.random.normal(ks[i], shape, jnp.float32) * scale

    def gain(i, shape):
        return 1.0 + nrm(i, shape, 0.01)

    L, D = DEPTH, D_MODEL
    return {
        "x": nrm(0, (BATCH, SEQ, D), 1.0),
        "c": nrm(1, (BATCH, D), 1.0),
        "ctx": nrm(2, (BATCH, CTX_LEN, D), 1.0),
        "c_ctx": nrm(3, (D,), 1.0),
        "ada_w": nrm(4, (L, D, N_MOD * D), 0.5 * D ** -0.5),
        "ada_b": nrm(5, (L, N_MOD * D), 0.01),
        "ffn1_norm": gain(6, (L, D)),
        "ffn1_w_gu": nrm(7, (L, D, 2 * D_FF), D ** -0.5),
        "ffn1_w_down": nrm(8, (L, D_FF, D), D_FF ** -0.5),
        "mix_norm": gain(9, (L, D)),
        "w_in": nrm(10, (L, D, IN_COLS), D ** -0.5),
        "conv_w": nrm(11, (L, CONV_K, CONV_WIDTH), CONV_K ** -0.5),
        "mla_q_norm": gain(12, (L, MLA_Q_LORA)),
        "mla_w_qb": nrm(13, (L, MLA_Q_LORA, MLA_HEADS * (MLA_NOPE + MLA_ROPE)), MLA_Q_LORA ** -0.5),
        "mla_kv_norm": gain(14, (L, MLA_KV_LORA)),
        "mla_w_kvb": nrm(15, (L, MLA_KV_LORA, MLA_HEADS * (MLA_NOPE + MLA_V)), MLA_KV_LORA ** -0.5),
        "gqa_sink": nrm(16, (L, GQA_HEADS), 0.5),
        "w_branch_conv": nrm(17, (L, CONV_WIDTH, D), CONV_WIDTH ** -0.5),
        "w_branch_mla": nrm(18, (L, MLA_HEADS * MLA_V, D), (MLA_HEADS * MLA_V) ** -0.5),
        "w_branch_gqa": nrm(19, (L, GQA_HEADS * GQA_HEAD_DIM, D), (GQA_HEADS * GQA_HEAD_DIM) ** -0.5),
        "w_out": nrm(20, (L, D, D), D ** -0.5),
        "ffn2_norm": gain(21, (L, D)),
        "ffn2_w_gu": nrm(22, (L, D, 2 * D_FF), D ** -0.5),
        "ffn2_w_down": nrm(23, (L, D_FF, D), D_FF ** -0.5),
        "final_norm": gain(24, (D,)),
    }


def reference(x, c, ctx, c_ctx, ada_w, ada_b, ffn1_norm, ffn1_w_gu, ffn1_w_down, mix_norm, w_in,
              conv_w, mla_q_norm, mla_w_qb, mla_kv_norm, mla_w_kvb, gqa_sink, w_branch_conv,
              w_branch_mla, w_branch_gqa, w_out, ffn2_norm, ffn2_w_gu, ffn2_w_down, final_norm):
    seq = x.shape[1]
    n_rows = seq // GRID_W
    row = jnp.repeat(jnp.arange(n_rows, dtype=jnp.int32), GRID_W)
    col = jnp.tile(jnp.arange(GRID_W, dtype=jnp.int32), n_rows)

    silu_c = jax.nn.silu(c)
    silu_cc = jax.nn.silu(c_ctx)[None, :]
    hx, hc = x, ctx
    for l in range(DEPTH):
        with_ctx = l < DEPTH - 1
        mx = [m[:, None, :] for m in jnp.split(silu_c @ ada_w[l] + ada_b[l], N_MOD, axis=-1)]
        mc = [m[:, None, :] for m in jnp.split(silu_cc @ ada_w[l] + ada_b[l], N_MOD, axis=-1)]

        hx = hx + 0.5 * mx[2] * swiglu(modulate(rmsnorm(hx, ffn1_norm[l]), mx[0], mx[1]),
                                       ffn1_w_gu[l], ffn1_w_down[l])
        hc = hc + 0.5 * mc[2] * swiglu(modulate(rmsnorm(hc, ffn1_norm[l]), mc[0], mc[1]),
                                       ffn1_w_gu[l], ffn1_w_down[l])

        px = modulate(rmsnorm(hx, mix_norm[l]), mx[3], mx[4]) @ w_in[l]
        pc = modulate(rmsnorm(hc, mix_norm[l]), mc[3], mc[4]) @ w_in[l]
        yx, yc = token_mixing(px, pc, row, col, conv_w[l], mla_q_norm[l], mla_w_qb[l],
                              mla_kv_norm[l], mla_w_kvb[l], gqa_sink[l], w_branch_conv[l],
                              w_branch_mla[l], w_branch_gqa[l], w_out[l], with_ctx)
        hx = hx + mx[5] * yx

        hx = hx + 0.5 * mx[8] * swiglu(modulate(rmsnorm(hx, ffn2_norm[l]), mx[6], mx[7]),
                                       ffn2_w_gu[l], ffn2_w_down[l])
        if with_ctx:
            hc = hc + mc[5] * yc
            hc = hc + 0.5 * mc[8] * swiglu(modulate(rmsnorm(hc, ffn2_norm[l]), mc[6], mc[7]),
                                           ffn2_w_gu[l], ffn2_w_down[l])
    return rmsnorm(hx, final_norm)
```

```python
import functools

import jax
import jax.numpy as jnp
from jax import lax
from jax.experimental import pallas as pl
from jax.experimental.pallas import tpu as pltpu

F32 = jnp.float32
BF16 = jnp.bfloat16

GRID_W = 64
N_MOD = 9
EPS = 1e-6
ROPE_BASE = 10000.0
CONV_K = 3
MLA_HEADS = 8
MLA_NOPE = 128
MLA_ROPE = 64
MLA_V = 128
MLA_SCALE = (MLA_NOPE + MLA_ROPE) ** -0.5
GQA_HEADS = 8
GQA_KV_HEADS = 2
GQA_GROUP = GQA_HEADS // GQA_KV_HEADS
GQA_HEAD_DIM = 128
GQA_SCALE = GQA_HEAD_DIM ** -0.5
WINDOW = 128
ATT_BLOCK = 128

LANE = 128
MLA_QK = 2 * LANE
VMEM_LIMIT = 56 << 20
NT_DIMS = (((1,), (1,)), ((), ()))


def _pick(target, *sizes):
    b = target
    while any(s % b for s in sizes):
        b //= 2
        assert b >= 8, (target, sizes)
    return b


def _params(sem):
    return pltpu.CompilerParams(dimension_semantics=sem, vmem_limit_bytes=VMEM_LIMIT)


def _dot(a, b):
    return jnp.dot(a, b, preferred_element_type=F32)


def _dot_nt(a, b):
    return lax.dot_general(a, b, NT_DIMS, preferred_element_type=F32)


def _rms(x, g):
    return x * lax.rsqrt(jnp.mean(x * x, axis=-1, keepdims=True) + EPS) * g


def _mods_kernel(c_ref, w_ref, b_ref, o_ref):
    c = c_ref[...]
    s = (c * jax.nn.sigmoid(c)).astype(BF16)
    o_ref[0] = _dot(s, w_ref[0].astype(BF16)) + b_ref[0]


def _mods(cvec, ada_w, ada_b):
    depth, d, n = ada_w.shape
    bn = _pick(1024, n)
    return pl.pallas_call(
        _mods_kernel,
        grid=(depth, n // bn),
        in_specs=[pl.BlockSpec((8, d), lambda l, j: (0, 0)),
                  pl.BlockSpec((1, d, bn), lambda l, j: (l, 0, j)),
                  pl.BlockSpec((1, 1, bn), lambda l, j: (l, 0, j))],
        out_specs=pl.BlockSpec((1, 8, bn), lambda l, j: (l, 0, j)),
        out_shape=jax.ShapeDtypeStruct((depth, 8, n), F32),
        compiler_params=_params(("parallel", "parallel")),
        name="adaln_mods",
    )(cvec, ada_w, ada_b.reshape(depth, 1, n))


def _norm_mod_rows(h_ref, mod_ref, g_ref, xn_ref, copy_ref, i_shift, i_scale, rc):
    shift = mod_ref[0, i_shift:i_shift + 1, :]
    scale1 = 1.0 + mod_ref[0, i_scale:i_scale + 1, :]
    g = g_ref[...]

    def body(r, carry):
        rows = pl.ds(pl.multiple_of(r * rc, rc), rc)
        x = h_ref[rows, :]
        xn_ref[rows, :] = (_rms(x, g) * scale1 + shift).astype(BF16)
        if copy_ref is not None:
            copy_ref[rows, :] = x
        return carry

    lax.fori_loop(0, h_ref.shape[0] // rc, body, 0)


def _mod_spec(d, bm, s, b):
    per = s // bm
    return pl.BlockSpec((1, N_MOD, d), lambda i, j: (jnp.minimum(i // per, b), 0, 0))


def _ffn_kernel(h_ref, mod_ref, g_ref, wg_ref, wu_ref, wd_ref, o_ref, xn_ref, *, i0, rc):
    @pl.when(pl.program_id(1) == 0)
    def _():
        _norm_mod_rows(h_ref, mod_ref, g_ref, xn_ref, o_ref, i0, i0 + 1, rc)

    xn = xn_ref[...]
    gg = _dot(xn, wg_ref[...])
    uu = _dot(xn, wu_ref[...])
    a = (gg * jax.nn.sigmoid(gg) * uu).astype(BF16)
    gate = 0.5 * mod_ref[0, i0 + 2:i0 + 3, :]
    o_ref[...] += gate * _dot(a, wd_ref[...])


def _ffn(h, rows, mods, norm_g, w_gu, w_down, i0, s, b):
    d = h.shape[1]
    f = w_down.shape[0]
    bm = _pick(512, s, rows)
    bf = _pick(512, f)
    nf = f // bf
    return pl.pallas_call(
        functools.partial(_ffn_kernel, i0=i0, rc=_pick(128, bm)),
        grid=(rows // bm, nf),
        in_specs=[pl.BlockSpec((bm, d), lambda i, j: (i, 0)),
                  _mod_spec(d, bm, s, b),
                  pl.BlockSpec((1, d), lambda i, j: (0, 0)),
                  pl.BlockSpec((d, bf), lambda i, j: (0, j)),
                  pl.BlockSpec((d, bf), lambda i, j: (0, j + nf)),
                  pl.BlockSpec((bf, d), lambda i, j: (j, 0))],
        out_specs=pl.BlockSpec((bm, d), lambda i, j: (i, 0)),
        out_shape=jax.ShapeDtypeStruct((rows, d), F32),
        scratch_shapes=[pltpu.VMEM((bm, d), BF16)],
        compiler_params=_params(("parallel", "arbitrary")),
        name="ffn_swiglu",
    )(h, mods, norm_g.reshape(1, d), w_gu, w_gu, w_down)


def _proj_kernel(h_ref, mod_ref, g_ref, w_ref, o_ref, xn_ref, *, n_gate0, rc):
    n = pl.program_id(1)

    @pl.when(n == 0)
    def _():
        _norm_mod_rows(h_ref, mod_ref, g_ref, xn_ref, None, 3, 4, rc)

    acc = _dot(xn_ref[...], w_ref[...])

    @pl.when(n < n_gate0)
    def _():
        o_ref[...] = acc.astype(BF16)

    @pl.when(n >= n_gate0)
    def _():
        o_ref[...] = jax.nn.sigmoid(acc).astype(BF16)


def _proj(h, mods, norm_g, w_p, o_gate, s, b):
    rows, d = h.shape
    n = w_p.shape[1]
    bm = _pick(1024, s, rows)
    bn = _pick(512, n, o_gate)
    return pl.pallas_call(
        functools.partial(_proj_kernel, n_gate0=o_gate // bn, rc=_pick(128, bm)),
        grid=(rows // bm, n // bn),
        in_specs=[pl.BlockSpec((bm, d), lambda i, j: (i, 0)),
                  _mod_spec(d, bm, s, b),
                  pl.BlockSpec((1, d), lambda i, j: (0, 0)),
                  pl.BlockSpec((d, bn), lambda i, j: (0, j))],
        out_specs=pl.BlockSpec((bm, bn), lambda i, j: (i, j)),
        out_shape=jax.ShapeDtypeStruct((rows, n), BF16),
        scratch_shapes=[pltpu.VMEM((bm, d), BF16)],
        compiler_params=_params(("parallel", "arbitrary")),
        name="in_proj",
    )(h, mods, norm_g.reshape(1, d), w_p)


def _conv_kernel(gb_ref, gc_ref, v_ref, gcp_ref, vp_ref, gcn_ref, vn_ref, w_ref, o_ref, *, bm, s, c, mx):
    row0 = pl.program_id(0) * bm
    is_lat = row0 < mx
    at_start = jnp.where(is_lat, row0 % s == 0, (row0 - mx) % c == 0)
    at_end = jnp.where(is_lat, (row0 + bm) % s == 0, (row0 - mx + bm) % c == 0)
    cv = gc_ref[...].astype(F32) * v_ref[...].astype(F32)
    hp = (gcp_ref[...].astype(F32) * vp_ref[...].astype(F32))[15:16, :]
    hn = (gcn_ref[...].astype(F32) * vn_ref[...].astype(F32))[0:1, :]
    hp = jnp.where(at_start, 0.0, hp)
    hn = jnp.where(at_end, 0.0, hn)
    rid = lax.broadcasted_iota(jnp.int32, (bm, 1), 0)
    prev = jnp.where(rid == 0, hp, pltpu.roll(cv, 1, 0))
    nxt = jnp.where(rid == bm - 1, hn, pltpu.roll(cv, bm - 1, 0))
    w = w_ref[...]
    y = gb_ref[...].astype(F32) * (prev * w[0:1, :] + cv * w[1:2, :] + nxt * w[2:3, :])
    o_ref[...] = y.astype(BF16)


def _conv(px, conv_w, rows, s, c, mx):
    cw = conv_w.shape[1]
    m = px.shape[0]
    bm = _pick(256, s, c)
    hb = bm // 16
    main = lambda col: pl.BlockSpec((bm, cw), lambda i: (i, col))
    prev = lambda col: pl.BlockSpec((16, cw), lambda i: (jnp.maximum(i * hb - 1, 0), col))
    nxt = lambda col: pl.BlockSpec((16, cw), lambda i: (jnp.minimum((i + 1) * hb, m // 16 - 1), col))
    return pl.pallas_call(
        functools.partial(_conv_kernel, bm=bm, s=s, c=c, mx=mx),
        grid=(rows // bm,),
        in_specs=[main(0), main(1), main(2), prev(1), prev(2), nxt(1), nxt(2),
                  pl.BlockSpec((CONV_K, cw), lambda i: (0, 0))],
        out_specs=pl.BlockSpec((bm, cw), lambda i: (i, 0)),
        out_shape=jax.ShapeDtypeStruct((rows, cw), BF16),
        compiler_params=_params(("parallel",)),
        name="gated_conv",
    )(px, px, px, px, px, px, px, conv_w)


def _rope(x, cos, sin, half):
    lane = lax.broadcasted_iota(jnp.int32, x.shape, 1)
    first = (lane % (2 * half)) < half
    rot = jnp.where(first, pltpu.roll(x, LANE - half, 1), pltpu.roll(x, half, 1))
    return x * cos + rot * sin


def _prep_kernel(gq_ref, cq_ref, ckv_ref, gk_ref, kr_ref, qn_ref, kvn_ref, wqb_ref, wkvb_ref,
                 mcos_ref, msin_ref, gcos_ref, gsin_ref, q_ref, k_ref, v_ref, gqo_ref, gko_ref):
    mcos, msin = mcos_ref[...], msin_ref[...]
    gcos, gsin = gcos_ref[...], gsin_ref[...]
    mh = MLA_ROPE // 4
    gh = GQA_HEAD_DIM // 4

    cqn = _rms(cq_ref[...].astype(F32), qn_ref[...]).astype(BF16)
    q = _dot(cqn, wqb_ref[...]) * MLA_SCALE
    ckvn = _rms(ckv_ref[...].astype(F32), kvn_ref[...]).astype(BF16)
    kv = _dot(ckvn, wkvb_ref[...])
    kr = _rope(kr_ref[...].astype(F32), mcos, msin, mh).astype(BF16)
    for h in range(MLA_HEADS):
        a = h * MLA_QK
        q_ref[:, a:a + LANE] = q[:, a:a + LANE].astype(BF16)
        q_ref[:, a + LANE:a + MLA_QK] = _rope(q[:, a + LANE:a + MLA_QK], mcos, msin, mh).astype(BF16)
        k_ref[:, a:a + LANE] = kv[:, h * LANE:(h + 1) * LANE].astype(BF16)
        k_ref[:, a + LANE:a + MLA_QK] = kr
    v_ref[...] = kv[:, MLA_HEADS * MLA_NOPE:].astype(BF16)

    gq = gq_ref[...].astype(F32)
    for h in range(GQA_HEADS):
        a = h * GQA_HEAD_DIM
        gqo_ref[:, a:a + LANE] = (_rope(gq[:, a:a + LANE], gcos, gsin, gh) * GQA_SCALE).astype(BF16)
    gk = gk_ref[...].astype(F32)
    for h in range(GQA_KV_HEADS):
        a = h * GQA_HEAD_DIM
        gko_ref[:, a:a + LANE] = _rope(gk[:, a:a + LANE], gcos, gsin, gh).astype(BF16)


def _prep(px, lay, q_norm, kv_norm, wqb_p, wkvb_r, tabs, s, mx):
    m = px.shape[0]
    bm = _pick(512, s, m - mx)
    n_lat = mx // bm
    per = s // bm
    ql, kvl = q_norm.shape[0], kv_norm.shape[0]
    gqw, gkw = GQA_HEADS * GQA_HEAD_DIM, GQA_KV_HEADS * GQA_HEAD_DIM

    def col(width, off):
        assert off % width == 0, (width, off)
        return pl.BlockSpec((bm, width), lambda i: (i, off // width))

    const = lambda r, c: pl.BlockSpec((r, c), lambda i: (0, 0))
    tab = pl.BlockSpec((bm, LANE), lambda i: (jnp.where(i < n_lat, i % per, per + i - n_lat), 0))
    row = lambda width: pl.BlockSpec((bm, width), lambda i: (i, 0))
    hq = MLA_HEADS * MLA_QK
    return pl.pallas_call(
        _prep_kernel,
        grid=(m // bm,),
        in_specs=[col(gqw, lay["gq"]), col(ql, lay["cq"]), col(kvl, lay["ckv"]), col(gkw, lay["gk"]),
                  col(LANE, lay["kr"]), const(1, ql), const(1, kvl), const(ql, hq),
                  const(kvl, MLA_HEADS * (MLA_NOPE + MLA_V)), tab, tab, tab, tab],
        out_specs=[row(hq), row(hq), row(MLA_HEADS * MLA_V), row(gqw), row(gkw)],
        out_shape=[jax.ShapeDtypeStruct((m, hq), BF16), jax.ShapeDtypeStruct((m, hq), BF16),
                   jax.ShapeDtypeStruct((m, MLA_HEADS * MLA_V), BF16),
                   jax.ShapeDtypeStruct((m, gqw), BF16), jax.ShapeDtypeStruct((m, gkw), BF16)],
        compiler_params=_params(("parallel",)),
        name="attn_prep",
    )(px, px, px, px, px, q_norm.reshape(1, ql), kv_norm.reshape(1, kvl), wqb_p, wkvb_r, *tabs)


def _mla_kernel(q_ref, kl_ref, kc_ref, vl_ref, vc_ref, o_ref, *, nq_lat):
    qi = pl.program_id(2)
    q = q_ref[...]
    s2 = _dot_nt(q, kc_ref[...])
    m2 = jnp.max(s2, axis=-1, keepdims=True)

    @pl.when(qi < nq_lat)
    def _():
        s1 = _dot_nt(q, kl_ref[...])
        m = jnp.maximum(jnp.max(s1, axis=-1, keepdims=True), m2)
        p1 = jnp.exp(s1 - m)
        p2 = jnp.exp(s2 - m)
        l = jnp.sum(p1, axis=-1, keepdims=True) + jnp.sum(p2, axis=-1, keepdims=True)
        o = _dot(p1.astype(BF16), vl_ref[...]) + _dot(p2.astype(BF16), vc_ref[...])
        o_ref[...] = (o / l).astype(BF16)

    @pl.when(qi >= nq_lat)
    def _():
        p2 = jnp.exp(s2 - m2)
        l = jnp.sum(p2, axis=-1, keepdims=True)
        o_ref[...] = (_dot(p2.astype(BF16), vc_ref[...]) / l).astype(BF16)


def _mla_attn(q, k, v, b, s, c, with_ctx):
    mx = b * s
    bq = c
    nq_lat = s // bq
    nq = nq_lat + (1 if with_ctx else 0)
    rows = mx + (b * c if with_ctx else 0)
    ctx0 = mx // c
    qmap = lambda bi, h, qi: (jnp.where(qi < nq_lat, bi * nq_lat + qi, ctx0 + bi), h)
    return pl.pallas_call(
        functools.partial(_mla_kernel, nq_lat=nq_lat),
        grid=(b, MLA_HEADS, nq),
        in_specs=[pl.BlockSpec((bq, MLA_QK), qmap),
                  pl.BlockSpec((s, MLA_QK), lambda bi, h, qi: (bi, h)),
                  pl.BlockSpec((c, MLA_QK), lambda bi, h, qi: (ctx0 + bi, h)),
                  pl.BlockSpec((s, MLA_V), lambda bi, h, qi: (bi, h)),
                  pl.BlockSpec((c, MLA_V), lambda bi, h, qi: (ctx0 + bi, h))],
        out_specs=pl.BlockSpec((bq, MLA_V), qmap),
        out_shape=jax.ShapeDtypeStruct((rows, MLA_HEADS * MLA_V), BF16),
        compiler_params=_params(("parallel", "parallel", "arbitrary")),
        name="mla_attn",
    )(q, k, k, v, v)


def _gqa_kernel(q_ref, kp_ref, kc_ref, kn_ref, kx_ref, vp_ref, vc_ref, vn_ref, vx_ref, sink_ref, o_ref,
                *, n_lat, s):
    g = pl.program_id(1)
    n = pl.program_id(2)
    kx = kx_ref[...]
    vx = vx_ref[...]
    blk = ATT_BLOCK

    def sink(j):
        return sink_ref[pl.ds(g * GQA_GROUP + j, 1), :][:, 0:1]

    @pl.when(n < n_lat)
    def _():
        kb = jnp.concatenate([kp_ref[...], kc_ref[...], kn_ref[...]], axis=0)
        vb = jnp.concatenate([vp_ref[...], vc_ref[...], vn_ref[...]], axis=0)
        qi = n * blk + lax.broadcasted_iota(jnp.int32, (blk, 3 * blk), 0)
        kj = (n - 1) * blk + lax.broadcasted_iota(jnp.int32, (blk, 3 * blk), 1)
        valid = (jnp.abs(kj - qi) <= WINDOW) & (kj >= 0) & (kj < s)
        for j in range(GQA_GROUP):
            qj = q_ref[:, j * LANE:(j + 1) * LANE]
            sl = jnp.where(valid, _dot_nt(qj, kb), -jnp.inf)
            sc = _dot_nt(qj, kx)
            snk = sink(j)
            m = jnp.maximum(jnp.maximum(jnp.max(sl, axis=-1, keepdims=True),
                                        jnp.max(sc, axis=-1, keepdims=True)), snk)
            p_l = jnp.exp(sl - m)
            p_c = jnp.exp(sc - m)
            l = (jnp.sum(p_l, axis=-1, keepdims=True) + jnp.sum(p_c, axis=-1, keepdims=True)
                 + jnp.exp(snk - m))
            o = _dot(p_l.astype(BF16), vb) + _dot(p_c.astype(BF16), vx)
            o_ref[:, j * LANE:(j + 1) * LANE] = (o / l).astype(BF16)

    @pl.when(n >= n_lat)
    def _():
        for j in range(GQA_GROUP):
            qj = q_ref[:, j * LANE:(j + 1) * LANE]
            sc = _dot_nt(qj, kx)
            snk = sink(j)
            m = jnp.maximum(jnp.max(sc, axis=-1, keepdims=True), snk)
            p_c = jnp.exp(sc - m)
            l = jnp.sum(p_c, axis=-1, keepdims=True) + jnp.exp(snk - m)
            o_ref[:, j * LANE:(j + 1) * LANE] = (_dot(p_c.astype(BF16), vx) / l).astype(BF16)


def _gqa_attn(gq, gk, px, o_gv, sink, b, s, c, with_ctx):
    mx = b * s
    blk = ATT_BLOCK
    n_lat = s // blk
    n_ctx = c // blk
    nsteps = n_lat + (n_ctx if with_ctx else 0)
    rows = mx + (b * c if with_ctx else 0)
    ctx0 = mx // c
    gvc = o_gv // LANE
    gw = GQA_GROUP * GQA_HEAD_DIM

    def qrow(bi, n):
        return jnp.where(n < n_lat, bi * n_lat + n, mx // blk + bi * n_ctx + n - n_lat)

    def band(shift):
        def idx(bi, n):
            return bi * n_lat + jnp.clip(n + shift, 0, n_lat - 1)
        return idx

    kspec = lambda sh: pl.BlockSpec((blk, LANE), lambda bi, g, n: (band(sh)(bi, n), g))
    vspec = lambda sh: pl.BlockSpec((blk, LANE), lambda bi, g, n: (band(sh)(bi, n), gvc + g))
    return pl.pallas_call(
        functools.partial(_gqa_kernel, n_lat=n_lat, s=s),
        grid=(b, GQA_KV_HEADS, nsteps),
        in_specs=[pl.BlockSpec((blk, gw), lambda bi, g, n: (qrow(bi, n), g)),
                  kspec(-1), kspec(0), kspec(1),
                  pl.BlockSpec((c, LANE), lambda bi, g, n: (ctx0 + bi, g)),
                  vspec(-1), vspec(0), vspec(1),
                  pl.BlockSpec((c, LANE), lambda bi, g, n: (ctx0 + bi, gvc + g)),
                  pl.BlockSpec((GQA_HEADS, LANE), lambda bi, g, n: (0, 0))],
        out_specs=pl.BlockSpec((blk, gw), lambda bi, g, n: (qrow(bi, n), g)),
        out_shape=jax.ShapeDtypeStruct((rows, GQA_HEADS * GQA_HEAD_DIM), BF16),
        compiler_params=_params(("parallel", "parallel", "arbitrary")),
        name="gqa_attn",
    )(gq, gk, gk, gk, gk, px, px, px, px, sink)


def _merge_kernel(yc_ref, ym_ref, yg_ref, gc_ref, gm_ref, gg_ref, wc_ref, wm_ref, wg_ref, o_ref):
    acc = gc_ref[...].astype(F32) * _dot(yc_ref[...], wc_ref[...])
    acc += gm_ref[...].astype(F32) * _dot(ym_ref[...], wm_ref[...])
    acc += gg_ref[...].astype(F32) * _dot(yg_ref[...], wg_ref[...])
    o_ref[...] = acc.astype(BF16)


def _merge(yc, ym, yg, px, o_gate, w_bc, w_bm, w_bg, rows, s):
    d = w_bc.shape[1]
    bm = _pick(1024, s, rows)
    bn = _pick(512, d, o_gate)
    g0 = o_gate // bn
    per = d // bn
    yspec = lambda a: pl.BlockSpec((bm, a.shape[1]), lambda i, j: (i, 0))
    gspec = lambda k: pl.BlockSpec((bm, bn), lambda i, j: (i, g0 + k * per + j))
    wspec = lambda w: pl.BlockSpec((w.shape[0], bn), lambda i, j: (0, j))
    return pl.pallas_call(
        _merge_kernel,
        grid=(rows // bm, d // bn),
        in_specs=[yspec(yc), yspec(ym), yspec(yg), gspec(0), gspec(1), gspec(2),
                  wspec(w_bc), wspec(w_bm), wspec(w_bg)],
        out_specs=pl.BlockSpec((bm, bn), lambda i, j: (i, j)),
        out_shape=jax.ShapeDtypeStruct((rows, d), BF16),
        compiler_params=_params(("parallel", "arbitrary")),
        name="branch_merge",
    )(yc, ym, yg, px, px, px, w_bc, w_bm, w_bg)


def _out_kernel(m_ref, w_ref, h_ref, mod_ref, o_ref):
    o_ref[...] = h_ref[...] + mod_ref[0, 5:6, :] * _dot(m_ref[...], w_ref[...])


def _out_proj(merged, w_out, h, mods, rows, s, b):
    d = w_out.shape[1]
    bm = _pick(1024, s, rows)
    bn = _pick(512, d)
    per = s // bm
    return pl.pallas_call(
        _out_kernel,
        grid=(rows // bm, d // bn),
        in_specs=[pl.BlockSpec((bm, d), lambda i, j: (i, 0)),
                  pl.BlockSpec((d, bn), lambda i, j: (0, j)),
                  pl.BlockSpec((bm, bn), lambda i, j: (i, j)),
                  pl.BlockSpec((1, N_MOD, bn), lambda i, j: (jnp.minimum(i // per, b), 0, j))],
        out_specs=pl.BlockSpec((bm, bn), lambda i, j: (i, j)),
        out_shape=jax.ShapeDtypeStruct((rows, d), F32),
        compiler_params=_params(("parallel", "arbitrary")),
        name="out_proj",
    )(merged, w_out, h, mods)


def _final_kernel(h_ref, g_ref, o_ref):
    o_ref[...] = _rms(h_ref[...], g_ref[...])


def _final_norm(h, g):
    rows, d = h.shape
    bm = _pick(512, rows)
    return pl.pallas_call(
        _final_kernel,
        grid=(rows // bm,),
        in_specs=[pl.BlockSpec((bm, d), lambda i: (i, 0)), pl.BlockSpec((1, d), lambda i: (0, 0))],
        out_specs=pl.BlockSpec((bm, d), lambda i: (i, 0)),
        out_shape=jax.ShapeDtypeStruct((rows, d), F32),
        compiler_params=_params(("parallel",)),
        name="final_norm",
    )(h, g.reshape(1, d))


def _rope_table(s, n_ctx_rows, dim):
    t = jnp.arange(s, dtype=jnp.int32)
    pos = jnp.stack([t // GRID_W, t % GRID_W], axis=1).astype(F32)
    inv = ROPE_BASE ** (-jnp.arange(0, dim, 2, dtype=F32) / dim)
    lane = jnp.arange(LANE)
    ang = pos[:, jnp.minimum(lane // dim, 1)] * inv[lane % (dim // 2)][None, :]
    active = (lane < 2 * dim)[None, :]
    sign = jnp.where((lane % dim) < dim // 2, -1.0, 1.0)[None, :]
    cos = jnp.where(active, jnp.cos(ang), 1.0)
    sin = jnp.where(active, jnp.sin(ang) * sign, 0.0)
    pad = ((0, n_ctx_rows), (0, 0))
    return jnp.pad(cos, pad, constant_values=1.0), jnp.pad(sin, pad)


def _layout(cw, ql, kvl, d):
    lay = {"conv": 0}
    off = 3 * cw
    for name, width in (("gq", GQA_HEADS * GQA_HEAD_DIM), ("cq", ql), ("ckv", kvl),
                        ("gk", GQA_KV_HEADS * GQA_HEAD_DIM), ("gv", GQA_KV_HEADS * GQA_HEAD_DIM),
                        ("kr", LANE)):
        lay[name] = off
        off += width
    off = -(-off // 512) * 512
    lay["gate"] = off
    lay["total"] = off + 3 * d
    return lay


def _pack_w_in(w, lay, cw, ql, kvl):
    d = w.shape[0]
    o_mla = 3 * cw
    o_gqa = o_mla + ql + kvl + MLA_ROPE
    gqw, gkw = GQA_HEADS * GQA_HEAD_DIM, GQA_KV_HEADS * GQA_HEAD_DIM
    o_gate = o_gqa + gqw + 2 * gkw
    parts = [w[:, :o_mla], w[:, o_gqa:o_gqa + gqw], w[:, o_mla:o_mla + ql + kvl],
             w[:, o_gqa + gqw:o_gate], w[:, o_mla + ql + kvl:o_gqa]]
    used = o_gate
    parts.append(jnp.zeros((d, lay["gate"] - used), w.dtype))
    parts.append(w[:, o_gate:])
    return jnp.concatenate(parts, axis=1).astype(BF16)


def kernel(x, c, ctx, c_ctx, ada_w, ada_b, ffn1_norm, ffn1_w_gu, ffn1_w_down, mix_norm, w_in, conv_w,
           mla_q_norm, mla_w_qb, mla_kv_norm, mla_w_kvb, gqa_sink, w_branch_conv, w_branch_mla,
           w_branch_gqa, w_out, ffn2_norm, ffn2_w_gu, ffn2_w_down, final_norm):
    b, s, d = x.shape
    cl = ctx.shape[1]
    depth = ada_w.shape[0]
    cw = conv_w.shape[-1]
    ql, kvl = mla_q_norm.shape[-1], mla_kv_norm.shape[-1]
    mx, mc = b * s, b * cl
    assert b + 1 <= 8 and s % cl == 0 and cl % ATT_BLOCK == 0

    cvec = jnp.zeros((8, d), F32).at[:b].set(c).at[b].set(c_ctx)
    mods = _mods(cvec, ada_w, ada_b).reshape(depth, 8, N_MOD, d)

    lay = _layout(cw, ql, kvl, d)
    tabs = _rope_table(s, mc, MLA_ROPE // 2) + _rope_table(s, mc, GQA_HEAD_DIM // 2)
    h = jnp.concatenate([x.reshape(mx, d), ctx.reshape(mc, d)], axis=0)

    for l in range(depth):
        with_ctx = l < depth - 1
        rows = mx + mc if with_ctx else mx
        w_p = _pack_w_in(w_in[l], lay, cw, ql, kvl)
        wqb_p = jnp.pad(mla_w_qb[l].reshape(ql, MLA_HEADS, MLA_NOPE + MLA_ROPE),
                        ((0, 0), (0, 0), (0, MLA_QK - MLA_NOPE - MLA_ROPE))).reshape(ql, -1).astype(BF16)
        wkv = mla_w_kvb[l].reshape(kvl, MLA_HEADS, MLA_NOPE + MLA_V)
        wkvb_r = jnp.concatenate([wkv[:, :, :MLA_NOPE].reshape(kvl, -1), wkv[:, :, MLA_NOPE:].reshape(kvl, -1)],
                                 axis=1).astype(BF16)
        sink = jnp.broadcast_to(gqa_sink[l].astype(F32)[:, None], (GQA_HEADS, LANE))

        h = _ffn(h, mx + mc, mods[l], ffn1_norm[l], ffn1_w_gu[l].astype(BF16), ffn1_w_down[l].astype(BF16),
                 0, s, b)
        px = _proj(h, mods[l], mix_norm[l], w_p, lay["gate"], s, b)
        y_conv = _conv(px, conv_w[l], rows, s, cl, mx)
        q, k, v, gq, gk = _prep(px, lay, mla_q_norm[l], mla_kv_norm[l], wqb_p, wkvb_r, tabs, s, mx)
        y_mla = _mla_attn(q, k, v, b, s, cl, with_ctx)
        y_gqa = _gqa_attn(gq, gk, px, lay["gv"], sink, b, s, cl, with_ctx)
        merged = _merge(y_conv, y_mla, y_gqa, px, lay["gate"], w_branch_conv[l].astype(BF16),
                        w_branch_mla[l].astype(BF16), w_branch_gqa[l].astype(BF16), rows, s)
        h = _out_proj(merged, w_out[l].astype(BF16), h, mods[l], rows, s, b)
        h = _ffn(h, rows, mods[l], ffn2_norm[l], ffn2_w_gu[l].astype(BF16), ffn2_w_down[l].astype(BF16),
                 6, s, b)
    return _final_norm(h, final_norm).reshape(b, s, d)
```

```python
import functools

import jax
import jax.numpy as jnp
from jax import lax
from jax.experimental import pallas as pl
from jax.experimental.pallas import tpu as pltpu

F32 = jnp.float32
BF16 = jnp.bfloat16

GRID_W = 64
N_MOD = 9
EPS = 1e-6
ROPE_BASE = 10000.0
CONV_K = 3
MLA_HEADS = 8
MLA_NOPE = 128
MLA_ROPE = 64
MLA_V = 128
MLA_SCALE = (MLA_NOPE + MLA_ROPE) ** -0.5
GQA_HEADS = 8
GQA_KV_HEADS = 2
GQA_GROUP = GQA_HEADS // GQA_KV_HEADS
GQA_HEAD_DIM = 128
GQA_SCALE = GQA_HEAD_DIM ** -0.5
WINDOW = 128
ATT_BLOCK = 128

LANE = 128
MLA_QK = 2 * LANE
MLA_SUB = 256
LOG2E = 1.4426950408889634
VMEM_LIMIT = 56 << 20
NT_DIMS = (((1,), (1,)), ((), ()))


def _pick(target, *sizes):
    b = target
    while any(s % b for s in sizes):
        b //= 2
        assert b >= 8, (target, sizes)
    return b


def _params(sem):
    return pltpu.CompilerParams(dimension_semantics=sem, vmem_limit_bytes=VMEM_LIMIT)


def _dot(a, b):
    return jnp.dot(a, b, preferred_element_type=F32)


def _dot_nt(a, b):
    return lax.dot_general(a, b, NT_DIMS, preferred_element_type=F32)


def _rms(x, g):
    return x * lax.rsqrt(jnp.mean(x * x, axis=-1, keepdims=True) + EPS) * g


def _mods_kernel(c_ref, w_ref, b_ref, o_ref):
    c = c_ref[...]
    s = (c * jax.nn.sigmoid(c)).astype(BF16)
    o_ref[0] = _dot(s, w_ref[0].astype(BF16)) + b_ref[0]


def _mods(cvec, ada_w, ada_b):
    depth, d, n = ada_w.shape
    bn = _pick(1024, n)
    return pl.pallas_call(
        _mods_kernel,
        grid=(depth, n // bn),
        in_specs=[pl.BlockSpec((8, d), lambda l, j: (0, 0)),
                  pl.BlockSpec((1, d, bn), lambda l, j: (l, 0, j)),
                  pl.BlockSpec((1, 1, bn), lambda l, j: (l, 0, j))],
        out_specs=pl.BlockSpec((1, 8, bn), lambda l, j: (l, 0, j)),
        out_shape=jax.ShapeDtypeStruct((depth, 8, n), F32),
        compiler_params=_params(("parallel", "parallel")),
        name="adaln_mods",
    )(cvec, ada_w, ada_b.reshape(depth, 1, n))


def _norm_mod_rows(h_ref, mod_ref, g_ref, xn_ref, copy_ref, i_shift, i_scale, rc):
    shift = mod_ref[0, i_shift:i_shift + 1, :]
    scale1 = 1.0 + mod_ref[0, i_scale:i_scale + 1, :]
    g = g_ref[...]

    def body(r, carry):
        rows = pl.ds(pl.multiple_of(r * rc, rc), rc)
        x = h_ref[rows, :]
        xn_ref[rows, :] = (_rms(x, g) * scale1 + shift).astype(BF16)
        if copy_ref is not None:
            copy_ref[rows, :] = x
        return carry

    lax.fori_loop(0, h_ref.shape[0] // rc, body, 0)


def _mod_spec(d, bm, s, b):
    per = s // bm
    return pl.BlockSpec((1, N_MOD, d), lambda i, j: (jnp.minimum(i // per, b), 0, 0))


def _ffn_kernel(h_ref, mod_ref, g_ref, wg_ref, wu_ref, wd_ref, o_ref, xn_ref, *, i0, rc):
    @pl.when(pl.program_id(1) == 0)
    def _():
        _norm_mod_rows(h_ref, mod_ref, g_ref, xn_ref, o_ref, i0, i0 + 1, rc)

    xn = xn_ref[...]
    gg = _dot(xn, wg_ref[...])
    uu = _dot(xn, wu_ref[...])
    a = (gg * jax.nn.sigmoid(gg) * uu).astype(BF16)
    gate = 0.5 * mod_ref[0, i0 + 2:i0 + 3, :]
    o_ref[...] += gate * _dot(a, wd_ref[...])


def _ffn(h, rows, mods, norm_g, w_gu, w_down, i0, s, b):
    d = h.shape[1]
    f = w_down.shape[0]
    bm = _pick(512, s, rows)
    bf = _pick(512, f)
    nf = f // bf
    return pl.pallas_call(
        functools.partial(_ffn_kernel, i0=i0, rc=_pick(128, bm)),
        grid=(rows // bm, nf),
        in_specs=[pl.BlockSpec((bm, d), lambda i, j: (i, 0)),
                  _mod_spec(d, bm, s, b),
                  pl.BlockSpec((1, d), lambda i, j: (0, 0)),
                  pl.BlockSpec((d, bf), lambda i, j: (0, j)),
                  pl.BlockSpec((d, bf), lambda i, j: (0, j + nf)),
                  pl.BlockSpec((bf, d), lambda i, j: (j, 0))],
        out_specs=pl.BlockSpec((bm, d), lambda i, j: (i, 0)),
        out_shape=jax.ShapeDtypeStruct((rows, d), F32),
        scratch_shapes=[pltpu.VMEM((bm, d), BF16)],
        compiler_params=_params(("parallel", "arbitrary")),
        name="ffn_swiglu",
    )(h, mods, norm_g.reshape(1, d), w_gu, w_gu, w_down)


def _proj_kernel(h_ref, mod_ref, g_ref, w_ref, o_ref, xn_ref, *, n_gate0, rc):
    n = pl.program_id(1)

    @pl.when(n == 0)
    def _():
        _norm_mod_rows(h_ref, mod_ref, g_ref, xn_ref, None, 3, 4, rc)

    acc = _dot(xn_ref[...], w_ref[...])

    @pl.when(n < n_gate0)
    def _():
        o_ref[...] = acc.astype(BF16)

    @pl.when(n >= n_gate0)
    def _():
        o_ref[...] = jax.nn.sigmoid(acc).astype(BF16)


def _proj(h, mods, norm_g, w_p, o_gate, s, b):
    rows, d = h.shape
    n = w_p.shape[1]
    bm = _pick(1024, s, rows)
    bn = _pick(512, n, o_gate)
    return pl.pallas_call(
        functools.partial(_proj_kernel, n_gate0=o_gate // bn, rc=_pick(128, bm)),
        grid=(rows // bm, n // bn),
        in_specs=[pl.BlockSpec((bm, d), lambda i, j: (i, 0)),
                  _mod_spec(d, bm, s, b),
                  pl.BlockSpec((1, d), lambda i, j: (0, 0)),
                  pl.BlockSpec((d, bn), lambda i, j: (0, j))],
        out_specs=pl.BlockSpec((bm, bn), lambda i, j: (i, j)),
        out_shape=jax.ShapeDtypeStruct((rows, n), BF16),
        scratch_shapes=[pltpu.VMEM((bm, d), BF16)],
        compiler_params=_params(("parallel", "arbitrary")),
        name="in_proj",
    )(h, mods, norm_g.reshape(1, d), w_p)


def _conv_kernel(gb_ref, gc_ref, v_ref, gcp_ref, vp_ref, gcn_ref, vn_ref, w_ref, o_ref, *, bm, s, c, mx):
    row0 = pl.program_id(0) * bm
    is_lat = row0 < mx
    at_start = jnp.where(is_lat, row0 % s == 0, (row0 - mx) % c == 0)
    at_end = jnp.where(is_lat, (row0 + bm) % s == 0, (row0 - mx + bm) % c == 0)
    cv = gc_ref[...].astype(F32) * v_ref[...].astype(F32)
    hp = (gcp_ref[...].astype(F32) * vp_ref[...].astype(F32))[15:16, :]
    hn = (gcn_ref[...].astype(F32) * vn_ref[...].astype(F32))[0:1, :]
    hp = jnp.where(at_start, 0.0, hp)
    hn = jnp.where(at_end, 0.0, hn)
    rid = lax.broadcasted_iota(jnp.int32, (bm, 1), 0)
    prev = jnp.where(rid == 0, hp, pltpu.roll(cv, 1, 0))
    nxt = jnp.where(rid == bm - 1, hn, pltpu.roll(cv, bm - 1, 0))
    w = w_ref[...]
    y = gb_ref[...].astype(F32) * (prev * w[0:1, :] + cv * w[1:2, :] + nxt * w[2:3, :])
    o_ref[...] = y.astype(BF16)


def _conv(px, conv_w, rows, s, c, mx):
    cw = conv_w.shape[1]
    m = px.shape[0]
    bm = _pick(256, s, c)
    hb = bm // 16
    main = lambda col: pl.BlockSpec((bm, cw), lambda i: (i, col))
    prev = lambda col: pl.BlockSpec((16, cw), lambda i: (jnp.maximum(i * hb - 1, 0), col))
    nxt = lambda col: pl.BlockSpec((16, cw), lambda i: (jnp.minimum((i + 1) * hb, m // 16 - 1), col))
    return pl.pallas_call(
        functools.partial(_conv_kernel, bm=bm, s=s, c=c, mx=mx),
        grid=(rows // bm,),
        in_specs=[main(0), main(1), main(2), prev(1), prev(2), nxt(1), nxt(2),
                  pl.BlockSpec((CONV_K, cw), lambda i: (0, 0))],
        out_specs=pl.BlockSpec((bm, cw), lambda i: (i, 0)),
        out_shape=jax.ShapeDtypeStruct((rows, cw), BF16),
        compiler_params=_params(("parallel",)),
        name="gated_conv",
    )(px, px, px, px, px, px, px, conv_w)


def _rope(x, cos, sin, half):
    lane = lax.broadcasted_iota(jnp.int32, x.shape, 1)
    first = (lane % (2 * half)) < half
    rot = jnp.where(first, pltpu.roll(x, LANE - half, 1), pltpu.roll(x, half, 1))
    return x * cos + rot * sin


def _prep_kernel(gq_ref, cq_ref, ckv_ref, gk_ref, kr_ref, qn_ref, kvn_ref, wqb_ref, wkvb_ref,
                 mcos_ref, msin_ref, gcos_ref, gsin_ref, q_ref, k_ref, v_ref, gqo_ref, gko_ref):
    mcos, msin = mcos_ref[...], msin_ref[...]
    gcos, gsin = gcos_ref[...], gsin_ref[...]
    mh = MLA_ROPE // 4
    gh = GQA_HEAD_DIM // 4

    cqn = _rms(cq_ref[...].astype(F32), qn_ref[...]).astype(BF16)
    q = _dot(cqn, wqb_ref[...]) * (MLA_SCALE * LOG2E)
    ckvn = _rms(ckv_ref[...].astype(F32), kvn_ref[...]).astype(BF16)
    kv = _dot(ckvn, wkvb_ref[...])
    kr = _rope(kr_ref[...].astype(F32), mcos, msin, mh).astype(BF16)
    ones = jnp.ones((q.shape[0], LANE), BF16)
    v0 = MLA_HEADS * MLA_NOPE
    for h in range(MLA_HEADS):
        a = h * MLA_QK
        q_ref[:, a:a + LANE] = q[:, a:a + LANE].astype(BF16)
        q_ref[:, a + LANE:a + MLA_QK] = _rope(q[:, a + LANE:a + MLA_QK], mcos, msin, mh).astype(BF16)
        k_ref[:, a:a + LANE] = kv[:, h * LANE:(h + 1) * LANE].astype(BF16)
        k_ref[:, a + LANE:a + MLA_QK] = kr
        v_ref[:, a:a + LANE] = kv[:, v0 + h * LANE:v0 + (h + 1) * LANE].astype(BF16)
        v_ref[:, a + LANE:a + MLA_QK] = ones

    gq = gq_ref[...].astype(F32)
    for h in range(GQA_HEADS):
        a = h * GQA_HEAD_DIM
        gqo_ref[:, a:a + LANE] = (_rope(gq[:, a:a + LANE], gcos, gsin, gh) * (GQA_SCALE * LOG2E)).astype(BF16)
    gk = gk_ref[...].astype(F32)
    for h in range(GQA_KV_HEADS):
        a = h * GQA_HEAD_DIM
        gko_ref[:, a:a + LANE] = _rope(gk[:, a:a + LANE], gcos, gsin, gh).astype(BF16)


def _prep(px, lay, q_norm, kv_norm, wqb_p, wkvb_r, tabs, s, mx):
    m = px.shape[0]
    bm = _pick(512, s, m - mx)
    n_lat = mx // bm
    per = s // bm
    ql, kvl = q_norm.shape[0], kv_norm.shape[0]
    gqw, gkw = GQA_HEADS * GQA_HEAD_DIM, GQA_KV_HEADS * GQA_HEAD_DIM

    def col(width, off):
        assert off % width == 0, (width, off)
        return pl.BlockSpec((bm, width), lambda i: (i, off // width))

    const = lambda r, c: pl.BlockSpec((r, c), lambda i: (0, 0))
    tab = pl.BlockSpec((bm, LANE), lambda i: (jnp.where(i < n_lat, i % per, per + i - n_lat), 0))
    row = lambda width: pl.BlockSpec((bm, width), lambda i: (i, 0))
    hq = MLA_HEADS * MLA_QK
    return pl.pallas_call(
        _prep_kernel,
        grid=(m // bm,),
        in_specs=[col(gqw, lay["gq"]), col(ql, lay["cq"]), col(kvl, lay["ckv"]), col(gkw, lay["gk"]),
                  col(LANE, lay["kr"]), const(1, ql), const(1, kvl), const(ql, hq),
                  const(kvl, MLA_HEADS * (MLA_NOPE + MLA_V)), tab, tab, tab, tab],
        out_specs=[row(hq), row(hq), row(hq), row(gqw), row(gkw)],
        out_shape=[jax.ShapeDtypeStruct((m, hq), BF16), jax.ShapeDtypeStruct((m, hq), BF16),
                   jax.ShapeDtypeStruct((m, hq), BF16),
                   jax.ShapeDtypeStruct((m, gqw), BF16), jax.ShapeDtypeStruct((m, gkw), BF16)],
        compiler_params=_params(("parallel",)),
        name="attn_prep",
    )(px, px, px, px, px, q_norm.reshape(1, ql), kv_norm.reshape(1, kvl), wqb_p, wkvb_r, *tabs)


def _mla_kernel(q_ref, kl_ref, kc_ref, vl_ref, vc_ref, o_ref, *, n_sub):
    kl, kc = kl_ref[...], kc_ref[...]
    subs = [q_ref[i * MLA_SUB:(i + 1) * MLA_SUB, :] for i in range(n_sub)]
    scores = [(_dot_nt(q, kl), _dot_nt(q, kc)) for q in subs]
    for i, (s1, s2) in enumerate(scores):
        m = jnp.maximum(jnp.max(s1, axis=-1, keepdims=True), jnp.max(s2, axis=-1, keepdims=True))
        p1 = jnp.exp2(s1 - m).astype(BF16)
        p2 = jnp.exp2(s2 - m).astype(BF16)
        o = _dot(p1, vl_ref[...]) + _dot(p2, vc_ref[...])
        o_ref[i * MLA_SUB:(i + 1) * MLA_SUB, :] = (o[:, :MLA_V] / o[:, MLA_V:]).astype(BF16)


def _mla_ctx_kernel(q_ref, kc_ref, vc_ref, y_ref, o_ref):
    del y_ref
    s2 = _dot_nt(q_ref[...], kc_ref[...])
    p2 = jnp.exp2(s2 - jnp.max(s2, axis=-1, keepdims=True)).astype(BF16)
    o = _dot(p2, vc_ref[...])
    o_ref[...] = (o[:, :MLA_V] / o[:, MLA_V:]).astype(BF16)


def _mla_attn(q, k, v, b, s, c, with_ctx):
    mx = b * s
    n_sub = max(d for d in (1, 2, 4) if (s // MLA_SUB) % d == 0)
    bq = n_sub * MLA_SUB
    nq = s // bq
    rows = mx + (b * c if with_ctx else 0)
    ctx0 = mx // c
    y = pl.pallas_call(
        functools.partial(_mla_kernel, n_sub=n_sub),
        grid=(b, MLA_HEADS, nq),
        in_specs=[pl.BlockSpec((bq, MLA_QK), lambda bi, h, qi: (bi * nq + qi, h)),
                  pl.BlockSpec((s, MLA_QK), lambda bi, h, qi: (bi, h)),
                  pl.BlockSpec((c, MLA_QK), lambda bi, h, qi: (ctx0 + bi, h)),
                  pl.BlockSpec((s, MLA_QK), lambda bi, h, qi: (bi, h)),
                  pl.BlockSpec((c, MLA_QK), lambda bi, h, qi: (ctx0 + bi, h))],
        out_specs=pl.BlockSpec((bq, MLA_V), lambda bi, h, qi: (bi * nq + qi, h)),
        out_shape=jax.ShapeDtypeStruct((rows, MLA_HEADS * MLA_V), BF16),
        compiler_params=_params(("parallel", "parallel", "arbitrary")),
        name="mla_attn",
    )(q, k, k, v, v)
    if not with_ctx:
        return y
    return pl.pallas_call(
        _mla_ctx_kernel,
        grid=(b, MLA_HEADS),
        in_specs=[pl.BlockSpec((c, MLA_QK), lambda bi, h: (ctx0 + bi, h)),
                  pl.BlockSpec((c, MLA_QK), lambda bi, h: (ctx0 + bi, h)),
                  pl.BlockSpec((c, MLA_QK), lambda bi, h: (ctx0 + bi, h)),
                  pl.BlockSpec(memory_space=pl.ANY)],
        out_specs=pl.BlockSpec((c, MLA_V), lambda bi, h: (ctx0 + bi, h)),
        out_shape=jax.ShapeDtypeStruct(y.shape, BF16),
        input_output_aliases={3: 0},
        compiler_params=_params(("parallel", "parallel")),
        name="mla_attn_ctx",
    )(q, k, v, y)


GQA_QB = 4


def _sink_rows(sink_ref, g, j, rows):
    return jnp.broadcast_to(sink_ref[pl.ds(g * GQA_GROUP + j, 1), :], (rows, LANE))[:, 0:1]


def _gqa_kernel(q_ref, kp_ref, kc_ref, kn_ref, kx_ref, vp_ref, vc_ref, vn_ref, vx_ref, sink_ref, o_ref,
                *, n_lat):
    g = pl.program_id(1)
    n = pl.program_id(2)
    blk = ATT_BLOCK
    kx, vx = kx_ref[...], vx_ref[...]
    kband = jnp.concatenate([kp_ref[...], kc_ref[...], kn_ref[...]], axis=0)
    vband = jnp.concatenate([vp_ref[...], vc_ref[...], vn_ref[...]], axis=0)
    nk = 3 * blk + kx.shape[0]
    ones = jnp.ones((nk, LANE), BF16)
    snk = jnp.concatenate([jnp.broadcast_to(sink_ref[pl.ds(g * GQA_GROUP + j, 1), :], (blk, LANE))
                           for j in range(GQA_GROUP)], axis=0)
    rows = GQA_GROUP * blk
    r = lax.broadcasted_iota(jnp.int32, (rows, nk), 0) % blk
    col = lax.broadcasted_iota(jnp.int32, (rows, nk), 1)
    in_window = (jnp.abs(col - blk - r) <= WINDOW) | (col >= 3 * blk)
    for i in range(GQA_QB):
        qs = jnp.concatenate([q_ref[i * blk:(i + 1) * blk, j * LANE:(j + 1) * LANE] for j in range(GQA_GROUP)],
                             axis=0)
        keys = jnp.concatenate([kband[i * blk:(i + 3) * blk, :], kx], axis=0)
        vals = jnp.concatenate([jnp.concatenate([vband[i * blk:(i + 3) * blk, :], vx], axis=0), ones], axis=1)
        block = n * GQA_QB + i
        valid = in_window & ((col >= blk) | (block > 0)) & ((col < 2 * blk) | (col >= 3 * blk) | (block < n_lat - 1))
        sc = jnp.where(valid, _dot_nt(qs, keys), -jnp.inf)
        e = snk
        for t in range(nk // LANE):
            e = jnp.maximum(e, sc[:, t * LANE:(t + 1) * LANE])
        m = jnp.max(e, axis=-1, keepdims=True)
        ol = _dot(jnp.exp2(sc - m).astype(BF16), vals)
        o = ol[:, :LANE] / (ol[:, LANE:] + jnp.exp2(snk - m))
        for j in range(GQA_GROUP):
            o_ref[i * blk:(i + 1) * blk, j * LANE:(j + 1) * LANE] = o[j * blk:(j + 1) * blk, :].astype(BF16)


def _gqa_ctx_kernel(q_ref, kx_ref, vx_ref, sink_ref, y_ref, o_ref):
    del y_ref
    g = pl.program_id(1)
    kx, vx = kx_ref[...], vx_ref[...]
    for j in range(GQA_GROUP):
        sc = _dot_nt(q_ref[:, j * LANE:(j + 1) * LANE], kx)
        snk = _sink_rows(sink_ref, g, j, sc.shape[0])
        m = jnp.maximum(jnp.max(sc, axis=-1, keepdims=True), snk)
        p_c = jnp.exp2(sc - m)
        l = jnp.sum(p_c, axis=-1, keepdims=True) + jnp.exp2(snk - m)
        o_ref[:, j * LANE:(j + 1) * LANE] = (_dot(p_c.astype(BF16), vx) / l).astype(BF16)


def _gqa_attn(gq, gk, px, o_gv, sink, b, s, c, with_ctx):
    mx = b * s
    blk = ATT_BLOCK
    n_lat = s // blk
    nstep = n_lat // GQA_QB
    rows = mx + (b * c if with_ctx else 0)
    ctx0 = mx // c
    gvc = o_gv // LANE
    gw = GQA_GROUP * GQA_HEAD_DIM
    big = GQA_QB * blk

    def edge(shift, col0):
        def idx(bi, g, n):
            return bi * n_lat + jnp.clip(n * GQA_QB + shift, 0, n_lat - 1), col0 + g
        return pl.BlockSpec((blk, LANE), idx)

    main = lambda col0: pl.BlockSpec((big, LANE), lambda bi, g, n: (bi * nstep + n, col0 + g))
    ctx = lambda col0: pl.BlockSpec((c, LANE), lambda bi, g, n: (ctx0 + bi, col0 + g))
    sink_spec = pl.BlockSpec((GQA_HEADS, LANE), lambda *_: (0, 0))
    y = pl.pallas_call(
        functools.partial(_gqa_kernel, n_lat=n_lat),
        grid=(b, GQA_KV_HEADS, nstep),
        in_specs=[pl.BlockSpec((big, gw), lambda bi, g, n: (bi * nstep + n, g)),
                  edge(-1, 0), main(0), edge(GQA_QB, 0), ctx(0),
                  edge(-1, gvc), main(gvc), edge(GQA_QB, gvc), ctx(gvc), sink_spec],
        out_specs=pl.BlockSpec((big, gw), lambda bi, g, n: (bi * nstep + n, g)),
        out_shape=jax.ShapeDtypeStruct((rows, GQA_HEADS * GQA_HEAD_DIM), BF16),
        compiler_params=_params(("parallel", "parallel", "arbitrary")),
        name="gqa_attn",
    )(gq, gk, gk, gk, gk, px, px, px, px, sink)
    if not with_ctx:
        return y
    return pl.pallas_call(
        _gqa_ctx_kernel,
        grid=(b, GQA_KV_HEADS),
        in_specs=[pl.BlockSpec((c, gw), lambda bi, g: (ctx0 + bi, g)),
                  pl.BlockSpec((c, LANE), lambda bi, g: (ctx0 + bi, g)),
                  pl.BlockSpec((c, LANE), lambda bi, g: (ctx0 + bi, gvc + g)),
                  sink_spec, pl.BlockSpec(memory_space=pl.ANY)],
        out_specs=pl.BlockSpec((c, gw), lambda bi, g: (ctx0 + bi, g)),
        out_shape=jax.ShapeDtypeStruct(y.shape, BF16),
        input_output_aliases={4: 0},
        compiler_params=_params(("parallel", "parallel")),
        name="gqa_attn_ctx",
    )(gq, gk, px, sink, y)


def _merge_kernel(yc_ref, ym_ref, yg_ref, gc_ref, gm_ref, gg_ref, wc_ref, wm_ref, wg_ref, o_ref):
    acc = gc_ref[...].astype(F32) * _dot(yc_ref[...], wc_ref[...])
    acc += gm_ref[...].astype(F32) * _dot(ym_ref[...], wm_ref[...])
    acc += gg_ref[...].astype(F32) * _dot(yg_ref[...], wg_ref[...])
    o_ref[...] = acc.astype(BF16)


def _merge(yc, ym, yg, px, o_gate, w_bc, w_bm, w_bg, rows, s):
    d = w_bc.shape[1]
    bm = _pick(1024, s, rows)
    bn = _pick(512, d, o_gate)
    g0 = o_gate // bn
    per = d // bn
    yspec = lambda a: pl.BlockSpec((bm, a.shape[1]), lambda i, j: (i, 0))
    gspec = lambda k: pl.BlockSpec((bm, bn), lambda i, j: (i, g0 + k * per + j))
    wspec = lambda w: pl.BlockSpec((w.shape[0], bn), lambda i, j: (0, j))
    return pl.pallas_call(
        _merge_kernel,
        grid=(rows // bm, d // bn),
        in_specs=[yspec(yc), yspec(ym), yspec(yg), gspec(0), gspec(1), gspec(2),
                  wspec(w_bc), wspec(w_bm), wspec(w_bg)],
        out_specs=pl.BlockSpec((bm, bn), lambda i, j: (i, j)),
        out_shape=jax.ShapeDtypeStruct((rows, d), BF16),
        compiler_params=_params(("parallel", "arbitrary")),
        name="branch_merge",
    )(yc, ym, yg, px, px, px, w_bc, w_bm, w_bg)


def _out_kernel(m_ref, w_ref, h_ref, mod_ref, o_ref):
    o_ref[...] = h_ref[...] + mod_ref[0, 5:6, :] * _dot(m_ref[...], w_ref[...])


def _out_proj(merged, w_out, h, mods, rows, s, b):
    d = w_out.shape[1]
    bm = _pick(1024, s, rows)
    bn = _pick(512, d)
    per = s // bm
    return pl.pallas_call(
        _out_kernel,
        grid=(rows // bm, d // bn),
        in_specs=[pl.BlockSpec((bm, d), lambda i, j: (i, 0)),
                  pl.BlockSpec((d, bn), lambda i, j: (0, j)),
                  pl.BlockSpec((bm, bn), lambda i, j: (i, j)),
                  pl.BlockSpec((1, N_MOD, bn), lambda i, j: (jnp.minimum(i // per, b), 0, j))],
        out_specs=pl.BlockSpec((bm, bn), lambda i, j: (i, j)),
        out_shape=jax.ShapeDtypeStruct((rows, d), F32),
        compiler_params=_params(("parallel", "arbitrary")),
        name="out_proj",
    )(merged, w_out, h, mods)


def _final_kernel(h_ref, g_ref, o_ref):
    o_ref[...] = _rms(h_ref[...], g_ref[...])


def _final_norm(h, g):
    rows, d = h.shape
    bm = _pick(512, rows)
    return pl.pallas_call(
        _final_kernel,
        grid=(rows // bm,),
        in_specs=[pl.BlockSpec((bm, d), lambda i: (i, 0)), pl.BlockSpec((1, d), lambda i: (0, 0))],
        out_specs=pl.BlockSpec((bm, d), lambda i: (i, 0)),
        out_shape=jax.ShapeDtypeStruct((rows, d), F32),
        compiler_params=_params(("parallel",)),
        name="final_norm",
    )(h, g.reshape(1, d))


def _rope_table(s, n_ctx_rows, dim):
    t = jnp.arange(s, dtype=jnp.int32)
    pos = jnp.stack([t // GRID_W, t % GRID_W], axis=1).astype(F32)
    inv = ROPE_BASE ** (-jnp.arange(0, dim, 2, dtype=F32) / dim)
    lane = jnp.arange(LANE)
    ang = pos[:, jnp.minimum(lane // dim, 1)] * inv[lane % (dim // 2)][None, :]
    active = (lane < 2 * dim)[None, :]
    sign = jnp.where((lane % dim) < dim // 2, -1.0, 1.0)[None, :]
    cos = jnp.where(active, jnp.cos(ang), 1.0)
    sin = jnp.where(active, jnp.sin(ang) * sign, 0.0)
    pad = ((0, n_ctx_rows), (0, 0))
    return jnp.pad(cos, pad, constant_values=1.0), jnp.pad(sin, pad)


def _layout(cw, ql, kvl, d):
    lay = {"conv": 0}
    off = 3 * cw
    for name, width in (("gq", GQA_HEADS * GQA_HEAD_DIM), ("cq", ql), ("ckv", kvl),
                        ("gk", GQA_KV_HEADS * GQA_HEAD_DIM), ("gv", GQA_KV_HEADS * GQA_HEAD_DIM),
                        ("kr", LANE)):
        lay[name] = off
        off += width
    off = -(-off // 512) * 512
    lay["gate"] = off
    lay["total"] = off + 3 * d
    return lay


def _pack_w_in(w, lay, cw, ql, kvl):
    d = w.shape[0]
    o_mla = 3 * cw
    o_gqa = o_mla + ql + kvl + MLA_ROPE
    gqw, gkw = GQA_HEADS * GQA_HEAD_DIM, GQA_KV_HEADS * GQA_HEAD_DIM
    o_gate = o_gqa + gqw + 2 * gkw
    parts = [w[:, :o_mla], w[:, o_gqa:o_gqa + gqw], w[:, o_mla:o_mla + ql + kvl],
             w[:, o_gqa + gqw:o_gate], w[:, o_mla + ql + kvl:o_gqa]]
    used = o_gate
    parts.append(jnp.zeros((d, lay["gate"] - used), w.dtype))
    parts.append(w[:, o_gate:])
    return jnp.concatenate(parts, axis=1).astype(BF16)


def kernel(x, c, ctx, c_ctx, ada_w, ada_b, ffn1_norm, ffn1_w_gu, ffn1_w_down, mix_norm, w_in, conv_w,
           mla_q_norm, mla_w_qb, mla_kv_norm, mla_w_kvb, gqa_sink, w_branch_conv, w_branch_mla,
           w_branch_gqa, w_out, ffn2_norm, ffn2_w_gu, ffn2_w_down, final_norm):
    b, s, d = x.shape
    cl = ctx.shape[1]
    depth = ada_w.shape[0]
    cw = conv_w.shape[-1]
    ql, kvl = mla_q_norm.shape[-1], mla_kv_norm.shape[-1]
    mx, mc = b * s, b * cl
    assert b + 1 <= 8 and s % cl == 0 and cl % ATT_BLOCK == 0

    cvec = jnp.zeros((8, d), F32).at[:b].set(c).at[b].set(c_ctx)
    mods = _mods(cvec, ada_w, ada_b).reshape(depth, 8, N_MOD, d)

    lay = _layout(cw, ql, kvl, d)
    tabs = _rope_table(s, mc, MLA_ROPE // 2) + _rope_table(s, mc, GQA_HEAD_DIM // 2)
    h = jnp.concatenate([x.reshape(mx, d), ctx.reshape(mc, d)], axis=0)

    for l in range(depth):
        with_ctx = l < depth - 1
        rows = mx + mc if with_ctx else mx
        w_p = _pack_w_in(w_in[l], lay, cw, ql, kvl)
        wqb_p = jnp.pad(mla_w_qb[l].reshape(ql, MLA_HEADS, MLA_NOPE + MLA_ROPE),
                        ((0, 0), (0, 0), (0, MLA_QK - MLA_NOPE - MLA_ROPE))).reshape(ql, -1).astype(BF16)
        wkv = mla_w_kvb[l].reshape(kvl, MLA_HEADS, MLA_NOPE + MLA_V)
        wkvb_r = jnp.concatenate([wkv[:, :, :MLA_NOPE].reshape(kvl, -1), wkv[:, :, MLA_NOPE:].reshape(kvl, -1)],
                                 axis=1).astype(BF16)
        sink = jnp.broadcast_to((gqa_sink[l].astype(F32) * LOG2E)[:, None], (GQA_HEADS, LANE))

        h = _ffn(h, mx + mc, mods[l], ffn1_norm[l], ffn1_w_gu[l].astype(BF16), ffn1_w_down[l].astype(BF16),
                 0, s, b)
        px = _proj(h, mods[l], mix_norm[l], w_p, lay["gate"], s, b)
        y_conv = _conv(px, conv_w[l], rows, s, cl, mx)
        q, k, v, gq, gk = _prep(px, lay, mla_q_norm[l], mla_kv_norm[l], wqb_p, wkvb_r, tabs, s, mx)
        y_mla = _mla_attn(q, k, v, b, s, cl, with_ctx)
        y_gqa = _gqa_attn(gq, gk, px, lay["gv"], sink, b, s, cl, with_ctx)
        merged = _merge(y_conv, y_mla, y_gqa, px, lay["gate"], w_branch_conv[l].astype(BF16),
                        w_branch_mla[l].astype(BF16), w_branch_gqa[l].astype(BF16), rows, s)
        h = _out_proj(merged, w_out[l].astype(BF16), h, mods[l], rows, s, b)
        h = _ffn(h, rows, mods[l], ffn2_norm[l], ffn2_w_gu[l].astype(BF16), ffn2_w_down[l].astype(BF16),
                 6, s, b)
    return _final_norm(h, final_norm).reshape(b, s, d)
```

```python
import functools

import jax
import jax.numpy as jnp
from jax import lax
from jax.experimental import pallas as pl
from jax.experimental.pallas import tpu as pltpu

F32 = jnp.float32
BF16 = jnp.bfloat16

GRID_W = 64
N_MOD = 9
EPS = 1e-6
ROPE_BASE = 10000.0
CONV_K = 3
MLA_HEADS = 8
MLA_NOPE = 128
MLA_ROPE = 64
MLA_V = 128
MLA_SCALE = (MLA_NOPE + MLA_ROPE) ** -0.5
GQA_HEADS = 8
GQA_KV_HEADS = 2
GQA_GROUP = GQA_HEADS // GQA_KV_HEADS
GQA_HEAD_DIM = 128
GQA_SCALE = GQA_HEAD_DIM ** -0.5
WINDOW = 128
ATT_BLOCK = 128

LANE = 128
MLA_QK = 2 * LANE
MLA_SUB = 256
LOG2E = 1.4426950408889634
VMEM_LIMIT = 56 << 20
NT_DIMS = (((1,), (1,)), ((), ()))


def _pick(target, *sizes):
    b = target
    while any(s % b for s in sizes):
        b //= 2
        assert b >= 8, (target, sizes)
    return b


def _params(sem):
    return pltpu.CompilerParams(dimension_semantics=sem, vmem_limit_bytes=VMEM_LIMIT)


def _dot(a, b):
    return jnp.dot(a, b, preferred_element_type=F32)


def _dot_nt(a, b):
    return lax.dot_general(a, b, NT_DIMS, preferred_element_type=F32)


def _rms(x, g):
    return x * lax.rsqrt(jnp.mean(x * x, axis=-1, keepdims=True) + EPS) * g


CAST_BLOCK_BYTES = 6 << 20


def _cast_kernel(w_ref, o_ref):
    o_ref[...] = w_ref[...].astype(BF16)


def _to_bf16(w):
    depth, k, n = w.shape
    rows = depth * k
    rb = _pick(max(16, 1 << ((CAST_BLOCK_BYTES // (4 * n)).bit_length() - 1)), rows)
    out = pl.pallas_call(
        _cast_kernel,
        grid=(rows // rb,),
        in_specs=[pl.BlockSpec((rb, n), lambda i: (i, 0))],
        out_specs=pl.BlockSpec((rb, n), lambda i: (i, 0)),
        out_shape=jax.ShapeDtypeStruct((rows, n), BF16),
        compiler_params=_params(("parallel",)),
        name="cast_bf16",
    )(w.reshape(rows, n))
    return out.reshape(depth, k, n)


def _mods_kernel(c_ref, w_ref, b_ref, o_ref):
    c = c_ref[...]
    s = (c * jax.nn.sigmoid(c)).astype(BF16)
    o_ref[0] = _dot(s, w_ref[0].astype(BF16)) + b_ref[0]


def _mods(cvec, ada_w, ada_b):
    depth, d, n = ada_w.shape
    bn = _pick(1024, n)
    return pl.pallas_call(
        _mods_kernel,
        grid=(depth, n // bn),
        in_specs=[pl.BlockSpec((8, d), lambda l, j: (0, 0)),
                  pl.BlockSpec((1, d, bn), lambda l, j: (l, 0, j)),
                  pl.BlockSpec((1, 1, bn), lambda l, j: (l, 0, j))],
        out_specs=pl.BlockSpec((1, 8, bn), lambda l, j: (l, 0, j)),
        out_shape=jax.ShapeDtypeStruct((depth, 8, n), F32),
        compiler_params=_params(("parallel", "parallel")),
        name="adaln_mods",
    )(cvec, ada_w, ada_b.reshape(depth, 1, n))


def _norm_mod_rows(h_ref, mod_ref, g_ref, xn_ref, copy_ref, i_shift, i_scale, rc):
    shift = mod_ref[0, i_shift:i_shift + 1, :]
    scale1 = 1.0 + mod_ref[0, i_scale:i_scale + 1, :]
    g = g_ref[...]

    def body(r, carry):
        rows = pl.ds(pl.multiple_of(r * rc, rc), rc)
        x = h_ref[rows, :]
        xn_ref[rows, :] = (_rms(x, g) * scale1 + shift).astype(BF16)
        if copy_ref is not None:
            copy_ref[rows, :] = x
        return carry

    lax.fori_loop(0, h_ref.shape[0] // rc, body, 0)


def _mod_spec(d, bm, s, b):
    per = s // bm
    return pl.BlockSpec((1, N_MOD, d), lambda i, j: (jnp.minimum(i // per, b), 0, 0))


def _ffn_kernel(h_ref, mod_ref, g_ref, wg_ref, wu_ref, wd_ref, fg_ref, o_ref, xn_ref, *, i0, rc, final):
    @pl.when(pl.program_id(1) == 0)
    def _():
        _norm_mod_rows(h_ref, mod_ref, g_ref, xn_ref, o_ref, i0, i0 + 1, rc)

    xn = xn_ref[...]
    gg = _dot(xn, wg_ref[...])
    uu = _dot(xn, wu_ref[...])
    a = (gg * jax.nn.sigmoid(gg) * uu).astype(BF16)
    gate = 0.5 * mod_ref[0, i0 + 2:i0 + 3, :]
    o_ref[...] += gate * _dot(a, wd_ref[...])

    if final:
        @pl.when(pl.program_id(1) == pl.num_programs(1) - 1)
        def _():
            fg = fg_ref[...]

            def body(r, carry):
                rows = pl.ds(pl.multiple_of(r * rc, rc), rc)
                o_ref[rows, :] = _rms(o_ref[rows, :], fg)
                return carry

            lax.fori_loop(0, o_ref.shape[0] // rc, body, 0)


def _ffn(h, rows, mods, norm_g, w_gu, w_down, l, i0, s, b, final_g, final):
    d = h.shape[1]
    f = w_down.shape[1]
    bm = _pick(512, s, rows)
    bf = _pick(512, f)
    nf = f // bf
    return pl.pallas_call(
        functools.partial(_ffn_kernel, i0=i0, rc=_pick(128, bm), final=final),
        grid=(rows // bm, nf),
        in_specs=[pl.BlockSpec((bm, d), lambda i, j: (i, 0)),
                  _mod_spec(d, bm, s, b),
                  pl.BlockSpec((1, d), lambda i, j: (0, 0)),
                  pl.BlockSpec((None, d, bf), lambda i, j: (l, 0, j)),
                  pl.BlockSpec((None, d, bf), lambda i, j: (l, 0, j + nf)),
                  pl.BlockSpec((None, bf, d), lambda i, j: (l, j, 0)),
                  pl.BlockSpec((1, d), lambda i, j: (0, 0))],
        out_specs=pl.BlockSpec((bm, d), lambda i, j: (i, 0)),
        out_shape=jax.ShapeDtypeStruct((rows, d), F32),
        scratch_shapes=[pltpu.VMEM((bm, d), BF16)],
        compiler_params=_params(("parallel", "arbitrary")),
        name="ffn_swiglu",
    )(h, mods, norm_g.reshape(1, d), w_gu, w_gu, w_down, final_g.reshape(1, d))


def _proj_kernel(h_ref, mod_ref, g_ref, w_ref, o_ref, xn_ref, *, rc):
    @pl.when(pl.program_id(1) == 0)
    def _():
        _norm_mod_rows(h_ref, mod_ref, g_ref, xn_ref, None, 3, 4, rc)

    o_ref[...] = _dot(xn_ref[...], w_ref[...]).astype(BF16)


def _proj(h, mods, norm_g, w_p, s, b):
    rows, d = h.shape
    n = w_p.shape[1]
    bm = _pick(1024, s, rows)
    bn = _pick(512, n)
    return pl.pallas_call(
        functools.partial(_proj_kernel, rc=_pick(128, bm)),
        grid=(rows // bm, n // bn),
        in_specs=[pl.BlockSpec((bm, d), lambda i, j: (i, 0)),
                  _mod_spec(d, bm, s, b),
                  pl.BlockSpec((1, d), lambda i, j: (0, 0)),
                  pl.BlockSpec((d, bn), lambda i, j: (0, j))],
        out_specs=pl.BlockSpec((bm, bn), lambda i, j: (i, j)),
        out_shape=jax.ShapeDtypeStruct((rows, n), BF16),
        scratch_shapes=[pltpu.VMEM((bm, d), BF16)],
        compiler_params=_params(("parallel", "arbitrary")),
        name="in_proj",
    )(h, mods, norm_g.reshape(1, d), w_p)


def _conv_kernel(gb_ref, gc_ref, v_ref, gcp_ref, vp_ref, gcn_ref, vn_ref, w_ref, o_ref, *, bm, s, c, mx):
    row0 = pl.program_id(0) * bm
    is_lat = row0 < mx
    at_start = jnp.where(is_lat, row0 % s == 0, (row0 - mx) % c == 0)
    at_end = jnp.where(is_lat, (row0 + bm) % s == 0, (row0 - mx + bm) % c == 0)
    cv = gc_ref[...].astype(F32) * v_ref[...].astype(F32)
    hp = (gcp_ref[...].astype(F32) * vp_ref[...].astype(F32))[15:16, :]
    hn = (gcn_ref[...].astype(F32) * vn_ref[...].astype(F32))[0:1, :]
    hp = jnp.where(at_start, 0.0, hp)
    hn = jnp.where(at_end, 0.0, hn)
    rid = lax.broadcasted_iota(jnp.int32, (bm, 1), 0)
    prev = jnp.where(rid == 0, hp, pltpu.roll(cv, 1, 0))
    nxt = jnp.where(rid == bm - 1, hn, pltpu.roll(cv, bm - 1, 0))
    w = w_ref[...]
    y = gb_ref[...].astype(F32) * (prev * w[0:1, :] + cv * w[1:2, :] + nxt * w[2:3, :])
    o_ref[...] = y.astype(BF16)


def _conv(px, conv_w, rows, s, c, mx):
    cw = conv_w.shape[1]
    m = px.shape[0]
    bm = _pick(256, s, c)
    hb = bm // 16
    main = lambda col: pl.BlockSpec((bm, cw), lambda i: (i, col))
    prev = lambda col: pl.BlockSpec((16, cw), lambda i: (jnp.maximum(i * hb - 1, 0), col))
    nxt = lambda col: pl.BlockSpec((16, cw), lambda i: (jnp.minimum((i + 1) * hb, m // 16 - 1), col))
    return pl.pallas_call(
        functools.partial(_conv_kernel, bm=bm, s=s, c=c, mx=mx),
        grid=(rows // bm,),
        in_specs=[main(0), main(1), main(2), prev(1), prev(2), nxt(1), nxt(2),
                  pl.BlockSpec((CONV_K, cw), lambda i: (0, 0))],
        out_specs=pl.BlockSpec((bm, cw), lambda i: (i, 0)),
        out_shape=jax.ShapeDtypeStruct((rows, cw), BF16),
        compiler_params=_params(("parallel",)),
        name="gated_conv",
    )(px, px, px, px, px, px, px, conv_w)


def _rope(x, cos, sin, half):
    lane = lax.broadcasted_iota(jnp.int32, x.shape, 1)
    first = (lane % (2 * half)) < half
    rot = jnp.where(first, pltpu.roll(x, LANE - half, 1), pltpu.roll(x, half, 1))
    return x * cos + rot * sin


def _prep_kernel(gq_ref, cq_ref, ckv_ref, gk_ref, kr_ref, qn_ref, kvn_ref, wqb_ref, wkvb_ref,
                 mcos_ref, msin_ref, gcos_ref, gsin_ref, q_ref, k_ref, v_ref, gqo_ref, gko_ref):
    mcos, msin = mcos_ref[...], msin_ref[...]
    gcos, gsin = gcos_ref[...], gsin_ref[...]
    mh = MLA_ROPE // 4
    gh = GQA_HEAD_DIM // 4

    cqn = _rms(cq_ref[...].astype(F32), qn_ref[...]).astype(BF16)
    q = _dot(cqn, wqb_ref[...]) * (MLA_SCALE * LOG2E)
    ckvn = _rms(ckv_ref[...].astype(F32), kvn_ref[...]).astype(BF16)
    kv = _dot(ckvn, wkvb_ref[...])
    kr = _rope(kr_ref[...].astype(F32), mcos, msin, mh).astype(BF16)
    ones = jnp.ones((q.shape[0], LANE), BF16)
    v0 = MLA_HEADS * MLA_NOPE
    for h in range(MLA_HEADS):
        a = h * MLA_QK
        q_ref[:, a:a + LANE] = q[:, a:a + LANE].astype(BF16)
        q_ref[:, a + LANE:a + MLA_QK] = _rope(q[:, a + LANE:a + MLA_QK], mcos, msin, mh).astype(BF16)
        k_ref[:, a:a + LANE] = kv[:, h * LANE:(h + 1) * LANE].astype(BF16)
        k_ref[:, a + LANE:a + MLA_QK] = kr
        v_ref[:, a:a + LANE] = kv[:, v0 + h * LANE:v0 + (h + 1) * LANE].astype(BF16)
        v_ref[:, a + LANE:a + MLA_QK] = ones

    gq = gq_ref[...].astype(F32)
    for h in range(GQA_HEADS):
        a = h * GQA_HEAD_DIM
        gqo_ref[:, a:a + LANE] = (_rope(gq[:, a:a + LANE], gcos, gsin, gh) * (GQA_SCALE * LOG2E)).astype(BF16)
    gk = gk_ref[...].astype(F32)
    for h in range(GQA_KV_HEADS):
        a = h * GQA_HEAD_DIM
        gko_ref[:, a:a + LANE] = _rope(gk[:, a:a + LANE], gcos, gsin, gh).astype(BF16)


def _prep(px, lay, q_norm, kv_norm, wqb_p, wkvb_r, tabs, s, mx):
    m = px.shape[0]
    bm = _pick(512, s, m - mx)
    n_lat = mx // bm
    per = s // bm
    ql, kvl = q_norm.shape[0], kv_norm.shape[0]
    gqw, gkw = GQA_HEADS * GQA_HEAD_DIM, GQA_KV_HEADS * GQA_HEAD_DIM

    def col(width, off):
        assert off % width == 0, (width, off)
        return pl.BlockSpec((bm, width), lambda i: (i, off // width))

    const = lambda r, c: pl.BlockSpec((r, c), lambda i: (0, 0))
    tab = pl.BlockSpec((bm, LANE), lambda i: (jnp.where(i < n_lat, i % per, per + i - n_lat), 0))
    row = lambda width: pl.BlockSpec((bm, width), lambda i: (i, 0))
    hq = MLA_HEADS * MLA_QK
    return pl.pallas_call(
        _prep_kernel,
        grid=(m // bm,),
        in_specs=[col(gqw, lay["gq"]), col(ql, lay["cq"]), col(kvl, lay["ckv"]), col(gkw, lay["gk"]),
                  col(LANE, lay["kr"]), const(1, ql), const(1, kvl), const(ql, hq),
                  const(kvl, MLA_HEADS * (MLA_NOPE + MLA_V)), tab, tab, tab, tab],
        out_specs=[row(hq), row(hq), row(hq), row(gqw), row(gkw)],
        out_shape=[jax.ShapeDtypeStruct((m, hq), BF16), jax.ShapeDtypeStruct((m, hq), BF16),
                   jax.ShapeDtypeStruct((m, hq), BF16),
                   jax.ShapeDtypeStruct((m, gqw), BF16), jax.ShapeDtypeStruct((m, gkw), BF16)],
        compiler_params=_params(("parallel",)),
        name="attn_prep",
    )(px, px, px, px, px, q_norm.reshape(1, ql), kv_norm.reshape(1, kvl), wqb_p, wkvb_r, *tabs)


def _mla_kernel(q_ref, kl_ref, kc_ref, vl_ref, vc_ref, o_ref, *, n_sub):
    kl, kc = kl_ref[...], kc_ref[...]
    subs = [q_ref[i * MLA_SUB:(i + 1) * MLA_SUB, :] for i in range(n_sub)]
    scores = [(_dot_nt(q, kl), _dot_nt(q, kc)) for q in subs]
    for i, (s1, s2) in enumerate(scores):
        m = jnp.maximum(jnp.max(s1, axis=-1, keepdims=True), jnp.max(s2, axis=-1, keepdims=True))
        p1 = jnp.exp2(s1 - m).astype(BF16)
        p2 = jnp.exp2(s2 - m).astype(BF16)
        o = _dot(p1, vl_ref[...]) + _dot(p2, vc_ref[...])
        o_ref[i * MLA_SUB:(i + 1) * MLA_SUB, :] = (o[:, :MLA_V] / o[:, MLA_V:]).astype(BF16)


def _mla_ctx_kernel(q_ref, kc_ref, vc_ref, y_ref, o_ref):
    del y_ref
    s2 = _dot_nt(q_ref[...], kc_ref[...])
    p2 = jnp.exp2(s2 - jnp.max(s2, axis=-1, keepdims=True)).astype(BF16)
    o = _dot(p2, vc_ref[...])
    o_ref[...] = (o[:, :MLA_V] / o[:, MLA_V:]).astype(BF16)


def _mla_attn(q, k, v, b, s, c, with_ctx):
    mx = b * s
    n_sub = max(d for d in (1, 2, 4) if (s // MLA_SUB) % d == 0)
    bq = n_sub * MLA_SUB
    nq = s // bq
    rows = mx + (b * c if with_ctx else 0)
    ctx0 = mx // c
    y = pl.pallas_call(
        functools.partial(_mla_kernel, n_sub=n_sub),
        grid=(b, MLA_HEADS, nq),
        in_specs=[pl.BlockSpec((bq, MLA_QK), lambda bi, h, qi: (bi * nq + qi, h)),
                  pl.BlockSpec((s, MLA_QK), lambda bi, h, qi: (bi, h)),
                  pl.BlockSpec((c, MLA_QK), lambda bi, h, qi: (ctx0 + bi, h)),
                  pl.BlockSpec((s, MLA_QK), lambda bi, h, qi: (bi, h)),
                  pl.BlockSpec((c, MLA_QK), lambda bi, h, qi: (ctx0 + bi, h))],
        out_specs=pl.BlockSpec((bq, MLA_V), lambda bi, h, qi: (bi * nq + qi, h)),
        out_shape=jax.ShapeDtypeStruct((rows, MLA_HEADS * MLA_V), BF16),
        compiler_params=_params(("parallel", "parallel", "arbitrary")),
        name="mla_attn",
    )(q, k, k, v, v)
    if not with_ctx:
        return y
    return pl.pallas_call(
        _mla_ctx_kernel,
        grid=(b, MLA_HEADS),
        in_specs=[pl.BlockSpec((c, MLA_QK), lambda bi, h: (ctx0 + bi, h)),
                  pl.BlockSpec((c, MLA_QK), lambda bi, h: (ctx0 + bi, h)),
                  pl.BlockSpec((c, MLA_QK), lambda bi, h: (ctx0 + bi, h)),
                  pl.BlockSpec(memory_space=pl.ANY)],
        out_specs=pl.BlockSpec((c, MLA_V), lambda bi, h: (ctx0 + bi, h)),
        out_shape=jax.ShapeDtypeStruct(y.shape, BF16),
        input_output_aliases={3: 0},
        compiler_params=_params(("parallel", "parallel")),
        name="mla_attn_ctx",
    )(q, k, v, y)


GQA_QB = 4


def _sink_rows(sink_ref, g, j, rows):
    return jnp.broadcast_to(sink_ref[pl.ds(g * GQA_GROUP + j, 1), :], (rows, LANE))[:, 0:1]


def _gqa_kernel(q_ref, kp_ref, kc_ref, kn_ref, kx_ref, vp_ref, vc_ref, vn_ref, vx_ref, sink_ref, o_ref,
                *, n_lat):
    g = pl.program_id(1)
    n = pl.program_id(2)
    blk = ATT_BLOCK
    kx, vx = kx_ref[...], vx_ref[...]
    kband = jnp.concatenate([kp_ref[...], kc_ref[...], kn_ref[...]], axis=0)
    vband = jnp.concatenate([vp_ref[...], vc_ref[...], vn_ref[...]], axis=0)
    nk = 3 * blk + kx.shape[0]
    ones = jnp.ones((nk, LANE), BF16)
    snk = jnp.concatenate([jnp.broadcast_to(sink_ref[pl.ds(g * GQA_GROUP + j, 1), :], (blk, LANE))
                           for j in range(GQA_GROUP)], axis=0)
    rows = GQA_GROUP * blk
    r = lax.broadcasted_iota(jnp.int32, (rows, nk), 0) % blk
    col = lax.broadcasted_iota(jnp.int32, (rows, nk), 1)
    in_window = (jnp.abs(col - blk - r) <= WINDOW) | (col >= 3 * blk)
    for i in range(GQA_QB):
        qs = jnp.concatenate([q_ref[i * blk:(i + 1) * blk, j * LANE:(j + 1) * LANE] for j in range(GQA_GROUP)],
                             axis=0)
        keys = jnp.concatenate([kband[i * blk:(i + 3) * blk, :], kx], axis=0)
        vals = jnp.concatenate([jnp.concatenate([vband[i * blk:(i + 3) * blk, :], vx], axis=0), ones], axis=1)
        block = n * GQA_QB + i
        valid = in_window & ((col >= blk) | (block > 0)) & ((col < 2 * blk) | (col >= 3 * blk) | (block < n_lat - 1))
        sc = jnp.where(valid, _dot_nt(qs, keys), -jnp.inf)
        e = snk
        for t in range(nk // LANE):
            e = jnp.maximum(e, sc[:, t * LANE:(t + 1) * LANE])
        m = jnp.max(e, axis=-1, keepdims=True)
        ol = _dot(jnp.exp2(sc - m).astype(BF16), vals)
        o = ol[:, :LANE] / (ol[:, LANE:] + jnp.exp2(snk - m))
        for j in range(GQA_GROUP):
            o_ref[i * blk:(i + 1) * blk, j * LANE:(j + 1) * LANE] = o[j * blk:(j + 1) * blk, :].astype(BF16)


def _gqa_ctx_kernel(q_ref, kx_ref, vx_ref, sink_ref, y_ref, o_ref):
    del y_ref
    g = pl.program_id(1)
    kx, vx = kx_ref[...], vx_ref[...]
    for j in range(GQA_GROUP):
        sc = _dot_nt(q_ref[:, j * LANE:(j + 1) * LANE], kx)
        snk = _sink_rows(sink_ref, g, j, sc.shape[0])
        m = jnp.maximum(jnp.max(sc, axis=-1, keepdims=True), snk)
        p_c = jnp.exp2(sc - m)
        l = jnp.sum(p_c, axis=-1, keepdims=True) + jnp.exp2(snk - m)
        o_ref[:, j * LANE:(j + 1) * LANE] = (_dot(p_c.astype(BF16), vx) / l).astype(BF16)


def _gqa_attn(gq, gk, px, o_gv, sink, b, s, c, with_ctx):
    mx = b * s
    blk = ATT_BLOCK
    n_lat = s // blk
    nstep = n_lat // GQA_QB
    rows = mx + (b * c if with_ctx else 0)
    ctx0 = mx // c
    gvc = o_gv // LANE
    gw = GQA_GROUP * GQA_HEAD_DIM
    big = GQA_QB * blk

    def edge(shift, col0):
        def idx(bi, g, n):
            return bi * n_lat + jnp.clip(n * GQA_QB + shift, 0, n_lat - 1), col0 + g
        return pl.BlockSpec((blk, LANE), idx)

    main = lambda col0: pl.BlockSpec((big, LANE), lambda bi, g, n: (bi * nstep + n, col0 + g))
    ctx = lambda col0: pl.BlockSpec((c, LANE), lambda bi, g, n: (ctx0 + bi, col0 + g))
    sink_spec = pl.BlockSpec((GQA_HEADS, LANE), lambda *_: (0, 0))
    y = pl.pallas_call(
        functools.partial(_gqa_kernel, n_lat=n_lat),
        grid=(b, GQA_KV_HEADS, nstep),
        in_specs=[pl.BlockSpec((big, gw), lambda bi, g, n: (bi * nstep + n, g)),
                  edge(-1, 0), main(0), edge(GQA_QB, 0), ctx(0),
                  edge(-1, gvc), main(gvc), edge(GQA_QB, gvc), ctx(gvc), sink_spec],
        out_specs=pl.BlockSpec((big, gw), lambda bi, g, n: (bi * nstep + n, g)),
        out_shape=jax.ShapeDtypeStruct((rows, GQA_HEADS * GQA_HEAD_DIM), BF16),
        compiler_params=_params(("parallel", "parallel", "arbitrary")),
        name="gqa_attn",
    )(gq, gk, gk, gk, gk, px, px, px, px, sink)
    if not with_ctx:
        return y
    return pl.pallas_call(
        _gqa_ctx_kernel,
        grid=(b, GQA_KV_HEADS),
        in_specs=[pl.BlockSpec((c, gw), lambda bi, g: (ctx0 + bi, g)),
                  pl.BlockSpec((c, LANE), lambda bi, g: (ctx0 + bi, g)),
                  pl.BlockSpec((c, LANE), lambda bi, g: (ctx0 + bi, gvc + g)),
                  sink_spec, pl.BlockSpec(memory_space=pl.ANY)],
        out_specs=pl.BlockSpec((c, gw), lambda bi, g: (ctx0 + bi, g)),
        out_shape=jax.ShapeDtypeStruct(y.shape, BF16),
        input_output_aliases={4: 0},
        compiler_params=_params(("parallel", "parallel")),
        name="gqa_attn_ctx",
    )(gq, gk, px, sink, y)


def _merge_kernel(yc_ref, ym_ref, yg_ref, gc_ref, gm_ref, gg_ref, wc_ref, wm_ref, wg_ref, o_ref):
    gate = lambda ref: jax.nn.sigmoid(ref[...].astype(F32))
    acc = gate(gc_ref) * _dot(yc_ref[...], wc_ref[...])
    acc += gate(gm_ref) * _dot(ym_ref[...], wm_ref[...])
    acc += gate(gg_ref) * _dot(yg_ref[...], wg_ref[...])
    o_ref[...] = acc.astype(BF16)


def _merge(yc, ym, yg, px, o_gate, w_bc, w_bm, w_bg, l, rows, s):
    d = w_bc.shape[2]
    bm = _pick(1024, s, rows)
    bn = _pick(512, d, o_gate)
    g0 = o_gate // bn
    per = d // bn
    yspec = lambda a: pl.BlockSpec((bm, a.shape[1]), lambda i, j: (i, 0))
    gspec = lambda k: pl.BlockSpec((bm, bn), lambda i, j: (i, g0 + k * per + j))
    wspec = lambda w: pl.BlockSpec((None, w.shape[1], bn), lambda i, j: (l, 0, j))
    return pl.pallas_call(
        _merge_kernel,
        grid=(rows // bm, d // bn),
        in_specs=[yspec(yc), yspec(ym), yspec(yg), gspec(0), gspec(1), gspec(2),
                  wspec(w_bc), wspec(w_bm), wspec(w_bg)],
        out_specs=pl.BlockSpec((bm, bn), lambda i, j: (i, j)),
        out_shape=jax.ShapeDtypeStruct((rows, d), BF16),
        compiler_params=_params(("parallel", "arbitrary")),
        name="branch_merge",
    )(yc, ym, yg, px, px, px, w_bc, w_bm, w_bg)


def _out_kernel(m_ref, w_ref, h_ref, mod_ref, o_ref):
    o_ref[...] = h_ref[...] + mod_ref[0, 5:6, :] * _dot(m_ref[...], w_ref[...])


def _out_proj(merged, w_out, l, h, mods, rows, s, b):
    d = w_out.shape[2]
    bm = _pick(1024, s, rows)
    bn = _pick(512, d)
    per = s // bm
    return pl.pallas_call(
        _out_kernel,
        grid=(rows // bm, d // bn),
        in_specs=[pl.BlockSpec((bm, d), lambda i, j: (i, 0)),
                  pl.BlockSpec((None, d, bn), lambda i, j: (l, 0, j)),
                  pl.BlockSpec((bm, bn), lambda i, j: (i, j)),
                  pl.BlockSpec((1, N_MOD, bn), lambda i, j: (jnp.minimum(i // per, b), 0, j))],
        out_specs=pl.BlockSpec((bm, bn), lambda i, j: (i, j)),
        out_shape=jax.ShapeDtypeStruct((rows, d), F32),
        compiler_params=_params(("parallel", "arbitrary")),
        name="out_proj",
    )(merged, w_out, h, mods)


def _rope_table(s, n_ctx_rows, dim):
    t = jnp.arange(s, dtype=jnp.int32)
    pos = jnp.stack([t // GRID_W, t % GRID_W], axis=1).astype(F32)
    inv = ROPE_BASE ** (-jnp.arange(0, dim, 2, dtype=F32) / dim)
    lane = jnp.arange(LANE)
    ang = pos[:, jnp.minimum(lane // dim, 1)] * inv[lane % (dim // 2)][None, :]
    active = (lane < 2 * dim)[None, :]
    sign = jnp.where((lane % dim) < dim // 2, -1.0, 1.0)[None, :]
    cos = jnp.where(active, jnp.cos(ang), 1.0)
    sin = jnp.where(active, jnp.sin(ang) * sign, 0.0)
    pad = ((0, n_ctx_rows), (0, 0))
    return jnp.pad(cos, pad, constant_values=1.0), jnp.pad(sin, pad)


def _layout(cw, ql, kvl, d):
    lay = {"conv": 0}
    off = 3 * cw
    for name, width in (("gq", GQA_HEADS * GQA_HEAD_DIM), ("cq", ql), ("ckv", kvl),
                        ("gk", GQA_KV_HEADS * GQA_HEAD_DIM), ("gv", GQA_KV_HEADS * GQA_HEAD_DIM),
                        ("kr", LANE)):
        lay[name] = off
        off += width
    off = -(-off // 512) * 512
    lay["gate"] = off
    lay["total"] = off + 3 * d
    return lay


def _pack_w_in(w, lay, cw, ql, kvl):
    d = w.shape[0]
    o_mla = 3 * cw
    o_gqa = o_mla + ql + kvl + MLA_ROPE
    gqw, gkw = GQA_HEADS * GQA_HEAD_DIM, GQA_KV_HEADS * GQA_HEAD_DIM
    o_gate = o_gqa + gqw + 2 * gkw
    parts = [w[:, :o_mla], w[:, o_gqa:o_gqa + gqw], w[:, o_mla:o_mla + ql + kvl],
             w[:, o_gqa + gqw:o_gate], w[:, o_mla + ql + kvl:o_gqa]]
    used = o_gate
    parts.append(jnp.zeros((d, lay["gate"] - used), w.dtype))
    parts.append(w[:, o_gate:])
    return jnp.concatenate(parts, axis=1).astype(BF16)


def kernel(x, c, ctx, c_ctx, ada_w, ada_b, ffn1_norm, ffn1_w_gu, ffn1_w_down, mix_norm, w_in, conv_w,
           mla_q_norm, mla_w_qb, mla_kv_norm, mla_w_kvb, gqa_sink, w_branch_conv, w_branch_mla,
           w_branch_gqa, w_out, ffn2_norm, ffn2_w_gu, ffn2_w_down, final_norm):
    b, s, d = x.shape
    cl = ctx.shape[1]
    depth = ada_w.shape[0]
    cw = conv_w.shape[-1]
    ql, kvl = mla_q_norm.shape[-1], mla_kv_norm.shape[-1]
    mx, mc = b * s, b * cl
    assert b + 1 <= 8 and s % cl == 0 and cl % ATT_BLOCK == 0

    cvec = jnp.zeros((8, d), F32).at[:b].set(c).at[b].set(c_ctx)
    mods = _mods(cvec, ada_w, ada_b).reshape(depth, 8, N_MOD, d)

    lay = _layout(cw, ql, kvl, d)
    tabs = _rope_table(s, mc, MLA_ROPE // 2) + _rope_table(s, mc, GQA_HEAD_DIM // 2)
    h = jnp.concatenate([x.reshape(mx, d), ctx.reshape(mc, d)], axis=0)
    w1_gu, w1_dn = _to_bf16(ffn1_w_gu), _to_bf16(ffn1_w_down)
    w2_gu, w2_dn = _to_bf16(ffn2_w_gu), _to_bf16(ffn2_w_down)
    w_bc, w_bm, w_bg = _to_bf16(w_branch_conv), _to_bf16(w_branch_mla), _to_bf16(w_branch_gqa)
    w_o = _to_bf16(w_out)

    for l in range(depth):
        with_ctx = l < depth - 1
        rows = mx + mc if with_ctx else mx
        w_p = _pack_w_in(w_in[l], lay, cw, ql, kvl)
        wqb_p = jnp.pad(mla_w_qb[l].reshape(ql, MLA_HEADS, MLA_NOPE + MLA_ROPE),
                        ((0, 0), (0, 0), (0, MLA_QK - MLA_NOPE - MLA_ROPE))).reshape(ql, -1).astype(BF16)
        wkv = mla_w_kvb[l].reshape(kvl, MLA_HEADS, MLA_NOPE + MLA_V)
        wkvb_r = jnp.concatenate([wkv[:, :, :MLA_NOPE].reshape(kvl, -1), wkv[:, :, MLA_NOPE:].reshape(kvl, -1)],
                                 axis=1).astype(BF16)
        sink = jnp.broadcast_to((gqa_sink[l].astype(F32) * LOG2E)[:, None], (GQA_HEADS, LANE))

        h = _ffn(h, mx + mc, mods[l], ffn1_norm[l], w1_gu, w1_dn, l, 0, s, b, final_norm, False)
        px = _proj(h, mods[l], mix_norm[l], w_p, s, b)
        y_conv = _conv(px, conv_w[l], rows, s, cl, mx)
        q, k, v, gq, gk = _prep(px, lay, mla_q_norm[l], mla_kv_norm[l], wqb_p, wkvb_r, tabs, s, mx)
        y_mla = _mla_attn(q, k, v, b, s, cl, with_ctx)
        y_gqa = _gqa_attn(gq, gk, px, lay["gv"], sink, b, s, cl, with_ctx)
        merged = _merge(y_conv, y_mla, y_gqa, px, lay["gate"], w_bc, w_bm, w_bg, l, rows, s)
        h = _out_proj(merged, w_o, l, h, mods[l], rows, s, b)
        h = _ffn(h, rows, mods[l], ffn2_norm[l], w2_gu, w2_dn, l, 6, s, b, final_norm, not with_ctx)
    return h.reshape(b, s, d)
```

```python
import functools

import jax
import jax.numpy as jnp
from jax import lax
from jax.experimental import pallas as pl
from jax.experimental.pallas import tpu as pltpu

F32 = jnp.float32
BF16 = jnp.bfloat16

GRID_W = 64
N_MOD = 9
EPS = 1e-6
ROPE_BASE = 10000.0
CONV_K = 3
MLA_HEADS = 8
MLA_NOPE = 128
MLA_ROPE = 64
MLA_V = 128
MLA_SCALE = (MLA_NOPE + MLA_ROPE) ** -0.5
GQA_HEADS = 8
GQA_KV_HEADS = 2
GQA_GROUP = GQA_HEADS // GQA_KV_HEADS
GQA_HEAD_DIM = 128
GQA_SCALE = GQA_HEAD_DIM ** -0.5
WINDOW = 128
ATT_BLOCK = 128

LANE = 128
MLA_QK = 2 * LANE
MLA_SUB = 256
LOG2E = 1.4426950408889634
VMEM_LIMIT = 56 << 20
NT_DIMS = (((1,), (1,)), ((), ()))


def _pick(target, *sizes):
    b = target
    while any(s % b for s in sizes):
        b //= 2
        assert b >= 8, (target, sizes)
    return b


def _params(sem):
    return pltpu.CompilerParams(dimension_semantics=sem, vmem_limit_bytes=VMEM_LIMIT)


def _dot(a, b):
    return jnp.dot(a, b, preferred_element_type=F32)


def _dot_nt(a, b):
    return lax.dot_general(a, b, NT_DIMS, preferred_element_type=F32)


def _rms(x, g):
    return x * lax.rsqrt(jnp.mean(x * x, axis=-1, keepdims=True) + EPS) * g


CAST_BLOCK_BYTES = 6 << 20


def _cast_kernel(w_ref, o_ref):
    o_ref[...] = w_ref[...].astype(BF16)


def _to_bf16(w):
    depth, k, n = w.shape
    rows = depth * k
    rb = _pick(max(16, 1 << ((CAST_BLOCK_BYTES // (4 * n)).bit_length() - 1)), rows)
    out = pl.pallas_call(
        _cast_kernel,
        grid=(rows // rb,),
        in_specs=[pl.BlockSpec((rb, n), lambda i: (i, 0))],
        out_specs=pl.BlockSpec((rb, n), lambda i: (i, 0)),
        out_shape=jax.ShapeDtypeStruct((rows, n), BF16),
        compiler_params=_params(("parallel",)),
        name="cast_bf16",
    )(w.reshape(rows, n))
    return out.reshape(depth, k, n)


def _mods_kernel(c_ref, w_ref, b_ref, o_ref):
    c = c_ref[...]
    s = (c * jax.nn.sigmoid(c)).astype(BF16)
    o_ref[0] = _dot(s, w_ref[0].astype(BF16)) + b_ref[0]


def _mods(cvec, ada_w, ada_b):
    depth, d, n = ada_w.shape
    bn = _pick(1024, n)
    return pl.pallas_call(
        _mods_kernel,
        grid=(depth, n // bn),
        in_specs=[pl.BlockSpec((8, d), lambda l, j: (0, 0)),
                  pl.BlockSpec((1, d, bn), lambda l, j: (l, 0, j)),
                  pl.BlockSpec((1, 1, bn), lambda l, j: (l, 0, j))],
        out_specs=pl.BlockSpec((1, 8, bn), lambda l, j: (l, 0, j)),
        out_shape=jax.ShapeDtypeStruct((depth, 8, n), F32),
        compiler_params=_params(("parallel", "parallel")),
        name="adaln_mods",
    )(cvec, ada_w, ada_b.reshape(depth, 1, n))


def _norm_mod_rows(h_ref, mod_ref, g_ref, xn_ref, copy_ref, i_shift, i_scale, rc):
    shift = mod_ref[0, i_shift:i_shift + 1, :]
    scale1 = 1.0 + mod_ref[0, i_scale:i_scale + 1, :]
    g = g_ref[...]

    def body(r, carry):
        rows = pl.ds(pl.multiple_of(r * rc, rc), rc)
        x = h_ref[rows, :]
        xn_ref[rows, :] = (_rms(x, g) * scale1 + shift).astype(BF16)
        if copy_ref is not None:
            copy_ref[rows, :] = x
        return carry

    lax.fori_loop(0, h_ref.shape[0] // rc, body, 0)


def _mod_spec(d, bm, s, b):
    per = s // bm
    return pl.BlockSpec((1, N_MOD, d), lambda i, j: (jnp.minimum(i // per, b), 0, 0))


def _ffn_kernel(h_ref, hc_ref, mod_ref, g_ref, wg_ref, wu_ref, wd_ref, fg_ref, o_ref, xn_ref,
                *, i0, rc, final, n_lat):
    first = pl.program_id(1) == 0
    if n_lat is None:
        @pl.when(first)
        def _():
            _norm_mod_rows(h_ref, mod_ref, g_ref, xn_ref, o_ref, i0, i0 + 1, rc)
    else:
        is_lat = pl.program_id(0) < n_lat

        @pl.when(first & is_lat)
        def _():
            _norm_mod_rows(h_ref, mod_ref, g_ref, xn_ref, o_ref, i0, i0 + 1, rc)

        @pl.when(first & jnp.logical_not(is_lat))
        def _():
            _norm_mod_rows(hc_ref, mod_ref, g_ref, xn_ref, o_ref, i0, i0 + 1, rc)

    xn = xn_ref[...]
    gg = _dot(xn, wg_ref[...])
    uu = _dot(xn, wu_ref[...])
    a = (gg * jax.nn.sigmoid(gg) * uu).astype(BF16)
    gate = 0.5 * mod_ref[0, i0 + 2:i0 + 3, :]
    o_ref[...] += gate * _dot(a, wd_ref[...])

    if final:
        @pl.when(pl.program_id(1) == pl.num_programs(1) - 1)
        def _():
            fg = fg_ref[...]

            def body(r, carry):
                rows = pl.ds(pl.multiple_of(r * rc, rc), rc)
                o_ref[rows, :] = _rms(o_ref[rows, :], fg)
                return carry

            lax.fori_loop(0, o_ref.shape[0] // rc, body, 0)


def _ffn(h, hc, rows, mods, norm_g, w_gu, w_down, l, i0, s, b, final_g, final):
    d = h.shape[1]
    f = w_down.shape[1]
    bm = _pick(512, s, rows) if hc is None else _pick(512, s, hc.shape[0])
    bf = _pick(512, f)
    nf = f // bf
    if hc is None:
        n_lat = None
        h_specs = [pl.BlockSpec((bm, d), lambda i, j: (i, 0)), pl.BlockSpec((1, d), lambda i, j: (0, 0))]
        hc = norm_g.reshape(1, d)
    else:
        n_lat = h.shape[0] // bm
        h_specs = [pl.BlockSpec((bm, d), lambda i, j: (jnp.minimum(i, n_lat - 1), 0)),
                   pl.BlockSpec((bm, d), lambda i, j: (jnp.maximum(i - n_lat, 0), 0))]
    return pl.pallas_call(
        functools.partial(_ffn_kernel, i0=i0, rc=_pick(128, bm), final=final, n_lat=n_lat),
        grid=(rows // bm, nf),
        in_specs=h_specs + [
                  _mod_spec(d, bm, s, b),
                  pl.BlockSpec((1, d), lambda i, j: (0, 0)),
                  pl.BlockSpec((None, d, bf), lambda i, j: (l, 0, j)),
                  pl.BlockSpec((None, d, bf), lambda i, j: (l, 0, j + nf)),
                  pl.BlockSpec((None, bf, d), lambda i, j: (l, j, 0)),
                  pl.BlockSpec((1, d), lambda i, j: (0, 0))],
        out_specs=pl.BlockSpec((bm, d), lambda i, j: (i, 0)),
        out_shape=jax.ShapeDtypeStruct((rows, d), F32),
        scratch_shapes=[pltpu.VMEM((bm, d), BF16)],
        compiler_params=_params(("parallel", "arbitrary")),
        name="ffn_swiglu",
    )(h, hc, mods, norm_g.reshape(1, d), w_gu, w_gu, w_down, final_g.reshape(1, d))


def _proj_kernel(h_ref, mod_ref, g_ref, w_ref, o_ref, xn_ref, *, rc):
    @pl.when(pl.program_id(1) == 0)
    def _():
        _norm_mod_rows(h_ref, mod_ref, g_ref, xn_ref, None, 3, 4, rc)

    o_ref[...] = _dot(xn_ref[...], w_ref[...]).astype(BF16)


def _proj(h, mods, norm_g, w_p, l, s, b):
    rows, d = h.shape
    n = w_p.shape[2]
    bm = _pick(1024, s, rows)
    bn = _pick(512, n)
    return pl.pallas_call(
        functools.partial(_proj_kernel, rc=_pick(128, bm)),
        grid=(rows // bm, n // bn),
        in_specs=[pl.BlockSpec((bm, d), lambda i, j: (i, 0)),
                  _mod_spec(d, bm, s, b),
                  pl.BlockSpec((1, d), lambda i, j: (0, 0)),
                  pl.BlockSpec((None, d, bn), lambda i, j: (l, 0, j))],
        out_specs=pl.BlockSpec((bm, bn), lambda i, j: (i, j)),
        out_shape=jax.ShapeDtypeStruct((rows, n), BF16),
        scratch_shapes=[pltpu.VMEM((bm, d), BF16)],
        compiler_params=_params(("parallel", "arbitrary")),
        name="in_proj",
    )(h, mods, norm_g.reshape(1, d), w_p)


def _conv_kernel(gb_ref, gc_ref, v_ref, gcp_ref, vp_ref, gcn_ref, vn_ref, w_ref, o_ref, *, bm, s, c, mx):
    row0 = pl.program_id(0) * bm
    is_lat = row0 < mx
    at_start = jnp.where(is_lat, row0 % s == 0, (row0 - mx) % c == 0)
    at_end = jnp.where(is_lat, (row0 + bm) % s == 0, (row0 - mx + bm) % c == 0)
    cv = gc_ref[...].astype(F32) * v_ref[...].astype(F32)
    hp = (gcp_ref[...].astype(F32) * vp_ref[...].astype(F32))[15:16, :]
    hn = (gcn_ref[...].astype(F32) * vn_ref[...].astype(F32))[0:1, :]
    hp = jnp.where(at_start, 0.0, hp)
    hn = jnp.where(at_end, 0.0, hn)
    rid = lax.broadcasted_iota(jnp.int32, (bm, 1), 0)
    prev = jnp.where(rid == 0, hp, pltpu.roll(cv, 1, 0))
    nxt = jnp.where(rid == bm - 1, hn, pltpu.roll(cv, bm - 1, 0))
    w = w_ref[...]
    y = gb_ref[...].astype(F32) * (prev * w[0:1, :] + cv * w[1:2, :] + nxt * w[2:3, :])
    o_ref[...] = y.astype(BF16)


def _conv(px, conv_w, rows, s, c, mx):
    cw = conv_w.shape[1]
    m = px.shape[0]
    bm = _pick(256, s, c)
    hb = bm // 16
    main = lambda col: pl.BlockSpec((bm, cw), lambda i: (i, col))
    prev = lambda col: pl.BlockSpec((16, cw), lambda i: (jnp.maximum(i * hb - 1, 0), col))
    nxt = lambda col: pl.BlockSpec((16, cw), lambda i: (jnp.minimum((i + 1) * hb, m // 16 - 1), col))
    return pl.pallas_call(
        functools.partial(_conv_kernel, bm=bm, s=s, c=c, mx=mx),
        grid=(rows // bm,),
        in_specs=[main(0), main(1), main(2), prev(1), prev(2), nxt(1), nxt(2),
                  pl.BlockSpec((CONV_K, cw), lambda i: (0, 0))],
        out_specs=pl.BlockSpec((bm, cw), lambda i: (i, 0)),
        out_shape=jax.ShapeDtypeStruct((rows, cw), BF16),
        compiler_params=_params(("parallel",)),
        name="gated_conv",
    )(px, px, px, px, px, px, px, conv_w)


def _rope(x, cos, sin, half):
    lane = lax.broadcasted_iota(jnp.int32, x.shape, 1)
    first = (lane % (2 * half)) < half
    rot = jnp.where(first, pltpu.roll(x, LANE - half, 1), pltpu.roll(x, half, 1))
    return x * cos + rot * sin


def _prep_kernel(gq_ref, cq_ref, ckv_ref, gk_ref, kr_ref, qn_ref, kvn_ref, wqb_ref, wkvb_ref,
                 mcos_ref, msin_ref, gcos_ref, gsin_ref, q_ref, k_ref, v_ref, gqo_ref, gko_ref):
    mcos, msin = mcos_ref[...], msin_ref[...]
    gcos, gsin = gcos_ref[...], gsin_ref[...]
    mh = MLA_ROPE // 4
    gh = GQA_HEAD_DIM // 4

    cqn = _rms(cq_ref[...].astype(F32), qn_ref[...]).astype(BF16)
    q = _dot(cqn, wqb_ref[...]) * (MLA_SCALE * LOG2E)
    ckvn = _rms(ckv_ref[...].astype(F32), kvn_ref[...]).astype(BF16)
    kv = _dot(ckvn, wkvb_ref[...])
    kr = _rope(kr_ref[...].astype(F32), mcos, msin, mh).astype(BF16)
    ones = jnp.ones((q.shape[0], LANE), BF16)
    v0 = MLA_HEADS * MLA_NOPE
    for h in range(MLA_HEADS):
        a = h * MLA_QK
        q_ref[:, a:a + LANE] = q[:, a:a + LANE].astype(BF16)
        q_ref[:, a + LANE:a + MLA_QK] = _rope(q[:, a + LANE:a + MLA_QK], mcos, msin, mh).astype(BF16)
        k_ref[:, a:a + LANE] = kv[:, h * LANE:(h + 1) * LANE].astype(BF16)
        k_ref[:, a + LANE:a + MLA_QK] = kr
        v_ref[:, a:a + LANE] = kv[:, v0 + h * LANE:v0 + (h + 1) * LANE].astype(BF16)
        v_ref[:, a + LANE:a + MLA_QK] = ones

    gq = gq_ref[...].astype(F32)
    for h in range(GQA_HEADS):
        a = h * GQA_HEAD_DIM
        gqo_ref[:, a:a + LANE] = (_rope(gq[:, a:a + LANE], gcos, gsin, gh) * (GQA_SCALE * LOG2E)).astype(BF16)
    gk = gk_ref[...].astype(F32)
    for h in range(GQA_KV_HEADS):
        a = h * GQA_HEAD_DIM
        gko_ref[:, a:a + LANE] = _rope(gk[:, a:a + LANE], gcos, gsin, gh).astype(BF16)


def _prep(px, lay, q_norm, kv_norm, wqb_p, wkvb_r, tabs, s, mx):
    m = px.shape[0]
    bm = _pick(512, s, m - mx)
    n_lat = mx // bm
    per = s // bm
    ql, kvl = q_norm.shape[0], kv_norm.shape[0]
    gqw, gkw = GQA_HEADS * GQA_HEAD_DIM, GQA_KV_HEADS * GQA_HEAD_DIM

    def col(width, off):
        assert off % width == 0, (width, off)
        return pl.BlockSpec((bm, width), lambda i: (i, off // width))

    const = lambda r, c: pl.BlockSpec((r, c), lambda i: (0, 0))
    tab = pl.BlockSpec((bm, LANE), lambda i: (jnp.where(i < n_lat, i % per, per + i - n_lat), 0))
    row = lambda width: pl.BlockSpec((bm, width), lambda i: (i, 0))
    hq = MLA_HEADS * MLA_QK
    return pl.pallas_call(
        _prep_kernel,
        grid=(m // bm,),
        in_specs=[col(gqw, lay["gq"]), col(ql, lay["cq"]), col(kvl, lay["ckv"]), col(gkw, lay["gk"]),
                  col(LANE, lay["kr"]), const(1, ql), const(1, kvl), const(ql, hq),
                  const(kvl, MLA_HEADS * (MLA_NOPE + MLA_V)), tab, tab, tab, tab],
        out_specs=[row(hq), row(hq), row(hq), row(gqw), row(gkw)],
        out_shape=[jax.ShapeDtypeStruct((m, hq), BF16), jax.ShapeDtypeStruct((m, hq), BF16),
                   jax.ShapeDtypeStruct((m, hq), BF16),
                   jax.ShapeDtypeStruct((m, gqw), BF16), jax.ShapeDtypeStruct((m, gkw), BF16)],
        compiler_params=_params(("parallel",)),
        name="attn_prep",
    )(px, px, px, px, px, q_norm.reshape(1, ql), kv_norm.reshape(1, kvl), wqb_p, wkvb_r, *tabs)


def _mla_kernel(q_ref, kl_ref, kc_ref, vl_ref, vc_ref, o_ref, *, n_sub):
    kl, kc = kl_ref[...], kc_ref[...]
    subs = [q_ref[i * MLA_SUB:(i + 1) * MLA_SUB, :] for i in range(n_sub)]
    scores = [(_dot_nt(q, kl), _dot_nt(q, kc)) for q in subs]
    for i, (s1, s2) in enumerate(scores):
        m = jnp.maximum(jnp.max(s1, axis=-1, keepdims=True), jnp.max(s2, axis=-1, keepdims=True))
        p1 = jnp.exp2(s1 - m).astype(BF16)
        p2 = jnp.exp2(s2 - m).astype(BF16)
        o = _dot(p1, vl_ref[...]) + _dot(p2, vc_ref[...])
        o_ref[i * MLA_SUB:(i + 1) * MLA_SUB, :] = (o[:, :MLA_V] / o[:, MLA_V:]).astype(BF16)


def _mla_ctx_kernel(q_ref, kc_ref, vc_ref, y_ref, o_ref):
    del y_ref
    s2 = _dot_nt(q_ref[...], kc_ref[...])
    p2 = jnp.exp2(s2 - jnp.max(s2, axis=-1, keepdims=True)).astype(BF16)
    o = _dot(p2, vc_ref[...])
    o_ref[...] = (o[:, :MLA_V] / o[:, MLA_V:]).astype(BF16)


def _mla_attn(q, k, v, b, s, c, with_ctx):
    mx = b * s
    n_sub = max(d for d in (1, 2, 4) if (s // MLA_SUB) % d == 0)
    bq = n_sub * MLA_SUB
    nq = s // bq
    rows = mx + (b * c if with_ctx else 0)
    ctx0 = mx // c
    y = pl.pallas_call(
        functools.partial(_mla_kernel, n_sub=n_sub),
        grid=(b, MLA_HEADS, nq),
        in_specs=[pl.BlockSpec((bq, MLA_QK), lambda bi, h, qi: (bi * nq + qi, h)),
                  pl.BlockSpec((s, MLA_QK), lambda bi, h, qi: (bi, h)),
                  pl.BlockSpec((c, MLA_QK), lambda bi, h, qi: (ctx0 + bi, h)),
                  pl.BlockSpec((s, MLA_QK), lambda bi, h, qi: (bi, h)),
                  pl.BlockSpec((c, MLA_QK), lambda bi, h, qi: (ctx0 + bi, h))],
        out_specs=pl.BlockSpec((bq, MLA_V), lambda bi, h, qi: (bi * nq + qi, h)),
        out_shape=jax.ShapeDtypeStruct((rows, MLA_HEADS * MLA_V), BF16),
        compiler_params=_params(("parallel", "parallel", "arbitrary")),
        name="mla_attn",
    )(q, k, k, v, v)
    if not with_ctx:
        return y
    return pl.pallas_call(
        _mla_ctx_kernel,
        grid=(b, MLA_HEADS),
        in_specs=[pl.BlockSpec((c, MLA_QK), lambda bi, h: (ctx0 + bi, h)),
                  pl.BlockSpec((c, MLA_QK), lambda bi, h: (ctx0 + bi, h)),
                  pl.BlockSpec((c, MLA_QK), lambda bi, h: (ctx0 + bi, h)),
                  pl.BlockSpec(memory_space=pl.ANY)],
        out_specs=pl.BlockSpec((c, MLA_V), lambda bi, h: (ctx0 + bi, h)),
        out_shape=jax.ShapeDtypeStruct(y.shape, BF16),
        input_output_aliases={3: 0},
        compiler_params=_params(("parallel", "parallel")),
        name="mla_attn_ctx",
    )(q, k, v, y)


GQA_QB = 4


def _sink_rows(sink_ref, g, j, rows):
    return jnp.broadcast_to(sink_ref[pl.ds(g * GQA_GROUP + j, 1), :], (rows, LANE))[:, 0:1]


def _gqa_kernel(q_ref, kp_ref, kc_ref, kn_ref, kx_ref, vp_ref, vc_ref, vn_ref, vx_ref, sink_ref, o_ref,
                *, n_lat):
    g = pl.program_id(1)
    n = pl.program_id(2)
    blk = ATT_BLOCK
    kx, vx = kx_ref[...], vx_ref[...]
    kband = jnp.concatenate([kp_ref[...], kc_ref[...], kn_ref[...]], axis=0)
    vband = jnp.concatenate([vp_ref[...], vc_ref[...], vn_ref[...]], axis=0)
    nk = 3 * blk + kx.shape[0]
    ones = jnp.ones((nk, LANE), BF16)
    snk = jnp.concatenate([jnp.broadcast_to(sink_ref[pl.ds(g * GQA_GROUP + j, 1), :], (blk, LANE))
                           for j in range(GQA_GROUP)], axis=0)
    rows = GQA_GROUP * blk
    r = lax.broadcasted_iota(jnp.int32, (rows, nk), 0) % blk
    col = lax.broadcasted_iota(jnp.int32, (rows, nk), 1)
    in_window = (jnp.abs(col - blk - r) <= WINDOW) | (col >= 3 * blk)
    for i in range(GQA_QB):
        qs = jnp.concatenate([q_ref[i * blk:(i + 1) * blk, j * LANE:(j + 1) * LANE] for j in range(GQA_GROUP)],
                             axis=0)
        keys = jnp.concatenate([kband[i * blk:(i + 3) * blk, :], kx], axis=0)
        vals = jnp.concatenate([jnp.concatenate([vband[i * blk:(i + 3) * blk, :], vx], axis=0), ones], axis=1)
        block = n * GQA_QB + i
        valid = in_window & ((col >= blk) | (block > 0)) & ((col < 2 * blk) | (col >= 3 * blk) | (block < n_lat - 1))
        sc = jnp.where(valid, _dot_nt(qs, keys), -jnp.inf)
        e = snk
        for t in range(nk // LANE):
            e = jnp.maximum(e, sc[:, t * LANE:(t + 1) * LANE])
        m = jnp.max(e, axis=-1, keepdims=True)
        ol = _dot(jnp.exp2(sc - m).astype(BF16), vals)
        o = ol[:, :LANE] / (ol[:, LANE:] + jnp.exp2(snk - m))
        for j in range(GQA_GROUP):
            o_ref[i * blk:(i + 1) * blk, j * LANE:(j + 1) * LANE] = o[j * blk:(j + 1) * blk, :].astype(BF16)


def _gqa_ctx_kernel(q_ref, kx_ref, vx_ref, sink_ref, y_ref, o_ref):
    del y_ref
    g = pl.program_id(1)
    kx, vx = kx_ref[...], vx_ref[...]
    for j in range(GQA_GROUP):
        sc = _dot_nt(q_ref[:, j * LANE:(j + 1) * LANE], kx)
        snk = _sink_rows(sink_ref, g, j, sc.shape[0])
        m = jnp.maximum(jnp.max(sc, axis=-1, keepdims=True), snk)
        p_c = jnp.exp2(sc - m)
        l = jnp.sum(p_c, axis=-1, keepdims=True) + jnp.exp2(snk - m)
        o_ref[:, j * LANE:(j + 1) * LANE] = (_dot(p_c.astype(BF16), vx) / l).astype(BF16)


def _gqa_attn(gq, gk, px, o_gv, sink, b, s, c, with_ctx):
    mx = b * s
    blk = ATT_BLOCK
    n_lat = s // blk
    nstep = n_lat // GQA_QB
    rows = mx + (b * c if with_ctx else 0)
    ctx0 = mx // c
    gvc = o_gv // LANE
    gw = GQA_GROUP * GQA_HEAD_DIM
    big = GQA_QB * blk

    def edge(shift, col0):
        def idx(bi, g, n):
            return bi * n_lat + jnp.clip(n * GQA_QB + shift, 0, n_lat - 1), col0 + g
        return pl.BlockSpec((blk, LANE), idx)

    main = lambda col0: pl.BlockSpec((big, LANE), lambda bi, g, n: (bi * nstep + n, col0 + g))
    ctx = lambda col0: pl.BlockSpec((c, LANE), lambda bi, g, n: (ctx0 + bi, col0 + g))
    sink_spec = pl.BlockSpec((GQA_HEADS, LANE), lambda *_: (0, 0))
    y = pl.pallas_call(
        functools.partial(_gqa_kernel, n_lat=n_lat),
        grid=(b, GQA_KV_HEADS, nstep),
        in_specs=[pl.BlockSpec((big, gw), lambda bi, g, n: (bi * nstep + n, g)),
                  edge(-1, 0), main(0), edge(GQA_QB, 0), ctx(0),
                  edge(-1, gvc), main(gvc), edge(GQA_QB, gvc), ctx(gvc), sink_spec],
        out_specs=pl.BlockSpec((big, gw), lambda bi, g, n: (bi * nstep + n, g)),
        out_shape=jax.ShapeDtypeStruct((rows, GQA_HEADS * GQA_HEAD_DIM), BF16),
        compiler_params=_params(("parallel", "parallel", "arbitrary")),
        name="gqa_attn",
    )(gq, gk, gk, gk, gk, px, px, px, px, sink)
    if not with_ctx:
        return y
    return pl.pallas_call(
        _gqa_ctx_kernel,
        grid=(b, GQA_KV_HEADS),
        in_specs=[pl.BlockSpec((c, gw), lambda bi, g: (ctx0 + bi, g)),
                  pl.BlockSpec((c, LANE), lambda bi, g: (ctx0 + bi, g)),
                  pl.BlockSpec((c, LANE), lambda bi, g: (ctx0 + bi, gvc + g)),
                  sink_spec, pl.BlockSpec(memory_space=pl.ANY)],
        out_specs=pl.BlockSpec((c, gw), lambda bi, g: (ctx0 + bi, g)),
        out_shape=jax.ShapeDtypeStruct(y.shape, BF16),
        input_output_aliases={4: 0},
        compiler_params=_params(("parallel", "parallel")),
        name="gqa_attn_ctx",
    )(gq, gk, px, sink, y)


def _merge_kernel(yc_ref, ym_ref, yg_ref, gc_ref, gm_ref, gg_ref, wc_ref, wm_ref, wg_ref, o_ref):
    gate = lambda ref: jax.nn.sigmoid(ref[...].astype(F32))
    acc = gate(gc_ref) * _dot(yc_ref[...], wc_ref[...])
    acc += gate(gm_ref) * _dot(ym_ref[...], wm_ref[...])
    acc += gate(gg_ref) * _dot(yg_ref[...], wg_ref[...])
    o_ref[...] = acc.astype(BF16)


def _merge(yc, ym, yg, px, o_gate, w_bc, w_bm, w_bg, l, rows, s):
    d = w_bc.shape[2]
    bm = _pick(1024, s, rows)
    bn = _pick(512, d, o_gate)
    g0 = o_gate // bn
    per = d // bn
    yspec = lambda a: pl.BlockSpec((bm, a.shape[1]), lambda i, j: (i, 0))
    gspec = lambda k: pl.BlockSpec((bm, bn), lambda i, j: (i, g0 + k * per + j))
    wspec = lambda w: pl.BlockSpec((None, w.shape[1], bn), lambda i, j: (l, 0, j))
    return pl.pallas_call(
        _merge_kernel,
        grid=(rows // bm, d // bn),
        in_specs=[yspec(yc), yspec(ym), yspec(yg), gspec(0), gspec(1), gspec(2),
                  wspec(w_bc), wspec(w_bm), wspec(w_bg)],
        out_specs=pl.BlockSpec((bm, bn), lambda i, j: (i, j)),
        out_shape=jax.ShapeDtypeStruct((rows, d), BF16),
        compiler_params=_params(("parallel", "arbitrary")),
        name="branch_merge",
    )(yc, ym, yg, px, px, px, w_bc, w_bm, w_bg)


def _out_kernel(m_ref, w_ref, h_ref, mod_ref, o_ref):
    o_ref[...] = h_ref[...] + mod_ref[0, 5:6, :] * _dot(m_ref[...], w_ref[...])


def _out_proj(merged, w_out, l, h, mods, rows, s, b):
    d = w_out.shape[2]
    bm = _pick(1024, s, rows)
    bn = _pick(512, d)
    per = s // bm
    return pl.pallas_call(
        _out_kernel,
        grid=(rows // bm, d // bn),
        in_specs=[pl.BlockSpec((bm, d), lambda i, j: (i, 0)),
                  pl.BlockSpec((None, d, bn), lambda i, j: (l, 0, j)),
                  pl.BlockSpec((bm, bn), lambda i, j: (i, j)),
                  pl.BlockSpec((1, N_MOD, bn), lambda i, j: (jnp.minimum(i // per, b), 0, j))],
        out_specs=pl.BlockSpec((bm, bn), lambda i, j: (i, j)),
        out_shape=jax.ShapeDtypeStruct((rows, d), F32),
        compiler_params=_params(("parallel", "arbitrary")),
        name="out_proj",
    )(merged, w_out, h, mods)


def _rope_table(s, n_ctx_rows, dim):
    t = jnp.arange(s, dtype=jnp.int32)
    pos = jnp.stack([t // GRID_W, t % GRID_W], axis=1).astype(F32)
    inv = ROPE_BASE ** (-jnp.arange(0, dim, 2, dtype=F32) / dim)
    lane = jnp.arange(LANE)
    ang = pos[:, jnp.minimum(lane // dim, 1)] * inv[lane % (dim // 2)][None, :]
    active = (lane < 2 * dim)[None, :]
    sign = jnp.where((lane % dim) < dim // 2, -1.0, 1.0)[None, :]
    cos = jnp.where(active, jnp.cos(ang), 1.0)
    sin = jnp.where(active, jnp.sin(ang) * sign, 0.0)
    pad = ((0, n_ctx_rows), (0, 0))
    return jnp.pad(cos, pad, constant_values=1.0), jnp.pad(sin, pad)


def _layout(cw, ql, kvl, d):
    lay = {"conv": 0}
    off = 3 * cw
    for name, width in (("gq", GQA_HEADS * GQA_HEAD_DIM), ("cq", ql), ("ckv", kvl),
                        ("gk", GQA_KV_HEADS * GQA_HEAD_DIM), ("gv", GQA_KV_HEADS * GQA_HEAD_DIM),
                        ("kr", LANE)):
        lay[name] = off
        off += width
    off = -(-off // 512) * 512
    lay["gate"] = off
    lay["total"] = off + 3 * d
    return lay


def _pack_kernel(w_ref, o_ref, *, moves, zero):
    for dst, src, width in moves:
        o_ref[:, dst:dst + width] = w_ref[:, src:src + width].astype(BF16)
    o_ref[:, zero[0]:zero[1]] = jnp.zeros((o_ref.shape[0], zero[1] - zero[0]), BF16)


def _pack_w_in(w, lay, cw, ql, kvl):
    depth, d, n = w.shape
    o_mla = 3 * cw
    o_gqa = o_mla + ql + kvl + MLA_ROPE
    gqw, gkw = GQA_HEADS * GQA_HEAD_DIM, GQA_KV_HEADS * GQA_HEAD_DIM
    o_gate = o_gqa + gqw + 2 * gkw
    moves = ((0, 0, o_mla), (lay["gq"], o_gqa, gqw), (lay["cq"], o_mla, ql + kvl),
             (lay["gk"], o_gqa + gqw, 2 * gkw), (lay["kr"], o_mla + ql + kvl, MLA_ROPE),
             (lay["gate"], o_gate, n - o_gate))
    rows = depth * d
    rb = _pick(64, rows)
    out = pl.pallas_call(
        functools.partial(_pack_kernel, moves=moves, zero=(lay["kr"] + MLA_ROPE, lay["gate"])),
        grid=(rows // rb,),
        in_specs=[pl.BlockSpec((rb, n), lambda i: (i, 0))],
        out_specs=pl.BlockSpec((rb, lay["total"]), lambda i: (i, 0)),
        out_shape=jax.ShapeDtypeStruct((rows, lay["total"]), BF16),
        compiler_params=_params(("parallel",)),
        name="pack_w_in",
    )(w.reshape(rows, n))
    return out.reshape(depth, d, lay["total"])


def kernel(x, c, ctx, c_ctx, ada_w, ada_b, ffn1_norm, ffn1_w_gu, ffn1_w_down, mix_norm, w_in, conv_w,
           mla_q_norm, mla_w_qb, mla_kv_norm, mla_w_kvb, gqa_sink, w_branch_conv, w_branch_mla,
           w_branch_gqa, w_out, ffn2_norm, ffn2_w_gu, ffn2_w_down, final_norm):
    b, s, d = x.shape
    cl = ctx.shape[1]
    depth = ada_w.shape[0]
    cw = conv_w.shape[-1]
    ql, kvl = mla_q_norm.shape[-1], mla_kv_norm.shape[-1]
    mx, mc = b * s, b * cl
    assert b + 1 <= 8 and s % cl == 0 and cl % ATT_BLOCK == 0

    cvec = jnp.zeros((8, d), F32).at[:b].set(c).at[b].set(c_ctx)
    mods = _mods(cvec, ada_w, ada_b).reshape(depth, 8, N_MOD, d)

    lay = _layout(cw, ql, kvl, d)
    tabs = _rope_table(s, mc, MLA_ROPE // 2) + _rope_table(s, mc, GQA_HEAD_DIM // 2)
    h = None
    w1_gu, w1_dn = _to_bf16(ffn1_w_gu), _to_bf16(ffn1_w_down)
    w2_gu, w2_dn = _to_bf16(ffn2_w_gu), _to_bf16(ffn2_w_down)
    w_bc, w_bm, w_bg = _to_bf16(w_branch_conv), _to_bf16(w_branch_mla), _to_bf16(w_branch_gqa)
    w_o = _to_bf16(w_out)
    w_p = _pack_w_in(w_in, lay, cw, ql, kvl)

    for l in range(depth):
        with_ctx = l < depth - 1
        rows = mx + mc if with_ctx else mx
        wqb_p = jnp.pad(mla_w_qb[l].reshape(ql, MLA_HEADS, MLA_NOPE + MLA_ROPE),
                        ((0, 0), (0, 0), (0, MLA_QK - MLA_NOPE - MLA_ROPE))).reshape(ql, -1).astype(BF16)
        wkv = mla_w_kvb[l].reshape(kvl, MLA_HEADS, MLA_NOPE + MLA_V)
        wkvb_r = jnp.concatenate([wkv[:, :, :MLA_NOPE].reshape(kvl, -1), wkv[:, :, MLA_NOPE:].reshape(kvl, -1)],
                                 axis=1).astype(BF16)
        sink = jnp.broadcast_to((gqa_sink[l].astype(F32) * LOG2E)[:, None], (GQA_HEADS, LANE))

        if l == 0:
            h = _ffn(x.reshape(mx, d), ctx.reshape(mc, d), mx + mc, mods[l], ffn1_norm[l], w1_gu, w1_dn, l, 0,
                     s, b, final_norm, False)
        else:
            h = _ffn(h, None, mx + mc, mods[l], ffn1_norm[l], w1_gu, w1_dn, l, 0, s, b, final_norm, False)
        px = _proj(h, mods[l], mix_norm[l], w_p, l, s, b)
        y_conv = _conv(px, conv_w[l], rows, s, cl, mx)
        q, k, v, gq, gk = _prep(px, lay, mla_q_norm[l], mla_kv_norm[l], wqb_p, wkvb_r, tabs, s, mx)
        y_mla = _mla_attn(q, k, v, b, s, cl, with_ctx)
        y_gqa = _gqa_attn(gq, gk, px, lay["gv"], sink, b, s, cl, with_ctx)
        merged = _merge(y_conv, y_mla, y_gqa, px, lay["gate"], w_bc, w_bm, w_bg, l, rows, s)
        h = _out_proj(merged, w_o, l, h, mods[l], rows, s, b)
        h = _ffn(h, None, rows, mods[l], ffn2_norm[l], w2_gu, w2_dn, l, 6, s, b, final_norm, not with_ctx)
    return h.reshape(b, s, d)
```

```python
import functools

import jax
import jax.numpy as jnp
from jax import lax
from jax.experimental import pallas as pl
from jax.experimental.pallas import tpu as pltpu

F32 = jnp.float32
BF16 = jnp.bfloat16

GRID_W = 64
N_MOD = 9
EPS = 1e-6
ROPE_BASE = 10000.0
CONV_K = 3
MLA_HEADS = 8
MLA_NOPE = 128
MLA_ROPE = 64
MLA_V = 128
MLA_SCALE = (MLA_NOPE + MLA_ROPE) ** -0.5
GQA_HEADS = 8
GQA_KV_HEADS = 2
GQA_GROUP = GQA_HEADS // GQA_KV_HEADS
GQA_HEAD_DIM = 128
GQA_SCALE = GQA_HEAD_DIM ** -0.5
WINDOW = 128
ATT_BLOCK = 128

LANE = 128
MLA_QK = 2 * LANE
MLA_SUB = 256
LOG2E = 1.4426950408889634
VMEM_LIMIT = 56 << 20
NT_DIMS = (((1,), (1,)), ((), ()))


def _pick(target, *sizes):
    b = target
    while any(s % b for s in sizes):
        b //= 2
        assert b >= 8, (target, sizes)
    return b


def _params(sem):
    return pltpu.CompilerParams(dimension_semantics=sem, vmem_limit_bytes=VMEM_LIMIT)


def _dot(a, b):
    return jnp.dot(a, b, preferred_element_type=F32)


def _dot_nt(a, b):
    return lax.dot_general(a, b, NT_DIMS, preferred_element_type=F32)


def _rms(x, g):
    return x * lax.rsqrt(jnp.mean(x * x, axis=-1, keepdims=True) + EPS) * g


CAST_BLOCK_BYTES = 6 << 20


def _cast_kernel(w_ref, o_ref):
    o_ref[...] = w_ref[...].astype(BF16)


def _to_bf16(w):
    depth, k, n = w.shape
    rows = depth * k
    rb = _pick(max(16, 1 << ((CAST_BLOCK_BYTES // (4 * n)).bit_length() - 1)), rows)
    out = pl.pallas_call(
        _cast_kernel,
        grid=(rows // rb,),
        in_specs=[pl.BlockSpec((rb, n), lambda i: (i, 0))],
        out_specs=pl.BlockSpec((rb, n), lambda i: (i, 0)),
        out_shape=jax.ShapeDtypeStruct((rows, n), BF16),
        compiler_params=_params(("parallel",)),
        name="cast_bf16",
    )(w.reshape(rows, n))
    return out.reshape(depth, k, n)


def _mods_kernel(c_ref, w_ref, b_ref, o_ref):
    c = c_ref[...]
    s = (c * jax.nn.sigmoid(c)).astype(BF16)
    o_ref[0] = _dot(s, w_ref[0].astype(BF16)) + b_ref[0]


def _mods(cvec, ada_w, ada_b):
    depth, d, n = ada_w.shape
    bn = _pick(1024, n)
    return pl.pallas_call(
        _mods_kernel,
        grid=(depth, n // bn),
        in_specs=[pl.BlockSpec((8, d), lambda l, j: (0, 0)),
                  pl.BlockSpec((1, d, bn), lambda l, j: (l, 0, j)),
                  pl.BlockSpec((1, 1, bn), lambda l, j: (l, 0, j))],
        out_specs=pl.BlockSpec((1, 8, bn), lambda l, j: (l, 0, j)),
        out_shape=jax.ShapeDtypeStruct((depth, 8, n), F32),
        compiler_params=_params(("parallel", "parallel")),
        name="adaln_mods",
    )(cvec, ada_w, ada_b.reshape(depth, 1, n))


def _norm_mod_rows(h_ref, mod_ref, g_ref, xn_ref, copy_ref, i_shift, i_scale, rc):
    shift = mod_ref[0, i_shift:i_shift + 1, :]
    scale1 = 1.0 + mod_ref[0, i_scale:i_scale + 1, :]
    g = g_ref[...]

    def body(r, carry):
        rows = pl.ds(pl.multiple_of(r * rc, rc), rc)
        x = h_ref[rows, :]
        xn_ref[rows, :] = (_rms(x, g) * scale1 + shift).astype(BF16)
        if copy_ref is not None:
            copy_ref[rows, :] = x
        return carry

    lax.fori_loop(0, h_ref.shape[0] // rc, body, 0)


def _mod_spec(d, bm, s, b):
    per = s // bm
    return pl.BlockSpec((1, N_MOD, d), lambda i, j: (jnp.minimum(i // per, b), 0, 0))


def _ffn_kernel(h_ref, hc_ref, mod_ref, g_ref, wg_ref, wu_ref, wd_ref, fg_ref, o_ref, xn_ref,
                *, i0, rc, final, n_lat):
    first = pl.program_id(1) == 0
    if n_lat is None:
        @pl.when(first)
        def _():
            _norm_mod_rows(h_ref, mod_ref, g_ref, xn_ref, o_ref, i0, i0 + 1, rc)
    else:
        is_lat = pl.program_id(0) < n_lat

        @pl.when(first & is_lat)
        def _():
            _norm_mod_rows(h_ref, mod_ref, g_ref, xn_ref, o_ref, i0, i0 + 1, rc)

        @pl.when(first & jnp.logical_not(is_lat))
        def _():
            _norm_mod_rows(hc_ref, mod_ref, g_ref, xn_ref, o_ref, i0, i0 + 1, rc)

    xn = xn_ref[...]
    gg = _dot(xn, wg_ref[...])
    uu = _dot(xn, wu_ref[...])
    a = (gg * jax.nn.sigmoid(gg) * uu).astype(BF16)
    gate = 0.5 * mod_ref[0, i0 + 2:i0 + 3, :]
    o_ref[...] += gate * _dot(a, wd_ref[...])

    if final:
        @pl.when(pl.program_id(1) == pl.num_programs(1) - 1)
        def _():
            fg = fg_ref[...]

            def body(r, carry):
                rows = pl.ds(pl.multiple_of(r * rc, rc), rc)
                o_ref[rows, :] = _rms(o_ref[rows, :], fg)
                return carry

            lax.fori_loop(0, o_ref.shape[0] // rc, body, 0)


def _ffn(h, hc, rows, mods, norm_g, w_gu, w_down, l, i0, s, b, final_g, final):
    d = h.shape[1]
    f = w_down.shape[1]
    bm = _pick(512, s, rows) if hc is None else _pick(512, s, hc.shape[0])
    bf = _pick(512, f)
    nf = f // bf
    if hc is None:
        n_lat = None
        h_specs = [pl.BlockSpec((bm, d), lambda i, j: (i, 0)), pl.BlockSpec((1, d), lambda i, j: (0, 0))]
        hc = norm_g.reshape(1, d)
    else:
        n_lat = h.shape[0] // bm
        h_specs = [pl.BlockSpec((bm, d), lambda i, j: (jnp.minimum(i, n_lat - 1), 0)),
                   pl.BlockSpec((bm, d), lambda i, j: (jnp.maximum(i - n_lat, 0), 0))]
    return pl.pallas_call(
        functools.partial(_ffn_kernel, i0=i0, rc=_pick(128, bm), final=final, n_lat=n_lat),
        grid=(rows // bm, nf),
        in_specs=h_specs + [
                  _mod_spec(d, bm, s, b),
                  pl.BlockSpec((1, d), lambda i, j: (0, 0)),
                  pl.BlockSpec((None, d, bf), lambda i, j: (l, 0, j)),
                  pl.BlockSpec((None, d, bf), lambda i, j: (l, 0, j + nf)),
                  pl.BlockSpec((None, bf, d), lambda i, j: (l, j, 0)),
                  pl.BlockSpec((1, d), lambda i, j: (0, 0))],
        out_specs=pl.BlockSpec((bm, d), lambda i, j: (i, 0)),
        out_shape=jax.ShapeDtypeStruct((rows, d), F32),
        scratch_shapes=[pltpu.VMEM((bm, d), BF16)],
        compiler_params=_params(("parallel", "arbitrary")),
        name="ffn_swiglu",
    )(h, hc, mods, norm_g.reshape(1, d), w_gu, w_gu, w_down, final_g.reshape(1, d))


def _proj_kernel(h_ref, mod_ref, g_ref, w_ref, o_ref, xn_ref, *, rc):
    @pl.when(pl.program_id(1) == 0)
    def _():
        _norm_mod_rows(h_ref, mod_ref, g_ref, xn_ref, None, 3, 4, rc)

    o_ref[...] = _dot_nt(xn_ref[...], w_ref[...]).astype(BF16)


def _proj(h, mods, norm_g, w_p, l, s, b):
    rows, d = h.shape
    n = w_p.shape[1]
    bm = _pick(1024, s, rows)
    bn = _pick(512, n)
    return pl.pallas_call(
        functools.partial(_proj_kernel, rc=_pick(128, bm)),
        grid=(rows // bm, n // bn),
        in_specs=[pl.BlockSpec((bm, d), lambda i, j: (i, 0)),
                  _mod_spec(d, bm, s, b),
                  pl.BlockSpec((1, d), lambda i, j: (0, 0)),
                  pl.BlockSpec((None, bn, d), lambda i, j: (l, j, 0))],
        out_specs=pl.BlockSpec((bm, bn), lambda i, j: (i, j)),
        out_shape=jax.ShapeDtypeStruct((rows, n), BF16),
        scratch_shapes=[pltpu.VMEM((bm, d), BF16)],
        compiler_params=_params(("parallel", "arbitrary")),
        name="in_proj",
    )(h, mods, norm_g.reshape(1, d), w_p)


def _conv_kernel(gb_ref, gc_ref, v_ref, gcp_ref, vp_ref, gcn_ref, vn_ref, w_ref, o_ref, *, bm, s, c, mx):
    row0 = pl.program_id(0) * bm
    is_lat = row0 < mx
    at_start = jnp.where(is_lat, row0 % s == 0, (row0 - mx) % c == 0)
    at_end = jnp.where(is_lat, (row0 + bm) % s == 0, (row0 - mx + bm) % c == 0)
    cv = gc_ref[...].astype(F32) * v_ref[...].astype(F32)
    hp = (gcp_ref[...].astype(F32) * vp_ref[...].astype(F32))[15:16, :]
    hn = (gcn_ref[...].astype(F32) * vn_ref[...].astype(F32))[0:1, :]
    hp = jnp.where(at_start, 0.0, hp)
    hn = jnp.where(at_end, 0.0, hn)
    rid = lax.broadcasted_iota(jnp.int32, (bm, 1), 0)
    prev = jnp.where(rid == 0, hp, pltpu.roll(cv, 1, 0))
    nxt = jnp.where(rid == bm - 1, hn, pltpu.roll(cv, bm - 1, 0))
    w = w_ref[...]
    y = gb_ref[...].astype(F32) * (prev * w[0:1, :] + cv * w[1:2, :] + nxt * w[2:3, :])
    o_ref[...] = y.astype(BF16)


def _conv(px, conv_w, rows, s, c, mx):
    cw = conv_w.shape[1]
    m = px.shape[0]
    bm = _pick(256, s, c)
    hb = bm // 16
    main = lambda col: pl.BlockSpec((bm, cw), lambda i: (i, col))
    prev = lambda col: pl.BlockSpec((16, cw), lambda i: (jnp.maximum(i * hb - 1, 0), col))
    nxt = lambda col: pl.BlockSpec((16, cw), lambda i: (jnp.minimum((i + 1) * hb, m // 16 - 1), col))
    return pl.pallas_call(
        functools.partial(_conv_kernel, bm=bm, s=s, c=c, mx=mx),
        grid=(rows // bm,),
        in_specs=[main(0), main(1), main(2), prev(1), prev(2), nxt(1), nxt(2),
                  pl.BlockSpec((CONV_K, cw), lambda i: (0, 0))],
        out_specs=pl.BlockSpec((bm, cw), lambda i: (i, 0)),
        out_shape=jax.ShapeDtypeStruct((rows, cw), BF16),
        compiler_params=_params(("parallel",)),
        name="gated_conv",
    )(px, px, px, px, px, px, px, conv_w)


def _rope(x, cos, sin, half):
    lane = lax.broadcasted_iota(jnp.int32, x.shape, 1)
    first = (lane % (2 * half)) < half
    rot = jnp.where(first, pltpu.roll(x, LANE - half, 1), pltpu.roll(x, half, 1))
    return x * cos + rot * sin


def _prep_kernel(gq_ref, cq_ref, ckv_ref, gk_ref, kr_ref, qn_ref, kvn_ref, wqb_ref, wkvb_ref,
                 mcos_ref, msin_ref, gcos_ref, gsin_ref, q_ref, k_ref, v_ref, gqo_ref, gko_ref):
    mcos, msin = mcos_ref[...], msin_ref[...]
    gcos, gsin = gcos_ref[...], gsin_ref[...]
    mh = MLA_ROPE // 4
    gh = GQA_HEAD_DIM // 4

    cqn = _rms(cq_ref[...].astype(F32), qn_ref[...]).astype(BF16)
    q = _dot(cqn, wqb_ref[...]) * (MLA_SCALE * LOG2E)
    ckvn = _rms(ckv_ref[...].astype(F32), kvn_ref[...]).astype(BF16)
    kv = _dot(ckvn, wkvb_ref[...])
    kr = _rope(kr_ref[...].astype(F32), mcos, msin, mh).astype(BF16)
    ones = jnp.ones((q.shape[0], LANE), BF16)
    v0 = MLA_HEADS * MLA_NOPE
    for h in range(MLA_HEADS):
        a = h * MLA_QK
        q_ref[:, a:a + LANE] = q[:, a:a + LANE].astype(BF16)
        q_ref[:, a + LANE:a + MLA_QK] = _rope(q[:, a + LANE:a + MLA_QK], mcos, msin, mh).astype(BF16)
        k_ref[:, a:a + LANE] = kv[:, h * LANE:(h + 1) * LANE].astype(BF16)
        k_ref[:, a + LANE:a + MLA_QK] = kr
        v_ref[:, a:a + LANE] = kv[:, v0 + h * LANE:v0 + (h + 1) * LANE].astype(BF16)
        v_ref[:, a + LANE:a + MLA_QK] = ones

    gq = gq_ref[...].astype(F32)
    for h in range(GQA_HEADS):
        a = h * GQA_HEAD_DIM
        gqo_ref[:, a:a + LANE] = (_rope(gq[:, a:a + LANE], gcos, gsin, gh) * (GQA_SCALE * LOG2E)).astype(BF16)
    gk = gk_ref[...].astype(F32)
    for h in range(GQA_KV_HEADS):
        a = h * GQA_HEAD_DIM
        gko_ref[:, a:a + LANE] = _rope(gk[:, a:a + LANE], gcos, gsin, gh).astype(BF16)


def _prep(px, lay, q_norm, kv_norm, wqb_p, wkvb_r, tabs, s, mx):
    m = px.shape[0]
    bm = _pick(512, s, m - mx)
    n_lat = mx // bm
    per = s // bm
    ql, kvl = q_norm.shape[0], kv_norm.shape[0]
    gqw, gkw = GQA_HEADS * GQA_HEAD_DIM, GQA_KV_HEADS * GQA_HEAD_DIM

    def col(width, off):
        assert off % width == 0, (width, off)
        return pl.BlockSpec((bm, width), lambda i: (i, off // width))

    const = lambda r, c: pl.BlockSpec((r, c), lambda i: (0, 0))
    tab = pl.BlockSpec((bm, LANE), lambda i: (jnp.where(i < n_lat, i % per, per + i - n_lat), 0))
    row = lambda width: pl.BlockSpec((bm, width), lambda i: (i, 0))
    hq = MLA_HEADS * MLA_QK
    return pl.pallas_call(
        _prep_kernel,
        grid=(m // bm,),
        in_specs=[col(gqw, lay["gq"]), col(ql, lay["cq"]), col(kvl, lay["ckv"]), col(gkw, lay["gk"]),
                  col(LANE, lay["kr"]), const(1, ql), const(1, kvl), const(ql, hq),
                  const(kvl, MLA_HEADS * (MLA_NOPE + MLA_V)), tab, tab, tab, tab],
        out_specs=[row(hq), row(hq), row(hq), row(gqw), row(gkw)],
        out_shape=[jax.ShapeDtypeStruct((m, hq), BF16), jax.ShapeDtypeStruct((m, hq), BF16),
                   jax.ShapeDtypeStruct((m, hq), BF16),
                   jax.ShapeDtypeStruct((m, gqw), BF16), jax.ShapeDtypeStruct((m, gkw), BF16)],
        compiler_params=_params(("parallel",)),
        name="attn_prep",
    )(px, px, px, px, px, q_norm.reshape(1, ql), kv_norm.reshape(1, kvl), wqb_p, wkvb_r, *tabs)


def _mla_kernel(q_ref, kl_ref, kc_ref, vl_ref, vc_ref, o_ref, *, n_sub):
    kl, kc = kl_ref[...], kc_ref[...]
    subs = [q_ref[i * MLA_SUB:(i + 1) * MLA_SUB, :] for i in range(n_sub)]
    scores = [(_dot_nt(q, kl), _dot_nt(q, kc)) for q in subs]
    for i, (s1, s2) in enumerate(scores):
        m = jnp.maximum(jnp.max(s1, axis=-1, keepdims=True), jnp.max(s2, axis=-1, keepdims=True))
        p1 = jnp.exp2(s1 - m).astype(BF16)
        p2 = jnp.exp2(s2 - m).astype(BF16)
        o = _dot(p1, vl_ref[...]) + _dot(p2, vc_ref[...])
        o_ref[i * MLA_SUB:(i + 1) * MLA_SUB, :] = (o[:, :MLA_V] / o[:, MLA_V:]).astype(BF16)


def _mla_ctx_kernel(q_ref, kc_ref, vc_ref, y_ref, o_ref):
    del y_ref
    s2 = _dot_nt(q_ref[...], kc_ref[...])
    p2 = jnp.exp2(s2 - jnp.max(s2, axis=-1, keepdims=True)).astype(BF16)
    o = _dot(p2, vc_ref[...])
    o_ref[...] = (o[:, :MLA_V] / o[:, MLA_V:]).astype(BF16)


def _mla_attn(q, k, v, b, s, c, with_ctx):
    mx = b * s
    n_sub = max(d for d in (1, 2, 4) if (s // MLA_SUB) % d == 0)
    bq = n_sub * MLA_SUB
    nq = s // bq
    rows = mx + (b * c if with_ctx else 0)
    ctx0 = mx // c
    y = pl.pallas_call(
        functools.partial(_mla_kernel, n_sub=n_sub),
        grid=(b, MLA_HEADS, nq),
        in_specs=[pl.BlockSpec((bq, MLA_QK), lambda bi, h, qi: (bi * nq + qi, h)),
                  pl.BlockSpec((s, MLA_QK), lambda bi, h, qi: (bi, h)),
                  pl.BlockSpec((c, MLA_QK), lambda bi, h, qi: (ctx0 + bi, h)),
                  pl.BlockSpec((s, MLA_QK), lambda bi, h, qi: (bi, h)),
                  pl.BlockSpec((c, MLA_QK), lambda bi, h, qi: (ctx0 + bi, h))],
        out_specs=pl.BlockSpec((bq, MLA_V), lambda bi, h, qi: (bi * nq + qi, h)),
        out_shape=jax.ShapeDtypeStruct((rows, MLA_HEADS * MLA_V), BF16),
        compiler_params=_params(("parallel", "parallel", "arbitrary")),
        name="mla_attn",
    )(q, k, k, v, v)
    if not with_ctx:
        return y
    return pl.pallas_call(
        _mla_ctx_kernel,
        grid=(b, MLA_HEADS),
        in_specs=[pl.BlockSpec((c, MLA_QK), lambda bi, h: (ctx0 + bi, h)),
                  pl.BlockSpec((c, MLA_QK), lambda bi, h: (ctx0 + bi, h)),
                  pl.BlockSpec((c, MLA_QK), lambda bi, h: (ctx0 + bi, h)),
                  pl.BlockSpec(memory_space=pl.ANY)],
        out_specs=pl.BlockSpec((c, MLA_V), lambda bi, h: (ctx0 + bi, h)),
        out_shape=jax.ShapeDtypeStruct(y.shape, BF16),
        input_output_aliases={3: 0},
        compiler_params=_params(("parallel", "parallel")),
        name="mla_attn_ctx",
    )(q, k, v, y)


GQA_QB = 4


def _sink_rows(sink_ref, g, j, rows):
    return jnp.broadcast_to(sink_ref[pl.ds(g * GQA_GROUP + j, 1), :], (rows, LANE))[:, 0:1]


def _gqa_kernel(q_ref, kp_ref, kc_ref, kn_ref, kx_ref, vp_ref, vc_ref, vn_ref, vx_ref, sink_ref, o_ref,
                *, n_lat):
    g = pl.program_id(1)
    n = pl.program_id(2)
    blk = ATT_BLOCK
    kx, vx = kx_ref[...], vx_ref[...]
    kband = jnp.concatenate([kp_ref[...], kc_ref[...], kn_ref[...]], axis=0)
    vband = jnp.concatenate([vp_ref[...], vc_ref[...], vn_ref[...]], axis=0)
    nk = 3 * blk + kx.shape[0]
    ones = jnp.ones((nk, LANE), BF16)
    snk = jnp.concatenate([jnp.broadcast_to(sink_ref[pl.ds(g * GQA_GROUP + j, 1), :], (blk, LANE))
                           for j in range(GQA_GROUP)], axis=0)
    rows = GQA_GROUP * blk
    r = lax.broadcasted_iota(jnp.int32, (rows, nk), 0) % blk
    col = lax.broadcasted_iota(jnp.int32, (rows, nk), 1)
    in_window = (jnp.abs(col - blk - r) <= WINDOW) | (col >= 3 * blk)
    for i in range(GQA_QB):
        qs = jnp.concatenate([q_ref[i * blk:(i + 1) * blk, j * LANE:(j + 1) * LANE] for j in range(GQA_GROUP)],
                             axis=0)
        keys = jnp.concatenate([kband[i * blk:(i + 3) * blk, :], kx], axis=0)
        vals = jnp.concatenate([jnp.concatenate([vband[i * blk:(i + 3) * blk, :], vx], axis=0), ones], axis=1)
        block = n * GQA_QB + i
        valid = in_window & ((col >= blk) | (block > 0)) & ((col < 2 * blk) | (col >= 3 * blk) | (block < n_lat - 1))
        sc = jnp.where(valid, _dot_nt(qs, keys), -jnp.inf)
        e = snk
        for t in range(nk // LANE):
            e = jnp.maximum(e, sc[:, t * LANE:(t + 1) * LANE])
        m = jnp.max(e, axis=-1, keepdims=True)
        ol = _dot(jnp.exp2(sc - m).astype(BF16), vals)
        o = ol[:, :LANE] / (ol[:, LANE:] + jnp.exp2(snk - m))
        for j in range(GQA_GROUP):
            o_ref[i * blk:(i + 1) * blk, j * LANE:(j + 1) * LANE] = o[j * blk:(j + 1) * blk, :].astype(BF16)


def _gqa_ctx_kernel(q_ref, kx_ref, vx_ref, sink_ref, y_ref, o_ref):
    del y_ref
    g = pl.program_id(1)
    kx, vx = kx_ref[...], vx_ref[...]
    for j in range(GQA_GROUP):
        sc = _dot_nt(q_ref[:, j * LANE:(j + 1) * LANE], kx)
        snk = _sink_rows(sink_ref, g, j, sc.shape[0])
        m = jnp.maximum(jnp.max(sc, axis=-1, keepdims=True), snk)
        p_c = jnp.exp2(sc - m)
        l = jnp.sum(p_c, axis=-1, keepdims=True) + jnp.exp2(snk - m)
        o_ref[:, j * LANE:(j + 1) * LANE] = (_dot(p_c.astype(BF16), vx) / l).astype(BF16)


def _gqa_attn(gq, gk, px, o_gv, sink, b, s, c, with_ctx):
    mx = b * s
    blk = ATT_BLOCK
    n_lat = s // blk
    nstep = n_lat // GQA_QB
    rows = mx + (b * c if with_ctx else 0)
    ctx0 = mx // c
    gvc = o_gv // LANE
    gw = GQA_GROUP * GQA_HEAD_DIM
    big = GQA_QB * blk

    def edge(shift, col0):
        def idx(bi, g, n):
            return bi * n_lat + jnp.clip(n * GQA_QB + shift, 0, n_lat - 1), col0 + g
        return pl.BlockSpec((blk, LANE), idx)

    main = lambda col0: pl.BlockSpec((big, LANE), lambda bi, g, n: (bi * nstep + n, col0 + g))
    ctx = lambda col0: pl.BlockSpec((c, LANE), lambda bi, g, n: (ctx0 + bi, col0 + g))
    sink_spec = pl.BlockSpec((GQA_HEADS, LANE), lambda *_: (0, 0))
    y = pl.pallas_call(
        functools.partial(_gqa_kernel, n_lat=n_lat),
        grid=(b, GQA_KV_HEADS, nstep),
        in_specs=[pl.BlockSpec((big, gw), lambda bi, g, n: (bi * nstep + n, g)),
                  edge(-1, 0), main(0), edge(GQA_QB, 0), ctx(0),
                  edge(-1, gvc), main(gvc), edge(GQA_QB, gvc), ctx(gvc), sink_spec],
        out_specs=pl.BlockSpec((big, gw), lambda bi, g, n: (bi * nstep + n, g)),
        out_shape=jax.ShapeDtypeStruct((rows, GQA_HEADS * GQA_HEAD_DIM), BF16),
        compiler_params=_params(("parallel", "parallel", "arbitrary")),
        name="gqa_attn",
    )(gq, gk, gk, gk, gk, px, px, px, px, sink)
    if not with_ctx:
        return y
    return pl.pallas_call(
        _gqa_ctx_kernel,
        grid=(b, GQA_KV_HEADS),
        in_specs=[pl.BlockSpec((c, gw), lambda bi, g: (ctx0 + bi, g)),
                  pl.BlockSpec((c, LANE), lambda bi, g: (ctx0 + bi, g)),
                  pl.BlockSpec((c, LANE), lambda bi, g: (ctx0 + bi, gvc + g)),
                  sink_spec, pl.BlockSpec(memory_space=pl.ANY)],
        out_specs=pl.BlockSpec((c, gw), lambda bi, g: (ctx0 + bi, g)),
        out_shape=jax.ShapeDtypeStruct(y.shape, BF16),
        input_output_aliases={4: 0},
        compiler_params=_params(("parallel", "parallel")),
        name="gqa_attn_ctx",
    )(gq, gk, px, sink, y)


def _merge_kernel(yc_ref, ym_ref, yg_ref, gc_ref, gm_ref, gg_ref, wc_ref, wm_ref, wg_ref, o_ref):
    gate = lambda ref: jax.nn.sigmoid(ref[...].astype(F32))
    acc = gate(gc_ref) * _dot(yc_ref[...], wc_ref[...])
    acc += gate(gm_ref) * _dot(ym_ref[...], wm_ref[...])
    acc += gate(gg_ref) * _dot(yg_ref[...], wg_ref[...])
    o_ref[...] = acc.astype(BF16)


def _merge(yc, ym, yg, px, o_gate, w_bc, w_bm, w_bg, l, rows, s):
    d = w_bc.shape[2]
    bm = _pick(1024, s, rows)
    bn = _pick(512, d, o_gate)
    g0 = o_gate // bn
    per = d // bn
    yspec = lambda a: pl.BlockSpec((bm, a.shape[1]), lambda i, j: (i, 0))
    gspec = lambda k: pl.BlockSpec((bm, bn), lambda i, j: (i, g0 + k * per + j))
    wspec = lambda w: pl.BlockSpec((None, w.shape[1], bn), lambda i, j: (l, 0, j))
    return pl.pallas_call(
        _merge_kernel,
        grid=(rows // bm, d // bn),
        in_specs=[yspec(yc), yspec(ym), yspec(yg), gspec(0), gspec(1), gspec(2),
                  wspec(w_bc), wspec(w_bm), wspec(w_bg)],
        out_specs=pl.BlockSpec((bm, bn), lambda i, j: (i, j)),
        out_shape=jax.ShapeDtypeStruct((rows, d), BF16),
        compiler_params=_params(("parallel", "arbitrary")),
        name="branch_merge",
    )(yc, ym, yg, px, px, px, w_bc, w_bm, w_bg)


def _out_kernel(m_ref, w_ref, h_ref, mod_ref, o_ref):
    o_ref[...] = h_ref[...] + mod_ref[0, 5:6, :] * _dot(m_ref[...], w_ref[...])


def _out_proj(merged, w_out, l, h, mods, rows, s, b):
    d = w_out.shape[2]
    bm = _pick(1024, s, rows)
    bn = _pick(512, d)
    per = s // bm
    return pl.pallas_call(
        _out_kernel,
        grid=(rows // bm, d // bn),
        in_specs=[pl.BlockSpec((bm, d), lambda i, j: (i, 0)),
                  pl.BlockSpec((None, d, bn), lambda i, j: (l, 0, j)),
                  pl.BlockSpec((bm, bn), lambda i, j: (i, j)),
                  pl.BlockSpec((1, N_MOD, bn), lambda i, j: (jnp.minimum(i // per, b), 0, j))],
        out_specs=pl.BlockSpec((bm, bn), lambda i, j: (i, j)),
        out_shape=jax.ShapeDtypeStruct((rows, d), F32),
        compiler_params=_params(("parallel", "arbitrary")),
        name="out_proj",
    )(merged, w_out, h, mods)


def _rope_table(s, n_ctx_rows, dim):
    t = jnp.arange(s, dtype=jnp.int32)
    pos = jnp.stack([t // GRID_W, t % GRID_W], axis=1).astype(F32)
    inv = ROPE_BASE ** (-jnp.arange(0, dim, 2, dtype=F32) / dim)
    lane = jnp.arange(LANE)
    ang = pos[:, jnp.minimum(lane // dim, 1)] * inv[lane % (dim // 2)][None, :]
    active = (lane < 2 * dim)[None, :]
    sign = jnp.where((lane % dim) < dim // 2, -1.0, 1.0)[None, :]
    cos = jnp.where(active, jnp.cos(ang), 1.0)
    sin = jnp.where(active, jnp.sin(ang) * sign, 0.0)
    pad = ((0, n_ctx_rows), (0, 0))
    return jnp.pad(cos, pad, constant_values=1.0), jnp.pad(sin, pad)


def _layout(cw, ql, kvl, d):
    lay = {"conv": 0}
    off = 3 * cw
    for name, width in (("gq", GQA_HEADS * GQA_HEAD_DIM), ("cq", ql), ("ckv", kvl),
                        ("gk", GQA_KV_HEADS * GQA_HEAD_DIM), ("gv", GQA_KV_HEADS * GQA_HEAD_DIM),
                        ("kr", LANE)):
        lay[name] = off
        off += width
    off = -(-off // 512) * 512
    lay["gate"] = off
    lay["total"] = off + 3 * d
    return lay


def _pack_kernel(w_ref, o_ref, *, moves, zero):
    for dst, src, width in moves:
        o_ref[dst:dst + width, :] = w_ref[src:src + width, :].astype(BF16)
    o_ref[zero[0]:zero[1], :] = jnp.zeros((zero[1] - zero[0], o_ref.shape[1]), BF16)


def _pack_w_in(w, lay, cw, ql, kvl):
    depth, d, n = w.shape
    w_t = jnp.swapaxes(w, 1, 2)
    o_mla = 3 * cw
    o_gqa = o_mla + ql + kvl + MLA_ROPE
    gqw, gkw = GQA_HEADS * GQA_HEAD_DIM, GQA_KV_HEADS * GQA_HEAD_DIM
    o_gate = o_gqa + gqw + 2 * gkw
    moves = ((0, 0, o_mla), (lay["gq"], o_gqa, gqw), (lay["cq"], o_mla, ql + kvl),
             (lay["gk"], o_gqa + gqw, 2 * gkw), (lay["kr"], o_mla + ql + kvl, MLA_ROPE),
             (lay["gate"], o_gate, n - o_gate))
    cb = _pick(256, d)
    return pl.pallas_call(
        functools.partial(_pack_kernel, moves=moves, zero=(lay["kr"] + MLA_ROPE, lay["gate"])),
        grid=(depth, d // cb),
        in_specs=[pl.BlockSpec((None, n, cb), lambda l, j: (l, 0, j))],
        out_specs=pl.BlockSpec((None, lay["total"], cb), lambda l, j: (l, 0, j)),
        out_shape=jax.ShapeDtypeStruct((depth, lay["total"], d), BF16),
        compiler_params=_params(("parallel", "parallel")),
        name="pack_w_in",
    )(w_t)


def kernel(x, c, ctx, c_ctx, ada_w, ada_b, ffn1_norm, ffn1_w_gu, ffn1_w_down, mix_norm, w_in, conv_w,
           mla_q_norm, mla_w_qb, mla_kv_norm, mla_w_kvb, gqa_sink, w_branch_conv, w_branch_mla,
           w_branch_gqa, w_out, ffn2_norm, ffn2_w_gu, ffn2_w_down, final_norm):
    b, s, d = x.shape
    cl = ctx.shape[1]
    depth = ada_w.shape[0]
    cw = conv_w.shape[-1]
    ql, kvl = mla_q_norm.shape[-1], mla_kv_norm.shape[-1]
    mx, mc = b * s, b * cl
    assert b + 1 <= 8 and s % cl == 0 and cl % ATT_BLOCK == 0

    cvec = jnp.zeros((8, d), F32).at[:b].set(c).at[b].set(c_ctx)
    mods = _mods(cvec, ada_w, ada_b).reshape(depth, 8, N_MOD, d)

    lay = _layout(cw, ql, kvl, d)
    tabs = _rope_table(s, mc, MLA_ROPE // 2) + _rope_table(s, mc, GQA_HEAD_DIM // 2)
    h = None
    w1_gu, w1_dn = _to_bf16(ffn1_w_gu), _to_bf16(ffn1_w_down)
    w2_gu, w2_dn = _to_bf16(ffn2_w_gu), _to_bf16(ffn2_w_down)
    w_bc, w_bm, w_bg = _to_bf16(w_branch_conv), _to_bf16(w_branch_mla), _to_bf16(w_branch_gqa)
    w_o = _to_bf16(w_out)
    w_p = _pack_w_in(w_in, lay, cw, ql, kvl)

    for l in range(depth):
        with_ctx = l < depth - 1
        rows = mx + mc if with_ctx else mx
        wqb_p = jnp.pad(mla_w_qb[l].reshape(ql, MLA_HEADS, MLA_NOPE + MLA_ROPE),
                        ((0, 0), (0, 0), (0, MLA_QK - MLA_NOPE - MLA_ROPE))).reshape(ql, -1).astype(BF16)
        wkv = mla_w_kvb[l].reshape(kvl, MLA_HEADS, MLA_NOPE + MLA_V)
        wkvb_r = jnp.concatenate([wkv[:, :, :MLA_NOPE].reshape(kvl, -1), wkv[:, :, MLA_NOPE:].reshape(kvl, -1)],
                                 axis=1).astype(BF16)
        sink = jnp.broadcast_to((gqa_sink[l].astype(F32) * LOG2E)[:, None], (GQA_HEADS, LANE))

        if l == 0:
            h = _ffn(x.reshape(mx, d), ctx.reshape(mc, d), mx + mc, mods[l], ffn1_norm[l], w1_gu, w1_dn, l, 0,
                     s, b, final_norm, False)
        else:
            h = _ffn(h, None, mx + mc, mods[l], ffn1_norm[l], w1_gu, w1_dn, l, 0, s, b, final_norm, False)
        px = _proj(h, mods[l], mix_norm[l], w_p, l, s, b)
        y_conv = _conv(px, conv_w[l], rows, s, cl, mx)
        q, k, v, gq, gk = _prep(px, lay, mla_q_norm[l], mla_kv_norm[l], wqb_p, wkvb_r, tabs, s, mx)
        y_mla = _mla_attn(q, k, v, b, s, cl, with_ctx)
        y_gqa = _gqa_attn(gq, gk, px, lay["gv"], sink, b, s, cl, with_ctx)
        merged = _merge(y_conv, y_mla, y_gqa, px, lay["gate"], w_bc, w_bm, w_bg, l, rows, s)
        h = _out_proj(merged, w_o, l, h, mods[l], rows, s, b)
        h = _ffn(h, None, rows, mods[l], ffn2_norm[l], w2_gu, w2_dn, l, 6, s, b, final_norm, not with_ctx)
    return h.reshape(b, s, d)
```

```python
import functools

import jax
import jax.numpy as jnp
from jax import lax
from jax.experimental import pallas as pl
from jax.experimental.pallas import tpu as pltpu

F32 = jnp.float32
BF16 = jnp.bfloat16

GRID_W = 64
N_MOD = 9
EPS = 1e-6
ROPE_BASE = 10000.0
CONV_K = 3
MLA_HEADS = 8
MLA_NOPE = 128
MLA_ROPE = 64
MLA_V = 128
MLA_SCALE = (MLA_NOPE + MLA_ROPE) ** -0.5
GQA_HEADS = 8
GQA_KV_HEADS = 2
GQA_GROUP = GQA_HEADS // GQA_KV_HEADS
GQA_HEAD_DIM = 128
GQA_SCALE = GQA_HEAD_DIM ** -0.5
WINDOW = 128
ATT_BLOCK = 128

LANE = 128
MLA_QK = 2 * LANE
MLA_SUB = 256
LOG2E = 1.4426950408889634
VMEM_LIMIT = 56 << 20
NT_DIMS = (((1,), (1,)), ((), ()))


def _pick(target, *sizes):
    b = target
    while any(s % b for s in sizes):
        b //= 2
        assert b >= 8, (target, sizes)
    return b


def _params(sem):
    return pltpu.CompilerParams(dimension_semantics=sem, vmem_limit_bytes=VMEM_LIMIT)


def _dot(a, b):
    return jnp.dot(a, b, preferred_element_type=F32)


def _dot_nt(a, b):
    return lax.dot_general(a, b, NT_DIMS, preferred_element_type=F32)


def _rms(x, g):
    return x * lax.rsqrt(jnp.mean(x * x, axis=-1, keepdims=True) + EPS) * g


CAST_BLOCK_BYTES = 6 << 20


def _cast_kernel(w_ref, o_ref):
    o_ref[...] = w_ref[...].astype(BF16)


def _to_bf16(w):
    depth, k, n = w.shape
    rows = depth * k
    rb = _pick(max(16, 1 << ((CAST_BLOCK_BYTES // (4 * n)).bit_length() - 1)), rows)
    out = pl.pallas_call(
        _cast_kernel,
        grid=(rows // rb,),
        in_specs=[pl.BlockSpec((rb, n), lambda i: (i, 0))],
        out_specs=pl.BlockSpec((rb, n), lambda i: (i, 0)),
        out_shape=jax.ShapeDtypeStruct((rows, n), BF16),
        compiler_params=_params(("parallel",)),
        name="cast_bf16",
    )(w.reshape(rows, n))
    return out.reshape(depth, k, n)


def _cast_gu_kernel(w_ref, o_ref, *, bf):
    nf = o_ref.shape[0]
    for j in range(nf):
        o_ref[j, :, :bf] = w_ref[:, j * bf:(j + 1) * bf].astype(BF16)
        o_ref[j, :, bf:] = w_ref[:, (nf + j) * bf:(nf + j + 1) * bf].astype(BF16)


def _gu_to_bf16(w, bf):
    depth, k, n = w.shape
    nf = n // (2 * bf)
    rb = _pick(max(16, 1 << ((CAST_BLOCK_BYTES // (4 * n)).bit_length() - 1)), k)
    kb = k // rb
    return pl.pallas_call(
        functools.partial(_cast_gu_kernel, bf=bf),
        grid=(depth, kb),
        in_specs=[pl.BlockSpec((None, rb, n), lambda l, i: (l, i, 0))],
        out_specs=pl.BlockSpec((None, nf, rb, 2 * bf), lambda l, i: (l, 0, i, 0)),
        out_shape=jax.ShapeDtypeStruct((depth, nf, k, 2 * bf), BF16),
        compiler_params=_params(("parallel", "parallel")),
        name="cast_gate_up",
    )(w)


def _mods_kernel(c_ref, w_ref, b_ref, o_ref):
    c = c_ref[...]
    s = (c * jax.nn.sigmoid(c)).astype(BF16)
    o_ref[0] = _dot(s, w_ref[0].astype(BF16)) + b_ref[0]


def _mods(cvec, ada_w, ada_b):
    depth, d, n = ada_w.shape
    bn = _pick(1024, n)
    return pl.pallas_call(
        _mods_kernel,
        grid=(depth, n // bn),
        in_specs=[pl.BlockSpec((8, d), lambda l, j: (0, 0)),
                  pl.BlockSpec((1, d, bn), lambda l, j: (l, 0, j)),
                  pl.BlockSpec((1, 1, bn), lambda l, j: (l, 0, j))],
        out_specs=pl.BlockSpec((1, 8, bn), lambda l, j: (l, 0, j)),
        out_shape=jax.ShapeDtypeStruct((depth, 8, n), F32),
        compiler_params=_params(("parallel", "parallel")),
        name="adaln_mods",
    )(cvec, ada_w, ada_b.reshape(depth, 1, n))


def _norm_mod_rows(h_ref, mod_ref, g_ref, xn_ref, copy_ref, i_shift, i_scale, rc):
    shift = mod_ref[0, i_shift:i_shift + 1, :]
    scale1 = 1.0 + mod_ref[0, i_scale:i_scale + 1, :]
    g = g_ref[...]

    def body(r, carry):
        rows = pl.ds(pl.multiple_of(r * rc, rc), rc)
        x = h_ref[rows, :]
        xn_ref[rows, :] = (_rms(x, g) * scale1 + shift).astype(BF16)
        if copy_ref is not None:
            copy_ref[rows, :] = x
        return carry

    lax.fori_loop(0, h_ref.shape[0] // rc, body, 0)


def _norm_mod_chunk(hn_ref, modn_ref, g_ref, xn_ref, slot, k, ch, i_shift, i_scale):
    r0 = pl.multiple_of(jnp.minimum(k * ch, hn_ref.shape[0] - ch), 16)
    shift = modn_ref[0, i_shift:i_shift + 1, :]
    scale1 = 1.0 + modn_ref[0, i_scale:i_scale + 1, :]
    x = hn_ref[pl.ds(r0, ch), :]
    xn_ref[slot, pl.ds(r0, ch), :] = (_rms(x, g_ref[...]) * scale1 + shift).astype(BF16)


def _chunk_rows(bm, steps):
    return min(bm, -(-bm // (16 * steps)) * 16)


def _mod_spec(d, bm, s, b, ahead=0, last=None):
    per = s // bm

    def idx(i, j):
        t = i if not ahead else jnp.minimum(i + ahead, last)
        return jnp.minimum(t // per, b), 0, 0

    return pl.BlockSpec((1, N_MOD, d), idx)


def _ffn_math(xn, mod_ref, wgu_ref, wd_ref, i0):
    gu = _dot(xn, wgu_ref[...])
    bf = gu.shape[1] // 2
    gg, uu = gu[:, :bf], gu[:, bf:]
    a = (gg * jax.nn.sigmoid(gg) * uu).astype(BF16)
    return (0.5 * mod_ref[0, i0 + 2:i0 + 3, :]) * _dot(a, wd_ref[...])


def _ffn_split_kernel(h_ref, hc_ref, mod_ref, g_ref, wgu_ref, wd_ref, o_ref, xn_ref, *, i0, rc, n_lat):
    first = pl.program_id(1) == 0
    is_lat = pl.program_id(0) < n_lat

    @pl.when(first & is_lat)
    def _():
        _norm_mod_rows(h_ref, mod_ref, g_ref, xn_ref, o_ref, i0, i0 + 1, rc)

    @pl.when(first & jnp.logical_not(is_lat))
    def _():
        _norm_mod_rows(hc_ref, mod_ref, g_ref, xn_ref, o_ref, i0, i0 + 1, rc)

    o_ref[...] += _ffn_math(xn_ref[...], mod_ref, wgu_ref, wd_ref, i0)


def _ffn_kernel(h_ref, hn_ref, mod_ref, modn_ref, g_ref, wgu_ref, wd_ref, fg_ref, o_ref, xn_ref,
                *, i0, rc, ch, final):
    i, f = pl.program_id(0), pl.program_id(1)
    slot = i % 2

    @pl.when((f == 0) & (i == 0))
    def _():
        _norm_mod_rows(h_ref, mod_ref, g_ref, xn_ref.at[0], None, i0, i0 + 1, rc)

    @pl.when(f == 0)
    def _():
        o_ref[...] = h_ref[...]

    o_ref[...] += _ffn_math(xn_ref[slot], mod_ref, wgu_ref, wd_ref, i0)
    _norm_mod_chunk(hn_ref, modn_ref, g_ref, xn_ref, 1 - slot, f, ch, i0, i0 + 1)

    if final:
        @pl.when(f == pl.num_programs(1) - 1)
        def _():
            fg = fg_ref[...]

            def body(r, carry):
                rows = pl.ds(pl.multiple_of(r * rc, rc), rc)
                o_ref[rows, :] = _rms(o_ref[rows, :], fg)
                return carry

            lax.fori_loop(0, o_ref.shape[0] // rc, body, 0)


def _ffn(h, hc, rows, mods, norm_g, w_gu, w_down, l, i0, s, b, final_g, final):
    d = h.shape[1]
    nf, bf = w_gu.shape[1], w_gu.shape[3] // 2
    bm = _pick(512, s, rows) if hc is None else _pick(512, s, hc.shape[0])
    nt = rows // bm
    w_specs = [pl.BlockSpec((None, None, d, 2 * bf), lambda i, j: (l, j, 0, 0)),
               pl.BlockSpec((None, bf, d), lambda i, j: (l, j, 0))]
    vec = pl.BlockSpec((1, d), lambda i, j: (0, 0))
    common = dict(
        grid=(nt, nf),
        out_specs=pl.BlockSpec((bm, d), lambda i, j: (i, 0)),
        out_shape=jax.ShapeDtypeStruct((rows, d), F32),
        compiler_params=_params(("arbitrary", "arbitrary")),
        name="ffn_swiglu")
    if hc is not None:
        n_lat = h.shape[0] // bm
        return pl.pallas_call(
            functools.partial(_ffn_split_kernel, i0=i0, rc=_pick(128, bm), n_lat=n_lat),
            in_specs=[pl.BlockSpec((bm, d), lambda i, j: (jnp.minimum(i, n_lat - 1), 0)),
                      pl.BlockSpec((bm, d), lambda i, j: (jnp.maximum(i - n_lat, 0), 0)),
                      _mod_spec(d, bm, s, b), vec] + w_specs,
            scratch_shapes=[pltpu.VMEM((bm, d), BF16)],
            **common,
        )(h, hc, mods, norm_g.reshape(1, d), w_gu, w_down)
    return pl.pallas_call(
        functools.partial(_ffn_kernel, i0=i0, rc=_pick(128, bm), ch=_chunk_rows(bm, nf), final=final),
        in_specs=[pl.BlockSpec((bm, d), lambda i, j: (i, 0)),
                  pl.BlockSpec((bm, d), lambda i, j: (jnp.minimum(i + 1, nt - 1), 0)),
                  _mod_spec(d, bm, s, b), _mod_spec(d, bm, s, b, 1, nt - 1), vec] + w_specs + [vec],
        scratch_shapes=[pltpu.VMEM((2, bm, d), BF16)],
        **common,
    )(h, h, mods, mods, norm_g.reshape(1, d), w_gu, w_down, final_g.reshape(1, d))


def _proj_kernel(h_ref, mod_ref, modn_ref, g_ref, w_ref, o_ref, xn_ref, *, rc, ch):
    i, j = pl.program_id(0), pl.program_id(1)
    slot = i % 2

    @pl.when((j == 0) & (i == 0))
    def _():
        _norm_mod_rows(h_ref, mod_ref, g_ref, xn_ref.at[0], None, 3, 4, rc)

    o_ref[...] = _dot_nt(xn_ref[slot], w_ref[...]).astype(BF16)
    _norm_mod_chunk(h_ref, modn_ref, g_ref, xn_ref, 1 - slot, jnp.maximum(j - 1, 0), ch, 3, 4)


def _proj(h, mods, norm_g, w_p, l, s, b):
    rows, d = h.shape
    n = w_p.shape[1]
    bm = _pick(1024, s, rows)
    bn = _pick(512, n)
    nt = rows // bm
    assert n // bn >= 2
    h_idx = lambda i, j: (jnp.where((i == 0) & (j == 0), 0, jnp.minimum(i + 1, nt - 1)), 0)
    return pl.pallas_call(
        functools.partial(_proj_kernel, rc=_pick(128, bm), ch=_chunk_rows(bm, n // bn - 1)),
        grid=(nt, n // bn),
        in_specs=[pl.BlockSpec((bm, d), h_idx),
                  _mod_spec(d, bm, s, b), _mod_spec(d, bm, s, b, 1, nt - 1),
                  pl.BlockSpec((1, d), lambda i, j: (0, 0)),
                  pl.BlockSpec((None, bn, d), lambda i, j: (l, j, 0))],
        out_specs=pl.BlockSpec((bm, bn), lambda i, j: (i, j)),
        out_shape=jax.ShapeDtypeStruct((rows, n), BF16),
        scratch_shapes=[pltpu.VMEM((2, bm, d), BF16)],
        compiler_params=_params(("arbitrary", "arbitrary")),
        name="in_proj",
    )(h, mods, mods, norm_g.reshape(1, d), w_p)


def _conv_kernel(gb_ref, gc_ref, v_ref, gcp_ref, vp_ref, gcn_ref, vn_ref, w_ref, o_ref, *, bm, s, c, mx):
    row0 = pl.program_id(0) * bm
    is_lat = row0 < mx
    at_start = jnp.where(is_lat, row0 % s == 0, (row0 - mx) % c == 0)
    at_end = jnp.where(is_lat, (row0 + bm) % s == 0, (row0 - mx + bm) % c == 0)
    cv = gc_ref[...].astype(F32) * v_ref[...].astype(F32)
    hp = (gcp_ref[...].astype(F32) * vp_ref[...].astype(F32))[15:16, :]
    hn = (gcn_ref[...].astype(F32) * vn_ref[...].astype(F32))[0:1, :]
    hp = jnp.where(at_start, 0.0, hp)
    hn = jnp.where(at_end, 0.0, hn)
    rid = lax.broadcasted_iota(jnp.int32, (bm, 1), 0)
    prev = jnp.where(rid == 0, hp, pltpu.roll(cv, 1, 0))
    nxt = jnp.where(rid == bm - 1, hn, pltpu.roll(cv, bm - 1, 0))
    w = w_ref[...]
    y = gb_ref[...].astype(F32) * (prev * w[0:1, :] + cv * w[1:2, :] + nxt * w[2:3, :])
    o_ref[...] = y.astype(BF16)


def _conv(px, conv_w, rows, s, c, mx):
    cw = conv_w.shape[1]
    m = px.shape[0]
    bm = _pick(256, s, c)
    hb = bm // 16
    main = lambda col: pl.BlockSpec((bm, cw), lambda i: (i, col))
    prev = lambda col: pl.BlockSpec((16, cw), lambda i: (jnp.maximum(i * hb - 1, 0), col))
    nxt = lambda col: pl.BlockSpec((16, cw), lambda i: (jnp.minimum((i + 1) * hb, m // 16 - 1), col))
    return pl.pallas_call(
        functools.partial(_conv_kernel, bm=bm, s=s, c=c, mx=mx),
        grid=(rows // bm,),
        in_specs=[main(0), main(1), main(2), prev(1), prev(2), nxt(1), nxt(2),
                  pl.BlockSpec((CONV_K, cw), lambda i: (0, 0))],
        out_specs=pl.BlockSpec((bm, cw), lambda i: (i, 0)),
        out_shape=jax.ShapeDtypeStruct((rows, cw), BF16),
        compiler_params=_params(("parallel",)),
        name="gated_conv",
    )(px, px, px, px, px, px, px, conv_w)


def _rope(x, cos, sin, half):
    lane = lax.broadcasted_iota(jnp.int32, x.shape, 1)
    first = (lane % (2 * half)) < half
    rot = jnp.where(first, pltpu.roll(x, LANE - half, 1), pltpu.roll(x, half, 1))
    return x * cos + rot * sin


def _prep_kernel(gq_ref, cq_ref, ckv_ref, gk_ref, kr_ref, qn_ref, kvn_ref, wqb_ref, wkvb_ref,
                 mcos_ref, msin_ref, gcos_ref, gsin_ref, q_ref, k_ref, v_ref, gqo_ref, gko_ref):
    mcos, msin = mcos_ref[...], msin_ref[...]
    gcos, gsin = gcos_ref[...], gsin_ref[...]
    mh = MLA_ROPE // 4
    gh = GQA_HEAD_DIM // 4

    cqn = _rms(cq_ref[...].astype(F32), qn_ref[...]).astype(BF16)
    q = _dot(cqn, wqb_ref[...]) * (MLA_SCALE * LOG2E)
    ckvn = _rms(ckv_ref[...].astype(F32), kvn_ref[...]).astype(BF16)
    kv = _dot(ckvn, wkvb_ref[...])
    kr = _rope(kr_ref[...].astype(F32), mcos, msin, mh).astype(BF16)
    ones = jnp.ones((q.shape[0], LANE), BF16)
    v0 = MLA_HEADS * MLA_NOPE
    for h in range(MLA_HEADS):
        a = h * MLA_QK
        q_ref[:, a:a + LANE] = q[:, a:a + LANE].astype(BF16)
        q_ref[:, a + LANE:a + MLA_QK] = _rope(q[:, a + LANE:a + MLA_QK], mcos, msin, mh).astype(BF16)
        k_ref[:, a:a + LANE] = kv[:, h * LANE:(h + 1) * LANE].astype(BF16)
        k_ref[:, a + LANE:a + MLA_QK] = kr
        v_ref[:, a:a + LANE] = kv[:, v0 + h * LANE:v0 + (h + 1) * LANE].astype(BF16)
        v_ref[:, a + LANE:a + MLA_QK] = ones

    gq = gq_ref[...].astype(F32)
    for h in range(GQA_HEADS):
        a = h * GQA_HEAD_DIM
        gqo_ref[:, a:a + LANE] = (_rope(gq[:, a:a + LANE], gcos, gsin, gh) * (GQA_SCALE * LOG2E)).astype(BF16)
    gk = gk_ref[...].astype(F32)
    for h in range(GQA_KV_HEADS):
        a = h * GQA_HEAD_DIM
        gko_ref[:, a:a + LANE] = _rope(gk[:, a:a + LANE], gcos, gsin, gh).astype(BF16)


def _prep(px, lay, q_norm, kv_norm, wqb_p, wkvb_r, tabs, s, mx):
    m = px.shape[0]
    bm = _pick(512, s, m - mx)
    n_lat = mx // bm
    per = s // bm
    ql, kvl = q_norm.shape[0], kv_norm.shape[0]
    gqw, gkw = GQA_HEADS * GQA_HEAD_DIM, GQA_KV_HEADS * GQA_HEAD_DIM

    def col(width, off):
        assert off % width == 0, (width, off)
        return pl.BlockSpec((bm, width), lambda i: (i, off // width))

    const = lambda r, c: pl.BlockSpec((r, c), lambda i: (0, 0))
    tab = pl.BlockSpec((bm, LANE), lambda i: (jnp.where(i < n_lat, i % per, per + i - n_lat), 0))
    row = lambda width: pl.BlockSpec((bm, width), lambda i: (i, 0))
    hq = MLA_HEADS * MLA_QK
    return pl.pallas_call(
        _prep_kernel,
        grid=(m // bm,),
        in_specs=[col(gqw, lay["gq"]), col(ql, lay["cq"]), col(kvl, lay["ckv"]), col(gkw, lay["gk"]),
                  col(LANE, lay["kr"]), const(1, ql), const(1, kvl), const(ql, hq),
                  const(kvl, MLA_HEADS * (MLA_NOPE + MLA_V)), tab, tab, tab, tab],
        out_specs=[row(hq), row(hq), row(hq), row(gqw), row(gkw)],
        out_shape=[jax.ShapeDtypeStruct((m, hq), BF16), jax.ShapeDtypeStruct((m, hq), BF16),
                   jax.ShapeDtypeStruct((m, hq), BF16),
                   jax.ShapeDtypeStruct((m, gqw), BF16), jax.ShapeDtypeStruct((m, gkw), BF16)],
        compiler_params=_params(("parallel",)),
        name="attn_prep",
    )(px, px, px, px, px, q_norm.reshape(1, ql), kv_norm.reshape(1, kvl), wqb_p, wkvb_r, *tabs)


def _mla_kernel(q_ref, kl_ref, kc_ref, vl_ref, vc_ref, o_ref, *, n_sub):
    kl, kc = kl_ref[...], kc_ref[...]
    subs = [q_ref[i * MLA_SUB:(i + 1) * MLA_SUB, :] for i in range(n_sub)]
    scores = [(_dot_nt(q, kl), _dot_nt(q, kc)) for q in subs]
    for i, (s1, s2) in enumerate(scores):
        m = jnp.maximum(jnp.max(s1, axis=-1, keepdims=True), jnp.max(s2, axis=-1, keepdims=True))
        p1 = jnp.exp2(s1 - m).astype(BF16)
        p2 = jnp.exp2(s2 - m).astype(BF16)
        o = _dot(p1, vl_ref[...]) + _dot(p2, vc_ref[...])
        o_ref[i * MLA_SUB:(i + 1) * MLA_SUB, :] = (o[:, :MLA_V] / o[:, MLA_V:]).astype(BF16)


def _mla_ctx_kernel(q_ref, kc_ref, vc_ref, y_ref, o_ref):
    del y_ref
    s2 = _dot_nt(q_ref[...], kc_ref[...])
    p2 = jnp.exp2(s2 - jnp.max(s2, axis=-1, keepdims=True)).astype(BF16)
    o = _dot(p2, vc_ref[...])
    o_ref[...] = (o[:, :MLA_V] / o[:, MLA_V:]).astype(BF16)


def _mla_attn(q, k, v, b, s, c, with_ctx):
    mx = b * s
    n_sub = max(d for d in (1, 2, 4) if (s // MLA_SUB) % d == 0)
    bq = n_sub * MLA_SUB
    nq = s // bq
    rows = mx + (b * c if with_ctx else 0)
    ctx0 = mx // c
    y = pl.pallas_call(
        functools.partial(_mla_kernel, n_sub=n_sub),
        grid=(b, MLA_HEADS, nq),
        in_specs=[pl.BlockSpec((bq, MLA_QK), lambda bi, h, qi: (bi * nq + qi, h)),
                  pl.BlockSpec((s, MLA_QK), lambda bi, h, qi: (bi, h)),
                  pl.BlockSpec((c, MLA_QK), lambda bi, h, qi: (ctx0 + bi, h)),
                  pl.BlockSpec((s, MLA_QK), lambda bi, h, qi: (bi, h)),
                  pl.BlockSpec((c, MLA_QK), lambda bi, h, qi: (ctx0 + bi, h))],
        out_specs=pl.BlockSpec((bq, MLA_V), lambda bi, h, qi: (bi * nq + qi, h)),
        out_shape=jax.ShapeDtypeStruct((rows, MLA_HEADS * MLA_V), BF16),
        compiler_params=_params(("parallel", "parallel", "arbitrary")),
        name="mla_attn",
    )(q, k, k, v, v)
    if not with_ctx:
        return y
    return pl.pallas_call(
        _mla_ctx_kernel,
        grid=(b, MLA_HEADS),
        in_specs=[pl.BlockSpec((c, MLA_QK), lambda bi, h: (ctx0 + bi, h)),
                  pl.BlockSpec((c, MLA_QK), lambda bi, h: (ctx0 + bi, h)),
                  pl.BlockSpec((c, MLA_QK), lambda bi, h: (ctx0 + bi, h)),
                  pl.BlockSpec(memory_space=pl.ANY)],
        out_specs=pl.BlockSpec((c, MLA_V), lambda bi, h: (ctx0 + bi, h)),
        out_shape=jax.ShapeDtypeStruct(y.shape, BF16),
        input_output_aliases={3: 0},
        compiler_params=_params(("parallel", "parallel")),
        name="mla_attn_ctx",
    )(q, k, v, y)


GQA_QB = 4


def _sink_rows(sink_ref, g, j, rows):
    return jnp.broadcast_to(sink_ref[pl.ds(g * GQA_GROUP + j, 1), :], (rows, LANE))[:, 0:1]


def _gqa_kernel(q_ref, kp_ref, kc_ref, kn_ref, kx_ref, vp_ref, vc_ref, vn_ref, vx_ref, sink_ref, o_ref,
                *, n_lat):
    g = pl.program_id(1)
    n = pl.program_id(2)
    blk = ATT_BLOCK
    kx, vx = kx_ref[...], vx_ref[...]
    kband = jnp.concatenate([kp_ref[...], kc_ref[...], kn_ref[...]], axis=0)
    vband = jnp.concatenate([vp_ref[...], vc_ref[...], vn_ref[...]], axis=0)
    nk = 3 * blk + kx.shape[0]
    ones = jnp.ones((nk, LANE), BF16)
    snk = jnp.concatenate([jnp.broadcast_to(sink_ref[pl.ds(g * GQA_GROUP + j, 1), :], (blk, LANE))
                           for j in range(GQA_GROUP)], axis=0)
    rows = GQA_GROUP * blk
    r = lax.broadcasted_iota(jnp.int32, (rows, nk), 0) % blk
    col = lax.broadcasted_iota(jnp.int32, (rows, nk), 1)
    in_window = (jnp.abs(col - blk - r) <= WINDOW) | (col >= 3 * blk)
    for i in range(GQA_QB):
        qs = jnp.concatenate([q_ref[i * blk:(i + 1) * blk, j * LANE:(j + 1) * LANE] for j in range(GQA_GROUP)],
                             axis=0)
        keys = jnp.concatenate([kband[i * blk:(i + 3) * blk, :], kx], axis=0)
        vals = jnp.concatenate([jnp.concatenate([vband[i * blk:(i + 3) * blk, :], vx], axis=0), ones], axis=1)
        block = n * GQA_QB + i
        valid = in_window & ((col >= blk) | (block > 0)) & ((col < 2 * blk) | (col >= 3 * blk) | (block < n_lat - 1))
        sc = jnp.where(valid, _dot_nt(qs, keys), -jnp.inf)
        e = snk
        for t in range(nk // LANE):
            e = jnp.maximum(e, sc[:, t * LANE:(t + 1) * LANE])
        m = jnp.max(e, axis=-1, keepdims=True)
        ol = _dot(jnp.exp2(sc - m).astype(BF16), vals)
        o = ol[:, :LANE] / (ol[:, LANE:] + jnp.exp2(snk - m))
        for j in range(GQA_GROUP):
            o_ref[i * blk:(i + 1) * blk, j * LANE:(j + 1) * LANE] = o[j * blk:(j + 1) * blk, :].astype(BF16)


def _gqa_ctx_kernel(q_ref, kx_ref, vx_ref, sink_ref, y_ref, o_ref):
    del y_ref
    g = pl.program_id(1)
    kx, vx = kx_ref[...], vx_ref[...]
    for j in range(GQA_GROUP):
        sc = _dot_nt(q_ref[:, j * LANE:(j + 1) * LANE], kx)
        snk = _sink_rows(sink_ref, g, j, sc.shape[0])
        m = jnp.maximum(jnp.max(sc, axis=-1, keepdims=True), snk)
        p_c = jnp.exp2(sc - m)
        l = jnp.sum(p_c, axis=-1, keepdims=True) + jnp.exp2(snk - m)
        o_ref[:, j * LANE:(j + 1) * LANE] = (_dot(p_c.astype(BF16), vx) / l).astype(BF16)


def _gqa_attn(gq, gk, px, o_gv, sink, b, s, c, with_ctx):
    mx = b * s
    blk = ATT_BLOCK
    n_lat = s // blk
    nstep = n_lat // GQA_QB
    rows = mx + (b * c if with_ctx else 0)
    ctx0 = mx // c
    gvc = o_gv // LANE
    gw = GQA_GROUP * GQA_HEAD_DIM
    big = GQA_QB * blk

    def edge(shift, col0):
        def idx(bi, g, n):
            return bi * n_lat + jnp.clip(n * GQA_QB + shift, 0, n_lat - 1), col0 + g
        return pl.BlockSpec((blk, LANE), idx)

    main = lambda col0: pl.BlockSpec((big, LANE), lambda bi, g, n: (bi * nstep + n, col0 + g))
    ctx = lambda col0: pl.BlockSpec((c, LANE), lambda bi, g, n: (ctx0 + bi, col0 + g))
    sink_spec = pl.BlockSpec((GQA_HEADS, LANE), lambda *_: (0, 0))
    y = pl.pallas_call(
        functools.partial(_gqa_kernel, n_lat=n_lat),
        grid=(b, GQA_KV_HEADS, nstep),
        in_specs=[pl.BlockSpec((big, gw), lambda bi, g, n: (bi * nstep + n, g)),
                  edge(-1, 0), main(0), edge(GQA_QB, 0), ctx(0),
                  edge(-1, gvc), main(gvc), edge(GQA_QB, gvc), ctx(gvc), sink_spec],
        out_specs=pl.BlockSpec((big, gw), lambda bi, g, n: (bi * nstep + n, g)),
        out_shape=jax.ShapeDtypeStruct((rows, GQA_HEADS * GQA_HEAD_DIM), BF16),
        compiler_params=_params(("parallel", "parallel", "arbitrary")),
        name="gqa_attn",
    )(gq, gk, gk, gk, gk, px, px, px, px, sink)
    if not with_ctx:
        return y
    return pl.pallas_call(
        _gqa_ctx_kernel,
        grid=(b, GQA_KV_HEADS),
        in_specs=[pl.BlockSpec((c, gw), lambda bi, g: (ctx0 + bi, g)),
                  pl.BlockSpec((c, LANE), lambda bi, g: (ctx0 + bi, g)),
                  pl.BlockSpec((c, LANE), lambda bi, g: (ctx0 + bi, gvc + g)),
                  sink_spec, pl.BlockSpec(memory_space=pl.ANY)],
        out_specs=pl.BlockSpec((c, gw), lambda bi, g: (ctx0 + bi, g)),
        out_shape=jax.ShapeDtypeStruct(y.shape, BF16),
        input_output_aliases={4: 0},
        compiler_params=_params(("parallel", "parallel")),
        name="gqa_attn_ctx",
    )(gq, gk, px, sink, y)


def _merge_kernel(yc_ref, ym_ref, yg_ref, gc_ref, gm_ref, gg_ref, wc_ref, wm_ref, wg_ref, o_ref):
    gate = lambda ref: jax.nn.sigmoid(ref[...].astype(F32))
    acc = gate(gc_ref) * _dot(yc_ref[...], wc_ref[...])
    acc += gate(gm_ref) * _dot(ym_ref[...], wm_ref[...])
    acc += gate(gg_ref) * _dot(yg_ref[...], wg_ref[...])
    o_ref[...] = acc.astype(BF16)


def _merge(yc, ym, yg, px, o_gate, w_bc, w_bm, w_bg, l, rows, s):
    d = w_bc.shape[2]
    bm = _pick(1024, s, rows)
    bn = _pick(512, d, o_gate)
    g0 = o_gate // bn
    per = d // bn
    yspec = lambda a: pl.BlockSpec((bm, a.shape[1]), lambda i, j: (i, 0))
    gspec = lambda k: pl.BlockSpec((bm, bn), lambda i, j: (i, g0 + k * per + j))
    wspec = lambda w: pl.BlockSpec((None, w.shape[1], bn), lambda i, j: (l, 0, j))
    return pl.pallas_call(
        _merge_kernel,
        grid=(rows // bm, d // bn),
        in_specs=[yspec(yc), yspec(ym), yspec(yg), gspec(0), gspec(1), gspec(2),
                  wspec(w_bc), wspec(w_bm), wspec(w_bg)],
        out_specs=pl.BlockSpec((bm, bn), lambda i, j: (i, j)),
        out_shape=jax.ShapeDtypeStruct((rows, d), BF16),
        compiler_params=_params(("parallel", "arbitrary")),
        name="branch_merge",
    )(yc, ym, yg, px, px, px, w_bc, w_bm, w_bg)


def _out_kernel(m_ref, w_ref, h_ref, mod_ref, o_ref):
    o_ref[...] = h_ref[...] + mod_ref[0, 5:6, :] * _dot(m_ref[...], w_ref[...])


def _out_proj(merged, w_out, l, h, mods, rows, s, b):
    d = w_out.shape[2]
    bm = _pick(1024, s, rows)
    bn = _pick(512, d)
    per = s // bm
    return pl.pallas_call(
        _out_kernel,
        grid=(rows // bm, d // bn),
        in_specs=[pl.BlockSpec((bm, d), lambda i, j: (i, 0)),
                  pl.BlockSpec((None, d, bn), lambda i, j: (l, 0, j)),
                  pl.BlockSpec((bm, bn), lambda i, j: (i, j)),
                  pl.BlockSpec((1, N_MOD, bn), lambda i, j: (jnp.minimum(i // per, b), 0, j))],
        out_specs=pl.BlockSpec((bm, bn), lambda i, j: (i, j)),
        out_shape=jax.ShapeDtypeStruct((rows, d), F32),
        compiler_params=_params(("parallel", "arbitrary")),
        name="out_proj",
    )(merged, w_out, h, mods)


def _rope_table(s, n_ctx_rows, dim):
    t = jnp.arange(s, dtype=jnp.int32)
    pos = jnp.stack([t // GRID_W, t % GRID_W], axis=1).astype(F32)
    inv = ROPE_BASE ** (-jnp.arange(0, dim, 2, dtype=F32) / dim)
    lane = jnp.arange(LANE)
    ang = pos[:, jnp.minimum(lane // dim, 1)] * inv[lane % (dim // 2)][None, :]
    active = (lane < 2 * dim)[None, :]
    sign = jnp.where((lane % dim) < dim // 2, -1.0, 1.0)[None, :]
    cos = jnp.where(active, jnp.cos(ang), 1.0)
    sin = jnp.where(active, jnp.sin(ang) * sign, 0.0)
    pad = ((0, n_ctx_rows), (0, 0))
    return jnp.pad(cos, pad, constant_values=1.0), jnp.pad(sin, pad)


def _layout(cw, ql, kvl, d):
    lay = {"conv": 0}
    off = 3 * cw
    for name, width in (("gq", GQA_HEADS * GQA_HEAD_DIM), ("cq", ql), ("ckv", kvl),
                        ("gk", GQA_KV_HEADS * GQA_HEAD_DIM), ("gv", GQA_KV_HEADS * GQA_HEAD_DIM),
                        ("kr", LANE)):
        lay[name] = off
        off += width
    off = -(-off // 512) * 512
    lay["gate"] = off
    lay["total"] = off + 3 * d
    return lay


def _pack_kernel(w_ref, o_ref, *, moves, zero):
    for dst, src, width in moves:
        o_ref[dst:dst + width, :] = w_ref[src:src + width, :].astype(BF16)
    o_ref[zero[0]:zero[1], :] = jnp.zeros((zero[1] - zero[0], o_ref.shape[1]), BF16)


def _pack_w_in(w, lay, cw, ql, kvl):
    depth, d, n = w.shape
    w_t = jnp.swapaxes(w, 1, 2)
    o_mla = 3 * cw
    o_gqa = o_mla + ql + kvl + MLA_ROPE
    gqw, gkw = GQA_HEADS * GQA_HEAD_DIM, GQA_KV_HEADS * GQA_HEAD_DIM
    o_gate = o_gqa + gqw + 2 * gkw
    moves = ((0, 0, o_mla), (lay["gq"], o_gqa, gqw), (lay["cq"], o_mla, ql + kvl),
             (lay["gk"], o_gqa + gqw, 2 * gkw), (lay["kr"], o_mla + ql + kvl, MLA_ROPE),
             (lay["gate"], o_gate, n - o_gate))
    cb = _pick(256, d)
    return pl.pallas_call(
        functools.partial(_pack_kernel, moves=moves, zero=(lay["kr"] + MLA_ROPE, lay["gate"])),
        grid=(depth, d // cb),
        in_specs=[pl.BlockSpec((None, n, cb), lambda l, j: (l, 0, j))],
        out_specs=pl.BlockSpec((None, lay["total"], cb), lambda l, j: (l, 0, j)),
        out_shape=jax.ShapeDtypeStruct((depth, lay["total"], d), BF16),
        compiler_params=_params(("parallel", "parallel")),
        name="pack_w_in",
    )(w_t)


def kernel(x, c, ctx, c_ctx, ada_w, ada_b, ffn1_norm, ffn1_w_gu, ffn1_w_down, mix_norm, w_in, conv_w,
           mla_q_norm, mla_w_qb, mla_kv_norm, mla_w_kvb, gqa_sink, w_branch_conv, w_branch_mla,
           w_branch_gqa, w_out, ffn2_norm, ffn2_w_gu, ffn2_w_down, final_norm):
    b, s, d = x.shape
    cl = ctx.shape[1]
    depth = ada_w.shape[0]
    cw = conv_w.shape[-1]
    ql, kvl = mla_q_norm.shape[-1], mla_kv_norm.shape[-1]
    mx, mc = b * s, b * cl
    assert b + 1 <= 8 and s % cl == 0 and cl % ATT_BLOCK == 0

    cvec = jnp.zeros((8, d), F32).at[:b].set(c).at[b].set(c_ctx)
    mods = _mods(cvec, ada_w, ada_b).reshape(depth, 8, N_MOD, d)

    lay = _layout(cw, ql, kvl, d)
    tabs = _rope_table(s, mc, MLA_ROPE // 2) + _rope_table(s, mc, GQA_HEAD_DIM // 2)
    h = None
    bf = _pick(512, ffn1_w_down.shape[1])
    w1_gu, w1_dn = _gu_to_bf16(ffn1_w_gu, bf), _to_bf16(ffn1_w_down)
    w2_gu, w2_dn = _gu_to_bf16(ffn2_w_gu, bf), _to_bf16(ffn2_w_down)
    w_bc, w_bm, w_bg = _to_bf16(w_branch_conv), _to_bf16(w_branch_mla), _to_bf16(w_branch_gqa)
    w_o = _to_bf16(w_out)
    w_p = _pack_w_in(w_in, lay, cw, ql, kvl)

    for l in range(depth):
        with_ctx = l < depth - 1
        rows = mx + mc if with_ctx else mx
        wqb_p = jnp.pad(mla_w_qb[l].reshape(ql, MLA_HEADS, MLA_NOPE + MLA_ROPE),
                        ((0, 0), (0, 0), (0, MLA_QK - MLA_NOPE - MLA_ROPE))).reshape(ql, -1).astype(BF16)
        wkv = mla_w_kvb[l].reshape(kvl, MLA_HEADS, MLA_NOPE + MLA_V)
        wkvb_r = jnp.concatenate([wkv[:, :, :MLA_NOPE].reshape(kvl, -1), wkv[:, :, MLA_NOPE:].reshape(kvl, -1)],
                                 axis=1).astype(BF16)
        sink = jnp.broadcast_to((gqa_sink[l].astype(F32) * LOG2E)[:, None], (GQA_HEADS, LANE))

        if l == 0:
            h = _ffn(x.reshape(mx, d), ctx.reshape(mc, d), mx + mc, mods[l], ffn1_norm[l], w1_gu, w1_dn, l, 0,
                     s, b, final_norm, False)
        else:
            h = _ffn(h, None, mx + mc, mods[l], ffn1_norm[l], w1_gu, w1_dn, l, 0, s, b, final_norm, False)
        px = _proj(h, mods[l], mix_norm[l], w_p, l, s, b)
        y_conv = _conv(px, conv_w[l], rows, s, cl, mx)
        q, k, v, gq, gk = _prep(px, lay, mla_q_norm[l], mla_kv_norm[l], wqb_p, wkvb_r, tabs, s, mx)
        y_mla = _mla_attn(q, k, v, b, s, cl, with_ctx)
        y_gqa = _gqa_attn(gq, gk, px, lay["gv"], sink, b, s, cl, with_ctx)
        merged = _merge(y_conv, y_mla, y_gqa, px, lay["gate"], w_bc, w_bm, w_bg, l, rows, s)
        h = _out_proj(merged, w_o, l, h, mods[l], rows, s, b)
        h = _ffn(h, None, rows, mods[l], ffn2_norm[l], w2_gu, w2_dn, l, 6, s, b, final_norm, not with_ctx)
    return h.reshape(b, s, d)
```

```python
import functools

import jax
import jax.numpy as jnp
from jax import lax
from jax.experimental import pallas as pl
from jax.experimental.pallas import tpu as pltpu

F32 = jnp.float32
BF16 = jnp.bfloat16

GRID_W = 64
N_MOD = 9
EPS = 1e-6
ROPE_BASE = 10000.0
CONV_K = 3
MLA_HEADS = 8
MLA_NOPE = 128
MLA_ROPE = 64
MLA_V = 128
MLA_SCALE = (MLA_NOPE + MLA_ROPE) ** -0.5
GQA_HEADS = 8
GQA_KV_HEADS = 2
GQA_GROUP = GQA_HEADS // GQA_KV_HEADS
GQA_HEAD_DIM = 128
GQA_SCALE = GQA_HEAD_DIM ** -0.5
WINDOW = 128
ATT_BLOCK = 128

LANE = 128
MLA_QK = 2 * LANE
MLA_SUB = 256
LOG2E = 1.4426950408889634
VMEM_LIMIT = 56 << 20
FFN_ROWS = 1024
NT_DIMS = (((1,), (1,)), ((), ()))


def _pick(target, *sizes):
    b = target
    while any(s % b for s in sizes):
        b //= 2
        assert b >= 8, (target, sizes)
    return b


def _params(sem):
    return pltpu.CompilerParams(dimension_semantics=sem, vmem_limit_bytes=VMEM_LIMIT)


def _dot(a, b):
    return jnp.dot(a, b, preferred_element_type=F32)


def _dot_nt(a, b):
    return lax.dot_general(a, b, NT_DIMS, preferred_element_type=F32)


def _rms(x, g):
    return x * lax.rsqrt(jnp.mean(x * x, axis=-1, keepdims=True) + EPS) * g


CAST_BLOCK_BYTES = 6 << 20


def _cast_kernel(w_ref, o_ref):
    o_ref[...] = w_ref[...].astype(BF16)


def _to_bf16(w):
    depth, k, n = w.shape
    rows = depth * k
    rb = _pick(max(16, 1 << ((CAST_BLOCK_BYTES // (4 * n)).bit_length() - 1)), rows)
    out = pl.pallas_call(
        _cast_kernel,
        grid=(rows // rb,),
        in_specs=[pl.BlockSpec((rb, n), lambda i: (i, 0))],
        out_specs=pl.BlockSpec((rb, n), lambda i: (i, 0)),
        out_shape=jax.ShapeDtypeStruct((rows, n), BF16),
        compiler_params=_params(("parallel",)),
        name="cast_bf16",
    )(w.reshape(rows, n))
    return out.reshape(depth, k, n)


def _cast_gu_kernel(w_ref, o_ref, *, bf):
    nf = o_ref.shape[0]
    for j in range(nf):
        o_ref[j, :, :bf] = w_ref[:, j * bf:(j + 1) * bf].astype(BF16)
        o_ref[j, :, bf:] = w_ref[:, (nf + j) * bf:(nf + j + 1) * bf].astype(BF16)


def _gu_to_bf16(w, bf):
    depth, k, n = w.shape
    nf = n // (2 * bf)
    rb = _pick(max(16, 1 << ((CAST_BLOCK_BYTES // (4 * n)).bit_length() - 1)), k)
    kb = k // rb
    return pl.pallas_call(
        functools.partial(_cast_gu_kernel, bf=bf),
        grid=(depth, kb),
        in_specs=[pl.BlockSpec((None, rb, n), lambda l, i: (l, i, 0))],
        out_specs=pl.BlockSpec((None, nf, rb, 2 * bf), lambda l, i: (l, 0, i, 0)),
        out_shape=jax.ShapeDtypeStruct((depth, nf, k, 2 * bf), BF16),
        compiler_params=_params(("parallel", "parallel")),
        name="cast_gate_up",
    )(w)


def _mods_kernel(c_ref, w_ref, b_ref, o_ref):
    c = c_ref[...]
    s = (c * jax.nn.sigmoid(c)).astype(BF16)
    o_ref[0] = _dot(s, w_ref[0].astype(BF16)) + b_ref[0]


def _mods(cvec, ada_w, ada_b):
    depth, d, n = ada_w.shape
    bn = _pick(1024, n)
    return pl.pallas_call(
        _mods_kernel,
        grid=(depth, n // bn),
        in_specs=[pl.BlockSpec((8, d), lambda l, j: (0, 0)),
                  pl.BlockSpec((1, d, bn), lambda l, j: (l, 0, j)),
                  pl.BlockSpec((1, 1, bn), lambda l, j: (l, 0, j))],
        out_specs=pl.BlockSpec((1, 8, bn), lambda l, j: (l, 0, j)),
        out_shape=jax.ShapeDtypeStruct((depth, 8, n), F32),
        compiler_params=_params(("parallel", "parallel")),
        name="adaln_mods",
    )(cvec, ada_w, ada_b.reshape(depth, 1, n))


def _norm_mod_rows(h_ref, mod_ref, g_ref, xn_ref, copy_ref, i_shift, i_scale, rc):
    shift = mod_ref[0, i_shift:i_shift + 1, :]
    scale1 = 1.0 + mod_ref[0, i_scale:i_scale + 1, :]
    g = g_ref[...]

    def body(r, carry):
        rows = pl.ds(pl.multiple_of(r * rc, rc), rc)
        x = h_ref[rows, :]
        xn_ref[rows, :] = (_rms(x, g) * scale1 + shift).astype(BF16)
        if copy_ref is not None:
            copy_ref[rows, :] = x
        return carry

    lax.fori_loop(0, h_ref.shape[0] // rc, body, 0)


def _norm_mod_chunk(hn_ref, modn_ref, g_ref, xn_ref, slot, k, ch, i_shift, i_scale):
    r0 = pl.multiple_of(jnp.minimum(k * ch, hn_ref.shape[0] - ch), 16)
    shift = modn_ref[0, i_shift:i_shift + 1, :]
    scale1 = 1.0 + modn_ref[0, i_scale:i_scale + 1, :]
    x = hn_ref[pl.ds(r0, ch), :]
    xn_ref[slot, pl.ds(r0, ch), :] = (_rms(x, g_ref[...]) * scale1 + shift).astype(BF16)


def _chunk_rows(bm, steps):
    return min(bm, -(-bm // (16 * steps)) * 16)


def _mod_spec(d, bm, s, b, ahead=0, last=None):
    per = s // bm

    def idx(i, j):
        t = i if not ahead else jnp.minimum(i + ahead, last)
        return jnp.minimum(t // per, b), 0, 0

    return pl.BlockSpec((1, N_MOD, d), idx)


def _ffn_math(xn, mod_ref, wgu_ref, wd_ref, i0):
    gu = _dot(xn, wgu_ref[...])
    bf = gu.shape[1] // 2
    gg, uu = gu[:, :bf], gu[:, bf:]
    a = (gg * jax.nn.sigmoid(gg) * uu).astype(BF16)
    return (0.5 * mod_ref[0, i0 + 2:i0 + 3, :]) * _dot(a, wd_ref[...])


def _ffn_split_kernel(h_ref, hc_ref, mod_ref, g_ref, wgu_ref, wd_ref, o_ref, xn_ref, *, i0, rc, n_lat):
    first = pl.program_id(1) == 0
    is_lat = pl.program_id(0) < n_lat

    @pl.when(first & is_lat)
    def _():
        _norm_mod_rows(h_ref, mod_ref, g_ref, xn_ref, o_ref, i0, i0 + 1, rc)

    @pl.when(first & jnp.logical_not(is_lat))
    def _():
        _norm_mod_rows(hc_ref, mod_ref, g_ref, xn_ref, o_ref, i0, i0 + 1, rc)

    o_ref[...] += _ffn_math(xn_ref[...], mod_ref, wgu_ref, wd_ref, i0)


def _ffn_kernel(h_ref, mod_ref, g_ref, wgu_ref, wd_ref, fg_ref, o_ref, xn_ref, *, i0, rc, final):
    @pl.when(pl.program_id(1) == 0)
    def _():
        _norm_mod_rows(h_ref, mod_ref, g_ref, xn_ref, o_ref, i0, i0 + 1, rc)

    o_ref[...] += _ffn_math(xn_ref[...], mod_ref, wgu_ref, wd_ref, i0)

    if final:
        @pl.when(pl.program_id(1) == pl.num_programs(1) - 1)
        def _():
            fg = fg_ref[...]

            def body(r, carry):
                rows = pl.ds(pl.multiple_of(r * rc, rc), rc)
                o_ref[rows, :] = _rms(o_ref[rows, :], fg)
                return carry

            lax.fori_loop(0, o_ref.shape[0] // rc, body, 0)


def _ffn(h, hc, rows, mods, norm_g, w_gu, w_down, l, i0, s, b, final_g, final):
    d = h.shape[1]
    nf, bf = w_gu.shape[1], w_gu.shape[3] // 2
    bm = _pick(FFN_ROWS, s, rows) if hc is None else _pick(FFN_ROWS, s, hc.shape[0])
    nt = rows // bm
    w_specs = [pl.BlockSpec((None, None, d, 2 * bf), lambda i, j: (l, j, 0, 0)),
               pl.BlockSpec((None, bf, d), lambda i, j: (l, j, 0))]
    vec = pl.BlockSpec((1, d), lambda i, j: (0, 0))
    once = functools.partial(pl.BlockSpec, (bm, d), pipeline_mode=pl.Buffered(1))
    common = dict(
        grid=(nt, nf),
        out_specs=pl.BlockSpec((bm, d), lambda i, j: (i, 0)),
        out_shape=jax.ShapeDtypeStruct((rows, d), F32),
        scratch_shapes=[pltpu.VMEM((bm, d), BF16)],
        compiler_params=_params(("parallel", "arbitrary")),
        name="ffn_swiglu")
    if hc is not None:
        n_lat = h.shape[0] // bm
        return pl.pallas_call(
            functools.partial(_ffn_split_kernel, i0=i0, rc=_pick(128, bm), n_lat=n_lat),
            in_specs=[once(lambda i, j: (jnp.minimum(i, n_lat - 1), 0)),
                      once(lambda i, j: (jnp.maximum(i - n_lat, 0), 0)),
                      _mod_spec(d, bm, s, b), vec] + w_specs,
            **common,
        )(h, hc, mods, norm_g.reshape(1, d), w_gu, w_down)
    return pl.pallas_call(
        functools.partial(_ffn_kernel, i0=i0, rc=_pick(128, bm), final=final),
        in_specs=[once(lambda i, j: (i, 0)), _mod_spec(d, bm, s, b), vec] + w_specs + [vec],
        **common,
    )(h, mods, norm_g.reshape(1, d), w_gu, w_down, final_g.reshape(1, d))


def _proj_kernel(h_ref, mod_ref, modn_ref, g_ref, w_ref, o_ref, xn_ref, *, rc, ch):
    i, j = pl.program_id(0), pl.program_id(1)
    slot = i % 2

    @pl.when((j == 0) & (i == 0))
    def _():
        _norm_mod_rows(h_ref, mod_ref, g_ref, xn_ref.at[0], None, 3, 4, rc)

    o_ref[...] = _dot_nt(xn_ref[slot], w_ref[...]).astype(BF16)
    _norm_mod_chunk(h_ref, modn_ref, g_ref, xn_ref, 1 - slot, jnp.maximum(j - 1, 0), ch, 3, 4)


def _proj(h, mods, norm_g, w_p, l, s, b):
    rows, d = h.shape
    n = w_p.shape[1]
    bm = _pick(1024, s, rows)
    bn = _pick(512, n)
    nt = rows // bm
    assert n // bn >= 2
    h_idx = lambda i, j: (jnp.where((i == 0) & (j == 0), 0, jnp.minimum(i + 1, nt - 1)), 0)
    return pl.pallas_call(
        functools.partial(_proj_kernel, rc=_pick(128, bm), ch=_chunk_rows(bm, n // bn - 1)),
        grid=(nt, n // bn),
        in_specs=[pl.BlockSpec((bm, d), h_idx),
                  _mod_spec(d, bm, s, b), _mod_spec(d, bm, s, b, 1, nt - 1),
                  pl.BlockSpec((1, d), lambda i, j: (0, 0)),
                  pl.BlockSpec((None, bn, d), lambda i, j: (l, j, 0))],
        out_specs=pl.BlockSpec((bm, bn), lambda i, j: (i, j)),
        out_shape=jax.ShapeDtypeStruct((rows, n), BF16),
        scratch_shapes=[pltpu.VMEM((2, bm, d), BF16)],
        compiler_params=_params(("arbitrary", "arbitrary")),
        name="in_proj",
    )(h, mods, mods, norm_g.reshape(1, d), w_p)


def _conv_kernel(gb_ref, gc_ref, v_ref, gcp_ref, vp_ref, gcn_ref, vn_ref, w_ref, o_ref, *, bm, s, c, mx):
    row0 = pl.program_id(0) * bm
    is_lat = row0 < mx
    at_start = jnp.where(is_lat, row0 % s == 0, (row0 - mx) % c == 0)
    at_end = jnp.where(is_lat, (row0 + bm) % s == 0, (row0 - mx + bm) % c == 0)
    cv = gc_ref[...].astype(F32) * v_ref[...].astype(F32)
    hp = (gcp_ref[...].astype(F32) * vp_ref[...].astype(F32))[15:16, :]
    hn = (gcn_ref[...].astype(F32) * vn_ref[...].astype(F32))[0:1, :]
    hp = jnp.where(at_start, 0.0, hp)
    hn = jnp.where(at_end, 0.0, hn)
    rid = lax.broadcasted_iota(jnp.int32, (bm, 1), 0)
    prev = jnp.where(rid == 0, hp, pltpu.roll(cv, 1, 0))
    nxt = jnp.where(rid == bm - 1, hn, pltpu.roll(cv, bm - 1, 0))
    w = w_ref[...]
    y = gb_ref[...].astype(F32) * (prev * w[0:1, :] + cv * w[1:2, :] + nxt * w[2:3, :])
    o_ref[...] = y.astype(BF16)


def _conv(px, conv_w, rows, s, c, mx):
    cw = conv_w.shape[1]
    m = px.shape[0]
    bm = _pick(256, s, c)
    hb = bm // 16
    main = lambda col: pl.BlockSpec((bm, cw), lambda i: (i, col))
    prev = lambda col: pl.BlockSpec((16, cw), lambda i: (jnp.maximum(i * hb - 1, 0), col))
    nxt = lambda col: pl.BlockSpec((16, cw), lambda i: (jnp.minimum((i + 1) * hb, m // 16 - 1), col))
    return pl.pallas_call(
        functools.partial(_conv_kernel, bm=bm, s=s, c=c, mx=mx),
        grid=(rows // bm,),
        in_specs=[main(0), main(1), main(2), prev(1), prev(2), nxt(1), nxt(2),
                  pl.BlockSpec((CONV_K, cw), lambda i: (0, 0))],
        out_specs=pl.BlockSpec((bm, cw), lambda i: (i, 0)),
        out_shape=jax.ShapeDtypeStruct((rows, cw), BF16),
        compiler_params=_params(("parallel",)),
        name="gated_conv",
    )(px, px, px, px, px, px, px, conv_w)


def _rope(x, cos, sin, half):
    lane = lax.broadcasted_iota(jnp.int32, x.shape, 1)
    first = (lane % (2 * half)) < half
    rot = jnp.where(first, pltpu.roll(x, LANE - half, 1), pltpu.roll(x, half, 1))
    return x * cos + rot * sin


def _prep_kernel(gq_ref, cq_ref, ckv_ref, gk_ref, kr_ref, qn_ref, kvn_ref, wqb_ref, wkvb_ref,
                 mcos_ref, msin_ref, gcos_ref, gsin_ref, q_ref, k_ref, v_ref, gqo_ref, gko_ref):
    mcos, msin = mcos_ref[...], msin_ref[...]
    gcos, gsin = gcos_ref[...], gsin_ref[...]
    mh = MLA_ROPE // 4
    gh = GQA_HEAD_DIM // 4

    cqn = _rms(cq_ref[...].astype(F32), qn_ref[...]).astype(BF16)
    q = _dot(cqn, wqb_ref[...]) * (MLA_SCALE * LOG2E)
    ckvn = _rms(ckv_ref[...].astype(F32), kvn_ref[...]).astype(BF16)
    kv = _dot(ckvn, wkvb_ref[...])
    kr = _rope(kr_ref[...].astype(F32), mcos, msin, mh).astype(BF16)
    ones = jnp.ones((q.shape[0], LANE), BF16)
    v0 = MLA_HEADS * MLA_NOPE
    for h in range(MLA_HEADS):
        a = h * MLA_QK
        q_ref[:, a:a + LANE] = q[:, a:a + LANE].astype(BF16)
        q_ref[:, a + LANE:a + MLA_QK] = _rope(q[:, a + LANE:a + MLA_QK], mcos, msin, mh).astype(BF16)
        k_ref[:, a:a + LANE] = kv[:, h * LANE:(h + 1) * LANE].astype(BF16)
        k_ref[:, a + LANE:a + MLA_QK] = kr
        v_ref[:, a:a + LANE] = kv[:, v0 + h * LANE:v0 + (h + 1) * LANE].astype(BF16)
        v_ref[:, a + LANE:a + MLA_QK] = ones

    gq = gq_ref[...].astype(F32)
    for h in range(GQA_HEADS):
        a = h * GQA_HEAD_DIM
        gqo_ref[:, a:a + LANE] = (_rope(gq[:, a:a + LANE], gcos, gsin, gh) * (GQA_SCALE * LOG2E)).astype(BF16)
    gk = gk_ref[...].astype(F32)
    for h in range(GQA_KV_HEADS):
        a = h * GQA_HEAD_DIM
        gko_ref[:, a:a + LANE] = _rope(gk[:, a:a + LANE], gcos, gsin, gh).astype(BF16)


def _prep(px, lay, q_norm, kv_norm, wqb_p, wkvb_r, tabs, s, mx):
    m = px.shape[0]
    bm = _pick(512, s, m - mx)
    n_lat = mx // bm
    per = s // bm
    ql, kvl = q_norm.shape[0], kv_norm.shape[0]
    gqw, gkw = GQA_HEADS * GQA_HEAD_DIM, GQA_KV_HEADS * GQA_HEAD_DIM

    def col(width, off):
        assert off % width == 0, (width, off)
        return pl.BlockSpec((bm, width), lambda i: (i, off // width))

    const = lambda r, c: pl.BlockSpec((r, c), lambda i: (0, 0))
    tab = pl.BlockSpec((bm, LANE), lambda i: (jnp.where(i < n_lat, i % per, per + i - n_lat), 0))
    row = lambda width: pl.BlockSpec((bm, width), lambda i: (i, 0))
    hq = MLA_HEADS * MLA_QK
    return pl.pallas_call(
        _prep_kernel,
        grid=(m // bm,),
        in_specs=[col(gqw, lay["gq"]), col(ql, lay["cq"]), col(kvl, lay["ckv"]), col(gkw, lay["gk"]),
                  col(LANE, lay["kr"]), const(1, ql), const(1, kvl), const(ql, hq),
                  const(kvl, MLA_HEADS * (MLA_NOPE + MLA_V)), tab, tab, tab, tab],
        out_specs=[row(hq), row(hq), row(hq), row(gqw), row(gkw)],
        out_shape=[jax.ShapeDtypeStruct((m, hq), BF16), jax.ShapeDtypeStruct((m, hq), BF16),
                   jax.ShapeDtypeStruct((m, hq), BF16),
                   jax.ShapeDtypeStruct((m, gqw), BF16), jax.ShapeDtypeStruct((m, gkw), BF16)],
        compiler_params=_params(("parallel",)),
        name="attn_prep",
    )(px, px, px, px, px, q_norm.reshape(1, ql), kv_norm.reshape(1, kvl), wqb_p, wkvb_r, *tabs)


def _mla_kernel(q_ref, kl_ref, kc_ref, vl_ref, vc_ref, o_ref, *, n_sub):
    kl, kc = kl_ref[...], kc_ref[...]
    subs = [q_ref[i * MLA_SUB:(i + 1) * MLA_SUB, :] for i in range(n_sub)]
    scores = [(_dot_nt(q, kl), _dot_nt(q, kc)) for q in subs]
    for i, (s1, s2) in enumerate(scores):
        m = jnp.maximum(jnp.max(s1, axis=-1, keepdims=True), jnp.max(s2, axis=-1, keepdims=True))
        p1 = jnp.exp2(s1 - m).astype(BF16)
        p2 = jnp.exp2(s2 - m).astype(BF16)
        o = _dot(p1, vl_ref[...]) + _dot(p2, vc_ref[...])
        o_ref[i * MLA_SUB:(i + 1) * MLA_SUB, :] = (o[:, :MLA_V] / o[:, MLA_V:]).astype(BF16)


def _mla_ctx_kernel(q_ref, kc_ref, vc_ref, y_ref, o_ref):
    del y_ref
    s2 = _dot_nt(q_ref[...], kc_ref[...])
    p2 = jnp.exp2(s2 - jnp.max(s2, axis=-1, keepdims=True)).astype(BF16)
    o = _dot(p2, vc_ref[...])
    o_ref[...] = (o[:, :MLA_V] / o[:, MLA_V:]).astype(BF16)


def _mla_attn(q, k, v, b, s, c, with_ctx):
    mx = b * s
    n_sub = max(d for d in (1, 2, 4) if (s // MLA_SUB) % d == 0)
    bq = n_sub * MLA_SUB
    nq = s // bq
    rows = mx + (b * c if with_ctx else 0)
    ctx0 = mx // c
    y = pl.pallas_call(
        functools.partial(_mla_kernel, n_sub=n_sub),
        grid=(b, MLA_HEADS, nq),
        in_specs=[pl.BlockSpec((bq, MLA_QK), lambda bi, h, qi: (bi * nq + qi, h)),
                  pl.BlockSpec((s, MLA_QK), lambda bi, h, qi: (bi, h)),
                  pl.BlockSpec((c, MLA_QK), lambda bi, h, qi: (ctx0 + bi, h)),
                  pl.BlockSpec((s, MLA_QK), lambda bi, h, qi: (bi, h)),
                  pl.BlockSpec((c, MLA_QK), lambda bi, h, qi: (ctx0 + bi, h))],
        out_specs=pl.BlockSpec((bq, MLA_V), lambda bi, h, qi: (bi * nq + qi, h)),
        out_shape=jax.ShapeDtypeStruct((rows, MLA_HEADS * MLA_V), BF16),
        compiler_params=_params(("parallel", "parallel", "arbitrary")),
        name="mla_attn",
    )(q, k, k, v, v)
    if not with_ctx:
        return y
    return pl.pallas_call(
        _mla_ctx_kernel,
        grid=(b, MLA_HEADS),
        in_specs=[pl.BlockSpec((c, MLA_QK), lambda bi, h: (ctx0 + bi, h)),
                  pl.BlockSpec((c, MLA_QK), lambda bi, h: (ctx0 + bi, h)),
                  pl.BlockSpec((c, MLA_QK), lambda bi, h: (ctx0 + bi, h)),
                  pl.BlockSpec(memory_space=pl.ANY)],
        out_specs=pl.BlockSpec((c, MLA_V), lambda bi, h: (ctx0 + bi, h)),
        out_shape=jax.ShapeDtypeStruct(y.shape, BF16),
        input_output_aliases={3: 0},
        compiler_params=_params(("parallel", "parallel")),
        name="mla_attn_ctx",
    )(q, k, v, y)


GQA_QB = 4


def _sink_rows(sink_ref, g, j, rows):
    return jnp.broadcast_to(sink_ref[pl.ds(g * GQA_GROUP + j, 1), :], (rows, LANE))[:, 0:1]


def _gqa_kernel(q_ref, kp_ref, kc_ref, kn_ref, kx_ref, vp_ref, vc_ref, vn_ref, vx_ref, sink_ref, o_ref,
                *, n_lat):
    g = pl.program_id(1)
    n = pl.program_id(2)
    blk = ATT_BLOCK
    kx, vx = kx_ref[...], vx_ref[...]
    kband = jnp.concatenate([kp_ref[...], kc_ref[...], kn_ref[...]], axis=0)
    vband = jnp.concatenate([vp_ref[...], vc_ref[...], vn_ref[...]], axis=0)
    nk = 3 * blk + kx.shape[0]
    ones = jnp.ones((nk, LANE), BF16)
    snk = jnp.concatenate([jnp.broadcast_to(sink_ref[pl.ds(g * GQA_GROUP + j, 1), :], (blk, LANE))
                           for j in range(GQA_GROUP)], axis=0)
    rows = GQA_GROUP * blk
    r = lax.broadcasted_iota(jnp.int32, (rows, nk), 0) % blk
    col = lax.broadcasted_iota(jnp.int32, (rows, nk), 1)
    in_window = (jnp.abs(col - blk - r) <= WINDOW) | (col >= 3 * blk)
    for i in range(GQA_QB):
        qs = jnp.concatenate([q_ref[i * blk:(i + 1) * blk, j * LANE:(j + 1) * LANE] for j in range(GQA_GROUP)],
                             axis=0)
        keys = jnp.concatenate([kband[i * blk:(i + 3) * blk, :], kx], axis=0)
        vals = jnp.concatenate([jnp.concatenate([vband[i * blk:(i + 3) * blk, :], vx], axis=0), ones], axis=1)
        block = n * GQA_QB + i
        valid = in_window & ((col >= blk) | (block > 0)) & ((col < 2 * blk) | (col >= 3 * blk) | (block < n_lat - 1))
        sc = jnp.where(valid, _dot_nt(qs, keys), -jnp.inf)
        e = snk
        for t in range(nk // LANE):
            e = jnp.maximum(e, sc[:, t * LANE:(t + 1) * LANE])
        m = jnp.max(e, axis=-1, keepdims=True)
        ol = _dot(jnp.exp2(sc - m).astype(BF16), vals)
        o = ol[:, :LANE] / (ol[:, LANE:] + jnp.exp2(snk - m))
        for j in range(GQA_GROUP):
            o_ref[i * blk:(i + 1) * blk, j * LANE:(j + 1) * LANE] = o[j * blk:(j + 1) * blk, :].astype(BF16)


def _gqa_ctx_kernel(q_ref, kx_ref, vx_ref, sink_ref, y_ref, o_ref):
    del y_ref
    g = pl.program_id(1)
    kx, vx = kx_ref[...], vx_ref[...]
    for j in range(GQA_GROUP):
        sc = _dot_nt(q_ref[:, j * LANE:(j + 1) * LANE], kx)
        snk = _sink_rows(sink_ref, g, j, sc.shape[0])
        m = jnp.maximum(jnp.max(sc, axis=-1, keepdims=True), snk)
        p_c = jnp.exp2(sc - m)
        l = jnp.sum(p_c, axis=-1, keepdims=True) + jnp.exp2(snk - m)
        o_ref[:, j * LANE:(j + 1) * LANE] = (_dot(p_c.astype(BF16), vx) / l).astype(BF16)


def _gqa_attn(gq, gk, px, o_gv, sink, b, s, c, with_ctx):
    mx = b * s
    blk = ATT_BLOCK
    n_lat = s // blk
    nstep = n_lat // GQA_QB
    rows = mx + (b * c if with_ctx else 0)
    ctx0 = mx // c
    gvc = o_gv // LANE
    gw = GQA_GROUP * GQA_HEAD_DIM
    big = GQA_QB * blk

    def edge(shift, col0):
        def idx(bi, g, n):
            return bi * n_lat + jnp.clip(n * GQA_QB + shift, 0, n_lat - 1), col0 + g
        return pl.BlockSpec((blk, LANE), idx)

    main = lambda col0: pl.BlockSpec((big, LANE), lambda bi, g, n: (bi * nstep + n, col0 + g))
    ctx = lambda col0: pl.BlockSpec((c, LANE), lambda bi, g, n: (ctx0 + bi, col0 + g))
    sink_spec = pl.BlockSpec((GQA_HEADS, LANE), lambda *_: (0, 0))
    y = pl.pallas_call(
        functools.partial(_gqa_kernel, n_lat=n_lat),
        grid=(b, GQA_KV_HEADS, nstep),
        in_specs=[pl.BlockSpec((big, gw), lambda bi, g, n: (bi * nstep + n, g)),
                  edge(-1, 0), main(0), edge(GQA_QB, 0), ctx(0),
                  edge(-1, gvc), main(gvc), edge(GQA_QB, gvc), ctx(gvc), sink_spec],
        out_specs=pl.BlockSpec((big, gw), lambda bi, g, n: (bi * nstep + n, g)),
        out_shape=jax.ShapeDtypeStruct((rows, GQA_HEADS * GQA_HEAD_DIM), BF16),
        compiler_params=_params(("parallel", "parallel", "arbitrary")),
        name="gqa_attn",
    )(gq, gk, gk, gk, gk, px, px, px, px, sink)
    if not with_ctx:
        return y
    return pl.pallas_call(
        _gqa_ctx_kernel,
        grid=(b, GQA_KV_HEADS),
        in_specs=[pl.BlockSpec((c, gw), lambda bi, g: (ctx0 + bi, g)),
                  pl.BlockSpec((c, LANE), lambda bi, g: (ctx0 + bi, g)),
                  pl.BlockSpec((c, LANE), lambda bi, g: (ctx0 + bi, gvc + g)),
                  sink_spec, pl.BlockSpec(memory_space=pl.ANY)],
        out_specs=pl.BlockSpec((c, gw), lambda bi, g: (ctx0 + bi, g)),
        out_shape=jax.ShapeDtypeStruct(y.shape, BF16),
        input_output_aliases={4: 0},
        compiler_params=_params(("parallel", "parallel")),
        name="gqa_attn_ctx",
    )(gq, gk, px, sink, y)


def _merge_kernel(yc_ref, ym_ref, yg_ref, gc_ref, gm_ref, gg_ref, wc_ref, wm_ref, wg_ref, o_ref):
    gate = lambda ref: jax.nn.sigmoid(ref[...].astype(F32))
    acc = gate(gc_ref) * _dot(yc_ref[...], wc_ref[...])
    acc += gate(gm_ref) * _dot(ym_ref[...], wm_ref[...])
    acc += gate(gg_ref) * _dot(yg_ref[...], wg_ref[...])
    o_ref[...] = acc.astype(BF16)


def _merge(yc, ym, yg, px, o_gate, w_bc, w_bm, w_bg, l, rows, s):
    d = w_bc.shape[2]
    bm = _pick(1024, s, rows)
    bn = _pick(512, d, o_gate)
    g0 = o_gate // bn
    per = d // bn
    yspec = lambda a: pl.BlockSpec((bm, a.shape[1]), lambda i, j: (i, 0))
    gspec = lambda k: pl.BlockSpec((bm, bn), lambda i, j: (i, g0 + k * per + j))
    wspec = lambda w: pl.BlockSpec((None, w.shape[1], bn), lambda i, j: (l, 0, j))
    return pl.pallas_call(
        _merge_kernel,
        grid=(rows // bm, d // bn),
        in_specs=[yspec(yc), yspec(ym), yspec(yg), gspec(0), gspec(1), gspec(2),
                  wspec(w_bc), wspec(w_bm), wspec(w_bg)],
        out_specs=pl.BlockSpec((bm, bn), lambda i, j: (i, j)),
        out_shape=jax.ShapeDtypeStruct((rows, d), BF16),
        compiler_params=_params(("parallel", "arbitrary")),
        name="branch_merge",
    )(yc, ym, yg, px, px, px, w_bc, w_bm, w_bg)


def _out_kernel(m_ref, w_ref, h_ref, mod_ref, o_ref):
    o_ref[...] = h_ref[...] + mod_ref[0, 5:6, :] * _dot(m_ref[...], w_ref[...])


def _out_proj(merged, w_out, l, h, mods, rows, s, b):
    d = w_out.shape[2]
    bm = _pick(1024, s, rows)
    bn = _pick(512, d)
    per = s // bm
    return pl.pallas_call(
        _out_kernel,
        grid=(rows // bm, d // bn),
        in_specs=[pl.BlockSpec((bm, d), lambda i, j: (i, 0)),
                  pl.BlockSpec((None, d, bn), lambda i, j: (l, 0, j)),
                  pl.BlockSpec((bm, bn), lambda i, j: (i, j)),
                  pl.BlockSpec((1, N_MOD, bn), lambda i, j: (jnp.minimum(i // per, b), 0, j))],
        out_specs=pl.BlockSpec((bm, bn), lambda i, j: (i, j)),
        out_shape=jax.ShapeDtypeStruct((rows, d), F32),
        compiler_params=_params(("parallel", "arbitrary")),
        name="out_proj",
    )(merged, w_out, h, mods)


def _rope_table(s, n_ctx_rows, dim):
    t = jnp.arange(s, dtype=jnp.int32)
    pos = jnp.stack([t // GRID_W, t % GRID_W], axis=1).astype(F32)
    inv = ROPE_BASE ** (-jnp.arange(0, dim, 2, dtype=F32) / dim)
    lane = jnp.arange(LANE)
    ang = pos[:, jnp.minimum(lane // dim, 1)] * inv[lane % (dim // 2)][None, :]
    active = (lane < 2 * dim)[None, :]
    sign = jnp.where((lane % dim) < dim // 2, -1.0, 1.0)[None, :]
    cos = jnp.where(active, jnp.cos(ang), 1.0)
    sin = jnp.where(active, jnp.sin(ang) * sign, 0.0)
    pad = ((0, n_ctx_rows), (0, 0))
    return jnp.pad(cos, pad, constant_values=1.0), jnp.pad(sin, pad)


def _layout(cw, ql, kvl, d):
    lay = {"conv": 0}
    off = 3 * cw
    for name, width in (("gq", GQA_HEADS * GQA_HEAD_DIM), ("cq", ql), ("ckv", kvl),
                        ("gk", GQA_KV_HEADS * GQA_HEAD_DIM), ("gv", GQA_KV_HEADS * GQA_HEAD_DIM),
                        ("kr", LANE)):
        lay[name] = off
        off += width
    off = -(-off // 512) * 512
    lay["gate"] = off
    lay["total"] = off + 3 * d
    return lay


def _pack_kernel(w_ref, o_ref, *, moves, zero):
    for dst, src, width in moves:
        o_ref[dst:dst + width, :] = w_ref[src:src + width, :].astype(BF16)
    o_ref[zero[0]:zero[1], :] = jnp.zeros((zero[1] - zero[0], o_ref.shape[1]), BF16)


def _pack_w_in(w, lay, cw, ql, kvl):
    depth, d, n = w.shape
    w_t = jnp.swapaxes(w, 1, 2)
    o_mla = 3 * cw
    o_gqa = o_mla + ql + kvl + MLA_ROPE
    gqw, gkw = GQA_HEADS * GQA_HEAD_DIM, GQA_KV_HEADS * GQA_HEAD_DIM
    o_gate = o_gqa + gqw + 2 * gkw
    moves = ((0, 0, o_mla), (lay["gq"], o_gqa, gqw), (lay["cq"], o_mla, ql + kvl),
             (lay["gk"], o_gqa + gqw, 2 * gkw), (lay["kr"], o_mla + ql + kvl, MLA_ROPE),
             (lay["gate"], o_gate, n - o_gate))
    cb = _pick(256, d)
    return pl.pallas_call(
        functools.partial(_pack_kernel, moves=moves, zero=(lay["kr"] + MLA_ROPE, lay["gate"])),
        grid=(depth, d // cb),
        in_specs=[pl.BlockSpec((None, n, cb), lambda l, j: (l, 0, j))],
        out_specs=pl.BlockSpec((None, lay["total"], cb), lambda l, j: (l, 0, j)),
        out_shape=jax.ShapeDtypeStruct((depth, lay["total"], d), BF16),
        compiler_params=_params(("parallel", "parallel")),
        name="pack_w_in",
    )(w_t)


def kernel(x, c, ctx, c_ctx, ada_w, ada_b, ffn1_norm, ffn1_w_gu, ffn1_w_down, mix_norm, w_in, conv_w,
           mla_q_norm, mla_w_qb, mla_kv_norm, mla_w_kvb, gqa_sink, w_branch_conv, w_branch_mla,
           w_branch_gqa, w_out, ffn2_norm, ffn2_w_gu, ffn2_w_down, final_norm):
    b, s, d = x.shape
    cl = ctx.shape[1]
    depth = ada_w.shape[0]
    cw = conv_w.shape[-1]
    ql, kvl = mla_q_norm.shape[-1], mla_kv_norm.shape[-1]
    mx, mc = b * s, b * cl
    assert b + 1 <= 8 and s % cl == 0 and cl % ATT_BLOCK == 0

    cvec = jnp.zeros((8, d), F32).at[:b].set(c).at[b].set(c_ctx)
    mods = _mods(cvec, ada_w, ada_b).reshape(depth, 8, N_MOD, d)

    lay = _layout(cw, ql, kvl, d)
    tabs = _rope_table(s, mc, MLA_ROPE // 2) + _rope_table(s, mc, GQA_HEAD_DIM // 2)
    h = None
    bf = _pick(512, ffn1_w_down.shape[1])
    w1_gu, w1_dn = _gu_to_bf16(ffn1_w_gu, bf), _to_bf16(ffn1_w_down)
    w2_gu, w2_dn = _gu_to_bf16(ffn2_w_gu, bf), _to_bf16(ffn2_w_down)
    w_bc, w_bm, w_bg = _to_bf16(w_branch_conv), _to_bf16(w_branch_mla), _to_bf16(w_branch_gqa)
    w_o = _to_bf16(w_out)
    w_p = _pack_w_in(w_in, lay, cw, ql, kvl)

    for l in range(depth):
        with_ctx = l < depth - 1
        rows = mx + mc if with_ctx else mx
        wqb_p = jnp.pad(mla_w_qb[l].reshape(ql, MLA_HEADS, MLA_NOPE + MLA_ROPE),
                        ((0, 0), (0, 0), (0, MLA_QK - MLA_NOPE - MLA_ROPE))).reshape(ql, -1).astype(BF16)
        wkv = mla_w_kvb[l].reshape(kvl, MLA_HEADS, MLA_NOPE + MLA_V)
        wkvb_r = jnp.concatenate([wkv[:, :, :MLA_NOPE].reshape(kvl, -1), wkv[:, :, MLA_NOPE:].reshape(kvl, -1)],
                                 axis=1).astype(BF16)
        sink = jnp.broadcast_to((gqa_sink[l].astype(F32) * LOG2E)[:, None], (GQA_HEADS, LANE))

        if l == 0:
            h = _ffn(x.reshape(mx, d), ctx.reshape(mc, d), mx + mc, mods[l], ffn1_norm[l], w1_gu, w1_dn, l, 0,
                     s, b, final_norm, False)
        else:
            h = _ffn(h, None, mx + mc, mods[l], ffn1_norm[l], w1_gu, w1_dn, l, 0, s, b, final_norm, False)
        px = _proj(h, mods[l], mix_norm[l], w_p, l, s, b)
        y_conv = _conv(px, conv_w[l], rows, s, cl, mx)
        q, k, v, gq, gk = _prep(px, lay, mla_q_norm[l], mla_kv_norm[l], wqb_p, wkvb_r, tabs, s, mx)
        y_mla = _mla_attn(q, k, v, b, s, cl, with_ctx)
        y_gqa = _gqa_attn(gq, gk, px, lay["gv"], sink, b, s, cl, with_ctx)
        merged = _merge(y_conv, y_mla, y_gqa, px, lay["gate"], w_bc, w_bm, w_bg, l, rows, s)
        h = _out_proj(merged, w_o, l, h, mods[l], rows, s, b)
        h = _ffn(h, None, rows, mods[l], ffn2_norm[l], w2_gu, w2_dn, l, 6, s, b, final_norm, not with_ctx)
    return h.reshape(b, s, d)
```

```python
import functools

import jax
import jax.numpy as jnp
from jax import lax
from jax.experimental import pallas as pl
from jax.experimental.pallas import tpu as pltpu

F32 = jnp.float32
BF16 = jnp.bfloat16

GRID_W = 64
N_MOD = 9
EPS = 1e-6
ROPE_BASE = 10000.0
CONV_K = 3
MLA_HEADS = 8
MLA_NOPE = 128
MLA_ROPE = 64
MLA_V = 128
MLA_SCALE = (MLA_NOPE + MLA_ROPE) ** -0.5
GQA_HEADS = 8
GQA_KV_HEADS = 2
GQA_GROUP = GQA_HEADS // GQA_KV_HEADS
GQA_HEAD_DIM = 128
GQA_SCALE = GQA_HEAD_DIM ** -0.5
WINDOW = 128
ATT_BLOCK = 128

LANE = 128
MLA_QK = 2 * LANE
MLA_SUB = 256
LOG2E = 1.4426950408889634
VMEM_LIMIT = 56 << 20
NT_DIMS = (((1,), (1,)), ((), ()))


def _pick(target, *sizes):
    b = target
    while any(s % b for s in sizes):
        b //= 2
        assert b >= 8, (target, sizes)
    return b


def _params(sem):
    return pltpu.CompilerParams(dimension_semantics=sem, vmem_limit_bytes=VMEM_LIMIT)


def _dot(a, b):
    return jnp.dot(a, b, preferred_element_type=F32)


def _dot_nt(a, b):
    return lax.dot_general(a, b, NT_DIMS, preferred_element_type=F32)


def _rms(x, g):
    return x * lax.rsqrt(jnp.mean(x * x, axis=-1, keepdims=True) + EPS) * g


CAST_BLOCK_BYTES = 6 << 20


def _cast_kernel(w_ref, o_ref):
    o_ref[...] = w_ref[...].astype(BF16)


def _to_bf16(w):
    depth, k, n = w.shape
    rows = depth * k
    rb = _pick(max(16, 1 << ((CAST_BLOCK_BYTES // (4 * n)).bit_length() - 1)), rows)
    out = pl.pallas_call(
        _cast_kernel,
        grid=(rows // rb,),
        in_specs=[pl.BlockSpec((rb, n), lambda i: (i, 0))],
        out_specs=pl.BlockSpec((rb, n), lambda i: (i, 0)),
        out_shape=jax.ShapeDtypeStruct((rows, n), BF16),
        compiler_params=_params(("parallel",)),
        name="cast_bf16",
    )(w.reshape(rows, n))
    return out.reshape(depth, k, n)


def _cast_gu_kernel(w_ref, o_ref, *, bf):
    nf = o_ref.shape[0]
    for j in range(nf):
        o_ref[j, :, :bf] = w_ref[:, j * bf:(j + 1) * bf].astype(BF16)
        o_ref[j, :, bf:] = w_ref[:, (nf + j) * bf:(nf + j + 1) * bf].astype(BF16)


def _gu_to_bf16(w, bf):
    depth, k, n = w.shape
    nf = n // (2 * bf)
    rb = _pick(max(16, 1 << ((CAST_BLOCK_BYTES // (4 * n)).bit_length() - 1)), k)
    kb = k // rb
    return pl.pallas_call(
        functools.partial(_cast_gu_kernel, bf=bf),
        grid=(depth, kb),
        in_specs=[pl.BlockSpec((None, rb, n), lambda l, i: (l, i, 0))],
        out_specs=pl.BlockSpec((None, nf, rb, 2 * bf), lambda l, i: (l, 0, i, 0)),
        out_shape=jax.ShapeDtypeStruct((depth, nf, k, 2 * bf), BF16),
        compiler_params=_params(("parallel", "parallel")),
        name="cast_gate_up",
    )(w)


def _mods_kernel(c_ref, w_ref, b_ref, o_ref):
    c = c_ref[...]
    s = (c * jax.nn.sigmoid(c)).astype(BF16)
    o_ref[0] = _dot(s, w_ref[0].astype(BF16)) + b_ref[0]


def _mods(cvec, ada_w, ada_b):
    depth, d, n = ada_w.shape
    bn = _pick(1024, n)
    return pl.pallas_call(
        _mods_kernel,
        grid=(depth, n // bn),
        in_specs=[pl.BlockSpec((8, d), lambda l, j: (0, 0)),
                  pl.BlockSpec((1, d, bn), lambda l, j: (l, 0, j)),
                  pl.BlockSpec((1, 1, bn), lambda l, j: (l, 0, j))],
        out_specs=pl.BlockSpec((1, 8, bn), lambda l, j: (l, 0, j)),
        out_shape=jax.ShapeDtypeStruct((depth, 8, n), F32),
        compiler_params=_params(("parallel", "parallel")),
        name="adaln_mods",
    )(cvec, ada_w, ada_b.reshape(depth, 1, n))


def _norm_mod_rows(h_ref, mod_ref, g_ref, xn_ref, copy_ref, i_shift, i_scale, rc):
    shift = mod_ref[0, i_shift:i_shift + 1, :]
    scale1 = 1.0 + mod_ref[0, i_scale:i_scale + 1, :]
    g = g_ref[...]

    def body(r, carry):
        rows = pl.ds(pl.multiple_of(r * rc, rc), rc)
        x = h_ref[rows, :]
        xn_ref[rows, :] = (_rms(x, g) * scale1 + shift).astype(BF16)
        if copy_ref is not None:
            copy_ref[rows, :] = x
        return carry

    lax.fori_loop(0, h_ref.shape[0] // rc, body, 0)


def _norm_mod_chunk(hn_ref, modn_ref, g_ref, xn_ref, slot, k, ch, i_shift, i_scale):
    r0 = pl.multiple_of(jnp.minimum(k * ch, hn_ref.shape[0] - ch), 16)
    shift = modn_ref[0, i_shift:i_shift + 1, :]
    scale1 = 1.0 + modn_ref[0, i_scale:i_scale + 1, :]
    x = hn_ref[pl.ds(r0, ch), :]
    xn_ref[slot, pl.ds(r0, ch), :] = (_rms(x, g_ref[...]) * scale1 + shift).astype(BF16)


def _chunk_rows(bm, steps):
    return min(bm, -(-bm // (16 * steps)) * 16)


def _mod_spec(d, bm, s, b, ahead=0, last=None):
    per = s // bm

    def idx(i, j):
        t = i if not ahead else jnp.minimum(i + ahead, last)
        return jnp.minimum(t // per, b), 0, 0

    return pl.BlockSpec((1, N_MOD, d), idx)


FFN_UP_ROWS = 1024
FFN_DOWN_ROWS = 256


def _swiglu(gu):
    bf = gu.shape[1] // 2
    gg, uu = gu[:, :bf], gu[:, bf:]
    return (gg * jax.nn.sigmoid(gg) * uu).astype(BF16)


def _up_split_kernel(h_ref, hc_ref, mod_ref, g_ref, w_ref, o_ref, xn_ref, *, i0, rc, n_lat):
    first = pl.program_id(1) == 0
    is_lat = pl.program_id(0) < n_lat

    @pl.when(first & is_lat)
    def _():
        _norm_mod_rows(h_ref, mod_ref, g_ref, xn_ref, None, i0, i0 + 1, rc)

    @pl.when(first & jnp.logical_not(is_lat))
    def _():
        _norm_mod_rows(hc_ref, mod_ref, g_ref, xn_ref, None, i0, i0 + 1, rc)

    o_ref[...] = _swiglu(_dot(xn_ref[...], w_ref[...]))


def _up_kernel(h_ref, mod_ref, modn_ref, g_ref, w_ref, o_ref, xn_ref, *, i0, rc, ch):
    i, j = pl.program_id(0), pl.program_id(1)
    slot = i % 2

    @pl.when((j == 0) & (i == 0))
    def _():
        _norm_mod_rows(h_ref, mod_ref, g_ref, xn_ref.at[0], None, i0, i0 + 1, rc)

    o_ref[...] = _swiglu(_dot(xn_ref[slot], w_ref[...]))
    _norm_mod_chunk(h_ref, modn_ref, g_ref, xn_ref, 1 - slot, jnp.maximum(j - 1, 0), ch, i0, i0 + 1)


def _down_kernel(a_ref, w_ref, h_ref, hc_ref, mod_ref, fg_ref, o_ref, *, i_gate, n_lat, final):
    h = h_ref[...]
    if n_lat is not None:
        h = jnp.where(pl.program_id(0) < n_lat, h, hc_ref[...])
    out = h + (0.5 * mod_ref[0, i_gate:i_gate + 1, :]) * _dot(a_ref[...], w_ref[...])
    o_ref[...] = _rms(out, fg_ref[...]) if final else out


def _ffn(h, hc, rows, mods, norm_g, w_gu, w_down, l, i0, s, b, final_g, final):
    d = h.shape[1]
    nf, bf = w_gu.shape[1], w_gu.shape[3] // 2
    f = nf * bf
    vec = pl.BlockSpec((1, d), lambda i, j: (0, 0))
    w_spec = pl.BlockSpec((None, None, d, 2 * bf), lambda i, j: (l, j, 0, 0))
    bm = _pick(FFN_UP_ROWS, s, rows) if hc is None else _pick(FFN_UP_ROWS, s, hc.shape[0])
    nt = rows // bm
    up_common = dict(
        grid=(nt, nf),
        out_specs=pl.BlockSpec((bm, bf), lambda i, j: (i, j)),
        out_shape=jax.ShapeDtypeStruct((rows, f), BF16),
        compiler_params=_params(("arbitrary", "arbitrary")),
        name="ffn_up")
    if hc is not None:
        n_lat = h.shape[0] // bm
        act = pl.pallas_call(
            functools.partial(_up_split_kernel, i0=i0, rc=_pick(128, bm), n_lat=n_lat),
            in_specs=[pl.BlockSpec((bm, d), lambda i, j: (jnp.minimum(i, n_lat - 1), 0)),
                      pl.BlockSpec((bm, d), lambda i, j: (jnp.maximum(i - n_lat, 0), 0)),
                      _mod_spec(d, bm, s, b), vec, w_spec],
            scratch_shapes=[pltpu.VMEM((bm, d), BF16)],
            **up_common,
        )(h, hc, mods, norm_g.reshape(1, d), w_gu)
    else:
        assert nf >= 2
        h_idx = lambda i, j: (jnp.where((i == 0) & (j == 0), 0, jnp.minimum(i + 1, nt - 1)), 0)
        act = pl.pallas_call(
            functools.partial(_up_kernel, i0=i0, rc=_pick(128, bm), ch=_chunk_rows(bm, nf - 1)),
            in_specs=[pl.BlockSpec((bm, d), h_idx), _mod_spec(d, bm, s, b), _mod_spec(d, bm, s, b, 1, nt - 1),
                      vec, w_spec],
            scratch_shapes=[pltpu.VMEM((2, bm, d), BF16)],
            **up_common,
        )(h, mods, mods, norm_g.reshape(1, d), w_gu)

    bd = _pick(FFN_DOWN_ROWS, s, rows) if hc is None else _pick(FFN_DOWN_ROWS, s, hc.shape[0])
    per = s // bd
    if hc is None:
        n_lat = None
        h_specs = [pl.BlockSpec((bd, d), lambda i: (i, 0)), pl.BlockSpec((1, d), lambda i: (0, 0))]
        hc = norm_g.reshape(1, d)
    else:
        n_lat = h.shape[0] // bd
        h_specs = [pl.BlockSpec((bd, d), lambda i: (jnp.minimum(i, n_lat - 1), 0)),
                   pl.BlockSpec((bd, d), lambda i: (jnp.maximum(i - n_lat, 0), 0))]
    return pl.pallas_call(
        functools.partial(_down_kernel, i_gate=i0 + 2, n_lat=n_lat, final=final),
        grid=(rows // bd,),
        in_specs=[pl.BlockSpec((bd, f), lambda i: (i, 0)),
                  pl.BlockSpec((None, f, d), lambda i: (l, 0, 0), pipeline_mode=pl.Buffered(1))] + h_specs + [
                  pl.BlockSpec((1, N_MOD, d), lambda i: (jnp.minimum(i // per, b), 0, 0)),
                  pl.BlockSpec((1, d), lambda i: (0, 0))],
        out_specs=pl.BlockSpec((bd, d), lambda i: (i, 0)),
        out_shape=jax.ShapeDtypeStruct((rows, d), F32),
        compiler_params=_params(("parallel",)),
        name="ffn_down",
    )(act, w_down, h, hc, mods, final_g.reshape(1, d))


def _proj_kernel(h_ref, mod_ref, modn_ref, g_ref, w_ref, o_ref, xn_ref, *, rc, ch):
    i, j = pl.program_id(0), pl.program_id(1)
    slot = i % 2

    @pl.when((j == 0) & (i == 0))
    def _():
        _norm_mod_rows(h_ref, mod_ref, g_ref, xn_ref.at[0], None, 3, 4, rc)

    o_ref[...] = _dot_nt(xn_ref[slot], w_ref[...]).astype(BF16)
    _norm_mod_chunk(h_ref, modn_ref, g_ref, xn_ref, 1 - slot, jnp.maximum(j - 1, 0), ch, 3, 4)


def _proj(h, mods, norm_g, w_p, l, s, b):
    rows, d = h.shape
    n = w_p.shape[1]
    bm = _pick(1024, s, rows)
    bn = _pick(512, n)
    nt = rows // bm
    assert n // bn >= 2
    h_idx = lambda i, j: (jnp.where((i == 0) & (j == 0), 0, jnp.minimum(i + 1, nt - 1)), 0)
    return pl.pallas_call(
        functools.partial(_proj_kernel, rc=_pick(128, bm), ch=_chunk_rows(bm, n // bn - 1)),
        grid=(nt, n // bn),
        in_specs=[pl.BlockSpec((bm, d), h_idx),
                  _mod_spec(d, bm, s, b), _mod_spec(d, bm, s, b, 1, nt - 1),
                  pl.BlockSpec((1, d), lambda i, j: (0, 0)),
                  pl.BlockSpec((None, bn, d), lambda i, j: (l, j, 0))],
        out_specs=pl.BlockSpec((bm, bn), lambda i, j: (i, j)),
        out_shape=jax.ShapeDtypeStruct((rows, n), BF16),
        scratch_shapes=[pltpu.VMEM((2, bm, d), BF16)],
        compiler_params=_params(("arbitrary", "arbitrary")),
        name="in_proj",
    )(h, mods, mods, norm_g.reshape(1, d), w_p)


def _conv_kernel(gb_ref, gc_ref, v_ref, gcp_ref, vp_ref, gcn_ref, vn_ref, w_ref, o_ref, *, bm, s, c, mx):
    row0 = pl.program_id(0) * bm
    is_lat = row0 < mx
    at_start = jnp.where(is_lat, row0 % s == 0, (row0 - mx) % c == 0)
    at_end = jnp.where(is_lat, (row0 + bm) % s == 0, (row0 - mx + bm) % c == 0)
    cv = gc_ref[...].astype(F32) * v_ref[...].astype(F32)
    hp = (gcp_ref[...].astype(F32) * vp_ref[...].astype(F32))[15:16, :]
    hn = (gcn_ref[...].astype(F32) * vn_ref[...].astype(F32))[0:1, :]
    hp = jnp.where(at_start, 0.0, hp)
    hn = jnp.where(at_end, 0.0, hn)
    rid = lax.broadcasted_iota(jnp.int32, (bm, 1), 0)
    prev = jnp.where(rid == 0, hp, pltpu.roll(cv, 1, 0))
    nxt = jnp.where(rid == bm - 1, hn, pltpu.roll(cv, bm - 1, 0))
    w = w_ref[...]
    y = gb_ref[...].astype(F32) * (prev * w[0:1, :] + cv * w[1:2, :] + nxt * w[2:3, :])
    o_ref[...] = y.astype(BF16)


def _conv(px, conv_w, rows, s, c, mx):
    cw = conv_w.shape[1]
    m = px.shape[0]
    bm = _pick(256, s, c)
    hb = bm // 16
    main = lambda col: pl.BlockSpec((bm, cw), lambda i: (i, col))
    prev = lambda col: pl.BlockSpec((16, cw), lambda i: (jnp.maximum(i * hb - 1, 0), col))
    nxt = lambda col: pl.BlockSpec((16, cw), lambda i: (jnp.minimum((i + 1) * hb, m // 16 - 1), col))
    return pl.pallas_call(
        functools.partial(_conv_kernel, bm=bm, s=s, c=c, mx=mx),
        grid=(rows // bm,),
        in_specs=[main(0), main(1), main(2), prev(1), prev(2), nxt(1), nxt(2),
                  pl.BlockSpec((CONV_K, cw), lambda i: (0, 0))],
        out_specs=pl.BlockSpec((bm, cw), lambda i: (i, 0)),
        out_shape=jax.ShapeDtypeStruct((rows, cw), BF16),
        compiler_params=_params(("parallel",)),
        name="gated_conv",
    )(px, px, px, px, px, px, px, conv_w)


def _rope(x, cos, sin, half):
    lane = lax.broadcasted_iota(jnp.int32, x.shape, 1)
    first = (lane % (2 * half)) < half
    rot = jnp.where(first, pltpu.roll(x, LANE - half, 1), pltpu.roll(x, half, 1))
    return x * cos + rot * sin


def _prep_kernel(gq_ref, cq_ref, ckv_ref, gk_ref, kr_ref, qn_ref, kvn_ref, wqb_ref, wkvb_ref,
                 mcos_ref, msin_ref, gcos_ref, gsin_ref, q_ref, k_ref, v_ref, gqo_ref, gko_ref):
    mcos, msin = mcos_ref[...], msin_ref[...]
    gcos, gsin = gcos_ref[...], gsin_ref[...]
    mh = MLA_ROPE // 4
    gh = GQA_HEAD_DIM // 4

    cqn = _rms(cq_ref[...].astype(F32), qn_ref[...]).astype(BF16)
    q = _dot(cqn, wqb_ref[...]) * (MLA_SCALE * LOG2E)
    ckvn = _rms(ckv_ref[...].astype(F32), kvn_ref[...]).astype(BF16)
    kv = _dot(ckvn, wkvb_ref[...])
    kr = _rope(kr_ref[...].astype(F32), mcos, msin, mh).astype(BF16)
    ones = jnp.ones((q.shape[0], LANE), BF16)
    v0 = MLA_HEADS * MLA_NOPE
    for h in range(MLA_HEADS):
        a = h * MLA_QK
        q_ref[:, a:a + LANE] = q[:, a:a + LANE].astype(BF16)
        q_ref[:, a + LANE:a + MLA_QK] = _rope(q[:, a + LANE:a + MLA_QK], mcos, msin, mh).astype(BF16)
        k_ref[:, a:a + LANE] = kv[:, h * LANE:(h + 1) * LANE].astype(BF16)
        k_ref[:, a + LANE:a + MLA_QK] = kr
        v_ref[:, a:a + LANE] = kv[:, v0 + h * LANE:v0 + (h + 1) * LANE].astype(BF16)
        v_ref[:, a + LANE:a + MLA_QK] = ones

    gq = gq_ref[...].astype(F32)
    for h in range(GQA_HEADS):
        a = h * GQA_HEAD_DIM
        gqo_ref[:, a:a + LANE] = (_rope(gq[:, a:a + LANE], gcos, gsin, gh) * (GQA_SCALE * LOG2E)).astype(BF16)
    gk = gk_ref[...].astype(F32)
    for h in range(GQA_KV_HEADS):
        a = h * GQA_HEAD_DIM
        gko_ref[:, a:a + LANE] = _rope(gk[:, a:a + LANE], gcos, gsin, gh).astype(BF16)


def _prep(px, lay, q_norm, kv_norm, wqb_p, wkvb_r, tabs, s, mx):
    m = px.shape[0]
    bm = _pick(512, s, m - mx)
    n_lat = mx // bm
    per = s // bm
    ql, kvl = q_norm.shape[0], kv_norm.shape[0]
    gqw, gkw = GQA_HEADS * GQA_HEAD_DIM, GQA_KV_HEADS * GQA_HEAD_DIM

    def col(width, off):
        assert off % width == 0, (width, off)
        return pl.BlockSpec((bm, width), lambda i: (i, off // width))

    const = lambda r, c: pl.BlockSpec((r, c), lambda i: (0, 0))
    tab = pl.BlockSpec((bm, LANE), lambda i: (jnp.where(i < n_lat, i % per, per + i - n_lat), 0))
    row = lambda width: pl.BlockSpec((bm, width), lambda i: (i, 0))
    hq = MLA_HEADS * MLA_QK
    return pl.pallas_call(
        _prep_kernel,
        grid=(m // bm,),
        in_specs=[col(gqw, lay["gq"]), col(ql, lay["cq"]), col(kvl, lay["ckv"]), col(gkw, lay["gk"]),
                  col(LANE, lay["kr"]), const(1, ql), const(1, kvl), const(ql, hq),
                  const(kvl, MLA_HEADS * (MLA_NOPE + MLA_V)), tab, tab, tab, tab],
        out_specs=[row(hq), row(hq), row(hq), row(gqw), row(gkw)],
        out_shape=[jax.ShapeDtypeStruct((m, hq), BF16), jax.ShapeDtypeStruct((m, hq), BF16),
                   jax.ShapeDtypeStruct((m, hq), BF16),
                   jax.ShapeDtypeStruct((m, gqw), BF16), jax.ShapeDtypeStruct((m, gkw), BF16)],
        compiler_params=_params(("parallel",)),
        name="attn_prep",
    )(px, px, px, px, px, q_norm.reshape(1, ql), kv_norm.reshape(1, kvl), wqb_p, wkvb_r, *tabs)


def _mla_kernel(q_ref, kl_ref, kc_ref, vl_ref, vc_ref, o_ref, *, n_sub):
    kl, kc = kl_ref[...], kc_ref[...]
    subs = [q_ref[i * MLA_SUB:(i + 1) * MLA_SUB, :] for i in range(n_sub)]
    scores = [(_dot_nt(q, kl), _dot_nt(q, kc)) for q in subs]
    for i, (s1, s2) in enumerate(scores):
        m = jnp.maximum(jnp.max(s1, axis=-1, keepdims=True), jnp.max(s2, axis=-1, keepdims=True))
        p1 = jnp.exp2(s1 - m).astype(BF16)
        p2 = jnp.exp2(s2 - m).astype(BF16)
        o = _dot(p1, vl_ref[...]) + _dot(p2, vc_ref[...])
        o_ref[i * MLA_SUB:(i + 1) * MLA_SUB, :] = (o[:, :MLA_V] / o[:, MLA_V:]).astype(BF16)


def _mla_ctx_kernel(q_ref, kc_ref, vc_ref, y_ref, o_ref):
    del y_ref
    s2 = _dot_nt(q_ref[...], kc_ref[...])
    p2 = jnp.exp2(s2 - jnp.max(s2, axis=-1, keepdims=True)).astype(BF16)
    o = _dot(p2, vc_ref[...])
    o_ref[...] = (o[:, :MLA_V] / o[:, MLA_V:]).astype(BF16)


def _mla_attn(q, k, v, b, s, c, with_ctx):
    mx = b * s
    n_sub = max(d for d in (1, 2, 4) if (s // MLA_SUB) % d == 0)
    bq = n_sub * MLA_SUB
    nq = s // bq
    rows = mx + (b * c if with_ctx else 0)
    ctx0 = mx // c
    y = pl.pallas_call(
        functools.partial(_mla_kernel, n_sub=n_sub),
        grid=(b, MLA_HEADS, nq),
        in_specs=[pl.BlockSpec((bq, MLA_QK), lambda bi, h, qi: (bi * nq + qi, h)),
                  pl.BlockSpec((s, MLA_QK), lambda bi, h, qi: (bi, h)),
                  pl.BlockSpec((c, MLA_QK), lambda bi, h, qi: (ctx0 + bi, h)),
                  pl.BlockSpec((s, MLA_QK), lambda bi, h, qi: (bi, h)),
                  pl.BlockSpec((c, MLA_QK), lambda bi, h, qi: (ctx0 + bi, h))],
        out_specs=pl.BlockSpec((bq, MLA_V), lambda bi, h, qi: (bi * nq + qi, h)),
        out_shape=jax.ShapeDtypeStruct((rows, MLA_HEADS * MLA_V), BF16),
        compiler_params=_params(("parallel", "parallel", "arbitrary")),
        name="mla_attn",
    )(q, k, k, v, v)
    if not with_ctx:
        return y
    return pl.pallas_call(
        _mla_ctx_kernel,
        grid=(b, MLA_HEADS),
        in_specs=[pl.BlockSpec((c, MLA_QK), lambda bi, h: (ctx0 + bi, h)),
                  pl.BlockSpec((c, MLA_QK), lambda bi, h: (ctx0 + bi, h)),
                  pl.BlockSpec((c, MLA_QK), lambda bi, h: (ctx0 + bi, h)),
                  pl.BlockSpec(memory_space=pl.ANY)],
        out_specs=pl.BlockSpec((c, MLA_V), lambda bi, h: (ctx0 + bi, h)),
        out_shape=jax.ShapeDtypeStruct(y.shape, BF16),
        input_output_aliases={3: 0},
        compiler_params=_params(("parallel", "parallel")),
        name="mla_attn_ctx",
    )(q, k, v, y)


GQA_QB = 4


def _sink_rows(sink_ref, g, j, rows):
    return jnp.broadcast_to(sink_ref[pl.ds(g * GQA_GROUP + j, 1), :], (rows, LANE))[:, 0:1]


def _gqa_kernel(q_ref, kp_ref, kc_ref, kn_ref, kx_ref, vp_ref, vc_ref, vn_ref, vx_ref, sink_ref, o_ref,
                *, n_lat):
    g = pl.program_id(1)
    n = pl.program_id(2)
    blk = ATT_BLOCK
    kx, vx = kx_ref[...], vx_ref[...]
    kband = jnp.concatenate([kp_ref[...], kc_ref[...], kn_ref[...]], axis=0)
    vband = jnp.concatenate([vp_ref[...], vc_ref[...], vn_ref[...]], axis=0)
    nk = 3 * blk + kx.shape[0]
    ones = jnp.ones((nk, LANE), BF16)
    snk = jnp.concatenate([jnp.broadcast_to(sink_ref[pl.ds(g * GQA_GROUP + j, 1), :], (blk, LANE))
                           for j in range(GQA_GROUP)], axis=0)
    rows = GQA_GROUP * blk
    r = lax.broadcasted_iota(jnp.int32, (rows, nk), 0) % blk
    col = lax.broadcasted_iota(jnp.int32, (rows, nk), 1)
    in_window = (jnp.abs(col - blk - r) <= WINDOW) | (col >= 3 * blk)
    for i in range(GQA_QB):
        qs = jnp.concatenate([q_ref[i * blk:(i + 1) * blk, j * LANE:(j + 1) * LANE] for j in range(GQA_GROUP)],
                             axis=0)
        keys = jnp.concatenate([kband[i * blk:(i + 3) * blk, :], kx], axis=0)
        vals = jnp.concatenate([jnp.concatenate([vband[i * blk:(i + 3) * blk, :], vx], axis=0), ones], axis=1)
        block = n * GQA_QB + i
        valid = in_window & ((col >= blk) | (block > 0)) & ((col < 2 * blk) | (col >= 3 * blk) | (block < n_lat - 1))
        sc = jnp.where(valid, _dot_nt(qs, keys), -jnp.inf)
        e = snk
        for t in range(nk // LANE):
            e = jnp.maximum(e, sc[:, t * LANE:(t + 1) * LANE])
        m = jnp.max(e, axis=-1, keepdims=True)
        ol = _dot(jnp.exp2(sc - m).astype(BF16), vals)
        o = ol[:, :LANE] / (ol[:, LANE:] + jnp.exp2(snk - m))
        for j in range(GQA_GROUP):
            o_ref[i * blk:(i + 1) * blk, j * LANE:(j + 1) * LANE] = o[j * blk:(j + 1) * blk, :].astype(BF16)


def _gqa_ctx_kernel(q_ref, kx_ref, vx_ref, sink_ref, y_ref, o_ref):
    del y_ref
    g = pl.program_id(1)
    kx, vx = kx_ref[...], vx_ref[...]
    for j in range(GQA_GROUP):
        sc = _dot_nt(q_ref[:, j * LANE:(j + 1) * LANE], kx)
        snk = _sink_rows(sink_ref, g, j, sc.shape[0])
        m = jnp.maximum(jnp.max(sc, axis=-1, keepdims=True), snk)
        p_c = jnp.exp2(sc - m)
        l = jnp.sum(p_c, axis=-1, keepdims=True) + jnp.exp2(snk - m)
        o_ref[:, j * LANE:(j + 1) * LANE] = (_dot(p_c.astype(BF16), vx) / l).astype(BF16)


def _gqa_attn(gq, gk, px, o_gv, sink, b, s, c, with_ctx):
    mx = b * s
    blk = ATT_BLOCK
    n_lat = s // blk
    nstep = n_lat // GQA_QB
    rows = mx + (b * c if with_ctx else 0)
    ctx0 = mx // c
    gvc = o_gv // LANE
    gw = GQA_GROUP * GQA_HEAD_DIM
    big = GQA_QB * blk

    def edge(shift, col0):
        def idx(bi, g, n):
            return bi * n_lat + jnp.clip(n * GQA_QB + shift, 0, n_lat - 1), col0 + g
        return pl.BlockSpec((blk, LANE), idx)

    main = lambda col0: pl.BlockSpec((big, LANE), lambda bi, g, n: (bi * nstep + n, col0 + g))
    ctx = lambda col0: pl.BlockSpec((c, LANE), lambda bi, g, n: (ctx0 + bi, col0 + g))
    sink_spec = pl.BlockSpec((GQA_HEADS, LANE), lambda *_: (0, 0))
    y = pl.pallas_call(
        functools.partial(_gqa_kernel, n_lat=n_lat),
        grid=(b, GQA_KV_HEADS, nstep),
        in_specs=[pl.BlockSpec((big, gw), lambda bi, g, n: (bi * nstep + n, g)),
                  edge(-1, 0), main(0), edge(GQA_QB, 0), ctx(0),
                  edge(-1, gvc), main(gvc), edge(GQA_QB, gvc), ctx(gvc), sink_spec],
        out_specs=pl.BlockSpec((big, gw), lambda bi, g, n: (bi * nstep + n, g)),
        out_shape=jax.ShapeDtypeStruct((rows, GQA_HEADS * GQA_HEAD_DIM), BF16),
        compiler_params=_params(("parallel", "parallel", "arbitrary")),
        name="gqa_attn",
    )(gq, gk, gk, gk, gk, px, px, px, px, sink)
    if not with_ctx:
        return y
    return pl.pallas_call(
        _gqa_ctx_kernel,
        grid=(b, GQA_KV_HEADS),
        in_specs=[pl.BlockSpec((c, gw), lambda bi, g: (ctx0 + bi, g)),
                  pl.BlockSpec((c, LANE), lambda bi, g: (ctx0 + bi, g)),
                  pl.BlockSpec((c, LANE), lambda bi, g: (ctx0 + bi, gvc + g)),
                  sink_spec, pl.BlockSpec(memory_space=pl.ANY)],
        out_specs=pl.BlockSpec((c, gw), lambda bi, g: (ctx0 + bi, g)),
        out_shape=jax.ShapeDtypeStruct(y.shape, BF16),
        input_output_aliases={4: 0},
        compiler_params=_params(("parallel", "parallel")),
        name="gqa_attn_ctx",
    )(gq, gk, px, sink, y)


def _merge_kernel(yc_ref, ym_ref, yg_ref, gc_ref, gm_ref, gg_ref, wc_ref, wm_ref, wg_ref, o_ref):
    gate = lambda ref: jax.nn.sigmoid(ref[...].astype(F32))
    acc = gate(gc_ref) * _dot(yc_ref[...], wc_ref[...])
    acc += gate(gm_ref) * _dot(ym_ref[...], wm_ref[...])
    acc += gate(gg_ref) * _dot(yg_ref[...], wg_ref[...])
    o_ref[...] = acc.astype(BF16)


def _merge(yc, ym, yg, px, o_gate, w_bc, w_bm, w_bg, l, rows, s):
    d = w_bc.shape[2]
    bm = _pick(1024, s, rows)
    bn = _pick(512, d, o_gate)
    g0 = o_gate // bn
    per = d // bn
    yspec = lambda a: pl.BlockSpec((bm, a.shape[1]), lambda i, j: (i, 0))
    gspec = lambda k: pl.BlockSpec((bm, bn), lambda i, j: (i, g0 + k * per + j))
    wspec = lambda w: pl.BlockSpec((None, w.shape[1], bn), lambda i, j: (l, 0, j))
    return pl.pallas_call(
        _merge_kernel,
        grid=(rows // bm, d // bn),
        in_specs=[yspec(yc), yspec(ym), yspec(yg), gspec(0), gspec(1), gspec(2),
                  wspec(w_bc), wspec(w_bm), wspec(w_bg)],
        out_specs=pl.BlockSpec((bm, bn), lambda i, j: (i, j)),
        out_shape=jax.ShapeDtypeStruct((rows, d), BF16),
        compiler_params=_params(("parallel", "arbitrary")),
        name="branch_merge",
    )(yc, ym, yg, px, px, px, w_bc, w_bm, w_bg)


def _out_kernel(m_ref, w_ref, h_ref, mod_ref, o_ref):
    o_ref[...] = h_ref[...] + mod_ref[0, 5:6, :] * _dot(m_ref[...], w_ref[...])


def _out_proj(merged, w_out, l, h, mods, rows, s, b):
    d = w_out.shape[2]
    bm = _pick(1024, s, rows)
    bn = _pick(512, d)
    per = s // bm
    return pl.pallas_call(
        _out_kernel,
        grid=(rows // bm, d // bn),
        in_specs=[pl.BlockSpec((bm, d), lambda i, j: (i, 0)),
                  pl.BlockSpec((None, d, bn), lambda i, j: (l, 0, j)),
                  pl.BlockSpec((bm, bn), lambda i, j: (i, j)),
                  pl.BlockSpec((1, N_MOD, bn), lambda i, j: (jnp.minimum(i // per, b), 0, j))],
        out_specs=pl.BlockSpec((bm, bn), lambda i, j: (i, j)),
        out_shape=jax.ShapeDtypeStruct((rows, d), F32),
        compiler_params=_params(("parallel", "arbitrary")),
        name="out_proj",
    )(merged, w_out, h, mods)


def _rope_table(s, n_ctx_rows, dim):
    t = jnp.arange(s, dtype=jnp.int32)
    pos = jnp.stack([t // GRID_W, t % GRID_W], axis=1).astype(F32)
    inv = ROPE_BASE ** (-jnp.arange(0, dim, 2, dtype=F32) / dim)
    lane = jnp.arange(LANE)
    ang = pos[:, jnp.minimum(lane // dim, 1)] * inv[lane % (dim // 2)][None, :]
    active = (lane < 2 * dim)[None, :]
    sign = jnp.where((lane % dim) < dim // 2, -1.0, 1.0)[None, :]
    cos = jnp.where(active, jnp.cos(ang), 1.0)
    sin = jnp.where(active, jnp.sin(ang) * sign, 0.0)
    pad = ((0, n_ctx_rows), (0, 0))
    return jnp.pad(cos, pad, constant_values=1.0), jnp.pad(sin, pad)


def _layout(cw, ql, kvl, d):
    lay = {"conv": 0}
    off = 3 * cw
    for name, width in (("gq", GQA_HEADS * GQA_HEAD_DIM), ("cq", ql), ("ckv", kvl),
                        ("gk", GQA_KV_HEADS * GQA_HEAD_DIM), ("gv", GQA_KV_HEADS * GQA_HEAD_DIM),
                        ("kr", LANE)):
        lay[name] = off
        off += width
    off = -(-off // 512) * 512
    lay["gate"] = off
    lay["total"] = off + 3 * d
    return lay


def _pack_kernel(w_ref, o_ref, *, moves, zero):
    for dst, src, width in moves:
        o_ref[dst:dst + width, :] = w_ref[src:src + width, :].astype(BF16)
    o_ref[zero[0]:zero[1], :] = jnp.zeros((zero[1] - zero[0], o_ref.shape[1]), BF16)


def _pack_w_in(w, lay, cw, ql, kvl):
    depth, d, n = w.shape
    w_t = jnp.swapaxes(w, 1, 2)
    o_mla = 3 * cw
    o_gqa = o_mla + ql + kvl + MLA_ROPE
    gqw, gkw = GQA_HEADS * GQA_HEAD_DIM, GQA_KV_HEADS * GQA_HEAD_DIM
    o_gate = o_gqa + gqw + 2 * gkw
    moves = ((0, 0, o_mla), (lay["gq"], o_gqa, gqw), (lay["cq"], o_mla, ql + kvl),
             (lay["gk"], o_gqa + gqw, 2 * gkw), (lay["kr"], o_mla + ql + kvl, MLA_ROPE),
             (lay["gate"], o_gate, n - o_gate))
    cb = _pick(256, d)
    return pl.pallas_call(
        functools.partial(_pack_kernel, moves=moves, zero=(lay["kr"] + MLA_ROPE, lay["gate"])),
        grid=(depth, d // cb),
        in_specs=[pl.BlockSpec((None, n, cb), lambda l, j: (l, 0, j))],
        out_specs=pl.BlockSpec((None, lay["total"], cb), lambda l, j: (l, 0, j)),
        out_shape=jax.ShapeDtypeStruct((depth, lay["total"], d), BF16),
        compiler_params=_params(("parallel", "parallel")),
        name="pack_w_in",
    )(w_t)


def kernel(x, c, ctx, c_ctx, ada_w, ada_b, ffn1_norm, ffn1_w_gu, ffn1_w_down, mix_norm, w_in, conv_w,
           mla_q_norm, mla_w_qb, mla_kv_norm, mla_w_kvb, gqa_sink, w_branch_conv, w_branch_mla,
           w_branch_gqa, w_out, ffn2_norm, ffn2_w_gu, ffn2_w_down, final_norm):
    b, s, d = x.shape
    cl = ctx.shape[1]
    depth = ada_w.shape[0]
    cw = conv_w.shape[-1]
    ql, kvl = mla_q_norm.shape[-1], mla_kv_norm.shape[-1]
    mx, mc = b * s, b * cl
    assert b + 1 <= 8 and s % cl == 0 and cl % ATT_BLOCK == 0

    cvec = jnp.zeros((8, d), F32).at[:b].set(c).at[b].set(c_ctx)
    mods = _mods(cvec, ada_w, ada_b).reshape(depth, 8, N_MOD, d)

    lay = _layout(cw, ql, kvl, d)
    tabs = _rope_table(s, mc, MLA_ROPE // 2) + _rope_table(s, mc, GQA_HEAD_DIM // 2)
    h = None
    bf = _pick(512, ffn1_w_down.shape[1])
    w1_gu, w1_dn = _gu_to_bf16(ffn1_w_gu, bf), _to_bf16(ffn1_w_down)
    w2_gu, w2_dn = _gu_to_bf16(ffn2_w_gu, bf), _to_bf16(ffn2_w_down)
    w_bc, w_bm, w_bg = _to_bf16(w_branch_conv), _to_bf16(w_branch_mla), _to_bf16(w_branch_gqa)
    w_o = _to_bf16(w_out)
    w_p = _pack_w_in(w_in, lay, cw, ql, kvl)

    for l in range(depth):
        with_ctx = l < depth - 1
        rows = mx + mc if with_ctx else mx
        wqb_p = jnp.pad(mla_w_qb[l].reshape(ql, MLA_HEADS, MLA_NOPE + MLA_ROPE),
                        ((0, 0), (0, 0), (0, MLA_QK - MLA_NOPE - MLA_ROPE))).reshape(ql, -1).astype(BF16)
        wkv = mla_w_kvb[l].reshape(kvl, MLA_HEADS, MLA_NOPE + MLA_V)
        wkvb_r = jnp.concatenate([wkv[:, :, :MLA_NOPE].reshape(kvl, -1), wkv[:, :, MLA_NOPE:].reshape(kvl, -1)],
                                 axis=1).astype(BF16)
        sink = jnp.broadcast_to((gqa_sink[l].astype(F32) * LOG2E)[:, None], (GQA_HEADS, LANE))

        if l == 0:
            h = _ffn(x.reshape(mx, d), ctx.reshape(mc, d), mx + mc, mods[l], ffn1_norm[l], w1_gu, w1_dn, l, 0,
                     s, b, final_norm, False)
        else:
            h = _ffn(h, None, mx + mc, mods[l], ffn1_norm[l], w1_gu, w1_dn, l, 0, s, b, final_norm, False)
        px = _proj(h, mods[l], mix_norm[l], w_p, l, s, b)
        y_conv = _conv(px, conv_w[l], rows, s, cl, mx)
        q, k, v, gq, gk = _prep(px, lay, mla_q_norm[l], mla_kv_norm[l], wqb_p, wkvb_r, tabs, s, mx)
        y_mla = _mla_attn(q, k, v, b, s, cl, with_ctx)
        y_gqa = _gqa_attn(gq, gk, px, lay["gv"], sink, b, s, cl, with_ctx)
        merged = _merge(y_conv, y_mla, y_gqa, px, lay["gate"], w_bc, w_bm, w_bg, l, rows, s)
        h = _out_proj(merged, w_o, l, h, mods[l], rows, s, b)
        h = _ffn(h, None, rows, mods[l], ffn2_norm[l], w2_gu, w2_dn, l, 6, s, b, final_norm, not with_ctx)
    return h.reshape(b, s, d)
```

```python
import functools

import jax
import jax.numpy as jnp
from jax import lax
from jax.experimental import pallas as pl
from jax.experimental.pallas import tpu as pltpu

F32 = jnp.float32
BF16 = jnp.bfloat16

GRID_W = 64
N_MOD = 9
EPS = 1e-6
ROPE_BASE = 10000.0
CONV_K = 3
MLA_HEADS = 8
MLA_NOPE = 128
MLA_ROPE = 64
MLA_V = 128
MLA_SCALE = (MLA_NOPE + MLA_ROPE) ** -0.5
GQA_HEADS = 8
GQA_KV_HEADS = 2
GQA_GROUP = GQA_HEADS // GQA_KV_HEADS
GQA_HEAD_DIM = 128
GQA_SCALE = GQA_HEAD_DIM ** -0.5
WINDOW = 128
ATT_BLOCK = 128

LANE = 128
MLA_QK = 2 * LANE
MLA_SUB = 256
LOG2E = 1.4426950408889634
VMEM_LIMIT = 56 << 20
NT_DIMS = (((1,), (1,)), ((), ()))


def _pick(target, *sizes):
    b = target
    while any(s % b for s in sizes):
        b //= 2
        assert b >= 8, (target, sizes)
    return b


def _params(sem):
    return pltpu.CompilerParams(dimension_semantics=sem, vmem_limit_bytes=VMEM_LIMIT)


def _dot(a, b):
    return jnp.dot(a, b, preferred_element_type=F32)


def _dot_nt(a, b):
    return lax.dot_general(a, b, NT_DIMS, preferred_element_type=F32)


def _rms(x, g):
    return x * lax.rsqrt(jnp.mean(x * x, axis=-1, keepdims=True) + EPS) * g


CAST_BLOCK_BYTES = 6 << 20


def _cast_kernel(w_ref, o_ref):
    o_ref[...] = w_ref[...].astype(BF16)


def _to_bf16(w):
    depth, k, n = w.shape
    rows = depth * k
    rb = _pick(max(16, 1 << ((CAST_BLOCK_BYTES // (4 * n)).bit_length() - 1)), rows)
    out = pl.pallas_call(
        _cast_kernel,
        grid=(rows // rb,),
        in_specs=[pl.BlockSpec((rb, n), lambda i: (i, 0))],
        out_specs=pl.BlockSpec((rb, n), lambda i: (i, 0)),
        out_shape=jax.ShapeDtypeStruct((rows, n), BF16),
        compiler_params=_params(("parallel",)),
        name="cast_bf16",
    )(w.reshape(rows, n))
    return out.reshape(depth, k, n)


def _cast_gu_kernel(w_ref, o_ref, *, bf):
    nf = o_ref.shape[0]
    for j in range(nf):
        o_ref[j, :, :bf] = w_ref[:, j * bf:(j + 1) * bf].astype(BF16)
        o_ref[j, :, bf:] = w_ref[:, (nf + j) * bf:(nf + j + 1) * bf].astype(BF16)


def _gu_to_bf16(w, bf):
    depth, k, n = w.shape
    nf = n // (2 * bf)
    rb = _pick(max(16, 1 << ((CAST_BLOCK_BYTES // (4 * n)).bit_length() - 1)), k)
    kb = k // rb
    return pl.pallas_call(
        functools.partial(_cast_gu_kernel, bf=bf),
        grid=(depth, kb),
        in_specs=[pl.BlockSpec((None, rb, n), lambda l, i: (l, i, 0))],
        out_specs=pl.BlockSpec((None, nf, rb, 2 * bf), lambda l, i: (l, 0, i, 0)),
        out_shape=jax.ShapeDtypeStruct((depth, nf, k, 2 * bf), BF16),
        compiler_params=_params(("parallel", "parallel")),
        name="cast_gate_up",
    )(w)


def _mods_kernel(c_ref, w_ref, b_ref, o_ref):
    c = c_ref[...]
    s = (c * jax.nn.sigmoid(c)).astype(BF16)
    o_ref[0] = _dot(s, w_ref[0].astype(BF16)) + b_ref[0]


def _mods(cvec, ada_w, ada_b):
    depth, d, n = ada_w.shape
    bn = _pick(1024, n)
    return pl.pallas_call(
        _mods_kernel,
        grid=(depth, n // bn),
        in_specs=[pl.BlockSpec((8, d), lambda l, j: (0, 0)),
                  pl.BlockSpec((1, d, bn), lambda l, j: (l, 0, j)),
                  pl.BlockSpec((1, 1, bn), lambda l, j: (l, 0, j))],
        out_specs=pl.BlockSpec((1, 8, bn), lambda l, j: (l, 0, j)),
        out_shape=jax.ShapeDtypeStruct((depth, 8, n), F32),
        compiler_params=_params(("parallel", "parallel")),
        name="adaln_mods",
    )(cvec, ada_w, ada_b.reshape(depth, 1, n))


def _norm_mod_rows(h_ref, mod_ref, g_ref, xn_ref, copy_ref, i_shift, i_scale, rc):
    shift = mod_ref[0, i_shift:i_shift + 1, :]
    scale1 = 1.0 + mod_ref[0, i_scale:i_scale + 1, :]
    g = g_ref[...]

    def body(r, carry):
        rows = pl.ds(pl.multiple_of(r * rc, rc), rc)
        x = h_ref[rows, :]
        xn_ref[rows, :] = (_rms(x, g) * scale1 + shift).astype(BF16)
        if copy_ref is not None:
            copy_ref[rows, :] = x
        return carry

    lax.fori_loop(0, h_ref.shape[0] // rc, body, 0)


def _norm_mod_chunk(hn_ref, modn_ref, g_ref, xn_ref, slot, k, ch, i_shift, i_scale):
    r0 = pl.multiple_of(jnp.minimum(k * ch, hn_ref.shape[0] - ch), 16)
    shift = modn_ref[0, i_shift:i_shift + 1, :]
    scale1 = 1.0 + modn_ref[0, i_scale:i_scale + 1, :]
    x = hn_ref[pl.ds(r0, ch), :]
    xn_ref[slot, pl.ds(r0, ch), :] = (_rms(x, g_ref[...]) * scale1 + shift).astype(BF16)


def _chunk_rows(bm, steps):
    return min(bm, -(-bm // (16 * steps)) * 16)


def _mod_spec(d, bm, s, b, ahead=0, last=None):
    per = s // bm

    def idx(i, j):
        t = i if not ahead else jnp.minimum(i + ahead, last)
        return jnp.minimum(t // per, b), 0, 0

    return pl.BlockSpec((1, N_MOD, d), idx)


FFN_UP_ROWS = 1024
FFN_DOWN_ROWS = 256


def _swiglu(gu):
    bf = gu.shape[1] // 2
    gg, uu = gu[:, :bf], gu[:, bf:]
    return (gg * jax.nn.sigmoid(gg) * uu).astype(BF16)


def _up_split_kernel(h_ref, hc_ref, mod_ref, g_ref, w_ref, o_ref, xn_ref, *, i0, rc, n_lat):
    first = pl.program_id(1) == 0
    is_lat = pl.program_id(0) < n_lat

    @pl.when(first & is_lat)
    def _():
        _norm_mod_rows(h_ref, mod_ref, g_ref, xn_ref, None, i0, i0 + 1, rc)

    @pl.when(first & jnp.logical_not(is_lat))
    def _():
        _norm_mod_rows(hc_ref, mod_ref, g_ref, xn_ref, None, i0, i0 + 1, rc)

    o_ref[...] = _swiglu(_dot(xn_ref[...], w_ref[...]))


def _up_kernel(h_ref, mod_ref, modn_ref, g_ref, w_ref, o_ref, xn_ref, *, i0, rc, ch):
    i, j = pl.program_id(0), pl.program_id(1)
    slot = i % 2

    @pl.when((j == 0) & (i == 0))
    def _():
        _norm_mod_rows(h_ref, mod_ref, g_ref, xn_ref.at[0], None, i0, i0 + 1, rc)

    o_ref[...] = _swiglu(_dot(xn_ref[slot], w_ref[...]))
    _norm_mod_chunk(h_ref, modn_ref, g_ref, xn_ref, 1 - slot, jnp.maximum(j - 1, 0), ch, i0, i0 + 1)


def _down_kernel(a_ref, w_ref, h_ref, hc_ref, mod_ref, fg_ref, o_ref, *, i_gate, n_lat, final):
    h = h_ref[...]
    if n_lat is not None:
        h = jnp.where(pl.program_id(0) < n_lat, h, hc_ref[...])
    out = h + (0.5 * mod_ref[0, i_gate:i_gate + 1, :]) * _dot(a_ref[...], w_ref[...])
    o_ref[...] = _rms(out, fg_ref[...]) if final else out


def _ffn(h, hc, rows, mods, norm_g, w_gu, w_down, l, i0, s, b, final_g, final):
    d = h.shape[1]
    nf, bf = w_gu.shape[1], w_gu.shape[3] // 2
    f = nf * bf
    vec = pl.BlockSpec((1, d), lambda i, j: (0, 0))
    w_spec = pl.BlockSpec((None, None, d, 2 * bf), lambda i, j: (l, j, 0, 0))
    bm = _pick(FFN_UP_ROWS, s, rows) if hc is None else _pick(FFN_UP_ROWS, s, hc.shape[0])
    nt = rows // bm
    up_common = dict(
        grid=(nt, nf),
        out_specs=pl.BlockSpec((bm, bf), lambda i, j: (i, j)),
        out_shape=jax.ShapeDtypeStruct((rows, f), BF16),
        compiler_params=_params(("arbitrary", "arbitrary")),
        name="ffn_up")
    if hc is not None:
        n_lat = h.shape[0] // bm
        act = pl.pallas_call(
            functools.partial(_up_split_kernel, i0=i0, rc=_pick(128, bm), n_lat=n_lat),
            in_specs=[pl.BlockSpec((bm, d), lambda i, j: (jnp.minimum(i, n_lat - 1), 0)),
                      pl.BlockSpec((bm, d), lambda i, j: (jnp.maximum(i - n_lat, 0), 0)),
                      _mod_spec(d, bm, s, b), vec, w_spec],
            scratch_shapes=[pltpu.VMEM((bm, d), BF16)],
            **up_common,
        )(h, hc, mods, norm_g.reshape(1, d), w_gu)
    else:
        assert nf >= 2
        h_idx = lambda i, j: (jnp.where((i == 0) & (j == 0), 0, jnp.minimum(i + 1, nt - 1)), 0)
        act = pl.pallas_call(
            functools.partial(_up_kernel, i0=i0, rc=_pick(128, bm), ch=_chunk_rows(bm, nf - 1)),
            in_specs=[pl.BlockSpec((bm, d), h_idx), _mod_spec(d, bm, s, b), _mod_spec(d, bm, s, b, 1, nt - 1),
                      vec, w_spec],
            scratch_shapes=[pltpu.VMEM((2, bm, d), BF16)],
            **up_common,
        )(h, mods, mods, norm_g.reshape(1, d), w_gu)

    bd = _pick(FFN_DOWN_ROWS, s, rows) if hc is None else _pick(FFN_DOWN_ROWS, s, hc.shape[0])
    per = s // bd
    if hc is None:
        n_lat = None
        h_specs = [pl.BlockSpec((bd, d), lambda i: (i, 0)), pl.BlockSpec((1, d), lambda i: (0, 0))]
        hc = norm_g.reshape(1, d)
    else:
        n_lat = h.shape[0] // bd
        h_specs = [pl.BlockSpec((bd, d), lambda i: (jnp.minimum(i, n_lat - 1), 0)),
                   pl.BlockSpec((bd, d), lambda i: (jnp.maximum(i - n_lat, 0), 0))]
    return pl.pallas_call(
        functools.partial(_down_kernel, i_gate=i0 + 2, n_lat=n_lat, final=final),
        grid=(rows // bd,),
        in_specs=[pl.BlockSpec((bd, f), lambda i: (i, 0)),
                  pl.BlockSpec((None, f, d), lambda i: (l, 0, 0), pipeline_mode=pl.Buffered(1))] + h_specs + [
                  pl.BlockSpec((1, N_MOD, d), lambda i: (jnp.minimum(i // per, b), 0, 0)),
                  pl.BlockSpec((1, d), lambda i: (0, 0))],
        out_specs=pl.BlockSpec((bd, d), lambda i: (i, 0)),
        out_shape=jax.ShapeDtypeStruct((rows, d), F32),
        compiler_params=_params(("parallel",)),
        name="ffn_down",
    )(act, w_down, h, hc, mods, final_g.reshape(1, d))


def _proj_kernel(h_ref, mod_ref, modn_ref, g_ref, w_ref, o_ref, xn_ref, *, rc, ch):
    i, j = pl.program_id(0), pl.program_id(1)
    slot = i % 2

    @pl.when((j == 0) & (i == 0))
    def _():
        _norm_mod_rows(h_ref, mod_ref, g_ref, xn_ref.at[0], None, 3, 4, rc)

    o_ref[...] = _dot_nt(xn_ref[slot], w_ref[...]).astype(BF16)
    _norm_mod_chunk(h_ref, modn_ref, g_ref, xn_ref, 1 - slot, jnp.maximum(j - 1, 0), ch, 3, 4)


def _proj(h, mods, norm_g, w_p, l, s, b):
    rows, d = h.shape
    n = w_p.shape[1]
    bm = _pick(1024, s, rows)
    bn = _pick(1024, n)
    nt = rows // bm
    assert n // bn >= 2
    h_idx = lambda i, j: (jnp.where((i == 0) & (j == 0), 0, jnp.minimum(i + 1, nt - 1)), 0)
    return pl.pallas_call(
        functools.partial(_proj_kernel, rc=_pick(128, bm), ch=_chunk_rows(bm, n // bn - 1)),
        grid=(nt, n // bn),
        in_specs=[pl.BlockSpec((bm, d), h_idx),
                  _mod_spec(d, bm, s, b), _mod_spec(d, bm, s, b, 1, nt - 1),
                  pl.BlockSpec((1, d), lambda i, j: (0, 0)),
                  pl.BlockSpec((None, bn, d), lambda i, j: (l, j, 0))],
        out_specs=pl.BlockSpec((bm, bn), lambda i, j: (i, j)),
        out_shape=jax.ShapeDtypeStruct((rows, n), BF16),
        scratch_shapes=[pltpu.VMEM((2, bm, d), BF16)],
        compiler_params=_params(("arbitrary", "arbitrary")),
        name="in_proj",
    )(h, mods, mods, norm_g.reshape(1, d), w_p)


def _conv_kernel(gb_ref, gc_ref, v_ref, gcp_ref, vp_ref, gcn_ref, vn_ref, w_ref, o_ref, *, bm, s, c, mx):
    row0 = pl.program_id(0) * bm
    is_lat = row0 < mx
    at_start = jnp.where(is_lat, row0 % s == 0, (row0 - mx) % c == 0)
    at_end = jnp.where(is_lat, (row0 + bm) % s == 0, (row0 - mx + bm) % c == 0)
    cv = gc_ref[...].astype(F32) * v_ref[...].astype(F32)
    hp = (gcp_ref[...].astype(F32) * vp_ref[...].astype(F32))[15:16, :]
    hn = (gcn_ref[...].astype(F32) * vn_ref[...].astype(F32))[0:1, :]
    hp = jnp.where(at_start, 0.0, hp)
    hn = jnp.where(at_end, 0.0, hn)
    rid = lax.broadcasted_iota(jnp.int32, (bm, 1), 0)
    prev = jnp.where(rid == 0, hp, pltpu.roll(cv, 1, 0))
    nxt = jnp.where(rid == bm - 1, hn, pltpu.roll(cv, bm - 1, 0))
    w = w_ref[...]
    y = gb_ref[...].astype(F32) * (prev * w[0:1, :] + cv * w[1:2, :] + nxt * w[2:3, :])
    o_ref[...] = y.astype(BF16)


def _conv(px, conv_w, rows, s, c, mx):
    cw = conv_w.shape[1]
    m = px.shape[0]
    bm = _pick(256, s, c)
    hb = bm // 16
    main = lambda col: pl.BlockSpec((bm, cw), lambda i: (i, col))
    prev = lambda col: pl.BlockSpec((16, cw), lambda i: (jnp.maximum(i * hb - 1, 0), col))
    nxt = lambda col: pl.BlockSpec((16, cw), lambda i: (jnp.minimum((i + 1) * hb, m // 16 - 1), col))
    return pl.pallas_call(
        functools.partial(_conv_kernel, bm=bm, s=s, c=c, mx=mx),
        grid=(rows // bm,),
        in_specs=[main(0), main(1), main(2), prev(1), prev(2), nxt(1), nxt(2),
                  pl.BlockSpec((CONV_K, cw), lambda i: (0, 0))],
        out_specs=pl.BlockSpec((bm, cw), lambda i: (i, 0)),
        out_shape=jax.ShapeDtypeStruct((rows, cw), BF16),
        compiler_params=_params(("parallel",)),
        name="gated_conv",
    )(px, px, px, px, px, px, px, conv_w)


def _rope(x, cos, sin, half):
    lane = lax.broadcasted_iota(jnp.int32, x.shape, 1)
    first = (lane % (2 * half)) < half
    rot = jnp.where(first, pltpu.roll(x, LANE - half, 1), pltpu.roll(x, half, 1))
    return x * cos + rot * sin


def _prep_kernel(gq_ref, cq_ref, ckv_ref, gk_ref, kr_ref, qn_ref, kvn_ref, wqb_ref, wkvb_ref,
                 mcos_ref, msin_ref, gcos_ref, gsin_ref, q_ref, k_ref, v_ref, gqo_ref, gko_ref):
    mcos, msin = mcos_ref[...], msin_ref[...]
    gcos, gsin = gcos_ref[...], gsin_ref[...]
    mh = MLA_ROPE // 4
    gh = GQA_HEAD_DIM // 4

    cqn = _rms(cq_ref[...].astype(F32), qn_ref[...]).astype(BF16)
    q = _dot(cqn, wqb_ref[...]) * (MLA_SCALE * LOG2E)
    ckvn = _rms(ckv_ref[...].astype(F32), kvn_ref[...]).astype(BF16)
    kv = _dot(ckvn, wkvb_ref[...])
    kr = _rope(kr_ref[...].astype(F32), mcos, msin, mh).astype(BF16)
    ones = jnp.ones((q.shape[0], LANE), BF16)
    v0 = MLA_HEADS * MLA_NOPE
    for h in range(MLA_HEADS):
        a = h * MLA_QK
        q_ref[:, a:a + LANE] = q[:, a:a + LANE].astype(BF16)
        q_ref[:, a + LANE:a + MLA_QK] = _rope(q[:, a + LANE:a + MLA_QK], mcos, msin, mh).astype(BF16)
        k_ref[:, a:a + LANE] = kv[:, h * LANE:(h + 1) * LANE].astype(BF16)
        k_ref[:, a + LANE:a + MLA_QK] = kr
        v_ref[:, a:a + LANE] = kv[:, v0 + h * LANE:v0 + (h + 1) * LANE].astype(BF16)
        v_ref[:, a + LANE:a + MLA_QK] = ones

    gq = gq_ref[...].astype(F32)
    for h in range(GQA_HEADS):
        a = h * GQA_HEAD_DIM
        gqo_ref[:, a:a + LANE] = (_rope(gq[:, a:a + LANE], gcos, gsin, gh) * (GQA_SCALE * LOG2E)).astype(BF16)
    gk = gk_ref[...].astype(F32)
    for h in range(GQA_KV_HEADS):
        a = h * GQA_HEAD_DIM
        gko_ref[:, a:a + LANE] = _rope(gk[:, a:a + LANE], gcos, gsin, gh).astype(BF16)


def _prep(px, lay, q_norm, kv_norm, wqb_p, wkvb_r, tabs, s, mx):
    m = px.shape[0]
    bm = _pick(512, s, m - mx)
    n_lat = mx // bm
    per = s // bm
    ql, kvl = q_norm.shape[0], kv_norm.shape[0]
    gqw, gkw = GQA_HEADS * GQA_HEAD_DIM, GQA_KV_HEADS * GQA_HEAD_DIM

    def col(width, off):
        assert off % width == 0, (width, off)
        return pl.BlockSpec((bm, width), lambda i: (i, off // width))

    const = lambda r, c: pl.BlockSpec((r, c), lambda i: (0, 0))
    tab = pl.BlockSpec((bm, LANE), lambda i: (jnp.where(i < n_lat, i % per, per + i - n_lat), 0))
    row = lambda width: pl.BlockSpec((bm, width), lambda i: (i, 0))
    hq = MLA_HEADS * MLA_QK
    return pl.pallas_call(
        _prep_kernel,
        grid=(m // bm,),
        in_specs=[col(gqw, lay["gq"]), col(ql, lay["cq"]), col(kvl, lay["ckv"]), col(gkw, lay["gk"]),
                  col(LANE, lay["kr"]), const(1, ql), const(1, kvl), const(ql, hq),
                  const(kvl, MLA_HEADS * (MLA_NOPE + MLA_V)), tab, tab, tab, tab],
        out_specs=[row(hq), row(hq), row(hq), row(gqw), row(gkw)],
        out_shape=[jax.ShapeDtypeStruct((m, hq), BF16), jax.ShapeDtypeStruct((m, hq), BF16),
                   jax.ShapeDtypeStruct((m, hq), BF16),
                   jax.ShapeDtypeStruct((m, gqw), BF16), jax.ShapeDtypeStruct((m, gkw), BF16)],
        compiler_params=_params(("parallel",)),
        name="attn_prep",
    )(px, px, px, px, px, q_norm.reshape(1, ql), kv_norm.reshape(1, kvl), wqb_p, wkvb_r, *tabs)


def _mla_kernel(q_ref, kl_ref, kc_ref, vl_ref, vc_ref, o_ref, *, n_sub):
    kl, kc = kl_ref[...], kc_ref[...]
    subs = [q_ref[i * MLA_SUB:(i + 1) * MLA_SUB, :] for i in range(n_sub)]
    scores = [(_dot_nt(q, kl), _dot_nt(q, kc)) for q in subs]
    for i, (s1, s2) in enumerate(scores):
        m = jnp.maximum(jnp.max(s1, axis=-1, keepdims=True), jnp.max(s2, axis=-1, keepdims=True))
        p1 = jnp.exp2(s1 - m).astype(BF16)
        p2 = jnp.exp2(s2 - m).astype(BF16)
        o = _dot(p1, vl_ref[...]) + _dot(p2, vc_ref[...])
        o_ref[i * MLA_SUB:(i + 1) * MLA_SUB, :] = (o[:, :MLA_V] / o[:, MLA_V:]).astype(BF16)


def _mla_ctx_kernel(q_ref, kc_ref, vc_ref, y_ref, o_ref):
    del y_ref
    s2 = _dot_nt(q_ref[...], kc_ref[...])
    p2 = jnp.exp2(s2 - jnp.max(s2, axis=-1, keepdims=True)).astype(BF16)
    o = _dot(p2, vc_ref[...])
    o_ref[...] = (o[:, :MLA_V] / o[:, MLA_V:]).astype(BF16)


def _mla_attn(q, k, v, b, s, c, with_ctx):
    mx = b * s
    n_sub = max(d for d in (1, 2, 4) if (s // MLA_SUB) % d == 0)
    bq = n_sub * MLA_SUB
    nq = s // bq
    rows = mx + (b * c if with_ctx else 0)
    ctx0 = mx // c
    y = pl.pallas_call(
        functools.partial(_mla_kernel, n_sub=n_sub),
        grid=(b, MLA_HEADS, nq),
        in_specs=[pl.BlockSpec((bq, MLA_QK), lambda bi, h, qi: (bi * nq + qi, h)),
                  pl.BlockSpec((s, MLA_QK), lambda bi, h, qi: (bi, h)),
                  pl.BlockSpec((c, MLA_QK), lambda bi, h, qi: (ctx0 + bi, h)),
                  pl.BlockSpec((s, MLA_QK), lambda bi, h, qi: (bi, h)),
                  pl.BlockSpec((c, MLA_QK), lambda bi, h, qi: (ctx0 + bi, h))],
        out_specs=pl.BlockSpec((bq, MLA_V), lambda bi, h, qi: (bi * nq + qi, h)),
        out_shape=jax.ShapeDtypeStruct((rows, MLA_HEADS * MLA_V), BF16),
        compiler_params=_params(("parallel", "parallel", "arbitrary")),
        name="mla_attn",
    )(q, k, k, v, v)
    if not with_ctx:
        return y
    return pl.pallas_call(
        _mla_ctx_kernel,
        grid=(b, MLA_HEADS),
        in_specs=[pl.BlockSpec((c, MLA_QK), lambda bi, h: (ctx0 + bi, h)),
                  pl.BlockSpec((c, MLA_QK), lambda bi, h: (ctx0 + bi, h)),
                  pl.BlockSpec((c, MLA_QK), lambda bi, h: (ctx0 + bi, h)),
                  pl.BlockSpec(memory_space=pl.ANY)],
        out_specs=pl.BlockSpec((c, MLA_V), lambda bi, h: (ctx0 + bi, h)),
        out_shape=jax.ShapeDtypeStruct(y.shape, BF16),
        input_output_aliases={3: 0},
        compiler_params=_params(("parallel", "parallel")),
        name="mla_attn_ctx",
    )(q, k, v, y)


GQA_QB = 4


def _sink_rows(sink_ref, g, j, rows):
    return jnp.broadcast_to(sink_ref[pl.ds(g * GQA_GROUP + j, 1), :], (rows, LANE))[:, 0:1]


def _gqa_kernel(q_ref, kp_ref, kc_ref, kn_ref, kx_ref, vp_ref, vc_ref, vn_ref, vx_ref, sink_ref, o_ref,
                *, n_lat):
    g = pl.program_id(1)
    n = pl.program_id(2)
    blk = ATT_BLOCK
    kx, vx = kx_ref[...], vx_ref[...]
    kband = jnp.concatenate([kp_ref[...], kc_ref[...], kn_ref[...]], axis=0)
    vband = jnp.concatenate([vp_ref[...], vc_ref[...], vn_ref[...]], axis=0)
    nk = 3 * blk + kx.shape[0]
    ones = jnp.ones((nk, LANE), BF16)
    snk = jnp.concatenate([jnp.broadcast_to(sink_ref[pl.ds(g * GQA_GROUP + j, 1), :], (blk, LANE))
                           for j in range(GQA_GROUP)], axis=0)
    rows = GQA_GROUP * blk
    r = lax.broadcasted_iota(jnp.int32, (rows, nk), 0) % blk
    col = lax.broadcasted_iota(jnp.int32, (rows, nk), 1)
    in_window = (jnp.abs(col - blk - r) <= WINDOW) | (col >= 3 * blk)
    for i in range(GQA_QB):
        qs = jnp.concatenate([q_ref[i * blk:(i + 1) * blk, j * LANE:(j + 1) * LANE] for j in range(GQA_GROUP)],
                             axis=0)
        keys = jnp.concatenate([kband[i * blk:(i + 3) * blk, :], kx], axis=0)
        vals = jnp.concatenate([jnp.concatenate([vband[i * blk:(i + 3) * blk, :], vx], axis=0), ones], axis=1)
        block = n * GQA_QB + i
        valid = in_window & ((col >= blk) | (block > 0)) & ((col < 2 * blk) | (col >= 3 * blk) | (block < n_lat - 1))
        sc = jnp.where(valid, _dot_nt(qs, keys), -jnp.inf)
        e = snk
        for t in range(nk // LANE):
            e = jnp.maximum(e, sc[:, t * LANE:(t + 1) * LANE])
        m = jnp.max(e, axis=-1, keepdims=True)
        ol = _dot(jnp.exp2(sc - m).astype(BF16), vals)
        o = ol[:, :LANE] / (ol[:, LANE:] + jnp.exp2(snk - m))
        for j in range(GQA_GROUP):
            o_ref[i * blk:(i + 1) * blk, j * LANE:(j + 1) * LANE] = o[j * blk:(j + 1) * blk, :].astype(BF16)


def _gqa_ctx_kernel(q_ref, kx_ref, vx_ref, sink_ref, y_ref, o_ref):
    del y_ref
    g = pl.program_id(1)
    kx, vx = kx_ref[...], vx_ref[...]
    for j in range(GQA_GROUP):
        sc = _dot_nt(q_ref[:, j * LANE:(j + 1) * LANE], kx)
        snk = _sink_rows(sink_ref, g, j, sc.shape[0])
        m = jnp.maximum(jnp.max(sc, axis=-1, keepdims=True), snk)
        p_c = jnp.exp2(sc - m)
        l = jnp.sum(p_c, axis=-1, keepdims=True) + jnp.exp2(snk - m)
        o_ref[:, j * LANE:(j + 1) * LANE] = (_dot(p_c.astype(BF16), vx) / l).astype(BF16)


def _gqa_attn(gq, gk, px, o_gv, sink, b, s, c, with_ctx):
    mx = b * s
    blk = ATT_BLOCK
    n_lat = s // blk
    nstep = n_lat // GQA_QB
    rows = mx + (b * c if with_ctx else 0)
    ctx0 = mx // c
    gvc = o_gv // LANE
    gw = GQA_GROUP * GQA_HEAD_DIM
    big = GQA_QB * blk

    def edge(shift, col0):
        def idx(bi, g, n):
            return bi * n_lat + jnp.clip(n * GQA_QB + shift, 0, n_lat - 1), col0 + g
        return pl.BlockSpec((blk, LANE), idx)

    main = lambda col0: pl.BlockSpec((big, LANE), lambda bi, g, n: (bi * nstep + n, col0 + g))
    ctx = lambda col0: pl.BlockSpec((c, LANE), lambda bi, g, n: (ctx0 + bi, col0 + g))
    sink_spec = pl.BlockSpec((GQA_HEADS, LANE), lambda *_: (0, 0))
    y = pl.pallas_call(
        functools.partial(_gqa_kernel, n_lat=n_lat),
        grid=(b, GQA_KV_HEADS, nstep),
        in_specs=[pl.BlockSpec((big, gw), lambda bi, g, n: (bi * nstep + n, g)),
                  edge(-1, 0), main(0), edge(GQA_QB, 0), ctx(0),
                  edge(-1, gvc), main(gvc), edge(GQA_QB, gvc), ctx(gvc), sink_spec],
        out_specs=pl.BlockSpec((big, gw), lambda bi, g, n: (bi * nstep + n, g)),
        out_shape=jax.ShapeDtypeStruct((rows, GQA_HEADS * GQA_HEAD_DIM), BF16),
        compiler_params=_params(("parallel", "parallel", "arbitrary")),
        name="gqa_attn",
    )(gq, gk, gk, gk, gk, px, px, px, px, sink)
    if not with_ctx:
        return y
    return pl.pallas_call(
        _gqa_ctx_kernel,
        grid=(b, GQA_KV_HEADS),
        in_specs=[pl.BlockSpec((c, gw), lambda bi, g: (ctx0 + bi, g)),
                  pl.BlockSpec((c, LANE), lambda bi, g: (ctx0 + bi, g)),
                  pl.BlockSpec((c, LANE), lambda bi, g: (ctx0 + bi, gvc + g)),
                  sink_spec, pl.BlockSpec(memory_space=pl.ANY)],
        out_specs=pl.BlockSpec((c, gw), lambda bi, g: (ctx0 + bi, g)),
        out_shape=jax.ShapeDtypeStruct(y.shape, BF16),
        input_output_aliases={4: 0},
        compiler_params=_params(("parallel", "parallel")),
        name="gqa_attn_ctx",
    )(gq, gk, px, sink, y)


MIX_ROWS = 256


def _mix_out_kernel(yc_ref, ym_ref, yg_ref, gc_ref, gm_ref, gg_ref, wc_ref, wm_ref, wg_ref, wo_ref, h_ref,
                    mod_ref, o_ref):
    gate = lambda ref: jax.nn.sigmoid(ref[...].astype(F32))
    merged = gate(gc_ref) * _dot(yc_ref[...], wc_ref[...])
    merged += gate(gm_ref) * _dot(ym_ref[...], wm_ref[...])
    merged += gate(gg_ref) * _dot(yg_ref[...], wg_ref[...])
    o_ref[...] = h_ref[...] + mod_ref[0, 5:6, :] * _dot(merged.astype(BF16), wo_ref[...])


def _mix_out(yc, ym, yg, px, o_gate, w_bc, w_bm, w_bg, w_out, l, h, mods, rows, s, b):
    d = w_out.shape[2]
    bm = _pick(MIX_ROWS, s, rows)
    per = s // bm
    assert o_gate % d == 0
    g0 = o_gate // d
    yspec = lambda a: pl.BlockSpec((bm, a.shape[1]), lambda i: (i, 0))
    gspec = lambda k: pl.BlockSpec((bm, d), lambda i: (i, g0 + k))
    wspec = lambda w: pl.BlockSpec((None,) + w.shape[1:], lambda i: (l, 0, 0), pipeline_mode=pl.Buffered(1))
    return pl.pallas_call(
        _mix_out_kernel,
        grid=(rows // bm,),
        in_specs=[yspec(yc), yspec(ym), yspec(yg), gspec(0), gspec(1), gspec(2),
                  wspec(w_bc), wspec(w_bm), wspec(w_bg), wspec(w_out),
                  pl.BlockSpec((bm, d), lambda i: (i, 0)),
                  pl.BlockSpec((1, N_MOD, d), lambda i: (jnp.minimum(i // per, b), 0, 0))],
        out_specs=pl.BlockSpec((bm, d), lambda i: (i, 0)),
        out_shape=jax.ShapeDtypeStruct((rows, d), F32),
        compiler_params=_params(("parallel",)),
        name="mix_out",
    )(yc, ym, yg, px, px, px, w_bc, w_bm, w_bg, w_out, h, mods)


def _rope_table(s, n_ctx_rows, dim):
    t = jnp.arange(s, dtype=jnp.int32)
    pos = jnp.stack([t // GRID_W, t % GRID_W], axis=1).astype(F32)
    inv = ROPE_BASE ** (-jnp.arange(0, dim, 2, dtype=F32) / dim)
    lane = jnp.arange(LANE)
    ang = pos[:, jnp.minimum(lane // dim, 1)] * inv[lane % (dim // 2)][None, :]
    active = (lane < 2 * dim)[None, :]
    sign = jnp.where((lane % dim) < dim // 2, -1.0, 1.0)[None, :]
    cos = jnp.where(active, jnp.cos(ang), 1.0)
    sin = jnp.where(active, jnp.sin(ang) * sign, 0.0)
    pad = ((0, n_ctx_rows), (0, 0))
    return jnp.pad(cos, pad, constant_values=1.0), jnp.pad(sin, pad)


def _layout(cw, ql, kvl, d):
    lay = {"conv": 0}
    off = 3 * cw
    for name, width in (("gq", GQA_HEADS * GQA_HEAD_DIM), ("cq", ql), ("ckv", kvl),
                        ("gk", GQA_KV_HEADS * GQA_HEAD_DIM), ("gv", GQA_KV_HEADS * GQA_HEAD_DIM),
                        ("kr", LANE)):
        lay[name] = off
        off += width
    q = max(d, 512)
    off = -(-off // q) * q
    lay["gate"] = off
    lay["total"] = off + 3 * d
    return lay


def _pack_kernel(w_ref, o_ref, *, moves, zero):
    for dst, src, width in moves:
        o_ref[dst:dst + width, :] = w_ref[src:src + width, :].astype(BF16)
    o_ref[zero[0]:zero[1], :] = jnp.zeros((zero[1] - zero[0], o_ref.shape[1]), BF16)


def _pack_w_in(w, lay, cw, ql, kvl):
    depth, d, n = w.shape
    w_t = jnp.swapaxes(w, 1, 2)
    o_mla = 3 * cw
    o_gqa = o_mla + ql + kvl + MLA_ROPE
    gqw, gkw = GQA_HEADS * GQA_HEAD_DIM, GQA_KV_HEADS * GQA_HEAD_DIM
    o_gate = o_gqa + gqw + 2 * gkw
    moves = ((0, 0, o_mla), (lay["gq"], o_gqa, gqw), (lay["cq"], o_mla, ql + kvl),
             (lay["gk"], o_gqa + gqw, 2 * gkw), (lay["kr"], o_mla + ql + kvl, MLA_ROPE),
             (lay["gate"], o_gate, n - o_gate))
    cb = _pick(256, d)
    return pl.pallas_call(
        functools.partial(_pack_kernel, moves=moves, zero=(lay["kr"] + MLA_ROPE, lay["gate"])),
        grid=(depth, d // cb),
        in_specs=[pl.BlockSpec((None, n, cb), lambda l, j: (l, 0, j))],
        out_specs=pl.BlockSpec((None, lay["total"], cb), lambda l, j: (l, 0, j)),
        out_shape=jax.ShapeDtypeStruct((depth, lay["total"], d), BF16),
        compiler_params=_params(("parallel", "parallel")),
        name="pack_w_in",
    )(w_t)


def kernel(x, c, ctx, c_ctx, ada_w, ada_b, ffn1_norm, ffn1_w_gu, ffn1_w_down, mix_norm, w_in, conv_w,
           mla_q_norm, mla_w_qb, mla_kv_norm, mla_w_kvb, gqa_sink, w_branch_conv, w_branch_mla,
           w_branch_gqa, w_out, ffn2_norm, ffn2_w_gu, ffn2_w_down, final_norm):
    b, s, d = x.shape
    cl = ctx.shape[1]
    depth = ada_w.shape[0]
    cw = conv_w.shape[-1]
    ql, kvl = mla_q_norm.shape[-1], mla_kv_norm.shape[-1]
    mx, mc = b * s, b * cl
    assert b + 1 <= 8 and s % cl == 0 and cl % ATT_BLOCK == 0

    cvec = jnp.zeros((8, d), F32).at[:b].set(c).at[b].set(c_ctx)
    mods = _mods(cvec, ada_w, ada_b).reshape(depth, 8, N_MOD, d)

    lay = _layout(cw, ql, kvl, d)
    tabs = _rope_table(s, mc, MLA_ROPE // 2) + _rope_table(s, mc, GQA_HEAD_DIM // 2)
    h = None
    bf = _pick(512, ffn1_w_down.shape[1])
    w1_gu, w1_dn = _gu_to_bf16(ffn1_w_gu, bf), _to_bf16(ffn1_w_down)
    w2_gu, w2_dn = _gu_to_bf16(ffn2_w_gu, bf), _to_bf16(ffn2_w_down)
    w_bc, w_bm, w_bg = _to_bf16(w_branch_conv), _to_bf16(w_branch_mla), _to_bf16(w_branch_gqa)
    w_o = _to_bf16(w_out)
    w_p = _pack_w_in(w_in, lay, cw, ql, kvl)

    for l in range(depth):
        with_ctx = l < depth - 1
        rows = mx + mc if with_ctx else mx
        wqb_p = jnp.pad(mla_w_qb[l].reshape(ql, MLA_HEADS, MLA_NOPE + MLA_ROPE),
                        ((0, 0), (0, 0), (0, MLA_QK - MLA_NOPE - MLA_ROPE))).reshape(ql, -1).astype(BF16)
        wkv = mla_w_kvb[l].reshape(kvl, MLA_HEADS, MLA_NOPE + MLA_V)
        wkvb_r = jnp.concatenate([wkv[:, :, :MLA_NOPE].reshape(kvl, -1), wkv[:, :, MLA_NOPE:].reshape(kvl, -1)],
                                 axis=1).astype(BF16)
        sink = jnp.broadcast_to((gqa_sink[l].astype(F32) * LOG2E)[:, None], (GQA_HEADS, LANE))

        if l == 0:
            h = _ffn(x.reshape(mx, d), ctx.reshape(mc, d), mx + mc, mods[l], ffn1_norm[l], w1_gu, w1_dn, l, 0,
                     s, b, final_norm, False)
        else:
            h = _ffn(h, None, mx + mc, mods[l], ffn1_norm[l], w1_gu, w1_dn, l, 0, s, b, final_norm, False)
        px = _proj(h, mods[l], mix_norm[l], w_p, l, s, b)
        y_conv = _conv(px, conv_w[l], rows, s, cl, mx)
        q, k, v, gq, gk = _prep(px, lay, mla_q_norm[l], mla_kv_norm[l], wqb_p, wkvb_r, tabs, s, mx)
        y_mla = _mla_attn(q, k, v, b, s, cl, with_ctx)
        y_gqa = _gqa_attn(gq, gk, px, lay["gv"], sink, b, s, cl, with_ctx)
        h = _mix_out(y_conv, y_mla, y_gqa, px, lay["gate"], w_bc, w_bm, w_bg, w_o, l, h, mods[l], rows, s, b)
        h = _ffn(h, None, rows, mods[l], ffn2_norm[l], w2_gu, w2_dn, l, 6, s, b, final_norm, not with_ctx)
    return h.reshape(b, s, d)
```

```python
import functools

import jax
import jax.numpy as jnp
from jax import lax
from jax.experimental import pallas as pl
from jax.experimental.pallas import tpu as pltpu

F32 = jnp.float32
BF16 = jnp.bfloat16

GRID_W = 64
N_MOD = 9
EPS = 1e-6
ROPE_BASE = 10000.0
CONV_K = 3
MLA_HEADS = 8
MLA_NOPE = 128
MLA_ROPE = 64
MLA_V = 128
MLA_SCALE = (MLA_NOPE + MLA_ROPE) ** -0.5
GQA_HEADS = 8
GQA_KV_HEADS = 2
GQA_GROUP = GQA_HEADS // GQA_KV_HEADS
GQA_HEAD_DIM = 128
GQA_SCALE = GQA_HEAD_DIM ** -0.5
WINDOW = 128
ATT_BLOCK = 128

LANE = 128
MLA_QK = 2 * LANE
MLA_SUB = 256
LOG2E = 1.4426950408889634
VMEM_LIMIT = 56 << 20
NT_DIMS = (((1,), (1,)), ((), ()))


def _pick(target, *sizes):
    b = target
    while any(s % b for s in sizes):
        b //= 2
        assert b >= 8, (target, sizes)
    return b


def _params(sem):
    return pltpu.CompilerParams(dimension_semantics=sem, vmem_limit_bytes=VMEM_LIMIT)


def _dot(a, b):
    return jnp.dot(a, b, preferred_element_type=F32)


def _dot_nt(a, b):
    return lax.dot_general(a, b, NT_DIMS, preferred_element_type=F32)


def _rms(x, g):
    return x * lax.rsqrt(jnp.mean(x * x, axis=-1, keepdims=True) + EPS) * g


CAST_BLOCK_BYTES = 6 << 20


def _cast_kernel(w_ref, o_ref):
    o_ref[...] = w_ref[...].astype(BF16)


def _to_bf16(w):
    depth, k, n = w.shape
    rows = depth * k
    rb = _pick(max(16, 1 << ((CAST_BLOCK_BYTES // (4 * n)).bit_length() - 1)), rows)
    out = pl.pallas_call(
        _cast_kernel,
        grid=(rows // rb,),
        in_specs=[pl.BlockSpec((rb, n), lambda i: (i, 0))],
        out_specs=pl.BlockSpec((rb, n), lambda i: (i, 0)),
        out_shape=jax.ShapeDtypeStruct((rows, n), BF16),
        compiler_params=_params(("parallel",)),
        name="cast_bf16",
    )(w.reshape(rows, n))
    return out.reshape(depth, k, n)


def _cast_gu_kernel(w_ref, o_ref, *, bf):
    nf = o_ref.shape[0]
    for j in range(nf):
        o_ref[j, :, :bf] = w_ref[:, j * bf:(j + 1) * bf].astype(BF16)
        o_ref[j, :, bf:] = w_ref[:, (nf + j) * bf:(nf + j + 1) * bf].astype(BF16)


def _gu_to_bf16(w, bf):
    depth, k, n = w.shape
    nf = n // (2 * bf)
    rb = _pick(max(16, 1 << ((CAST_BLOCK_BYTES // (4 * n)).bit_length() - 1)), k)
    kb = k // rb
    return pl.pallas_call(
        functools.partial(_cast_gu_kernel, bf=bf),
        grid=(depth, kb),
        in_specs=[pl.BlockSpec((None, rb, n), lambda l, i: (l, i, 0))],
        out_specs=pl.BlockSpec((None, nf, rb, 2 * bf), lambda l, i: (l, 0, i, 0)),
        out_shape=jax.ShapeDtypeStruct((depth, nf, k, 2 * bf), BF16),
        compiler_params=_params(("parallel", "parallel")),
        name="cast_gate_up",
    )(w)


def _mods_kernel(c_ref, w_ref, b_ref, o_ref):
    c = c_ref[...]
    s = (c * jax.nn.sigmoid(c)).astype(BF16)
    o_ref[0] = _dot(s, w_ref[0].astype(BF16)) + b_ref[0]


def _mods(cvec, ada_w, ada_b):
    depth, d, n = ada_w.shape
    bn = _pick(1024, n)
    return pl.pallas_call(
        _mods_kernel,
        grid=(depth, n // bn),
        in_specs=[pl.BlockSpec((8, d), lambda l, j: (0, 0)),
                  pl.BlockSpec((1, d, bn), lambda l, j: (l, 0, j)),
                  pl.BlockSpec((1, 1, bn), lambda l, j: (l, 0, j))],
        out_specs=pl.BlockSpec((1, 8, bn), lambda l, j: (l, 0, j)),
        out_shape=jax.ShapeDtypeStruct((depth, 8, n), F32),
        compiler_params=_params(("parallel", "parallel")),
        name="adaln_mods",
    )(cvec, ada_w, ada_b.reshape(depth, 1, n))


def _norm_mod_rows(h_ref, mod_ref, g_ref, xn_ref, copy_ref, i_shift, i_scale, rc):
    shift = mod_ref[0, i_shift:i_shift + 1, :]
    scale1 = 1.0 + mod_ref[0, i_scale:i_scale + 1, :]
    g = g_ref[...]

    def body(r, carry):
        rows = pl.ds(pl.multiple_of(r * rc, rc), rc)
        x = h_ref[rows, :]
        xn_ref[rows, :] = (_rms(x, g) * scale1 + shift).astype(BF16)
        if copy_ref is not None:
            copy_ref[rows, :] = x
        return carry

    lax.fori_loop(0, h_ref.shape[0] // rc, body, 0)


def _norm_mod_chunk(hn_ref, modn_ref, g_ref, xn_ref, slot, k, ch, i_shift, i_scale):
    r0 = pl.multiple_of(jnp.minimum(k * ch, hn_ref.shape[0] - ch), 16)
    shift = modn_ref[0, i_shift:i_shift + 1, :]
    scale1 = 1.0 + modn_ref[0, i_scale:i_scale + 1, :]
    x = hn_ref[pl.ds(r0, ch), :]
    xn_ref[slot, pl.ds(r0, ch), :] = (_rms(x, g_ref[...]) * scale1 + shift).astype(BF16)


def _chunk_rows(bm, steps):
    return min(bm, -(-bm // (16 * steps)) * 16)


def _mod_spec(d, bm, s, b, ahead=0, last=None):
    per = s // bm

    def idx(i, j):
        t = i if not ahead else jnp.minimum(i + ahead, last)
        return jnp.minimum(t // per, b), 0, 0

    return pl.BlockSpec((1, N_MOD, d), idx)


FFN_UP_ROWS = 1024
FFN_DOWN_ROWS = 256


def _swiglu(gu):
    bf = gu.shape[1] // 2
    gg, uu = gu[:, :bf], gu[:, bf:]
    return (gg * jax.nn.sigmoid(gg) * uu).astype(BF16)


def _up_split_kernel(h_ref, hc_ref, mod_ref, g_ref, w_ref, o_ref, xn_ref, *, i0, rc, n_lat):
    first = pl.program_id(1) == 0
    is_lat = pl.program_id(0) < n_lat

    @pl.when(first & is_lat)
    def _():
        _norm_mod_rows(h_ref, mod_ref, g_ref, xn_ref, None, i0, i0 + 1, rc)

    @pl.when(first & jnp.logical_not(is_lat))
    def _():
        _norm_mod_rows(hc_ref, mod_ref, g_ref, xn_ref, None, i0, i0 + 1, rc)

    o_ref[...] = _swiglu(_dot(xn_ref[...], w_ref[...]))


def _up_kernel(h_ref, mod_ref, modn_ref, g_ref, w_ref, o_ref, xn_ref, *, i0, rc, ch):
    i, j = pl.program_id(0), pl.program_id(1)
    slot = i % 2

    @pl.when((j == 0) & (i == 0))
    def _():
        _norm_mod_rows(h_ref, mod_ref, g_ref, xn_ref.at[0], None, i0, i0 + 1, rc)

    o_ref[...] = _swiglu(_dot(xn_ref[slot], w_ref[...]))
    _norm_mod_chunk(h_ref, modn_ref, g_ref, xn_ref, 1 - slot, jnp.maximum(j - 1, 0), ch, i0, i0 + 1)


def _down_kernel(a_ref, w_ref, h_ref, hc_ref, mod_ref, fg_ref, o_ref, *, i_gate, n_lat, final):
    h = h_ref[...]
    if n_lat is not None:
        h = jnp.where(pl.program_id(0) < n_lat, h, hc_ref[...])
    out = h + (0.5 * mod_ref[0, i_gate:i_gate + 1, :]) * _dot(a_ref[...], w_ref[...])
    o_ref[...] = _rms(out, fg_ref[...]) if final else out


def _ffn(h, hc, rows, mods, norm_g, w_gu, w_down, l, i0, s, b, final_g, final):
    d = h.shape[1]
    nf, bf = w_gu.shape[1], w_gu.shape[3] // 2
    f = nf * bf
    vec = pl.BlockSpec((1, d), lambda i, j: (0, 0))
    w_spec = pl.BlockSpec((None, None, d, 2 * bf), lambda i, j: (l, j, 0, 0))
    bm = _pick(FFN_UP_ROWS, s, rows) if hc is None else _pick(FFN_UP_ROWS, s, hc.shape[0])
    nt = rows // bm
    up_common = dict(
        grid=(nt, nf),
        out_specs=pl.BlockSpec((bm, bf), lambda i, j: (i, j)),
        out_shape=jax.ShapeDtypeStruct((rows, f), BF16),
        compiler_params=_params(("arbitrary", "arbitrary")),
        name="ffn_up")
    if hc is not None:
        n_lat = h.shape[0] // bm
        act = pl.pallas_call(
            functools.partial(_up_split_kernel, i0=i0, rc=_pick(128, bm), n_lat=n_lat),
            in_specs=[pl.BlockSpec((bm, d), lambda i, j: (jnp.minimum(i, n_lat - 1), 0)),
                      pl.BlockSpec((bm, d), lambda i, j: (jnp.maximum(i - n_lat, 0), 0)),
                      _mod_spec(d, bm, s, b), vec, w_spec],
            scratch_shapes=[pltpu.VMEM((bm, d), BF16)],
            **up_common,
        )(h, hc, mods, norm_g.reshape(1, d), w_gu)
    else:
        assert nf >= 2
        h_idx = lambda i, j: (jnp.where((i == 0) & (j == 0), 0, jnp.minimum(i + 1, nt - 1)), 0)
        act = pl.pallas_call(
            functools.partial(_up_kernel, i0=i0, rc=_pick(128, bm), ch=_chunk_rows(bm, nf - 1)),
            in_specs=[pl.BlockSpec((bm, d), h_idx), _mod_spec(d, bm, s, b), _mod_spec(d, bm, s, b, 1, nt - 1),
                      vec, w_spec],
            scratch_shapes=[pltpu.VMEM((2, bm, d), BF16)],
            **up_common,
        )(h, mods, mods, norm_g.reshape(1, d), w_gu)

    bd = _pick(FFN_DOWN_ROWS, s, rows) if hc is None else _pick(FFN_DOWN_ROWS, s, hc.shape[0])
    per = s // bd
    if hc is None:
        n_lat = None
        h_specs = [pl.BlockSpec((bd, d), lambda i: (i, 0)), pl.BlockSpec((1, d), lambda i: (0, 0))]
        hc = norm_g.reshape(1, d)
    else:
        n_lat = h.shape[0] // bd
        h_specs = [pl.BlockSpec((bd, d), lambda i: (jnp.minimum(i, n_lat - 1), 0)),
                   pl.BlockSpec((bd, d), lambda i: (jnp.maximum(i - n_lat, 0), 0))]
    return pl.pallas_call(
        functools.partial(_down_kernel, i_gate=i0 + 2, n_lat=n_lat, final=final),
        grid=(rows // bd,),
        in_specs=[pl.BlockSpec((bd, f), lambda i: (i, 0)),
                  pl.BlockSpec((None, f, d), lambda i: (l, 0, 0), pipeline_mode=pl.Buffered(1))] + h_specs + [
                  pl.BlockSpec((1, N_MOD, d), lambda i: (jnp.minimum(i // per, b), 0, 0)),
                  pl.BlockSpec((1, d), lambda i: (0, 0))],
        out_specs=pl.BlockSpec((bd, d), lambda i: (i, 0)),
        out_shape=jax.ShapeDtypeStruct((rows, d), F32),
        compiler_params=_params(("parallel",)),
        name="ffn_down",
    )(act, w_down, h, hc, mods, final_g.reshape(1, d))


def _proj_kernel(h_ref, mod_ref, modn_ref, g_ref, w_ref, o_ref, xn_ref, *, rc, ch):
    i, j = pl.program_id(0), pl.program_id(1)
    slot = i % 2

    @pl.when((j == 0) & (i == 0))
    def _():
        _norm_mod_rows(h_ref, mod_ref, g_ref, xn_ref.at[0], None, 3, 4, rc)

    o_ref[...] = _dot_nt(xn_ref[slot], w_ref[...]).astype(BF16)
    _norm_mod_chunk(h_ref, modn_ref, g_ref, xn_ref, 1 - slot, jnp.maximum(j - 1, 0), ch, 3, 4)


def _proj(h, mods, norm_g, w_p, l, s, b):
    rows, d = h.shape
    n = w_p.shape[1]
    bm = _pick(1024, s, rows)
    bn = _pick(1024, n)
    nt = rows // bm
    assert n // bn >= 2
    h_idx = lambda i, j: (jnp.where((i == 0) & (j == 0), 0, jnp.minimum(i + 1, nt - 1)), 0)
    return pl.pallas_call(
        functools.partial(_proj_kernel, rc=_pick(128, bm), ch=_chunk_rows(bm, n // bn - 1)),
        grid=(nt, n // bn),
        in_specs=[pl.BlockSpec((bm, d), h_idx),
                  _mod_spec(d, bm, s, b), _mod_spec(d, bm, s, b, 1, nt - 1),
                  pl.BlockSpec((1, d), lambda i, j: (0, 0)),
                  pl.BlockSpec((None, bn, d), lambda i, j: (l, j, 0))],
        out_specs=pl.BlockSpec((bm, bn), lambda i, j: (i, j)),
        out_shape=jax.ShapeDtypeStruct((rows, n), BF16),
        scratch_shapes=[pltpu.VMEM((2, bm, d), BF16)],
        compiler_params=_params(("arbitrary", "arbitrary")),
        name="in_proj",
    )(h, mods, mods, norm_g.reshape(1, d), w_p)


def _conv_kernel(gb_ref, gc_ref, v_ref, gcp_ref, vp_ref, gcn_ref, vn_ref, w_ref, o_ref, *, bm, s, c, mx):
    row0 = pl.program_id(0) * bm
    is_lat = row0 < mx
    at_start = jnp.where(is_lat, row0 % s == 0, (row0 - mx) % c == 0)
    at_end = jnp.where(is_lat, (row0 + bm) % s == 0, (row0 - mx + bm) % c == 0)
    cv = gc_ref[...].astype(F32) * v_ref[...].astype(F32)
    hp = (gcp_ref[...].astype(F32) * vp_ref[...].astype(F32))[15:16, :]
    hn = (gcn_ref[...].astype(F32) * vn_ref[...].astype(F32))[0:1, :]
    hp = jnp.where(at_start, 0.0, hp)
    hn = jnp.where(at_end, 0.0, hn)
    rid = lax.broadcasted_iota(jnp.int32, (bm, 1), 0)
    prev = jnp.where(rid == 0, hp, pltpu.roll(cv, 1, 0))
    nxt = jnp.where(rid == bm - 1, hn, pltpu.roll(cv, bm - 1, 0))
    w = w_ref[...]
    y = gb_ref[...].astype(F32) * (prev * w[0:1, :] + cv * w[1:2, :] + nxt * w[2:3, :])
    o_ref[...] = y.astype(BF16)


def _conv(px, conv_w, rows, s, c, mx):
    cw = conv_w.shape[1]
    m = px.shape[0]
    bm = _pick(256, s, c)
    hb = bm // 16
    main = lambda col: pl.BlockSpec((bm, cw), lambda i: (i, col))
    prev = lambda col: pl.BlockSpec((16, cw), lambda i: (jnp.maximum(i * hb - 1, 0), col))
    nxt = lambda col: pl.BlockSpec((16, cw), lambda i: (jnp.minimum((i + 1) * hb, m // 16 - 1), col))
    return pl.pallas_call(
        functools.partial(_conv_kernel, bm=bm, s=s, c=c, mx=mx),
        grid=(rows // bm,),
        in_specs=[main(0), main(1), main(2), prev(1), prev(2), nxt(1), nxt(2),
                  pl.BlockSpec((CONV_K, cw), lambda i: (0, 0))],
        out_specs=pl.BlockSpec((bm, cw), lambda i: (i, 0)),
        out_shape=jax.ShapeDtypeStruct((rows, cw), BF16),
        compiler_params=_params(("parallel",)),
        name="gated_conv",
    )(px, px, px, px, px, px, px, conv_w)


def _rope(x, cos, sin, half):
    lane = lax.broadcasted_iota(jnp.int32, x.shape, 1)
    first = (lane % (2 * half)) < half
    rot = jnp.where(first, pltpu.roll(x, LANE - half, 1), pltpu.roll(x, half, 1))
    return x * cos + rot * sin


def _prep_kernel(gq_ref, cq_ref, ckv_ref, gk_ref, kr_ref, qn_ref, kvn_ref, wqb_ref, wknt_ref, wv_ref,
                 mcos_ref, msin_ref, gcos_ref, gsin_ref, q_ref, kt_ref, v_ref, gqo_ref, gko_ref):
    mcos, msin = mcos_ref[...], msin_ref[...]
    gcos, gsin = gcos_ref[...], gsin_ref[...]
    mh = MLA_ROPE // 4
    gh = GQA_HEAD_DIM // 4

    cqn = _rms(cq_ref[...].astype(F32), qn_ref[...]).astype(BF16)
    q = _dot(cqn, wqb_ref[...]) * (MLA_SCALE * LOG2E)
    ckvn = _rms(ckv_ref[...].astype(F32), kvn_ref[...]).astype(BF16)
    knt = _dot_nt(wknt_ref[...], ckvn)
    vv = _dot(ckvn, wv_ref[...])
    krt = _rope(kr_ref[...].astype(F32), mcos, msin, mh).T.astype(BF16)
    ones = jnp.ones((q.shape[0], LANE), BF16)
    for h in range(MLA_HEADS):
        a = h * MLA_QK
        q_ref[:, a:a + LANE] = q[:, a:a + LANE].astype(BF16)
        q_ref[:, a + LANE:a + MLA_QK] = _rope(q[:, a + LANE:a + MLA_QK], mcos, msin, mh).astype(BF16)
        kt_ref[a:a + LANE, :] = knt[h * LANE:(h + 1) * LANE, :].astype(BF16)
        kt_ref[a + LANE:a + MLA_QK, :] = krt
        v_ref[:, a:a + LANE] = vv[:, h * LANE:(h + 1) * LANE].astype(BF16)
        v_ref[:, a + LANE:a + MLA_QK] = ones

    gq = gq_ref[...].astype(F32)
    for h in range(GQA_HEADS):
        a = h * GQA_HEAD_DIM
        gqo_ref[:, a:a + LANE] = (_rope(gq[:, a:a + LANE], gcos, gsin, gh) * (GQA_SCALE * LOG2E)).astype(BF16)
    gk = gk_ref[...].astype(F32)
    for h in range(GQA_KV_HEADS):
        a = h * GQA_HEAD_DIM
        gko_ref[:, a:a + LANE] = _rope(gk[:, a:a + LANE], gcos, gsin, gh).astype(BF16)


def _prep(px, lay, q_norm, kv_norm, wqb_p, wkn_t, wv, tabs, s, mx):
    m = px.shape[0]
    bm = _pick(512, s, m - mx)
    n_lat = mx // bm
    per = s // bm
    ql, kvl = q_norm.shape[0], kv_norm.shape[0]
    gqw, gkw = GQA_HEADS * GQA_HEAD_DIM, GQA_KV_HEADS * GQA_HEAD_DIM

    def col(width, off):
        assert off % width == 0, (width, off)
        return pl.BlockSpec((bm, width), lambda i: (i, off // width))

    const = lambda r, c: pl.BlockSpec((r, c), lambda i: (0, 0))
    tab = pl.BlockSpec((bm, LANE), lambda i: (jnp.where(i < n_lat, i % per, per + i - n_lat), 0))
    row = lambda width: pl.BlockSpec((bm, width), lambda i: (i, 0))
    hq = MLA_HEADS * MLA_QK
    return pl.pallas_call(
        _prep_kernel,
        grid=(m // bm,),
        in_specs=[col(gqw, lay["gq"]), col(ql, lay["cq"]), col(kvl, lay["ckv"]), col(gkw, lay["gk"]),
                  col(LANE, lay["kr"]), const(1, ql), const(1, kvl), const(ql, hq),
                  const(MLA_HEADS * MLA_NOPE, kvl), const(kvl, MLA_HEADS * MLA_V), tab, tab, tab, tab],
        out_specs=[row(hq), pl.BlockSpec((hq, bm), lambda i: (0, i)), row(hq), row(gqw), row(gkw)],
        out_shape=[jax.ShapeDtypeStruct((m, hq), BF16), jax.ShapeDtypeStruct((hq, m), BF16),
                   jax.ShapeDtypeStruct((m, hq), BF16),
                   jax.ShapeDtypeStruct((m, gqw), BF16), jax.ShapeDtypeStruct((m, gkw), BF16)],
        compiler_params=_params(("parallel",)),
        name="attn_prep",
    )(px, px, px, px, px, q_norm.reshape(1, ql), kv_norm.reshape(1, kvl), wqb_p, wkn_t, wv, *tabs)


def _mla_kernel(q_ref, kl_ref, kc_ref, vl_ref, vc_ref, o_ref, *, n_sub):
    kl, kc = kl_ref[...], kc_ref[...]
    subs = [q_ref[i * MLA_SUB:(i + 1) * MLA_SUB, :] for i in range(n_sub)]
    scores = [(_dot(q, kl), _dot(q, kc)) for q in subs]
    for i, (s1, s2) in enumerate(scores):
        m = jnp.maximum(jnp.max(s1, axis=-1, keepdims=True), jnp.max(s2, axis=-1, keepdims=True))
        p1 = jnp.exp2((s1 - m).astype(BF16))
        p2 = jnp.exp2((s2 - m).astype(BF16))
        o = _dot(p1, vl_ref[...]) + _dot(p2, vc_ref[...])
        o_ref[i * MLA_SUB:(i + 1) * MLA_SUB, :] = (o[:, :MLA_V] / o[:, MLA_V:]).astype(BF16)


def _mla_ctx_kernel(q_ref, kc_ref, vc_ref, y_ref, o_ref):
    del y_ref
    s2 = _dot(q_ref[...], kc_ref[...])
    p2 = jnp.exp2(s2 - jnp.max(s2, axis=-1, keepdims=True)).astype(BF16)
    o = _dot(p2, vc_ref[...])
    o_ref[...] = (o[:, :MLA_V] / o[:, MLA_V:]).astype(BF16)


def _mla_attn(q, kt, v, b, s, c, with_ctx):
    mx = b * s
    n_sub = max(d for d in (1, 2, 4) if (s // MLA_SUB) % d == 0)
    bq = n_sub * MLA_SUB
    nq = s // bq
    rows = mx + (b * c if with_ctx else 0)
    ctx0 = mx // c
    y = pl.pallas_call(
        functools.partial(_mla_kernel, n_sub=n_sub),
        grid=(b, MLA_HEADS, nq),
        in_specs=[pl.BlockSpec((bq, MLA_QK), lambda bi, h, qi: (bi * nq + qi, h)),
                  pl.BlockSpec((MLA_QK, s), lambda bi, h, qi: (h, bi)),
                  pl.BlockSpec((MLA_QK, c), lambda bi, h, qi: (h, ctx0 + bi)),
                  pl.BlockSpec((s, MLA_QK), lambda bi, h, qi: (bi, h)),
                  pl.BlockSpec((c, MLA_QK), lambda bi, h, qi: (ctx0 + bi, h))],
        out_specs=pl.BlockSpec((bq, MLA_V), lambda bi, h, qi: (bi * nq + qi, h)),
        out_shape=jax.ShapeDtypeStruct((rows, MLA_HEADS * MLA_V), BF16),
        compiler_params=_params(("parallel", "parallel", "arbitrary")),
        name="mla_attn",
    )(q, kt, kt, v, v)
    if not with_ctx:
        return y
    return pl.pallas_call(
        _mla_ctx_kernel,
        grid=(b, MLA_HEADS),
        in_specs=[pl.BlockSpec((c, MLA_QK), lambda bi, h: (ctx0 + bi, h)),
                  pl.BlockSpec((MLA_QK, c), lambda bi, h: (h, ctx0 + bi)),
                  pl.BlockSpec((c, MLA_QK), lambda bi, h: (ctx0 + bi, h)),
                  pl.BlockSpec(memory_space=pl.ANY)],
        out_specs=pl.BlockSpec((c, MLA_V), lambda bi, h: (ctx0 + bi, h)),
        out_shape=jax.ShapeDtypeStruct(y.shape, BF16),
        input_output_aliases={3: 0},
        compiler_params=_params(("parallel", "parallel")),
        name="mla_attn_ctx",
    )(q, kt, v, y)


GQA_QB_MAX = 8


def _sink_rows(sink_ref, g, j, rows):
    return jnp.broadcast_to(sink_ref[pl.ds(g * GQA_GROUP + j, 1), :], (rows, LANE))[:, 0:1]


def _gqa_kernel(q_ref, kp_ref, kc_ref, kn_ref, kx_ref, vp_ref, vc_ref, vn_ref, vx_ref, sink_ref, o_ref,
                *, n_lat, qb):
    g = pl.program_id(1)
    n = pl.program_id(2)
    blk = ATT_BLOCK
    kx, vx = kx_ref[...], vx_ref[...]
    kband = jnp.concatenate([kp_ref[...], kc_ref[...], kn_ref[...]], axis=0)
    vband = jnp.concatenate([vp_ref[...], vc_ref[...], vn_ref[...]], axis=0)
    nk = 3 * blk + kx.shape[0]
    ones = jnp.ones((nk, LANE), BF16)
    snk = jnp.concatenate([jnp.broadcast_to(sink_ref[pl.ds(g * GQA_GROUP + j, 1), :], (blk, LANE))
                           for j in range(GQA_GROUP)], axis=0)
    rows = GQA_GROUP * blk
    r = lax.broadcasted_iota(jnp.int32, (rows, nk), 0) % blk
    col = lax.broadcasted_iota(jnp.int32, (rows, nk), 1)
    in_window = (jnp.abs(col - blk - r) <= WINDOW) | (col >= 3 * blk)
    for i in range(qb):
        qs = jnp.concatenate([q_ref[i * blk:(i + 1) * blk, j * LANE:(j + 1) * LANE] for j in range(GQA_GROUP)],
                             axis=0)
        keys = jnp.concatenate([kband[i * blk:(i + 3) * blk, :], kx], axis=0)
        vals = jnp.concatenate([jnp.concatenate([vband[i * blk:(i + 3) * blk, :], vx], axis=0), ones], axis=1)
        block = n * qb + i
        valid = in_window & ((col >= blk) | (block > 0)) & ((col < 2 * blk) | (col >= 3 * blk) | (block < n_lat - 1))
        sc = jnp.where(valid, _dot_nt(qs, keys), -jnp.inf)
        e = snk
        for t in range(nk // LANE):
            e = jnp.maximum(e, sc[:, t * LANE:(t + 1) * LANE])
        m = jnp.max(e, axis=-1, keepdims=True)
        ol = _dot(jnp.exp2((sc - m).astype(BF16)), vals)
        o = ol[:, :LANE] / (ol[:, LANE:] + jnp.exp2(snk - m))
        for j in range(GQA_GROUP):
            o_ref[i * blk:(i + 1) * blk, j * LANE:(j + 1) * LANE] = o[j * blk:(j + 1) * blk, :].astype(BF16)


def _gqa_ctx_kernel(q_ref, kx_ref, vx_ref, sink_ref, y_ref, o_ref):
    del y_ref
    g = pl.program_id(1)
    kx, vx = kx_ref[...], vx_ref[...]
    for j in range(GQA_GROUP):
        sc = _dot_nt(q_ref[:, j * LANE:(j + 1) * LANE], kx)
        snk = _sink_rows(sink_ref, g, j, sc.shape[0])
        m = jnp.maximum(jnp.max(sc, axis=-1, keepdims=True), snk)
        p_c = jnp.exp2(sc - m)
        l = jnp.sum(p_c, axis=-1, keepdims=True) + jnp.exp2(snk - m)
        o_ref[:, j * LANE:(j + 1) * LANE] = (_dot(p_c.astype(BF16), vx) / l).astype(BF16)


def _gqa_attn(gq, gk, px, o_gv, sink, b, s, c, with_ctx):
    mx = b * s
    blk = ATT_BLOCK
    n_lat = s // blk
    qb = max(k for k in (1, 2, 4, 8) if k <= GQA_QB_MAX and n_lat % k == 0)
    nstep = n_lat // qb
    rows = mx + (b * c if with_ctx else 0)
    ctx0 = mx // c
    gvc = o_gv // LANE
    gw = GQA_GROUP * GQA_HEAD_DIM
    big = qb * blk

    def edge(shift, col0):
        def idx(bi, g, n):
            return bi * n_lat + jnp.clip(n * qb + shift, 0, n_lat - 1), col0 + g
        return pl.BlockSpec((blk, LANE), idx)

    main = lambda col0: pl.BlockSpec((big, LANE), lambda bi, g, n: (bi * nstep + n, col0 + g))
    ctx = lambda col0: pl.BlockSpec((c, LANE), lambda bi, g, n: (ctx0 + bi, col0 + g))
    sink_spec = pl.BlockSpec((GQA_HEADS, LANE), lambda *_: (0, 0))
    y = pl.pallas_call(
        functools.partial(_gqa_kernel, n_lat=n_lat, qb=qb),
        grid=(b, GQA_KV_HEADS, nstep),
        in_specs=[pl.BlockSpec((big, gw), lambda bi, g, n: (bi * nstep + n, g)),
                  edge(-1, 0), main(0), edge(qb, 0), ctx(0),
                  edge(-1, gvc), main(gvc), edge(qb, gvc), ctx(gvc), sink_spec],
        out_specs=pl.BlockSpec((big, gw), lambda bi, g, n: (bi * nstep + n, g)),
        out_shape=jax.ShapeDtypeStruct((rows, GQA_HEADS * GQA_HEAD_DIM), BF16),
        compiler_params=_params(("parallel", "parallel", "arbitrary")),
        name="gqa_attn",
    )(gq, gk, gk, gk, gk, px, px, px, px, sink)
    if not with_ctx:
        return y
    return pl.pallas_call(
        _gqa_ctx_kernel,
        grid=(b, GQA_KV_HEADS),
        in_specs=[pl.BlockSpec((c, gw), lambda bi, g: (ctx0 + bi, g)),
                  pl.BlockSpec((c, LANE), lambda bi, g: (ctx0 + bi, g)),
                  pl.BlockSpec((c, LANE), lambda bi, g: (ctx0 + bi, gvc + g)),
                  sink_spec, pl.BlockSpec(memory_space=pl.ANY)],
        out_specs=pl.BlockSpec((c, gw), lambda bi, g: (ctx0 + bi, g)),
        out_shape=jax.ShapeDtypeStruct(y.shape, BF16),
        input_output_aliases={4: 0},
        compiler_params=_params(("parallel", "parallel")),
        name="gqa_attn_ctx",
    )(gq, gk, px, sink, y)


MIX_ROWS = 256


def _mix_out_kernel(yc_ref, ym_ref, yg_ref, gc_ref, gm_ref, gg_ref, wc_ref, wm_ref, wg_ref, wo_ref, h_ref,
                    mod_ref, o_ref):
    gate = lambda ref: jax.nn.sigmoid(ref[...].astype(F32))
    merged = gate(gc_ref) * _dot(yc_ref[...], wc_ref[...])
    merged += gate(gm_ref) * _dot(ym_ref[...], wm_ref[...])
    merged += gate(gg_ref) * _dot(yg_ref[...], wg_ref[...])
    o_ref[...] = h_ref[...] + mod_ref[0, 5:6, :] * _dot(merged.astype(BF16), wo_ref[...])


def _mix_out(yc, ym, yg, px, o_gate, w_bc, w_bm, w_bg, w_out, l, h, mods, rows, s, b):
    d = w_out.shape[2]
    bm = _pick(MIX_ROWS, s, rows)
    per = s // bm
    assert o_gate % d == 0
    g0 = o_gate // d
    yspec = lambda a: pl.BlockSpec((bm, a.shape[1]), lambda i: (i, 0))
    gspec = lambda k: pl.BlockSpec((bm, d), lambda i: (i, g0 + k))
    wspec = lambda w: pl.BlockSpec((None,) + w.shape[1:], lambda i: (l, 0, 0), pipeline_mode=pl.Buffered(1))
    return pl.pallas_call(
        _mix_out_kernel,
        grid=(rows // bm,),
        in_specs=[yspec(yc), yspec(ym), yspec(yg), gspec(0), gspec(1), gspec(2),
                  wspec(w_bc), wspec(w_bm), wspec(w_bg), wspec(w_out),
                  pl.BlockSpec((bm, d), lambda i: (i, 0)),
                  pl.BlockSpec((1, N_MOD, d), lambda i: (jnp.minimum(i // per, b), 0, 0))],
        out_specs=pl.BlockSpec((bm, d), lambda i: (i, 0)),
        out_shape=jax.ShapeDtypeStruct((rows, d), F32),
        compiler_params=_params(("parallel",)),
        name="mix_out",
    )(yc, ym, yg, px, px, px, w_bc, w_bm, w_bg, w_out, h, mods)


def _rope_table(s, n_ctx_rows, dim):
    t = jnp.arange(s, dtype=jnp.int32)
    pos = jnp.stack([t // GRID_W, t % GRID_W], axis=1).astype(F32)
    inv = ROPE_BASE ** (-jnp.arange(0, dim, 2, dtype=F32) / dim)
    lane = jnp.arange(LANE)
    ang = pos[:, jnp.minimum(lane // dim, 1)] * inv[lane % (dim // 2)][None, :]
    active = (lane < 2 * dim)[None, :]
    sign = jnp.where((lane % dim) < dim // 2, -1.0, 1.0)[None, :]
    cos = jnp.where(active, jnp.cos(ang), 1.0)
    sin = jnp.where(active, jnp.sin(ang) * sign, 0.0)
    pad = ((0, n_ctx_rows), (0, 0))
    return jnp.pad(cos, pad, constant_values=1.0), jnp.pad(sin, pad)


def _layout(cw, ql, kvl, d):
    lay = {"conv": 0}
    off = 3 * cw
    for name, width in (("gq", GQA_HEADS * GQA_HEAD_DIM), ("cq", ql), ("ckv", kvl),
                        ("gk", GQA_KV_HEADS * GQA_HEAD_DIM), ("gv", GQA_KV_HEADS * GQA_HEAD_DIM),
                        ("kr", LANE)):
        lay[name] = off
        off += width
    q = max(d, 512)
    off = -(-off // q) * q
    lay["gate"] = off
    lay["total"] = off + 3 * d
    return lay


def _pack_kernel(w_ref, o_ref, *, moves, zero):
    for dst, src, width in moves:
        o_ref[dst:dst + width, :] = w_ref[src:src + width, :].astype(BF16)
    o_ref[zero[0]:zero[1], :] = jnp.zeros((zero[1] - zero[0], o_ref.shape[1]), BF16)


def _pack_w_in(w, lay, cw, ql, kvl):
    depth, d, n = w.shape
    w_t = jnp.swapaxes(w, 1, 2)
    o_mla = 3 * cw
    o_gqa = o_mla + ql + kvl + MLA_ROPE
    gqw, gkw = GQA_HEADS * GQA_HEAD_DIM, GQA_KV_HEADS * GQA_HEAD_DIM
    o_gate = o_gqa + gqw + 2 * gkw
    moves = ((0, 0, o_mla), (lay["gq"], o_gqa, gqw), (lay["cq"], o_mla, ql + kvl),
             (lay["gk"], o_gqa + gqw, 2 * gkw), (lay["kr"], o_mla + ql + kvl, MLA_ROPE),
             (lay["gate"], o_gate, n - o_gate))
    cb = _pick(256, d)
    return pl.pallas_call(
        functools.partial(_pack_kernel, moves=moves, zero=(lay["kr"] + MLA_ROPE, lay["gate"])),
        grid=(depth, d // cb),
        in_specs=[pl.BlockSpec((None, n, cb), lambda l, j: (l, 0, j))],
        out_specs=pl.BlockSpec((None, lay["total"], cb), lambda l, j: (l, 0, j)),
        out_shape=jax.ShapeDtypeStruct((depth, lay["total"], d), BF16),
        compiler_params=_params(("parallel", "parallel")),
        name="pack_w_in",
    )(w_t)


def kernel(x, c, ctx, c_ctx, ada_w, ada_b, ffn1_norm, ffn1_w_gu, ffn1_w_down, mix_norm, w_in, conv_w,
           mla_q_norm, mla_w_qb, mla_kv_norm, mla_w_kvb, gqa_sink, w_branch_conv, w_branch_mla,
           w_branch_gqa, w_out, ffn2_norm, ffn2_w_gu, ffn2_w_down, final_norm):
    b, s, d = x.shape
    cl = ctx.shape[1]
    depth = ada_w.shape[0]
    cw = conv_w.shape[-1]
    ql, kvl = mla_q_norm.shape[-1], mla_kv_norm.shape[-1]
    mx, mc = b * s, b * cl
    assert b + 1 <= 8 and s % cl == 0 and cl % ATT_BLOCK == 0

    cvec = jnp.zeros((8, d), F32).at[:b].set(c).at[b].set(c_ctx)
    mods = _mods(cvec, ada_w, ada_b).reshape(depth, 8, N_MOD, d)

    lay = _layout(cw, ql, kvl, d)
    tabs = _rope_table(s, mc, MLA_ROPE // 2) + _rope_table(s, mc, GQA_HEAD_DIM // 2)
    h = None
    bf = _pick(512, ffn1_w_down.shape[1])
    w1_gu, w1_dn = _gu_to_bf16(ffn1_w_gu, bf), _to_bf16(ffn1_w_down)
    w2_gu, w2_dn = _gu_to_bf16(ffn2_w_gu, bf), _to_bf16(ffn2_w_down)
    w_bc, w_bm, w_bg = _to_bf16(w_branch_conv), _to_bf16(w_branch_mla), _to_bf16(w_branch_gqa)
    w_o = _to_bf16(w_out)
    w_p = _pack_w_in(w_in, lay, cw, ql, kvl)

    for l in range(depth):
        with_ctx = l < depth - 1
        rows = mx + mc if with_ctx else mx
        wqb_p = jnp.pad(mla_w_qb[l].reshape(ql, MLA_HEADS, MLA_NOPE + MLA_ROPE),
                        ((0, 0), (0, 0), (0, MLA_QK - MLA_NOPE - MLA_ROPE))).reshape(ql, -1).astype(BF16)
        wkv = mla_w_kvb[l].reshape(kvl, MLA_HEADS, MLA_NOPE + MLA_V)
        wkn_t = wkv[:, :, :MLA_NOPE].reshape(kvl, -1).T.astype(BF16)
        wv = wkv[:, :, MLA_NOPE:].reshape(kvl, -1).astype(BF16)
        sink = jnp.broadcast_to((gqa_sink[l].astype(F32) * LOG2E)[:, None], (GQA_HEADS, LANE))

        if l == 0:
            h = _ffn(x.reshape(mx, d), ctx.reshape(mc, d), mx + mc, mods[l], ffn1_norm[l], w1_gu, w1_dn, l, 0,
                     s, b, final_norm, False)
        else:
            h = _ffn(h, None, mx + mc, mods[l], ffn1_norm[l], w1_gu, w1_dn, l, 0, s, b, final_norm, False)
        px = _proj(h, mods[l], mix_norm[l], w_p, l, s, b)
        y_conv = _conv(px, conv_w[l], rows, s, cl, mx)
        q, kt, v, gq, gk = _prep(px, lay, mla_q_norm[l], mla_kv_norm[l], wqb_p, wkn_t, wv, tabs, s, mx)
        y_mla = _mla_attn(q, kt, v, b, s, cl, with_ctx)
        y_gqa = _gqa_attn(gq, gk, px, lay["gv"], sink, b, s, cl, with_ctx)
        h = _mix_out(y_conv, y_mla, y_gqa, px, lay["gate"], w_bc, w_bm, w_bg, w_o, l, h, mods[l], rows, s, b)
        h = _ffn(h, None, rows, mods[l], ffn2_norm[l], w2_gu, w2_dn, l, 6, s, b, final_norm, not with_ctx)
    return h.reshape(b, s, d)
```

```python
import functools

import jax
import jax.numpy as jnp
from jax import lax
from jax.experimental import pallas as pl
from jax.experimental.pallas import tpu as pltpu

F32 = jnp.float32
BF16 = jnp.bfloat16

GRID_W = 64
N_MOD = 9
EPS = 1e-6
ROPE_BASE = 10000.0
CONV_K = 3
MLA_HEADS = 8
MLA_NOPE = 128
MLA_ROPE = 64
MLA_V = 128
MLA_SCALE = (MLA_NOPE + MLA_ROPE) ** -0.5
GQA_HEADS = 8
GQA_KV_HEADS = 2
GQA_GROUP = GQA_HEADS // GQA_KV_HEADS
GQA_HEAD_DIM = 128
GQA_SCALE = GQA_HEAD_DIM ** -0.5
WINDOW = 128
ATT_BLOCK = 128

LANE = 128
MLA_QK = 2 * LANE
MLA_SUB = 256
LOG2E = 1.4426950408889634
VMEM_LIMIT = 56 << 20
NT_DIMS = (((1,), (1,)), ((), ()))


def _pick(target, *sizes):
    b = target
    while any(s % b for s in sizes):
        b //= 2
        assert b >= 8, (target, sizes)
    return b


def _params(sem):
    return pltpu.CompilerParams(dimension_semantics=sem, vmem_limit_bytes=VMEM_LIMIT)


def _dot(a, b):
    return jnp.dot(a, b, preferred_element_type=F32)


def _dot_nt(a, b):
    return lax.dot_general(a, b, NT_DIMS, preferred_element_type=F32)


def _rms(x, g):
    return x * lax.rsqrt(jnp.mean(x * x, axis=-1, keepdims=True) + EPS) * g


CAST_BLOCK_BYTES = 6 << 20


def _cast_kernel(w_ref, o_ref):
    o_ref[...] = w_ref[...].astype(BF16)


def _to_bf16(w):
    depth, k, n = w.shape
    rows = depth * k
    rb = _pick(max(16, 1 << ((CAST_BLOCK_BYTES // (4 * n)).bit_length() - 1)), rows)
    out = pl.pallas_call(
        _cast_kernel,
        grid=(rows // rb,),
        in_specs=[pl.BlockSpec((rb, n), lambda i: (i, 0))],
        out_specs=pl.BlockSpec((rb, n), lambda i: (i, 0)),
        out_shape=jax.ShapeDtypeStruct((rows, n), BF16),
        compiler_params=_params(("parallel",)),
        name="cast_bf16",
    )(w.reshape(rows, n))
    return out.reshape(depth, k, n)


def _cast_gu_kernel(w_ref, o_ref, *, bf):
    nf = o_ref.shape[0]
    for j in range(nf):
        o_ref[j, :, :bf] = w_ref[:, j * bf:(j + 1) * bf].astype(BF16)
        o_ref[j, :, bf:] = w_ref[:, (nf + j) * bf:(nf + j + 1) * bf].astype(BF16)


def _gu_to_bf16(w, bf):
    depth, k, n = w.shape
    nf = n // (2 * bf)
    rb = _pick(max(16, 1 << ((CAST_BLOCK_BYTES // (4 * n)).bit_length() - 1)), k)
    kb = k // rb
    return pl.pallas_call(
        functools.partial(_cast_gu_kernel, bf=bf),
        grid=(depth, kb),
        in_specs=[pl.BlockSpec((None, rb, n), lambda l, i: (l, i, 0))],
        out_specs=pl.BlockSpec((None, nf, rb, 2 * bf), lambda l, i: (l, 0, i, 0)),
        out_shape=jax.ShapeDtypeStruct((depth, nf, k, 2 * bf), BF16),
        compiler_params=_params(("parallel", "parallel")),
        name="cast_gate_up",
    )(w)


def _mods_kernel(c_ref, w_ref, b_ref, o_ref):
    c = c_ref[...]
    s = (c * jax.nn.sigmoid(c)).astype(BF16)
    o_ref[0] = _dot(s, w_ref[0].astype(BF16)) + b_ref[0]


def _mods(cvec, ada_w, ada_b):
    depth, d, n = ada_w.shape
    bn = _pick(1024, n)
    return pl.pallas_call(
        _mods_kernel,
        grid=(depth, n // bn),
        in_specs=[pl.BlockSpec((8, d), lambda l, j: (0, 0)),
                  pl.BlockSpec((1, d, bn), lambda l, j: (l, 0, j)),
                  pl.BlockSpec((1, 1, bn), lambda l, j: (l, 0, j))],
        out_specs=pl.BlockSpec((1, 8, bn), lambda l, j: (l, 0, j)),
        out_shape=jax.ShapeDtypeStruct((depth, 8, n), F32),
        compiler_params=_params(("parallel", "parallel")),
        name="adaln_mods",
    )(cvec, ada_w, ada_b.reshape(depth, 1, n))


def _norm_mod_rows(h_ref, mod_ref, g_ref, xn_ref, copy_ref, i_shift, i_scale, rc):
    shift = mod_ref[0, i_shift:i_shift + 1, :]
    scale1 = 1.0 + mod_ref[0, i_scale:i_scale + 1, :]
    g = g_ref[...]

    def body(r, carry):
        rows = pl.ds(pl.multiple_of(r * rc, rc), rc)
        x = h_ref[rows, :]
        xn_ref[rows, :] = (_rms(x, g) * scale1 + shift).astype(BF16)
        if copy_ref is not None:
            copy_ref[rows, :] = x
        return carry

    lax.fori_loop(0, h_ref.shape[0] // rc, body, 0)


def _norm_mod_chunk(hn_ref, modn_ref, g_ref, xn_ref, slot, k, ch, i_shift, i_scale):
    r0 = pl.multiple_of(jnp.minimum(k * ch, hn_ref.shape[0] - ch), 16)
    shift = modn_ref[0, i_shift:i_shift + 1, :]
    scale1 = 1.0 + modn_ref[0, i_scale:i_scale + 1, :]
    x = hn_ref[pl.ds(r0, ch), :]
    xn_ref[slot, pl.ds(r0, ch), :] = (_rms(x, g_ref[...]) * scale1 + shift).astype(BF16)


def _chunk_rows(bm, steps):
    return min(bm, -(-bm // (16 * steps)) * 16)


def _mod_spec(d, bm, s, b, ahead=0, last=None):
    per = s // bm

    def idx(i, j):
        t = i if not ahead else jnp.minimum(i + ahead, last)
        return jnp.minimum(t // per, b), 0, 0

    return pl.BlockSpec((1, N_MOD, d), idx)


FFN_UP_ROWS = 1024
FFN_DOWN_ROWS = 256

def _swiglu(gu):
    bf = gu.shape[1] // 2
    gg, uu = gu[:, :bf], gu[:, bf:]
    return (gg * jax.nn.sigmoid(gg) * uu).astype(BF16)


def _up_split_kernel(h_ref, hc_ref, mod_ref, g_ref, w_ref, o_ref, xn_ref, *, i0, rc, n_lat):
    first = pl.program_id(1) == 0
    is_lat = pl.program_id(0) < n_lat

    @pl.when(first & is_lat)
    def _():
        _norm_mod_rows(h_ref, mod_ref, g_ref, xn_ref, None, i0, i0 + 1, rc)

    @pl.when(first & jnp.logical_not(is_lat))
    def _():
        _norm_mod_rows(hc_ref, mod_ref, g_ref, xn_ref, None, i0, i0 + 1, rc)

    o_ref[...] = _swiglu(_dot(xn_ref[...], w_ref[...]))


def _up_kernel(h_ref, mod_ref, modn_ref, g_ref, w_ref, o_ref, xn_ref, *, i0, rc, ch):
    i, j = pl.program_id(0), pl.program_id(1)
    slot = i % 2

    @pl.when((j == 0) & (i == 0))
    def _():
        _norm_mod_rows(h_ref, mod_ref, g_ref, xn_ref.at[0], None, i0, i0 + 1, rc)

    o_ref[...] = _swiglu(_dot(xn_ref[slot], w_ref[...]))
    _norm_mod_chunk(h_ref, modn_ref, g_ref, xn_ref, 1 - slot, jnp.maximum(j - 1, 0), ch, i0, i0 + 1)


def _down_kernel(a_ref, w_ref, h_ref, hc_ref, mod_ref, fg_ref, o_ref, *, i_gate, n_lat, final):
    h = h_ref[...]
    if n_lat is not None:
        h = jnp.where(pl.program_id(0) < n_lat, h, hc_ref[...])
    out = h + (0.5 * mod_ref[0, i_gate:i_gate + 1, :]) * _dot(a_ref[...], w_ref[...])
    o_ref[...] = _rms(out, fg_ref[...]) if final else out


def _ffn(h, hc, rows, mods, norm_g, w_gu, w_down, l, i0, s, b, final_g, final):
    d = h.shape[1]
    nf, bf = w_gu.shape[1], w_gu.shape[3] // 2
    f = nf * bf
    vec = pl.BlockSpec((1, d), lambda i, j: (0, 0))
    w_spec = pl.BlockSpec((None, None, d, 2 * bf), lambda i, j: (l, j, 0, 0))
    bm = _pick(FFN_UP_ROWS, s, rows) if hc is None else _pick(FFN_UP_ROWS, s, hc.shape[0])
    nt = rows // bm
    up_common = dict(
        grid=(nt, nf),
        out_specs=pl.BlockSpec((bm, bf), lambda i, j: (i, j)),
        out_shape=jax.ShapeDtypeStruct((rows, f), BF16),
        compiler_params=_params(("arbitrary", "arbitrary")),
        name="ffn_up")
    if hc is not None:
        n_lat = h.shape[0] // bm
        act = pl.pallas_call(
            functools.partial(_up_split_kernel, i0=i0, rc=_pick(128, bm), n_lat=n_lat),
            in_specs=[pl.BlockSpec((bm, d), lambda i, j: (jnp.minimum(i, n_lat - 1), 0)),
                      pl.BlockSpec((bm, d), lambda i, j: (jnp.maximum(i - n_lat, 0), 0)),
                      _mod_spec(d, bm, s, b), vec, w_spec],
            scratch_shapes=[pltpu.VMEM((bm, d), BF16)],
            **up_common,
        )(h, hc, mods, norm_g.reshape(1, d), w_gu)
    else:
        assert nf >= 2
        h_idx = lambda i, j: (jnp.where((i == 0) & (j == 0), 0, jnp.minimum(i + 1, nt - 1)), 0)
        act = pl.pallas_call(
            functools.partial(_up_kernel, i0=i0, rc=_pick(128, bm), ch=_chunk_rows(bm, nf - 1)),
            in_specs=[pl.BlockSpec((bm, d), h_idx), _mod_spec(d, bm, s, b), _mod_spec(d, bm, s, b, 1, nt - 1),
                      vec, w_spec],
            scratch_shapes=[pltpu.VMEM((2, bm, d), BF16)],
            **up_common,
        )(h, mods, mods, norm_g.reshape(1, d), w_gu)

    bd = _pick(FFN_DOWN_ROWS, s, rows) if hc is None else _pick(FFN_DOWN_ROWS, s, hc.shape[0])
    per = s // bd
    if hc is None:
        n_lat = None
        h_specs = [pl.BlockSpec((bd, d), lambda i: (i, 0)), pl.BlockSpec((1, d), lambda i: (0, 0))]
        hc = norm_g.reshape(1, d)
    else:
        n_lat = h.shape[0] // bd
        h_specs = [pl.BlockSpec((bd, d), lambda i: (jnp.minimum(i, n_lat - 1), 0)),
                   pl.BlockSpec((bd, d), lambda i: (jnp.maximum(i - n_lat, 0), 0))]
    return pl.pallas_call(
        functools.partial(_down_kernel, i_gate=i0 + 2, n_lat=n_lat, final=final),
        grid=(rows // bd,),
        in_specs=[pl.BlockSpec((bd, f), lambda i: (i, 0)),
                  pl.BlockSpec((None, f, d), lambda i: (l, 0, 0), pipeline_mode=pl.Buffered(1))] + h_specs + [
                  pl.BlockSpec((1, N_MOD, d), lambda i: (jnp.minimum(i // per, b), 0, 0)),
                  pl.BlockSpec((1, d), lambda i: (0, 0))],
        out_specs=pl.BlockSpec((bd, d), lambda i: (i, 0)),
        out_shape=jax.ShapeDtypeStruct((rows, d), F32),
        compiler_params=_params(("parallel",)),
        name="ffn_down",
    )(act, w_down, h, hc, mods, final_g.reshape(1, d))


def _proj_kernel(h_ref, mod_ref, modn_ref, g_ref, w_ref, o_ref, xn_ref, *, rc, ch):
    i, j = pl.program_id(0), pl.program_id(1)
    slot = i % 2

    @pl.when((j == 0) & (i == 0))
    def _():
        _norm_mod_rows(h_ref, mod_ref, g_ref, xn_ref.at[0], None, 3, 4, rc)

    o_ref[...] = _dot_nt(xn_ref[slot], w_ref[...]).astype(BF16)
    _norm_mod_chunk(h_ref, modn_ref, g_ref, xn_ref, 1 - slot, jnp.maximum(j - 1, 0), ch, 3, 4)


def _proj(h, mods, norm_g, w_p, l, s, b):
    rows, d = h.shape
    n = w_p.shape[1]
    bm = _pick(1024, s, rows)
    bn = _pick(1024, n)
    nt = rows // bm
    assert n // bn >= 2
    h_idx = lambda i, j: (jnp.where((i == 0) & (j == 0), 0, jnp.minimum(i + 1, nt - 1)), 0)
    return pl.pallas_call(
        functools.partial(_proj_kernel, rc=_pick(128, bm), ch=_chunk_rows(bm, n // bn - 1)),
        grid=(nt, n // bn),
        in_specs=[pl.BlockSpec((bm, d), h_idx),
                  _mod_spec(d, bm, s, b), _mod_spec(d, bm, s, b, 1, nt - 1),
                  pl.BlockSpec((1, d), lambda i, j: (0, 0)),
                  pl.BlockSpec((None, bn, d), lambda i, j: (l, j, 0))],
        out_specs=pl.BlockSpec((bm, bn), lambda i, j: (i, j)),
        out_shape=jax.ShapeDtypeStruct((rows, n), BF16),
        scratch_shapes=[pltpu.VMEM((2, bm, d), BF16)],
        compiler_params=_params(("arbitrary", "arbitrary")),
        name="in_proj",
    )(h, mods, mods, norm_g.reshape(1, d), w_p)


def _conv_tile(gb_ref, gc_ref, v_ref, gcp_ref, vp_ref, gcn_ref, vn_ref, w_ref, *, s, c, mx):
    bm = gb_ref.shape[0]
    row0 = pl.program_id(0) * bm
    is_lat = row0 < mx
    at_start = jnp.where(is_lat, row0 % s == 0, (row0 - mx) % c == 0)
    at_end = jnp.where(is_lat, (row0 + bm) % s == 0, (row0 - mx + bm) % c == 0)
    cv = gc_ref[...].astype(F32) * v_ref[...].astype(F32)
    hp = (gcp_ref[...].astype(F32) * vp_ref[...].astype(F32))[15:16, :]
    hn = (gcn_ref[...].astype(F32) * vn_ref[...].astype(F32))[0:1, :]
    hp = jnp.where(at_start, 0.0, hp)
    hn = jnp.where(at_end, 0.0, hn)
    rid = lax.broadcasted_iota(jnp.int32, (bm, 1), 0)
    prev = jnp.where(rid == 0, hp, pltpu.roll(cv, 1, 0))
    nxt = jnp.where(rid == bm - 1, hn, pltpu.roll(cv, bm - 1, 0))
    w = w_ref[...]
    y = gb_ref[...].astype(F32) * (prev * w[0:1, :] + cv * w[1:2, :] + nxt * w[2:3, :])
    return y.astype(BF16)


def _conv_specs(px, cw, bm):
    m = px.shape[0]
    hb = bm // 16
    main = lambda col: pl.BlockSpec((bm, cw), lambda i: (i, col))
    prev = lambda col: pl.BlockSpec((16, cw), lambda i: (jnp.maximum(i * hb - 1, 0), col))
    nxt = lambda col: pl.BlockSpec((16, cw), lambda i: (jnp.minimum((i + 1) * hb, m // 16 - 1), col))
    return [main(0), main(1), main(2), prev(1), prev(2), nxt(1), nxt(2), pl.BlockSpec((CONV_K, cw), lambda i: (0, 0))]


def _rope(x, cos, sin, half):
    lane = lax.broadcasted_iota(jnp.int32, x.shape, 1)
    first = (lane % (2 * half)) < half
    rot = jnp.where(first, pltpu.roll(x, LANE - half, 1), pltpu.roll(x, half, 1))
    return x * cos + rot * sin


def _prep_kernel(gq_ref, cq_ref, ckv_ref, gk_ref, kr_ref, qn_ref, kvn_ref, wqb_ref, wknt_ref, wv_ref,
                 mcos_ref, msin_ref, gcos_ref, gsin_ref, q_ref, kt_ref, v_ref, gqo_ref, gko_ref):
    mcos, msin = mcos_ref[...], msin_ref[...]
    gcos, gsin = gcos_ref[...], gsin_ref[...]
    mh = MLA_ROPE // 4
    gh = GQA_HEAD_DIM // 4

    cqn = _rms(cq_ref[...].astype(F32), qn_ref[...]).astype(BF16)
    q = _dot(cqn, wqb_ref[...]) * (MLA_SCALE * LOG2E)
    ckvn = _rms(ckv_ref[...].astype(F32), kvn_ref[...]).astype(BF16)
    knt = _dot_nt(wknt_ref[...], ckvn)
    vv = _dot(ckvn, wv_ref[...])
    krt = _rope(kr_ref[...].astype(F32), mcos, msin, mh).T.astype(BF16)
    ones = jnp.ones((q.shape[0], LANE), BF16)
    for h in range(MLA_HEADS):
        a = h * MLA_QK
        q_ref[:, a:a + LANE] = q[:, a:a + LANE].astype(BF16)
        q_ref[:, a + LANE:a + MLA_QK] = _rope(q[:, a + LANE:a + MLA_QK], mcos, msin, mh).astype(BF16)
        kt_ref[a:a + LANE, :] = knt[h * LANE:(h + 1) * LANE, :].astype(BF16)
        kt_ref[a + LANE:a + MLA_QK, :] = krt
        v_ref[:, a:a + LANE] = vv[:, h * LANE:(h + 1) * LANE].astype(BF16)
        v_ref[:, a + LANE:a + MLA_QK] = ones

    gq = gq_ref[...].astype(F32)
    for h in range(GQA_HEADS):
        a = h * GQA_HEAD_DIM
        gqo_ref[:, a:a + LANE] = (_rope(gq[:, a:a + LANE], gcos, gsin, gh) * (GQA_SCALE * LOG2E)).astype(BF16)
    gk = gk_ref[...].astype(F32)
    for h in range(GQA_KV_HEADS):
        a = h * GQA_HEAD_DIM
        gko_ref[:, a:a + LANE] = _rope(gk[:, a:a + LANE], gcos, gsin, gh).astype(BF16)


def _prep(px, lay, q_norm, kv_norm, wqb_p, wkn_t, wv, tabs, s, mx):
    m = px.shape[0]
    bm = _pick(512, s, m - mx)
    n_lat = mx // bm
    per = s // bm
    ql, kvl = q_norm.shape[0], kv_norm.shape[0]
    gqw, gkw = GQA_HEADS * GQA_HEAD_DIM, GQA_KV_HEADS * GQA_HEAD_DIM

    def col(width, off):
        assert off % width == 0, (width, off)
        return pl.BlockSpec((bm, width), lambda i: (i, off // width))

    const = lambda r, c: pl.BlockSpec((r, c), lambda i: (0, 0))
    tab = pl.BlockSpec((bm, LANE), lambda i: (jnp.where(i < n_lat, i % per, per + i - n_lat), 0))
    row = lambda width: pl.BlockSpec((bm, width), lambda i: (i, 0))
    hq = MLA_HEADS * MLA_QK
    return pl.pallas_call(
        _prep_kernel,
        grid=(m // bm,),
        in_specs=[col(gqw, lay["gq"]), col(ql, lay["cq"]), col(kvl, lay["ckv"]), col(gkw, lay["gk"]),
                  col(LANE, lay["kr"]), const(1, ql), const(1, kvl), const(ql, hq),
                  const(MLA_HEADS * MLA_NOPE, kvl), const(kvl, MLA_HEADS * MLA_V), tab, tab, tab, tab],
        out_specs=[row(hq), pl.BlockSpec((hq, bm), lambda i: (0, i)), row(hq), row(gqw), row(gkw)],
        out_shape=[jax.ShapeDtypeStruct((m, hq), BF16), jax.ShapeDtypeStruct((hq, m), BF16),
                   jax.ShapeDtypeStruct((m, hq), BF16),
                   jax.ShapeDtypeStruct((m, gqw), BF16), jax.ShapeDtypeStruct((m, gkw), BF16)],
        compiler_params=_params(("parallel",)),
        name="attn_prep",
    )(px, px, px, px, px, q_norm.reshape(1, ql), kv_norm.reshape(1, kvl), wqb_p, wkn_t, wv, *tabs)


def _mla_kernel(q_ref, kl_ref, kc_ref, vl_ref, vc_ref, o_ref, *, n_sub):
    kl, kc = kl_ref[...], kc_ref[...]
    subs = [q_ref[i * MLA_SUB:(i + 1) * MLA_SUB, :] for i in range(n_sub)]
    scores = [(_dot(q, kl), _dot(q, kc)) for q in subs]
    for i, (s1, s2) in enumerate(scores):
        m = jnp.maximum(jnp.max(s1, axis=-1, keepdims=True), jnp.max(s2, axis=-1, keepdims=True))
        p1 = jnp.exp2((s1 - m).astype(BF16))
        p2 = jnp.exp2((s2 - m).astype(BF16))
        o = _dot(p1, vl_ref[...]) + _dot(p2, vc_ref[...])
        o_ref[i * MLA_SUB:(i + 1) * MLA_SUB, :] = (o[:, :MLA_V] / o[:, MLA_V:]).astype(BF16)


def _mla_ctx_kernel(q_ref, kc_ref, vc_ref, y_ref, o_ref):
    del y_ref
    s2 = _dot(q_ref[...], kc_ref[...])
    p2 = jnp.exp2(s2 - jnp.max(s2, axis=-1, keepdims=True)).astype(BF16)
    o = _dot(p2, vc_ref[...])
    o_ref[...] = (o[:, :MLA_V] / o[:, MLA_V:]).astype(BF16)


def _mla_attn(q, kt, v, b, s, c, with_ctx):
    mx = b * s
    n_sub = max(d for d in (1, 2, 4) if (s // MLA_SUB) % d == 0)
    bq = n_sub * MLA_SUB
    nq = s // bq
    rows = mx + (b * c if with_ctx else 0)
    ctx0 = mx // c
    y = pl.pallas_call(
        functools.partial(_mla_kernel, n_sub=n_sub),
        grid=(b, MLA_HEADS, nq),
        in_specs=[pl.BlockSpec((bq, MLA_QK), lambda bi, h, qi: (bi * nq + qi, h)),
                  pl.BlockSpec((MLA_QK, s), lambda bi, h, qi: (h, bi)),
                  pl.BlockSpec((MLA_QK, c), lambda bi, h, qi: (h, ctx0 + bi)),
                  pl.BlockSpec((s, MLA_QK), lambda bi, h, qi: (bi, h)),
                  pl.BlockSpec((c, MLA_QK), lambda bi, h, qi: (ctx0 + bi, h))],
        out_specs=pl.BlockSpec((bq, MLA_V), lambda bi, h, qi: (bi * nq + qi, h)),
        out_shape=jax.ShapeDtypeStruct((rows, MLA_HEADS * MLA_V), BF16),
        compiler_params=_params(("parallel", "parallel", "arbitrary")),
        name="mla_attn",
    )(q, kt, kt, v, v)
    if not with_ctx:
        return y
    return pl.pallas_call(
        _mla_ctx_kernel,
        grid=(b, MLA_HEADS),
        in_specs=[pl.BlockSpec((c, MLA_QK), lambda bi, h: (ctx0 + bi, h)),
                  pl.BlockSpec((MLA_QK, c), lambda bi, h: (h, ctx0 + bi)),
                  pl.BlockSpec((c, MLA_QK), lambda bi, h: (ctx0 + bi, h)),
                  pl.BlockSpec(memory_space=pl.ANY)],
        out_specs=pl.BlockSpec((c, MLA_V), lambda bi, h: (ctx0 + bi, h)),
        out_shape=jax.ShapeDtypeStruct(y.shape, BF16),
        input_output_aliases={3: 0},
        compiler_params=_params(("parallel", "parallel")),
        name="mla_attn_ctx",
    )(q, kt, v, y)


GQA_QB_MAX = 8


def _sink_rows(sink_ref, g, j, rows):
    return jnp.broadcast_to(sink_ref[pl.ds(g * GQA_GROUP + j, 1), :], (rows, LANE))[:, 0:1]


def _gqa_kernel(q_ref, kp_ref, kc_ref, kn_ref, kx_ref, vp_ref, vc_ref, vn_ref, vx_ref, sink_ref, o_ref,
                *, n_lat, qb):
    g = pl.program_id(1)
    n = pl.program_id(2)
    blk = ATT_BLOCK
    kx, vx = kx_ref[...], vx_ref[...]
    kband = jnp.concatenate([kp_ref[...], kc_ref[...], kn_ref[...]], axis=0)
    vband = jnp.concatenate([vp_ref[...], vc_ref[...], vn_ref[...]], axis=0)
    nk = 3 * blk + kx.shape[0]
    ones = jnp.ones((nk, LANE), BF16)
    snk = jnp.concatenate([jnp.broadcast_to(sink_ref[pl.ds(g * GQA_GROUP + j, 1), :], (blk, LANE))
                           for j in range(GQA_GROUP)], axis=0)
    rows = GQA_GROUP * blk
    r = lax.broadcasted_iota(jnp.int32, (rows, nk), 0) % blk
    col = lax.broadcasted_iota(jnp.int32, (rows, nk), 1)
    in_window = (jnp.abs(col - blk - r) <= WINDOW) | (col >= 3 * blk)
    for i in range(qb):
        qs = jnp.concatenate([q_ref[i * blk:(i + 1) * blk, j * LANE:(j + 1) * LANE] for j in range(GQA_GROUP)],
                             axis=0)
        keys = jnp.concatenate([kband[i * blk:(i + 3) * blk, :], kx], axis=0)
        vals = jnp.concatenate([jnp.concatenate([vband[i * blk:(i + 3) * blk, :], vx], axis=0), ones], axis=1)
        block = n * qb + i
        valid = in_window & ((col >= blk) | (block > 0)) & ((col < 2 * blk) | (col >= 3 * blk) | (block < n_lat - 1))
        sc = jnp.where(valid, _dot_nt(qs, keys), -jnp.inf)
        e = snk
        for t in range(nk // LANE):
            e = jnp.maximum(e, sc[:, t * LANE:(t + 1) * LANE])
        m = jnp.max(e, axis=-1, keepdims=True)
        ol = _dot(jnp.exp2((sc - m).astype(BF16)), vals)
        o = ol[:, :LANE] / (ol[:, LANE:] + jnp.exp2(snk - m))
        for j in range(GQA_GROUP):
            o_ref[i * blk:(i + 1) * blk, j * LANE:(j + 1) * LANE] = o[j * blk:(j + 1) * blk, :].astype(BF16)


def _gqa_ctx_kernel(q_ref, kx_ref, vx_ref, sink_ref, y_ref, o_ref):
    del y_ref
    g = pl.program_id(1)
    kx, vx = kx_ref[...], vx_ref[...]
    for j in range(GQA_GROUP):
        sc = _dot_nt(q_ref[:, j * LANE:(j + 1) * LANE], kx)
        snk = _sink_rows(sink_ref, g, j, sc.shape[0])
        m = jnp.maximum(jnp.max(sc, axis=-1, keepdims=True), snk)
        p_c = jnp.exp2(sc - m)
        l = jnp.sum(p_c, axis=-1, keepdims=True) + jnp.exp2(snk - m)
        o_ref[:, j * LANE:(j + 1) * LANE] = (_dot(p_c.astype(BF16), vx) / l).astype(BF16)


def _gqa_attn(gq, gk, px, o_gv, sink, b, s, c, with_ctx):
    mx = b * s
    blk = ATT_BLOCK
    n_lat = s // blk
    qb = max(k for k in (1, 2, 4, 8) if k <= GQA_QB_MAX and n_lat % k == 0)
    nstep = n_lat // qb
    rows = mx + (b * c if with_ctx else 0)
    ctx0 = mx // c
    gvc = o_gv // LANE
    gw = GQA_GROUP * GQA_HEAD_DIM
    big = qb * blk

    def edge(shift, col0):
        def idx(bi, g, n):
            return bi * n_lat + jnp.clip(n * qb + shift, 0, n_lat - 1), col0 + g
        return pl.BlockSpec((blk, LANE), idx)

    main = lambda col0: pl.BlockSpec((big, LANE), lambda bi, g, n: (bi * nstep + n, col0 + g))
    ctx = lambda col0: pl.BlockSpec((c, LANE), lambda bi, g, n: (ctx0 + bi, col0 + g))
    sink_spec = pl.BlockSpec((GQA_HEADS, LANE), lambda *_: (0, 0))
    y = pl.pallas_call(
        functools.partial(_gqa_kernel, n_lat=n_lat, qb=qb),
        grid=(b, GQA_KV_HEADS, nstep),
        in_specs=[pl.BlockSpec((big, gw), lambda bi, g, n: (bi * nstep + n, g)),
                  edge(-1, 0), main(0), edge(qb, 0), ctx(0),
                  edge(-1, gvc), main(gvc), edge(qb, gvc), ctx(gvc), sink_spec],
        out_specs=pl.BlockSpec((big, gw), lambda bi, g, n: (bi * nstep + n, g)),
        out_shape=jax.ShapeDtypeStruct((rows, GQA_HEADS * GQA_HEAD_DIM), BF16),
        compiler_params=_params(("parallel", "parallel", "arbitrary")),
        name="gqa_attn",
    )(gq, gk, gk, gk, gk, px, px, px, px, sink)
    if not with_ctx:
        return y
    return pl.pallas_call(
        _gqa_ctx_kernel,
        grid=(b, GQA_KV_HEADS),
        in_specs=[pl.BlockSpec((c, gw), lambda bi, g: (ctx0 + bi, g)),
                  pl.BlockSpec((c, LANE), lambda bi, g: (ctx0 + bi, g)),
                  pl.BlockSpec((c, LANE), lambda bi, g: (ctx0 + bi, gvc + g)),
                  sink_spec, pl.BlockSpec(memory_space=pl.ANY)],
        out_specs=pl.BlockSpec((c, gw), lambda bi, g: (ctx0 + bi, g)),
        out_shape=jax.ShapeDtypeStruct(y.shape, BF16),
        input_output_aliases={4: 0},
        compiler_params=_params(("parallel", "parallel")),
        name="gqa_attn_ctx",
    )(gq, gk, px, sink, y)


MIX_ROWS = 256


def _mix_out_kernel(gb_ref, gcv_ref, v_ref, gcp_ref, vp_ref, gcn_ref, vn_ref, cw_ref, ym_ref, yg_ref,
                    gc_ref, gm_ref, gg_ref, wc_ref, wm_ref, wg_ref, wo_ref, h_ref, mod_ref, o_ref, *, s, c, mx):
    yc = _conv_tile(gb_ref, gcv_ref, v_ref, gcp_ref, vp_ref, gcn_ref, vn_ref, cw_ref, s=s, c=c, mx=mx)
    gate = lambda ref: jax.nn.sigmoid(ref[...].astype(F32))
    merged = gate(gc_ref) * _dot(yc, wc_ref[...])
    merged += gate(gm_ref) * _dot(ym_ref[...], wm_ref[...])
    merged += gate(gg_ref) * _dot(yg_ref[...], wg_ref[...])
    o_ref[...] = h_ref[...] + mod_ref[0, 5:6, :] * _dot(merged.astype(BF16), wo_ref[...])


def _mix_out(ym, yg, px, conv_w, o_gate, w_bc, w_bm, w_bg, w_out, l, h, mods, rows, s, c, mx, b):
    d = w_out.shape[2]
    bm = _pick(MIX_ROWS, s, c)
    per = s // bm
    assert o_gate % d == 0
    g0 = o_gate // d
    yspec = lambda a: pl.BlockSpec((bm, a.shape[1]), lambda i: (i, 0))
    gspec = lambda k: pl.BlockSpec((bm, d), lambda i: (i, g0 + k))
    wspec = lambda w: pl.BlockSpec((None,) + w.shape[1:], lambda i: (l, 0, 0), pipeline_mode=pl.Buffered(1))
    return pl.pallas_call(
        functools.partial(_mix_out_kernel, s=s, c=c, mx=mx),
        grid=(rows // bm,),
        in_specs=_conv_specs(px, conv_w.shape[1], bm) + [
                  yspec(ym), yspec(yg), gspec(0), gspec(1), gspec(2),
                  wspec(w_bc), wspec(w_bm), wspec(w_bg), wspec(w_out),
                  pl.BlockSpec((bm, d), lambda i: (i, 0)),
                  pl.BlockSpec((1, N_MOD, d), lambda i: (jnp.minimum(i // per, b), 0, 0))],
        out_specs=pl.BlockSpec((bm, d), lambda i: (i, 0)),
        out_shape=jax.ShapeDtypeStruct((rows, d), F32),
        compiler_params=_params(("parallel",)),
        name="mix_out",
    )(px, px, px, px, px, px, px, conv_w, ym, yg, px, px, px, w_bc, w_bm, w_bg, w_out, h, mods)


def _rope_table(s, n_ctx_rows, dim):
    t = jnp.arange(s, dtype=jnp.int32)
    pos = jnp.stack([t // GRID_W, t % GRID_W], axis=1).astype(F32)
    inv = ROPE_BASE ** (-jnp.arange(0, dim, 2, dtype=F32) / dim)
    lane = jnp.arange(LANE)
    ang = pos[:, jnp.minimum(lane // dim, 1)] * inv[lane % (dim // 2)][None, :]
    active = (lane < 2 * dim)[None, :]
    sign = jnp.where((lane % dim) < dim // 2, -1.0, 1.0)[None, :]
    cos = jnp.where(active, jnp.cos(ang), 1.0)
    sin = jnp.where(active, jnp.sin(ang) * sign, 0.0)
    pad = ((0, n_ctx_rows), (0, 0))
    return jnp.pad(cos, pad, constant_values=1.0), jnp.pad(sin, pad)


def _layout(cw, ql, kvl, d):
    lay = {"conv": 0}
    off = 3 * cw
    for name, width in (("gq", GQA_HEADS * GQA_HEAD_DIM), ("cq", ql), ("ckv", kvl),
                        ("gk", GQA_KV_HEADS * GQA_HEAD_DIM), ("gv", GQA_KV_HEADS * GQA_HEAD_DIM),
                        ("kr", LANE)):
        lay[name] = off
        off += width
    q = max(d, 512)
    off = -(-off // q) * q
    lay["gate"] = off
    lay["total"] = off + 3 * d
    return lay


def _pack_kernel(w_ref, o_ref, *, moves, zero):
    for dst, src, width in moves:
        o_ref[dst:dst + width, :] = w_ref[src:src + width, :].astype(BF16)
    o_ref[zero[0]:zero[1], :] = jnp.zeros((zero[1] - zero[0], o_ref.shape[1]), BF16)


def _pack_w_in(w, lay, cw, ql, kvl):
    depth, d, n = w.shape
    w_t = jnp.swapaxes(w, 1, 2)
    o_mla = 3 * cw
    o_gqa = o_mla + ql + kvl + MLA_ROPE
    gqw, gkw = GQA_HEADS * GQA_HEAD_DIM, GQA_KV_HEADS * GQA_HEAD_DIM
    o_gate = o_gqa + gqw + 2 * gkw
    moves = ((0, 0, o_mla), (lay["gq"], o_gqa, gqw), (lay["cq"], o_mla, ql + kvl),
             (lay["gk"], o_gqa + gqw, 2 * gkw), (lay["kr"], o_mla + ql + kvl, MLA_ROPE),
             (lay["gate"], o_gate, n - o_gate))
    cb = _pick(256, d)
    return pl.pallas_call(
        functools.partial(_pack_kernel, moves=moves, zero=(lay["kr"] + MLA_ROPE, lay["gate"])),
        grid=(depth, d // cb),
        in_specs=[pl.BlockSpec((None, n, cb), lambda l, j: (l, 0, j))],
        out_specs=pl.BlockSpec((None, lay["total"], cb), lambda l, j: (l, 0, j)),
        out_shape=jax.ShapeDtypeStruct((depth, lay["total"], d), BF16),
        compiler_params=_params(("parallel", "parallel")),
        name="pack_w_in",
    )(w_t)


def kernel(x, c, ctx, c_ctx, ada_w, ada_b, ffn1_norm, ffn1_w_gu, ffn1_w_down, mix_norm, w_in, conv_w,
           mla_q_norm, mla_w_qb, mla_kv_norm, mla_w_kvb, gqa_sink, w_branch_conv, w_branch_mla,
           w_branch_gqa, w_out, ffn2_norm, ffn2_w_gu, ffn2_w_down, final_norm):
    b, s, d = x.shape
    cl = ctx.shape[1]
    depth = ada_w.shape[0]
    cw = conv_w.shape[-1]
    ql, kvl = mla_q_norm.shape[-1], mla_kv_norm.shape[-1]
    mx, mc = b * s, b * cl
    assert b + 1 <= 8 and s % cl == 0 and cl % ATT_BLOCK == 0

    cvec = jnp.zeros((8, d), F32).at[:b].set(c).at[b].set(c_ctx)
    mods = _mods(cvec, ada_w, ada_b).reshape(depth, 8, N_MOD, d)

    lay = _layout(cw, ql, kvl, d)
    tabs = _rope_table(s, mc, MLA_ROPE // 2) + _rope_table(s, mc, GQA_HEAD_DIM // 2)
    h = None
    bf = _pick(512, ffn1_w_down.shape[1])
    w1_gu, w1_dn = _gu_to_bf16(ffn1_w_gu, bf), _to_bf16(ffn1_w_down)
    w2_gu, w2_dn = _gu_to_bf16(ffn2_w_gu, bf), _to_bf16(ffn2_w_down)
    w_bc, w_bm, w_bg = _to_bf16(w_branch_conv), _to_bf16(w_branch_mla), _to_bf16(w_branch_gqa)
    w_o = _to_bf16(w_out)
    w_p = _pack_w_in(w_in, lay, cw, ql, kvl)

    for l in range(depth):
        with_ctx = l < depth - 1
        rows = mx + mc if with_ctx else mx
        wqb_p = jnp.pad(mla_w_qb[l].reshape(ql, MLA_HEADS, MLA_NOPE + MLA_ROPE),
                        ((0, 0), (0, 0), (0, MLA_QK - MLA_NOPE - MLA_ROPE))).reshape(ql, -1).astype(BF16)
        wkv = mla_w_kvb[l].reshape(kvl, MLA_HEADS, MLA_NOPE + MLA_V)
        wkn_t = wkv[:, :, :MLA_NOPE].reshape(kvl, -1).T.astype(BF16)
        wv = wkv[:, :, MLA_NOPE:].reshape(kvl, -1).astype(BF16)
        sink = jnp.broadcast_to((gqa_sink[l].astype(F32) * LOG2E)[:, None], (GQA_HEADS, LANE))

        if l == 0:
            h = _ffn(x.reshape(mx, d), ctx.reshape(mc, d), mx + mc, mods[l], ffn1_norm[l], w1_gu, w1_dn, l, 0,
                     s, b, final_norm, False)
        else:
            h = _ffn(h, None, mx + mc, mods[l], ffn1_norm[l], w1_gu, w1_dn, l, 0, s, b, final_norm, False)
        px = _proj(h, mods[l], mix_norm[l], w_p, l, s, b)
        q, kt, v, gq, gk = _prep(px, lay, mla_q_norm[l], mla_kv_norm[l], wqb_p, wkn_t, wv, tabs, s, mx)
        y_mla = _mla_attn(q, kt, v, b, s, cl, with_ctx)
        y_gqa = _gqa_attn(gq, gk, px, lay["gv"], sink, b, s, cl, with_ctx)
        h = _mix_out(y_mla, y_gqa, px, conv_w[l], lay["gate"], w_bc, w_bm, w_bg, w_o, l, h, mods[l], rows, s, cl,
                     mx, b)
        h = _ffn(h, None, rows, mods[l], ffn2_norm[l], w2_gu, w2_dn, l, 6, s, b, final_norm, not with_ctx)
    return h.reshape(b, s, d)
```

```python
import functools

import jax
import jax.numpy as jnp
from jax import lax
from jax.experimental import pallas as pl
from jax.experimental.pallas import tpu as pltpu

F32 = jnp.float32
BF16 = jnp.bfloat16

GRID_W = 64
N_MOD = 9
EPS = 1e-6
ROPE_BASE = 10000.0
CONV_K = 3
MLA_HEADS = 8
MLA_NOPE = 128
MLA_ROPE = 64
MLA_V = 128
MLA_SCALE = (MLA_NOPE + MLA_ROPE) ** -0.5
GQA_HEADS = 8
GQA_KV_HEADS = 2
GQA_GROUP = GQA_HEADS // GQA_KV_HEADS
GQA_HEAD_DIM = 128
GQA_SCALE = GQA_HEAD_DIM ** -0.5
WINDOW = 128
ATT_BLOCK = 128

LANE = 128
MLA_QK = 2 * LANE
MLA_SUB = 256
LOG2E = 1.4426950408889634
VMEM_LIMIT = 56 << 20
NT_DIMS = (((1,), (1,)), ((), ()))


def _pick(target, *sizes):
    b = target
    while any(s % b for s in sizes):
        b //= 2
        assert b >= 8, (target, sizes)
    return b


def _params(sem):
    return pltpu.CompilerParams(dimension_semantics=sem, vmem_limit_bytes=VMEM_LIMIT)


def _dot(a, b):
    return jnp.dot(a, b, preferred_element_type=F32)


def _dot_nt(a, b):
    return lax.dot_general(a, b, NT_DIMS, preferred_element_type=F32)


def _rms(x, g):
    return x * lax.rsqrt(jnp.mean(x * x, axis=-1, keepdims=True) + EPS) * g


CAST_BLOCK_BYTES = 6 << 20


def _cast_kernel(w_ref, o_ref):
    o_ref[...] = w_ref[...].astype(BF16)


def _to_bf16(w):
    depth, k, n = w.shape
    rows = depth * k
    rb = _pick(max(16, 1 << ((CAST_BLOCK_BYTES // (4 * n)).bit_length() - 1)), rows)
    out = pl.pallas_call(
        _cast_kernel,
        grid=(rows // rb,),
        in_specs=[pl.BlockSpec((rb, n), lambda i: (i, 0))],
        out_specs=pl.BlockSpec((rb, n), lambda i: (i, 0)),
        out_shape=jax.ShapeDtypeStruct((rows, n), BF16),
        compiler_params=_params(("parallel",)),
        name="cast_bf16",
    )(w.reshape(rows, n))
    return out.reshape(depth, k, n)


def _cast_gu_kernel(w_ref, o_ref, *, bf):
    nf = o_ref.shape[0]
    for j in range(nf):
        o_ref[j, :, :bf] = w_ref[:, j * bf:(j + 1) * bf].astype(BF16)
        o_ref[j, :, bf:] = w_ref[:, (nf + j) * bf:(nf + j + 1) * bf].astype(BF16)


def _gu_to_bf16(w, bf):
    depth, k, n = w.shape
    nf = n // (2 * bf)
    rb = _pick(max(16, 1 << ((CAST_BLOCK_BYTES // (4 * n)).bit_length() - 1)), k)
    kb = k // rb
    return pl.pallas_call(
        functools.partial(_cast_gu_kernel, bf=bf),
        grid=(depth, kb),
        in_specs=[pl.BlockSpec((None, rb, n), lambda l, i: (l, i, 0))],
        out_specs=pl.BlockSpec((None, nf, rb, 2 * bf), lambda l, i: (l, 0, i, 0)),
        out_shape=jax.ShapeDtypeStruct((depth, nf, k, 2 * bf), BF16),
        compiler_params=_params(("parallel", "parallel")),
        name="cast_gate_up",
    )(w)


def _mods_kernel(c_ref, w_ref, b_ref, o_ref):
    c = c_ref[...]
    s = (c * jax.nn.sigmoid(c)).astype(BF16)
    o_ref[0] = _dot(s, w_ref[0].astype(BF16)) + b_ref[0]


def _mods(cvec, ada_w, ada_b):
    depth, d, n = ada_w.shape
    bn = _pick(1024, n)
    return pl.pallas_call(
        _mods_kernel,
        grid=(depth, n // bn),
        in_specs=[pl.BlockSpec((8, d), lambda l, j: (0, 0)),
                  pl.BlockSpec((1, d, bn), lambda l, j: (l, 0, j)),
                  pl.BlockSpec((1, 1, bn), lambda l, j: (l, 0, j))],
        out_specs=pl.BlockSpec((1, 8, bn), lambda l, j: (l, 0, j)),
        out_shape=jax.ShapeDtypeStruct((depth, 8, n), F32),
        compiler_params=_params(("parallel", "parallel")),
        name="adaln_mods",
    )(cvec, ada_w, ada_b.reshape(depth, 1, n))


def _norm_mod_rows(h_ref, mod_ref, g_ref, xn_ref, copy_ref, i_shift, i_scale, rc):
    shift = mod_ref[0, i_shift:i_shift + 1, :]
    scale1 = 1.0 + mod_ref[0, i_scale:i_scale + 1, :]
    g = g_ref[...]

    def body(r, carry):
        rows = pl.ds(pl.multiple_of(r * rc, rc), rc)
        x = h_ref[rows, :]
        xn_ref[rows, :] = (_rms(x, g) * scale1 + shift).astype(BF16)
        if copy_ref is not None:
            copy_ref[rows, :] = x
        return carry

    lax.fori_loop(0, h_ref.shape[0] // rc, body, 0)


def _norm_mod_chunk(hn_ref, modn_ref, g_ref, xn_ref, slot, k, ch, i_shift, i_scale):
    r0 = pl.multiple_of(jnp.minimum(k * ch, hn_ref.shape[0] - ch), 16)
    shift = modn_ref[0, i_shift:i_shift + 1, :]
    scale1 = 1.0 + modn_ref[0, i_scale:i_scale + 1, :]
    x = hn_ref[pl.ds(r0, ch), :]
    xn_ref[slot, pl.ds(r0, ch), :] = (_rms(x, g_ref[...]) * scale1 + shift).astype(BF16)


def _chunk_rows(bm, steps):
    return min(bm, -(-bm // (16 * steps)) * 16)


def _mod_spec(d, bm, s, b, ahead=0, last=None):
    per = s // bm

    def idx(i, j):
        t = i if not ahead else jnp.minimum(i + ahead, last)
        return jnp.minimum(t // per, b), 0, 0

    return pl.BlockSpec((1, N_MOD, d), idx)


FFN_UP_ROWS = 1024
FFN_DOWN_ROWS = 256

def _swiglu(gu):
    bf = gu.shape[1] // 2
    gg, uu = gu[:, :bf], gu[:, bf:]
    return (gg * jax.nn.sigmoid(gg) * uu).astype(BF16)


def _up_split_kernel(h_ref, hc_ref, mod_ref, g_ref, w_ref, o_ref, xn_ref, *, i0, rc, n_lat):
    first = pl.program_id(1) == 0
    is_lat = pl.program_id(0) < n_lat

    @pl.when(first & is_lat)
    def _():
        _norm_mod_rows(h_ref, mod_ref, g_ref, xn_ref, None, i0, i0 + 1, rc)

    @pl.when(first & jnp.logical_not(is_lat))
    def _():
        _norm_mod_rows(hc_ref, mod_ref, g_ref, xn_ref, None, i0, i0 + 1, rc)

    o_ref[...] = _swiglu(_dot(xn_ref[...], w_ref[...]))


def _up_kernel(h_ref, mod_ref, modn_ref, g_ref, w_ref, o_ref, xn_ref, *, i0, rc, ch):
    i, j = pl.program_id(0), pl.program_id(1)
    slot = i % 2

    @pl.when((j == 0) & (i == 0))
    def _():
        _norm_mod_rows(h_ref, mod_ref, g_ref, xn_ref.at[0], None, i0, i0 + 1, rc)

    o_ref[...] = _swiglu(_dot(xn_ref[slot], w_ref[...]))
    _norm_mod_chunk(h_ref, modn_ref, g_ref, xn_ref, 1 - slot, jnp.maximum(j - 1, 0), ch, i0, i0 + 1)


def _down_kernel(a_ref, w_ref, h_ref, hc_ref, mod_ref, fg_ref, o_ref, *, i_gate, n_lat, final):
    h = h_ref[...]
    if n_lat is not None:
        h = jnp.where(pl.program_id(0) < n_lat, h, hc_ref[...])
    out = h + (0.5 * mod_ref[0, i_gate:i_gate + 1, :]) * _dot(a_ref[...], w_ref[...])
    o_ref[...] = _rms(out, fg_ref[...]) if final else out


def _ffn(h, hc, rows, mods, norm_g, w_gu, w_down, l, i0, s, b, final_g, final):
    d = h.shape[1]
    nf, bf = w_gu.shape[1], w_gu.shape[3] // 2
    f = nf * bf
    vec = pl.BlockSpec((1, d), lambda i, j: (0, 0))
    w_spec = pl.BlockSpec((None, None, d, 2 * bf), lambda i, j: (l, j, 0, 0))
    bm = _pick(FFN_UP_ROWS, s, rows) if hc is None else _pick(FFN_UP_ROWS, s, hc.shape[0])
    nt = rows // bm
    up_common = dict(
        grid=(nt, nf),
        out_specs=pl.BlockSpec((bm, bf), lambda i, j: (i, j)),
        out_shape=jax.ShapeDtypeStruct((rows, f), BF16),
        compiler_params=_params(("arbitrary", "arbitrary")),
        name="ffn_up")
    if hc is not None:
        n_lat = h.shape[0] // bm
        act = pl.pallas_call(
            functools.partial(_up_split_kernel, i0=i0, rc=_pick(128, bm), n_lat=n_lat),
            in_specs=[pl.BlockSpec((bm, d), lambda i, j: (jnp.minimum(i, n_lat - 1), 0)),
                      pl.BlockSpec((bm, d), lambda i, j: (jnp.maximum(i - n_lat, 0), 0)),
                      _mod_spec(d, bm, s, b), vec, w_spec],
            scratch_shapes=[pltpu.VMEM((bm, d), BF16)],
            **up_common,
        )(h, hc, mods, norm_g.reshape(1, d), w_gu)
    else:
        assert nf >= 2
        h_idx = lambda i, j: (jnp.where((i == 0) & (j == 0), 0, jnp.minimum(i + 1, nt - 1)), 0)
        act = pl.pallas_call(
            functools.partial(_up_kernel, i0=i0, rc=_pick(128, bm), ch=_chunk_rows(bm, nf - 1)),
            in_specs=[pl.BlockSpec((bm, d), h_idx), _mod_spec(d, bm, s, b), _mod_spec(d, bm, s, b, 1, nt - 1),
                      vec, w_spec],
            scratch_shapes=[pltpu.VMEM((2, bm, d), BF16)],
            **up_common,
        )(h, mods, mods, norm_g.reshape(1, d), w_gu)

    bd = _pick(FFN_DOWN_ROWS, s, rows) if hc is None else _pick(FFN_DOWN_ROWS, s, hc.shape[0])
    per = s // bd
    if hc is None:
        n_lat = None
        h_specs = [pl.BlockSpec((bd, d), lambda i: (i, 0)), pl.BlockSpec((1, d), lambda i: (0, 0))]
        hc = norm_g.reshape(1, d)
    else:
        n_lat = h.shape[0] // bd
        h_specs = [pl.BlockSpec((bd, d), lambda i: (jnp.minimum(i, n_lat - 1), 0)),
                   pl.BlockSpec((bd, d), lambda i: (jnp.maximum(i - n_lat, 0), 0))]
    return pl.pallas_call(
        functools.partial(_down_kernel, i_gate=i0 + 2, n_lat=n_lat, final=final),
        grid=(rows // bd,),
        in_specs=[pl.BlockSpec((bd, f), lambda i: (i, 0)),
                  pl.BlockSpec((None, f, d), lambda i: (l, 0, 0), pipeline_mode=pl.Buffered(1))] + h_specs + [
                  pl.BlockSpec((1, N_MOD, d), lambda i: (jnp.minimum(i // per, b), 0, 0)),
                  pl.BlockSpec((1, d), lambda i: (0, 0))],
        out_specs=pl.BlockSpec((bd, d), lambda i: (i, 0)),
        out_shape=jax.ShapeDtypeStruct((rows, d), F32),
        compiler_params=_params(("parallel",)),
        name="ffn_down",
    )(act, w_down, h, hc, mods, final_g.reshape(1, d))


def _first_norm_kernel(h_ref, mod_ref, g_ref, o_ref, *, rc):
    _norm_mod_rows(h_ref, mod_ref, g_ref, o_ref, None, 3, 4, rc)


def _proj_kernel(x0_ref, h_ref, modn_ref, g_ref, w_ref, o_ref, xn_ref, *, nch):
    i, j = pl.program_id(0), pl.program_id(1)
    slot = i % 2

    @pl.when((j == 0) & (i == 0))
    def _():
        xn_ref[0] = x0_ref[...]

    o_ref[...] = _dot_nt(xn_ref[slot], w_ref[...]).astype(BF16)
    ch = h_ref.shape[0]
    r0 = pl.multiple_of(jnp.clip(j - 1, 0, nch - 1) * ch, ch)
    x = _rms(h_ref[...], g_ref[...]) * (1.0 + modn_ref[0, 4:5, :]) + modn_ref[0, 3:4, :]
    xn_ref[1 - slot, pl.ds(r0, ch), :] = x.astype(BF16)


def _proj(h, mods, norm_g, w_p, l, s, b):
    rows, d = h.shape
    n = w_p.shape[1]
    bm = _pick(1024, s, rows)
    bn = _pick(2048, n)
    nt, nj = rows // bm, n // bn
    assert nj >= 2
    nch = max(k for k in (1, 2, 4, 8, 16) if k <= nj - 1 and bm % (16 * k) == 0)
    ch = bm // nch
    vec = pl.BlockSpec((1, d), lambda i, j: (0, 0))
    x0 = pl.pallas_call(
        functools.partial(_first_norm_kernel, rc=_pick(128, bm)),
        grid=(1, 1),
        in_specs=[pl.BlockSpec((bm, d), lambda i, j: (0, 0)), _mod_spec(d, bm, s, b), vec],
        out_specs=pl.BlockSpec((bm, d), lambda i, j: (0, 0)),
        out_shape=jax.ShapeDtypeStruct((bm, d), BF16),
        compiler_params=_params(("arbitrary", "arbitrary")),
        name="in_proj_first_norm",
    )(h, mods, norm_g.reshape(1, d))
    h_idx = lambda i, j: (jnp.minimum(i + 1, nt - 1) * nch + jnp.clip(j - 1, 0, nch - 1), 0)
    return pl.pallas_call(
        functools.partial(_proj_kernel, nch=nch),
        grid=(nt, nj),
        in_specs=[pl.BlockSpec((bm, d), lambda i, j: (0, 0), pipeline_mode=pl.Buffered(1)),
                  pl.BlockSpec((ch, d), h_idx), _mod_spec(d, bm, s, b, 1, nt - 1), vec,
                  pl.BlockSpec((None, bn, d), lambda i, j: (l, j, 0))],
        out_specs=pl.BlockSpec((bm, bn), lambda i, j: (i, j)),
        out_shape=jax.ShapeDtypeStruct((rows, n), BF16),
        scratch_shapes=[pltpu.VMEM((2, bm, d), BF16)],
        compiler_params=_params(("arbitrary", "arbitrary")),
        name="in_proj",
    )(x0, h, mods, norm_g.reshape(1, d), w_p)


def _conv_tile(gb_ref, gc_ref, v_ref, gcp_ref, vp_ref, gcn_ref, vn_ref, w_ref, *, s, c, mx):
    bm = gb_ref.shape[0]
    row0 = pl.program_id(0) * bm
    is_lat = row0 < mx
    at_start = jnp.where(is_lat, row0 % s == 0, (row0 - mx) % c == 0)
    at_end = jnp.where(is_lat, (row0 + bm) % s == 0, (row0 - mx + bm) % c == 0)
    cv = gc_ref[...].astype(F32) * v_ref[...].astype(F32)
    hp = (gcp_ref[...].astype(F32) * vp_ref[...].astype(F32))[15:16, :]
    hn = (gcn_ref[...].astype(F32) * vn_ref[...].astype(F32))[0:1, :]
    hp = jnp.where(at_start, 0.0, hp)
    hn = jnp.where(at_end, 0.0, hn)
    rid = lax.broadcasted_iota(jnp.int32, (bm, 1), 0)
    prev = jnp.where(rid == 0, hp, pltpu.roll(cv, 1, 0))
    nxt = jnp.where(rid == bm - 1, hn, pltpu.roll(cv, bm - 1, 0))
    w = w_ref[...]
    y = gb_ref[...].astype(F32) * (prev * w[0:1, :] + cv * w[1:2, :] + nxt * w[2:3, :])
    return y.astype(BF16)


def _conv_specs(px, cw, bm):
    m = px.shape[0]
    hb = bm // 16
    main = lambda col: pl.BlockSpec((bm, cw), lambda i: (i, col))
    prev = lambda col: pl.BlockSpec((16, cw), lambda i: (jnp.maximum(i * hb - 1, 0), col))
    nxt = lambda col: pl.BlockSpec((16, cw), lambda i: (jnp.minimum((i + 1) * hb, m // 16 - 1), col))
    return [main(0), main(1), main(2), prev(1), prev(2), nxt(1), nxt(2), pl.BlockSpec((CONV_K, cw), lambda i: (0, 0))]


def _rope(x, cos, sin, half):
    lane = lax.broadcasted_iota(jnp.int32, x.shape, 1)
    first = (lane % (2 * half)) < half
    rot = jnp.where(first, pltpu.roll(x, LANE - half, 1), pltpu.roll(x, half, 1))
    return x * cos + rot * sin


def _prep_kernel(gq_ref, cq_ref, ckv_ref, gk_ref, kr_ref, qn_ref, kvn_ref, wqb_ref, wknt_ref, wv_ref,
                 mcos_ref, msin_ref, gcos_ref, gsin_ref, q_ref, kt_ref, v_ref, gqo_ref, gko_ref):
    mcos, msin = mcos_ref[...], msin_ref[...]
    gcos, gsin = gcos_ref[...], gsin_ref[...]
    mh = MLA_ROPE // 4
    gh = GQA_HEAD_DIM // 4

    cqn = _rms(cq_ref[...].astype(F32), qn_ref[...]).astype(BF16)
    q = _dot(cqn, wqb_ref[...]) * (MLA_SCALE * LOG2E)
    ckvn = _rms(ckv_ref[...].astype(F32), kvn_ref[...]).astype(BF16)
    knt = _dot_nt(wknt_ref[...], ckvn)
    vv = _dot(ckvn, wv_ref[...])
    krt = _rope(kr_ref[...].astype(F32), mcos, msin, mh).T.astype(BF16)
    ones = jnp.ones((q.shape[0], LANE), BF16)
    for h in range(MLA_HEADS):
        a = h * MLA_QK
        q_ref[:, a:a + LANE] = q[:, a:a + LANE].astype(BF16)
        q_ref[:, a + LANE:a + MLA_QK] = _rope(q[:, a + LANE:a + MLA_QK], mcos, msin, mh).astype(BF16)
        kt_ref[a:a + LANE, :] = knt[h * LANE:(h + 1) * LANE, :].astype(BF16)
        kt_ref[a + LANE:a + MLA_QK, :] = krt
        v_ref[:, a:a + LANE] = vv[:, h * LANE:(h + 1) * LANE].astype(BF16)
        v_ref[:, a + LANE:a + MLA_QK] = ones

    gq = gq_ref[...].astype(F32)
    for h in range(GQA_HEADS):
        a = h * GQA_HEAD_DIM
        gqo_ref[:, a:a + LANE] = (_rope(gq[:, a:a + LANE], gcos, gsin, gh) * (GQA_SCALE * LOG2E)).astype(BF16)
    gk = gk_ref[...].astype(F32)
    for h in range(GQA_KV_HEADS):
        a = h * GQA_HEAD_DIM
        gko_ref[:, a:a + LANE] = _rope(gk[:, a:a + LANE], gcos, gsin, gh).astype(BF16)


def _prep(px, lay, q_norm, kv_norm, wqb_p, wkn_t, wv, tabs, s, mx):
    m = px.shape[0]
    bm = _pick(512, s, m - mx)
    n_lat = mx // bm
    per = s // bm
    ql, kvl = q_norm.shape[0], kv_norm.shape[0]
    gqw, gkw = GQA_HEADS * GQA_HEAD_DIM, GQA_KV_HEADS * GQA_HEAD_DIM

    def col(width, off):
        assert off % width == 0, (width, off)
        return pl.BlockSpec((bm, width), lambda i: (i, off // width))

    const = lambda r, c: pl.BlockSpec((r, c), lambda i: (0, 0))
    tab = pl.BlockSpec((bm, LANE), lambda i: (jnp.where(i < n_lat, i % per, per + i - n_lat), 0))
    row = lambda width: pl.BlockSpec((bm, width), lambda i: (i, 0))
    hq = MLA_HEADS * MLA_QK
    return pl.pallas_call(
        _prep_kernel,
        grid=(m // bm,),
        in_specs=[col(gqw, lay["gq"]), col(ql, lay["cq"]), col(kvl, lay["ckv"]), col(gkw, lay["gk"]),
                  col(LANE, lay["kr"]), const(1, ql), const(1, kvl), const(ql, hq),
                  const(MLA_HEADS * MLA_NOPE, kvl), const(kvl, MLA_HEADS * MLA_V), tab, tab, tab, tab],
        out_specs=[row(hq), pl.BlockSpec((hq, bm), lambda i: (0, i)), row(hq), row(gqw), row(gkw)],
        out_shape=[jax.ShapeDtypeStruct((m, hq), BF16), jax.ShapeDtypeStruct((hq, m), BF16),
                   jax.ShapeDtypeStruct((m, hq), BF16),
                   jax.ShapeDtypeStruct((m, gqw), BF16), jax.ShapeDtypeStruct((m, gkw), BF16)],
        compiler_params=_params(("parallel",)),
        name="attn_prep",
    )(px, px, px, px, px, q_norm.reshape(1, ql), kv_norm.reshape(1, kvl), wqb_p, wkn_t, wv, *tabs)


def _mla_kernel(q_ref, kl_ref, kc_ref, vl_ref, vc_ref, o_ref, *, n_sub):
    kl, kc = kl_ref[...], kc_ref[...]
    subs = [q_ref[i * MLA_SUB:(i + 1) * MLA_SUB, :] for i in range(n_sub)]
    scores = [(_dot(q, kl), _dot(q, kc)) for q in subs]
    for i, (s1, s2) in enumerate(scores):
        m = jnp.maximum(jnp.max(s1, axis=-1, keepdims=True), jnp.max(s2, axis=-1, keepdims=True))
        p1 = jnp.exp2((s1 - m).astype(BF16))
        p2 = jnp.exp2((s2 - m).astype(BF16))
        o = _dot(p1, vl_ref[...]) + _dot(p2, vc_ref[...])
        o_ref[i * MLA_SUB:(i + 1) * MLA_SUB, :] = (o[:, :MLA_V] / o[:, MLA_V:]).astype(BF16)


def _mla_ctx_kernel(q_ref, kc_ref, vc_ref, y_ref, o_ref):
    del y_ref
    s2 = _dot(q_ref[...], kc_ref[...])
    p2 = jnp.exp2(s2 - jnp.max(s2, axis=-1, keepdims=True)).astype(BF16)
    o = _dot(p2, vc_ref[...])
    o_ref[...] = (o[:, :MLA_V] / o[:, MLA_V:]).astype(BF16)


def _mla_attn(q, kt, v, b, s, c, with_ctx):
    mx = b * s
    n_sub = max(d for d in (1, 2, 4, 8) if (s // MLA_SUB) % d == 0)
    bq = n_sub * MLA_SUB
    nq = s // bq
    rows = mx + (b * c if with_ctx else 0)
    ctx0 = mx // c
    y = pl.pallas_call(
        functools.partial(_mla_kernel, n_sub=n_sub),
        grid=(b, MLA_HEADS, nq),
        in_specs=[pl.BlockSpec((bq, MLA_QK), lambda bi, h, qi: (bi * nq + qi, h)),
                  pl.BlockSpec((MLA_QK, s), lambda bi, h, qi: (h, bi)),
                  pl.BlockSpec((MLA_QK, c), lambda bi, h, qi: (h, ctx0 + bi)),
                  pl.BlockSpec((s, MLA_QK), lambda bi, h, qi: (bi, h)),
                  pl.BlockSpec((c, MLA_QK), lambda bi, h, qi: (ctx0 + bi, h))],
        out_specs=pl.BlockSpec((bq, MLA_V), lambda bi, h, qi: (bi * nq + qi, h)),
        out_shape=jax.ShapeDtypeStruct((rows, MLA_HEADS * MLA_V), BF16),
        compiler_params=_params(("parallel", "parallel", "arbitrary")),
        name="mla_attn",
    )(q, kt, kt, v, v)
    if not with_ctx:
        return y
    return pl.pallas_call(
        _mla_ctx_kernel,
        grid=(b, MLA_HEADS),
        in_specs=[pl.BlockSpec((c, MLA_QK), lambda bi, h: (ctx0 + bi, h)),
                  pl.BlockSpec((MLA_QK, c), lambda bi, h: (h, ctx0 + bi)),
                  pl.BlockSpec((c, MLA_QK), lambda bi, h: (ctx0 + bi, h)),
                  pl.BlockSpec(memory_space=pl.ANY)],
        out_specs=pl.BlockSpec((c, MLA_V), lambda bi, h: (ctx0 + bi, h)),
        out_shape=jax.ShapeDtypeStruct(y.shape, BF16),
        input_output_aliases={3: 0},
        compiler_params=_params(("parallel", "parallel")),
        name="mla_attn_ctx",
    )(q, kt, v, y)


GQA_QB_MAX = 8


def _sink_rows(sink_ref, g, j, rows):
    return jnp.broadcast_to(sink_ref[pl.ds(g * GQA_GROUP + j, 1), :], (rows, LANE))[:, 0:1]


def _gqa_kernel(q_ref, kp_ref, kc_ref, kn_ref, kx_ref, vp_ref, vc_ref, vn_ref, vx_ref, sink_ref, o_ref,
                *, n_lat, qb):
    g = pl.program_id(1)
    n = pl.program_id(2)
    blk = ATT_BLOCK
    kx, vx = kx_ref[...], vx_ref[...]
    kband = jnp.concatenate([kp_ref[...], kc_ref[...], kn_ref[...]], axis=0)
    vband = jnp.concatenate([vp_ref[...], vc_ref[...], vn_ref[...]], axis=0)
    nk = 3 * blk + kx.shape[0]
    ones = jnp.ones((nk, LANE), BF16)
    snk = jnp.concatenate([jnp.broadcast_to(sink_ref[pl.ds(g * GQA_GROUP + j, 1), :], (blk, LANE))
                           for j in range(GQA_GROUP)], axis=0)
    rows = GQA_GROUP * blk
    r = lax.broadcasted_iota(jnp.int32, (rows, nk), 0) % blk
    col = lax.broadcasted_iota(jnp.int32, (rows, nk), 1)
    in_window = (jnp.abs(col - blk - r) <= WINDOW) | (col >= 3 * blk)
    for i in range(qb):
        qs = jnp.concatenate([q_ref[i * blk:(i + 1) * blk, j * LANE:(j + 1) * LANE] for j in range(GQA_GROUP)],
                             axis=0)
        keys = jnp.concatenate([kband[i * blk:(i + 3) * blk, :], kx], axis=0)
        vals = jnp.concatenate([jnp.concatenate([vband[i * blk:(i + 3) * blk, :], vx], axis=0), ones], axis=1)
        block = n * qb + i
        valid = in_window & ((col >= blk) | (block > 0)) & ((col < 2 * blk) | (col >= 3 * blk) | (block < n_lat - 1))
        sc = jnp.where(valid, _dot_nt(qs, keys), -jnp.inf)
        e = snk
        for t in range(nk // LANE):
            e = jnp.maximum(e, sc[:, t * LANE:(t + 1) * LANE])
        m = jnp.max(e, axis=-1, keepdims=True)
        ol = _dot(jnp.exp2((sc - m).astype(BF16)), vals)
        o = ol[:, :LANE] / (ol[:, LANE:] + jnp.exp2(snk - m))
        for j in range(GQA_GROUP):
            o_ref[i * blk:(i + 1) * blk, j * LANE:(j + 1) * LANE] = o[j * blk:(j + 1) * blk, :].astype(BF16)


def _gqa_ctx_kernel(q_ref, kx_ref, vx_ref, sink_ref, y_ref, o_ref):
    del y_ref
    g = pl.program_id(1)
    kx, vx = kx_ref[...], vx_ref[...]
    for j in range(GQA_GROUP):
        sc = _dot_nt(q_ref[:, j * LANE:(j + 1) * LANE], kx)
        snk = _sink_rows(sink_ref, g, j, sc.shape[0])
        m = jnp.maximum(jnp.max(sc, axis=-1, keepdims=True), snk)
        p_c = jnp.exp2(sc - m)
        l = jnp.sum(p_c, axis=-1, keepdims=True) + jnp.exp2(snk - m)
        o_ref[:, j * LANE:(j + 1) * LANE] = (_dot(p_c.astype(BF16), vx) / l).astype(BF16)


def _gqa_attn(gq, gk, px, o_gv, sink, b, s, c, with_ctx):
    mx = b * s
    blk = ATT_BLOCK
    n_lat = s // blk
    qb = max(k for k in (1, 2, 4, 8) if k <= GQA_QB_MAX and n_lat % k == 0)
    nstep = n_lat // qb
    rows = mx + (b * c if with_ctx else 0)
    ctx0 = mx // c
    gvc = o_gv // LANE
    gw = GQA_GROUP * GQA_HEAD_DIM
    big = qb * blk

    def edge(shift, col0):
        def idx(bi, g, n):
            return bi * n_lat + jnp.clip(n * qb + shift, 0, n_lat - 1), col0 + g
        return pl.BlockSpec((blk, LANE), idx)

    main = lambda col0: pl.BlockSpec((big, LANE), lambda bi, g, n: (bi * nstep + n, col0 + g))
    ctx = lambda col0: pl.BlockSpec((c, LANE), lambda bi, g, n: (ctx0 + bi, col0 + g))
    sink_spec = pl.BlockSpec((GQA_HEADS, LANE), lambda *_: (0, 0))
    y = pl.pallas_call(
        functools.partial(_gqa_kernel, n_lat=n_lat, qb=qb),
        grid=(b, GQA_KV_HEADS, nstep),
        in_specs=[pl.BlockSpec((big, gw), lambda bi, g, n: (bi * nstep + n, g)),
                  edge(-1, 0), main(0), edge(qb, 0), ctx(0),
                  edge(-1, gvc), main(gvc), edge(qb, gvc), ctx(gvc), sink_spec],
        out_specs=pl.BlockSpec((big, gw), lambda bi, g, n: (bi * nstep + n, g)),
        out_shape=jax.ShapeDtypeStruct((rows, GQA_HEADS * GQA_HEAD_DIM), BF16),
        compiler_params=_params(("parallel", "parallel", "arbitrary")),
        name="gqa_attn",
    )(gq, gk, gk, gk, gk, px, px, px, px, sink)
    if not with_ctx:
        return y
    return pl.pallas_call(
        _gqa_ctx_kernel,
        grid=(b, GQA_KV_HEADS),
        in_specs=[pl.BlockSpec((c, gw), lambda bi, g: (ctx0 + bi, g)),
                  pl.BlockSpec((c, LANE), lambda bi, g: (ctx0 + bi, g)),
                  pl.BlockSpec((c, LANE), lambda bi, g: (ctx0 + bi, gvc + g)),
                  sink_spec, pl.BlockSpec(memory_space=pl.ANY)],
        out_specs=pl.BlockSpec((c, gw), lambda bi, g: (ctx0 + bi, g)),
        out_shape=jax.ShapeDtypeStruct(y.shape, BF16),
        input_output_aliases={4: 0},
        compiler_params=_params(("parallel", "parallel")),
        name="gqa_attn_ctx",
    )(gq, gk, px, sink, y)


MIX_ROWS = 256


def _mix_out_kernel(gb_ref, gcv_ref, v_ref, gcp_ref, vp_ref, gcn_ref, vn_ref, cw_ref, ym_ref, yg_ref,
                    gc_ref, gm_ref, gg_ref, wc_ref, wm_ref, wg_ref, wo_ref, h_ref, mod_ref, o_ref, *, s, c, mx):
    yc = _conv_tile(gb_ref, gcv_ref, v_ref, gcp_ref, vp_ref, gcn_ref, vn_ref, cw_ref, s=s, c=c, mx=mx)
    gate = lambda ref: jax.nn.sigmoid(ref[...].astype(F32))
    merged = gate(gc_ref) * _dot(yc, wc_ref[...])
    merged += gate(gm_ref) * _dot(ym_ref[...], wm_ref[...])
    merged += gate(gg_ref) * _dot(yg_ref[...], wg_ref[...])
    o_ref[...] = h_ref[...] + mod_ref[0, 5:6, :] * _dot(merged.astype(BF16), wo_ref[...])


def _mix_out(ym, yg, px, conv_w, o_gate, w_bc, w_bm, w_bg, w_out, l, h, mods, rows, s, c, mx, b):
    d = w_out.shape[2]
    bm = _pick(MIX_ROWS, s, c)
    per = s // bm
    assert o_gate % d == 0
    g0 = o_gate // d
    yspec = lambda a: pl.BlockSpec((bm, a.shape[1]), lambda i: (i, 0))
    gspec = lambda k: pl.BlockSpec((bm, d), lambda i: (i, g0 + k))
    wspec = lambda w: pl.BlockSpec((None,) + w.shape[1:], lambda i: (l, 0, 0), pipeline_mode=pl.Buffered(1))
    return pl.pallas_call(
        functools.partial(_mix_out_kernel, s=s, c=c, mx=mx),
        grid=(rows // bm,),
        in_specs=_conv_specs(px, conv_w.shape[1], bm) + [
                  yspec(ym), yspec(yg), gspec(0), gspec(1), gspec(2),
                  wspec(w_bc), wspec(w_bm), wspec(w_bg), wspec(w_out),
                  pl.BlockSpec((bm, d), lambda i: (i, 0)),
                  pl.BlockSpec((1, N_MOD, d), lambda i: (jnp.minimum(i // per, b), 0, 0))],
        out_specs=pl.BlockSpec((bm, d), lambda i: (i, 0)),
        out_shape=jax.ShapeDtypeStruct((rows, d), F32),
        compiler_params=_params(("parallel",)),
        name="mix_out",
    )(px, px, px, px, px, px, px, conv_w, ym, yg, px, px, px, w_bc, w_bm, w_bg, w_out, h, mods)


def _rope_table(s, n_ctx_rows, dim):
    t = jnp.arange(s, dtype=jnp.int32)
    pos = jnp.stack([t // GRID_W, t % GRID_W], axis=1).astype(F32)
    inv = ROPE_BASE ** (-jnp.arange(0, dim, 2, dtype=F32) / dim)
    lane = jnp.arange(LANE)
    ang = pos[:, jnp.minimum(lane // dim, 1)] * inv[lane % (dim // 2)][None, :]
    active = (lane < 2 * dim)[None, :]
    sign = jnp.where((lane % dim) < dim // 2, -1.0, 1.0)[None, :]
    cos = jnp.where(active, jnp.cos(ang), 1.0)
    sin = jnp.where(active, jnp.sin(ang) * sign, 0.0)
    pad = ((0, n_ctx_rows), (0, 0))
    return jnp.pad(cos, pad, constant_values=1.0), jnp.pad(sin, pad)


def _layout(cw, ql, kvl, d):
    lay = {"conv": 0}
    off = 3 * cw
    for name, width in (("gq", GQA_HEADS * GQA_HEAD_DIM), ("cq", ql), ("ckv", kvl),
                        ("gk", GQA_KV_HEADS * GQA_HEAD_DIM), ("gv", GQA_KV_HEADS * GQA_HEAD_DIM),
                        ("kr", LANE)):
        lay[name] = off
        off += width
    q = max(d, 512)
    off = -(-off // q) * q
    lay["gate"] = off
    lay["total"] = off + 3 * d
    return lay


def _pack_kernel(w_ref, o_ref, *, moves, zero):
    for dst, src, width in moves:
        o_ref[dst:dst + width, :] = w_ref[src:src + width, :].astype(BF16)
    o_ref[zero[0]:zero[1], :] = jnp.zeros((zero[1] - zero[0], o_ref.shape[1]), BF16)


def _pack_w_in(w, lay, cw, ql, kvl):
    depth, d, n = w.shape
    w_t = jnp.swapaxes(w, 1, 2)
    o_mla = 3 * cw
    o_gqa = o_mla + ql + kvl + MLA_ROPE
    gqw, gkw = GQA_HEADS * GQA_HEAD_DIM, GQA_KV_HEADS * GQA_HEAD_DIM
    o_gate = o_gqa + gqw + 2 * gkw
    moves = ((0, 0, o_mla), (lay["gq"], o_gqa, gqw), (lay["cq"], o_mla, ql + kvl),
             (lay["gk"], o_gqa + gqw, 2 * gkw), (lay["kr"], o_mla + ql + kvl, MLA_ROPE),
             (lay["gate"], o_gate, n - o_gate))
    cb = _pick(256, d)
    return pl.pallas_call(
        functools.partial(_pack_kernel, moves=moves, zero=(lay["kr"] + MLA_ROPE, lay["gate"])),
        grid=(depth, d // cb),
        in_specs=[pl.BlockSpec((None, n, cb), lambda l, j: (l, 0, j))],
        out_specs=pl.BlockSpec((None, lay["total"], cb), lambda l, j: (l, 0, j)),
        out_shape=jax.ShapeDtypeStruct((depth, lay["total"], d), BF16),
        compiler_params=_params(("parallel", "parallel")),
        name="pack_w_in",
    )(w_t)


def kernel(x, c, ctx, c_ctx, ada_w, ada_b, ffn1_norm, ffn1_w_gu, ffn1_w_down, mix_norm, w_in, conv_w,
           mla_q_norm, mla_w_qb, mla_kv_norm, mla_w_kvb, gqa_sink, w_branch_conv, w_branch_mla,
           w_branch_gqa, w_out, ffn2_norm, ffn2_w_gu, ffn2_w_down, final_norm):
    b, s, d = x.shape
    cl = ctx.shape[1]
    depth = ada_w.shape[0]
    cw = conv_w.shape[-1]
    ql, kvl = mla_q_norm.shape[-1], mla_kv_norm.shape[-1]
    mx, mc = b * s, b * cl
    assert b + 1 <= 8 and s % cl == 0 and cl % ATT_BLOCK == 0

    cvec = jnp.zeros((8, d), F32).at[:b].set(c).at[b].set(c_ctx)
    mods = _mods(cvec, ada_w, ada_b).reshape(depth, 8, N_MOD, d)

    lay = _layout(cw, ql, kvl, d)
    tabs = _rope_table(s, mc, MLA_ROPE // 2) + _rope_table(s, mc, GQA_HEAD_DIM // 2)
    h = None
    bf = _pick(512, ffn1_w_down.shape[1])
    w1_gu, w1_dn = _gu_to_bf16(ffn1_w_gu, bf), _to_bf16(ffn1_w_down)
    w2_gu, w2_dn = _gu_to_bf16(ffn2_w_gu, bf), _to_bf16(ffn2_w_down)
    w_bc, w_bm, w_bg = _to_bf16(w_branch_conv), _to_bf16(w_branch_mla), _to_bf16(w_branch_gqa)
    w_o = _to_bf16(w_out)
    w_p = _pack_w_in(w_in, lay, cw, ql, kvl)

    for l in range(depth):
        with_ctx = l < depth - 1
        rows = mx + mc if with_ctx else mx
        wqb_p = jnp.pad(mla_w_qb[l].reshape(ql, MLA_HEADS, MLA_NOPE + MLA_ROPE),
                        ((0, 0), (0, 0), (0, MLA_QK - MLA_NOPE - MLA_ROPE))).reshape(ql, -1).astype(BF16)
        wkv = mla_w_kvb[l].reshape(kvl, MLA_HEADS, MLA_NOPE + MLA_V)
        wkn_t = wkv[:, :, :MLA_NOPE].reshape(kvl, -1).T.astype(BF16)
        wv = wkv[:, :, MLA_NOPE:].reshape(kvl, -1).astype(BF16)
        sink = jnp.broadcast_to((gqa_sink[l].astype(F32) * LOG2E)[:, None], (GQA_HEADS, LANE))

        if l == 0:
            h = _ffn(x.reshape(mx, d), ctx.reshape(mc, d), mx + mc, mods[l], ffn1_norm[l], w1_gu, w1_dn, l, 0,
                     s, b, final_norm, False)
        else:
            h = _ffn(h, None, mx + mc, mods[l], ffn1_norm[l], w1_gu, w1_dn, l, 0, s, b, final_norm, False)
        px = _proj(h, mods[l], mix_norm[l], w_p, l, s, b)
        q, kt, v, gq, gk = _prep(px, lay, mla_q_norm[l], mla_kv_norm[l], wqb_p, wkn_t, wv, tabs, s, mx)
        y_mla = _mla_attn(q, kt, v, b, s, cl, with_ctx)
        y_gqa = _gqa_attn(gq, gk, px, lay["gv"], sink, b, s, cl, with_ctx)
        h = _mix_out(y_mla, y_gqa, px, conv_w[l], lay["gate"], w_bc, w_bm, w_bg, w_o, l, h, mods[l], rows, s, cl,
                     mx, b)
        h = _ffn(h, None, rows, mods[l], ffn2_norm[l], w2_gu, w2_dn, l, 6, s, b, final_norm, not with_ctx)
    return h.reshape(b, s, d)
```

```python
import functools

import jax
import jax.numpy as jnp
from jax import lax
from jax.experimental import pallas as pl
from jax.experimental.pallas import tpu as pltpu

F32 = jnp.float32
BF16 = jnp.bfloat16

GRID_W = 64
N_MOD = 9
EPS = 1e-6
ROPE_BASE = 10000.0
CONV_K = 3
MLA_HEADS = 8
MLA_NOPE = 128
MLA_ROPE = 64
MLA_V = 128
MLA_SCALE = (MLA_NOPE + MLA_ROPE) ** -0.5
GQA_HEADS = 8
GQA_KV_HEADS = 2
GQA_GROUP = GQA_HEADS // GQA_KV_HEADS
GQA_HEAD_DIM = 128
GQA_SCALE = GQA_HEAD_DIM ** -0.5
WINDOW = 128
ATT_BLOCK = 128

LANE = 128
MLA_QK = 2 * LANE
MLA_SUB = 256
LOG2E = 1.4426950408889634
VMEM_LIMIT = 56 << 20
NT_DIMS = (((1,), (1,)), ((), ()))


def _pick(target, *sizes):
    b = target
    while any(s % b for s in sizes):
        b //= 2
        assert b >= 8, (target, sizes)
    return b


def _params(sem):
    return pltpu.CompilerParams(dimension_semantics=sem, vmem_limit_bytes=VMEM_LIMIT)


def _dot(a, b):
    return jnp.dot(a, b, preferred_element_type=F32)


def _dot_nt(a, b):
    return lax.dot_general(a, b, NT_DIMS, preferred_element_type=F32)


def _rms(x, g):
    return x * lax.rsqrt(jnp.mean(x * x, axis=-1, keepdims=True) + EPS) * g


CAST_BLOCK_BYTES = 6 << 20


def _cast_kernel(w_ref, o_ref):
    o_ref[...] = w_ref[...].astype(BF16)


def _to_bf16(w):
    depth, k, n = w.shape
    rows = depth * k
    rb = _pick(max(16, 1 << ((CAST_BLOCK_BYTES // (4 * n)).bit_length() - 1)), rows)
    out = pl.pallas_call(
        _cast_kernel,
        grid=(rows // rb,),
        in_specs=[pl.BlockSpec((rb, n), lambda i: (i, 0))],
        out_specs=pl.BlockSpec((rb, n), lambda i: (i, 0)),
        out_shape=jax.ShapeDtypeStruct((rows, n), BF16),
        compiler_params=_params(("parallel",)),
        name="cast_bf16",
    )(w.reshape(rows, n))
    return out.reshape(depth, k, n)


def _cast_gu_kernel(w_ref, o_ref, *, bf):
    nf = o_ref.shape[0]
    for j in range(nf):
        o_ref[j, :, :bf] = w_ref[:, j * bf:(j + 1) * bf].astype(BF16)
        o_ref[j, :, bf:] = w_ref[:, (nf + j) * bf:(nf + j + 1) * bf].astype(BF16)


def _gu_to_bf16(w, bf):
    depth, k, n = w.shape
    nf = n // (2 * bf)
    rb = _pick(max(16, 1 << ((CAST_BLOCK_BYTES // (4 * n)).bit_length() - 1)), k)
    kb = k // rb
    return pl.pallas_call(
        functools.partial(_cast_gu_kernel, bf=bf),
        grid=(depth, kb),
        in_specs=[pl.BlockSpec((None, rb, n), lambda l, i: (l, i, 0))],
        out_specs=pl.BlockSpec((None, nf, rb, 2 * bf), lambda l, i: (l, 0, i, 0)),
        out_shape=jax.ShapeDtypeStruct((depth, nf, k, 2 * bf), BF16),
        compiler_params=_params(("parallel", "parallel")),
        name="cast_gate_up",
    )(w)


def _mods_kernel(c_ref, w_ref, b_ref, o_ref):
    c = c_ref[...]
    s = (c * jax.nn.sigmoid(c)).astype(BF16)
    o_ref[0] = _dot(s, w_ref[0].astype(BF16)) + b_ref[0]


def _mods(cvec, ada_w, ada_b):
    depth, d, n = ada_w.shape
    bn = _pick(1024, n)
    return pl.pallas_call(
        _mods_kernel,
        grid=(depth, n // bn),
        in_specs=[pl.BlockSpec((8, d), lambda l, j: (0, 0)),
                  pl.BlockSpec((1, d, bn), lambda l, j: (l, 0, j)),
                  pl.BlockSpec((1, 1, bn), lambda l, j: (l, 0, j))],
        out_specs=pl.BlockSpec((1, 8, bn), lambda l, j: (l, 0, j)),
        out_shape=jax.ShapeDtypeStruct((depth, 8, n), F32),
        compiler_params=_params(("parallel", "parallel")),
        name="adaln_mods",
    )(cvec, ada_w, ada_b.reshape(depth, 1, n))


def _norm_mod_rows(h_ref, mod_ref, g_ref, xn_ref, copy_ref, i_shift, i_scale, rc):
    shift = mod_ref[0, i_shift:i_shift + 1, :]
    scale1 = 1.0 + mod_ref[0, i_scale:i_scale + 1, :]
    g = g_ref[...]

    def body(r, carry):
        rows = pl.ds(pl.multiple_of(r * rc, rc), rc)
        x = h_ref[rows, :]
        xn_ref[rows, :] = (_rms(x, g) * scale1 + shift).astype(BF16)
        if copy_ref is not None:
            copy_ref[rows, :] = x
        return carry

    lax.fori_loop(0, h_ref.shape[0] // rc, body, 0)


def _norm_mod_chunk(hn_ref, modn_ref, g_ref, xn_ref, slot, k, ch, i_shift, i_scale):
    r0 = pl.multiple_of(jnp.minimum(k * ch, hn_ref.shape[0] - ch), 16)
    shift = modn_ref[0, i_shift:i_shift + 1, :]
    scale1 = 1.0 + modn_ref[0, i_scale:i_scale + 1, :]
    x = hn_ref[pl.ds(r0, ch), :]
    xn_ref[slot, pl.ds(r0, ch), :] = (_rms(x, g_ref[...]) * scale1 + shift).astype(BF16)


def _chunk_rows(bm, steps):
    return min(bm, -(-bm // (16 * steps)) * 16)


def _mod_spec(d, bm, s, b, ahead=0, last=None):
    per = s // bm

    def idx(i, j):
        t = i if not ahead else jnp.minimum(i + ahead, last)
        return jnp.minimum(t // per, b), 0, 0

    return pl.BlockSpec((1, N_MOD, d), idx)


FFN_UP_ROWS = 1024
FFN_DOWN_ROWS = 256
WEIGHT_CHUNK_ROWS = 512

def _swiglu(gu):
    bf = gu.shape[1] // 2
    gg, uu = gu[:, :bf], gu[:, bf:]
    return (gg * jax.nn.sigmoid(gg) * uu).astype(BF16)


def _up_split_kernel(h_ref, hc_ref, mod_ref, g_ref, w_ref, o_ref, xn_ref, *, i0, rc, n_lat):
    first = pl.program_id(1) == 0
    is_lat = pl.program_id(0) < n_lat

    @pl.when(first & is_lat)
    def _():
        _norm_mod_rows(h_ref, mod_ref, g_ref, xn_ref, None, i0, i0 + 1, rc)

    @pl.when(first & jnp.logical_not(is_lat))
    def _():
        _norm_mod_rows(hc_ref, mod_ref, g_ref, xn_ref, None, i0, i0 + 1, rc)

    o_ref[...] = _swiglu(_dot(xn_ref[...], w_ref[...]))


def _up_kernel(h_ref, mod_ref, modn_ref, g_ref, w_ref, o_ref, xn_ref, *, i0, rc, ch):
    i, j = pl.program_id(0), pl.program_id(1)
    slot = i % 2

    @pl.when((j == 0) & (i == 0))
    def _():
        _norm_mod_rows(h_ref, mod_ref, g_ref, xn_ref.at[0], None, i0, i0 + 1, rc)

    o_ref[...] = _swiglu(_dot(xn_ref[slot], w_ref[...]))
    _norm_mod_chunk(h_ref, modn_ref, g_ref, xn_ref, 1 - slot, jnp.maximum(j - 1, 0), ch, i0, i0 + 1)


def _down_kernel(a_ref, w_ref, h_ref, hc_ref, mod_ref, fg_ref, o_ref, wbf_ref, *, i_gate, n_lat, final, nw):
    t = pl.program_id(0)

    @pl.when(t < nw)
    def _():
        ck = w_ref.shape[0]
        wbf_ref[pl.ds(pl.multiple_of(t * ck, ck), ck), :] = w_ref[...].astype(BF16)

    @pl.when(t >= nw)
    def _():
        h = h_ref[...]
        if n_lat is not None:
            h = jnp.where(t - nw < n_lat, h, hc_ref[...])
        out = h + (0.5 * mod_ref[0, i_gate:i_gate + 1, :]) * _dot(a_ref[...], wbf_ref[...])
        o_ref[...] = _rms(out, fg_ref[...]) if final else out


def _ffn(h, hc, rows, mods, norm_g, w_gu, w_down, l, i0, s, b, final_g, final):
    d = h.shape[1]
    nf, bf = w_gu.shape[1], w_gu.shape[3] // 2
    f = nf * bf
    vec = pl.BlockSpec((1, d), lambda i, j: (0, 0))
    w_spec = pl.BlockSpec((None, None, d, 2 * bf), lambda i, j: (l, j, 0, 0))
    bm = _pick(FFN_UP_ROWS, s, rows) if hc is None else _pick(FFN_UP_ROWS, s, hc.shape[0])
    nt = rows // bm
    up_common = dict(
        grid=(nt, nf),
        out_specs=pl.BlockSpec((bm, bf), lambda i, j: (i, j)),
        out_shape=jax.ShapeDtypeStruct((rows, f), BF16),
        compiler_params=_params(("arbitrary", "arbitrary")),
        name="ffn_up")
    if hc is not None:
        n_lat = h.shape[0] // bm
        act = pl.pallas_call(
            functools.partial(_up_split_kernel, i0=i0, rc=_pick(128, bm), n_lat=n_lat),
            in_specs=[pl.BlockSpec((bm, d), lambda i, j: (jnp.minimum(i, n_lat - 1), 0)),
                      pl.BlockSpec((bm, d), lambda i, j: (jnp.maximum(i - n_lat, 0), 0)),
                      _mod_spec(d, bm, s, b), vec, w_spec],
            scratch_shapes=[pltpu.VMEM((bm, d), BF16)],
            **up_common,
        )(h, hc, mods, norm_g.reshape(1, d), w_gu)
    else:
        assert nf >= 2
        h_idx = lambda i, j: (jnp.where((i == 0) & (j == 0), 0, jnp.minimum(i + 1, nt - 1)), 0)
        act = pl.pallas_call(
            functools.partial(_up_kernel, i0=i0, rc=_pick(128, bm), ch=_chunk_rows(bm, nf - 1)),
            in_specs=[pl.BlockSpec((bm, d), h_idx), _mod_spec(d, bm, s, b), _mod_spec(d, bm, s, b, 1, nt - 1),
                      vec, w_spec],
            scratch_shapes=[pltpu.VMEM((2, bm, d), BF16)],
            **up_common,
        )(h, mods, mods, norm_g.reshape(1, d), w_gu)

    bd = _pick(FFN_DOWN_ROWS, s, rows) if hc is None else _pick(FFN_DOWN_ROWS, s, hc.shape[0])
    per = s // bd
    ck = _pick(WEIGHT_CHUNK_ROWS, f)
    nw = f // ck
    row = lambda t: jnp.maximum(t - nw, 0)
    if hc is None:
        n_lat = None
        h_specs = [pl.BlockSpec((bd, d), lambda t: (row(t), 0)), pl.BlockSpec((1, d), lambda t: (0, 0))]
        hc = norm_g.reshape(1, d)
    else:
        n_lat = h.shape[0] // bd
        h_specs = [pl.BlockSpec((bd, d), lambda t: (jnp.minimum(row(t), n_lat - 1), 0)),
                   pl.BlockSpec((bd, d), lambda t: (jnp.maximum(row(t) - n_lat, 0), 0))]
    return pl.pallas_call(
        functools.partial(_down_kernel, i_gate=i0 + 2, n_lat=n_lat, final=final, nw=nw),
        grid=(nw + rows // bd,),
        in_specs=[pl.BlockSpec((bd, f), lambda t: (row(t), 0)),
                  pl.BlockSpec((None, ck, d), lambda t: (l, jnp.minimum(t, nw - 1), 0))] + h_specs + [
                  pl.BlockSpec((1, N_MOD, d), lambda t: (jnp.minimum(row(t) // per, b), 0, 0)),
                  pl.BlockSpec((1, d), lambda t: (0, 0))],
        out_specs=pl.BlockSpec((bd, d), lambda t: (row(t), 0)),
        out_shape=jax.ShapeDtypeStruct((rows, d), F32),
        scratch_shapes=[pltpu.VMEM((f, d), BF16)],
        compiler_params=_params(("arbitrary",)),
        name="ffn_down",
    )(act, w_down, h, hc, mods, final_g.reshape(1, d))


def _first_norm_kernel(h_ref, mod_ref, g_ref, o_ref, *, rc):
    _norm_mod_rows(h_ref, mod_ref, g_ref, o_ref, None, 3, 4, rc)


def _proj_kernel(x0_ref, h_ref, modn_ref, g_ref, w_ref, o_ref, xn_ref, *, nch):
    i, j = pl.program_id(0), pl.program_id(1)
    slot = i % 2

    @pl.when((j == 0) & (i == 0))
    def _():
        xn_ref[0] = x0_ref[...]

    o_ref[...] = _dot_nt(xn_ref[slot], w_ref[...]).astype(BF16)
    ch = h_ref.shape[0]
    r0 = pl.multiple_of(jnp.clip(j - 1, 0, nch - 1) * ch, ch)
    x = _rms(h_ref[...], g_ref[...]) * (1.0 + modn_ref[0, 4:5, :]) + modn_ref[0, 3:4, :]
    xn_ref[1 - slot, pl.ds(r0, ch), :] = x.astype(BF16)


def _proj(h, mods, norm_g, w_p, l, s, b):
    rows, d = h.shape
    n = w_p.shape[1]
    bm = _pick(1024, s, rows)
    bn = _pick(2048, n)
    nt, nj = rows // bm, n // bn
    assert nj >= 2
    nch = max(k for k in (1, 2, 4, 8, 16) if k <= nj - 1 and bm % (16 * k) == 0)
    ch = bm // nch
    vec = pl.BlockSpec((1, d), lambda i, j: (0, 0))
    x0 = pl.pallas_call(
        functools.partial(_first_norm_kernel, rc=_pick(128, bm)),
        grid=(1, 1),
        in_specs=[pl.BlockSpec((bm, d), lambda i, j: (0, 0)), _mod_spec(d, bm, s, b), vec],
        out_specs=pl.BlockSpec((bm, d), lambda i, j: (0, 0)),
        out_shape=jax.ShapeDtypeStruct((bm, d), BF16),
        compiler_params=_params(("arbitrary", "arbitrary")),
        name="in_proj_first_norm",
    )(h, mods, norm_g.reshape(1, d))
    h_idx = lambda i, j: (jnp.minimum(i + 1, nt - 1) * nch + jnp.clip(j - 1, 0, nch - 1), 0)
    return pl.pallas_call(
        functools.partial(_proj_kernel, nch=nch),
        grid=(nt, nj),
        in_specs=[pl.BlockSpec((bm, d), lambda i, j: (0, 0), pipeline_mode=pl.Buffered(1)),
                  pl.BlockSpec((ch, d), h_idx), _mod_spec(d, bm, s, b, 1, nt - 1), vec,
                  pl.BlockSpec((None, bn, d), lambda i, j: (l, j, 0))],
        out_specs=pl.BlockSpec((bm, bn), lambda i, j: (i, j)),
        out_shape=jax.ShapeDtypeStruct((rows, n), BF16),
        scratch_shapes=[pltpu.VMEM((2, bm, d), BF16)],
        compiler_params=_params(("arbitrary", "arbitrary")),
        name="in_proj",
    )(x0, h, mods, norm_g.reshape(1, d), w_p)


def _conv_tile(gb_ref, gc_ref, v_ref, gcp_ref, vp_ref, gcn_ref, vn_ref, w_ref, *, s, c, mx):
    bm = gb_ref.shape[0]
    row0 = pl.program_id(0) * bm
    is_lat = row0 < mx
    at_start = jnp.where(is_lat, row0 % s == 0, (row0 - mx) % c == 0)
    at_end = jnp.where(is_lat, (row0 + bm) % s == 0, (row0 - mx + bm) % c == 0)
    cv = gc_ref[...].astype(F32) * v_ref[...].astype(F32)
    hp = (gcp_ref[...].astype(F32) * vp_ref[...].astype(F32))[15:16, :]
    hn = (gcn_ref[...].astype(F32) * vn_ref[...].astype(F32))[0:1, :]
    hp = jnp.where(at_start, 0.0, hp)
    hn = jnp.where(at_end, 0.0, hn)
    rid = lax.broadcasted_iota(jnp.int32, (bm, 1), 0)
    prev = jnp.where(rid == 0, hp, pltpu.roll(cv, 1, 0))
    nxt = jnp.where(rid == bm - 1, hn, pltpu.roll(cv, bm - 1, 0))
    w = w_ref[...]
    y = gb_ref[...].astype(F32) * (prev * w[0:1, :] + cv * w[1:2, :] + nxt * w[2:3, :])
    return y.astype(BF16)


def _conv_specs(px, cw, bm):
    m = px.shape[0]
    hb = bm // 16
    main = lambda col: pl.BlockSpec((bm, cw), lambda i: (i, col))
    prev = lambda col: pl.BlockSpec((16, cw), lambda i: (jnp.maximum(i * hb - 1, 0), col))
    nxt = lambda col: pl.BlockSpec((16, cw), lambda i: (jnp.minimum((i + 1) * hb, m // 16 - 1), col))
    return [main(0), main(1), main(2), prev(1), prev(2), nxt(1), nxt(2), pl.BlockSpec((CONV_K, cw), lambda i: (0, 0))]


def _rope(x, cos, sin, half):
    lane = lax.broadcasted_iota(jnp.int32, x.shape, 1)
    first = (lane % (2 * half)) < half
    rot = jnp.where(first, pltpu.roll(x, LANE - half, 1), pltpu.roll(x, half, 1))
    return x * cos + rot * sin


def _prep_kernel(gq_ref, cq_ref, ckv_ref, gk_ref, kr_ref, qn_ref, kvn_ref, wqb_ref, wknt_ref, wv_ref,
                 mcos_ref, msin_ref, gcos_ref, gsin_ref, q_ref, kt_ref, v_ref, gqo_ref, gko_ref):
    mcos, msin = mcos_ref[...], msin_ref[...]
    gcos, gsin = gcos_ref[...], gsin_ref[...]
    mh = MLA_ROPE // 4
    gh = GQA_HEAD_DIM // 4

    cqn = _rms(cq_ref[...].astype(F32), qn_ref[...]).astype(BF16)
    q = _dot(cqn, wqb_ref[...]) * (MLA_SCALE * LOG2E)
    ckvn = _rms(ckv_ref[...].astype(F32), kvn_ref[...]).astype(BF16)
    knt = _dot_nt(wknt_ref[...], ckvn)
    vv = _dot(ckvn, wv_ref[...])
    krt = _rope(kr_ref[...].astype(F32), mcos, msin, mh).T.astype(BF16)
    ones = jnp.ones((q.shape[0], LANE), BF16)
    for h in range(MLA_HEADS):
        a = h * MLA_QK
        q_ref[:, a:a + LANE] = q[:, a:a + LANE].astype(BF16)
        q_ref[:, a + LANE:a + MLA_QK] = _rope(q[:, a + LANE:a + MLA_QK], mcos, msin, mh).astype(BF16)
        kt_ref[a:a + LANE, :] = knt[h * LANE:(h + 1) * LANE, :].astype(BF16)
        kt_ref[a + LANE:a + MLA_QK, :] = krt
        v_ref[:, a:a + LANE] = vv[:, h * LANE:(h + 1) * LANE].astype(BF16)
        v_ref[:, a + LANE:a + MLA_QK] = ones

    gq = gq_ref[...].astype(F32)
    for h in range(GQA_HEADS):
        a = h * GQA_HEAD_DIM
        gqo_ref[:, a:a + LANE] = (_rope(gq[:, a:a + LANE], gcos, gsin, gh) * (GQA_SCALE * LOG2E)).astype(BF16)
    gk = gk_ref[...].astype(F32)
    for h in range(GQA_KV_HEADS):
        a = h * GQA_HEAD_DIM
        gko_ref[:, a:a + LANE] = _rope(gk[:, a:a + LANE], gcos, gsin, gh).astype(BF16)


def _prep(px, lay, q_norm, kv_norm, wqb_p, wkn_t, wv, tabs, s, mx):
    m = px.shape[0]
    bm = _pick(512, s, m - mx)
    n_lat = mx // bm
    per = s // bm
    ql, kvl = q_norm.shape[0], kv_norm.shape[0]
    gqw, gkw = GQA_HEADS * GQA_HEAD_DIM, GQA_KV_HEADS * GQA_HEAD_DIM

    def col(width, off):
        assert off % width == 0, (width, off)
        return pl.BlockSpec((bm, width), lambda i: (i, off // width))

    const = lambda r, c: pl.BlockSpec((r, c), lambda i: (0, 0))
    tab = pl.BlockSpec((bm, LANE), lambda i: (jnp.where(i < n_lat, i % per, per + i - n_lat), 0))
    row = lambda width: pl.BlockSpec((bm, width), lambda i: (i, 0))
    hq = MLA_HEADS * MLA_QK
    return pl.pallas_call(
        _prep_kernel,
        grid=(m // bm,),
        in_specs=[col(gqw, lay["gq"]), col(ql, lay["cq"]), col(kvl, lay["ckv"]), col(gkw, lay["gk"]),
                  col(LANE, lay["kr"]), const(1, ql), const(1, kvl), const(ql, hq),
                  const(MLA_HEADS * MLA_NOPE, kvl), const(kvl, MLA_HEADS * MLA_V), tab, tab, tab, tab],
        out_specs=[row(hq), pl.BlockSpec((hq, bm), lambda i: (0, i)), row(hq), row(gqw), row(gkw)],
        out_shape=[jax.ShapeDtypeStruct((m, hq), BF16), jax.ShapeDtypeStruct((hq, m), BF16),
                   jax.ShapeDtypeStruct((m, hq), BF16),
                   jax.ShapeDtypeStruct((m, gqw), BF16), jax.ShapeDtypeStruct((m, gkw), BF16)],
        compiler_params=_params(("parallel",)),
        name="attn_prep",
    )(px, px, px, px, px, q_norm.reshape(1, ql), kv_norm.reshape(1, kvl), wqb_p, wkn_t, wv, *tabs)


def _mla_kernel(q_ref, kl_ref, kc_ref, vl_ref, vc_ref, o_ref, *, n_sub):
    kl, kc = kl_ref[...], kc_ref[...]
    subs = [q_ref[i * MLA_SUB:(i + 1) * MLA_SUB, :] for i in range(n_sub)]
    scores = [(_dot(q, kl), _dot(q, kc)) for q in subs]
    for i, (s1, s2) in enumerate(scores):
        m = jnp.maximum(jnp.max(s1, axis=-1, keepdims=True), jnp.max(s2, axis=-1, keepdims=True))
        p1 = jnp.exp2((s1 - m).astype(BF16))
        p2 = jnp.exp2((s2 - m).astype(BF16))
        o = _dot(p1, vl_ref[...]) + _dot(p2, vc_ref[...])
        o_ref[i * MLA_SUB:(i + 1) * MLA_SUB, :] = (o[:, :MLA_V] / o[:, MLA_V:]).astype(BF16)


def _mla_ctx_kernel(q_ref, kc_ref, vc_ref, y_ref, o_ref):
    del y_ref
    s2 = _dot(q_ref[...], kc_ref[...])
    p2 = jnp.exp2(s2 - jnp.max(s2, axis=-1, keepdims=True)).astype(BF16)
    o = _dot(p2, vc_ref[...])
    o_ref[...] = (o[:, :MLA_V] / o[:, MLA_V:]).astype(BF16)


def _mla_attn(q, kt, v, b, s, c, with_ctx):
    mx = b * s
    n_sub = max(d for d in (1, 2, 4, 8) if (s // MLA_SUB) % d == 0)
    bq = n_sub * MLA_SUB
    nq = s // bq
    rows = mx + (b * c if with_ctx else 0)
    ctx0 = mx // c
    y = pl.pallas_call(
        functools.partial(_mla_kernel, n_sub=n_sub),
        grid=(b, MLA_HEADS, nq),
        in_specs=[pl.BlockSpec((bq, MLA_QK), lambda bi, h, qi: (bi * nq + qi, h)),
                  pl.BlockSpec((MLA_QK, s), lambda bi, h, qi: (h, bi)),
                  pl.BlockSpec((MLA_QK, c), lambda bi, h, qi: (h, ctx0 + bi)),
                  pl.BlockSpec((s, MLA_QK), lambda bi, h, qi: (bi, h)),
                  pl.BlockSpec((c, MLA_QK), lambda bi, h, qi: (ctx0 + bi, h))],
        out_specs=pl.BlockSpec((bq, MLA_V), lambda bi, h, qi: (bi * nq + qi, h)),
        out_shape=jax.ShapeDtypeStruct((rows, MLA_HEADS * MLA_V), BF16),
        compiler_params=_params(("parallel", "parallel", "arbitrary")),
        name="mla_attn",
    )(q, kt, kt, v, v)
    if not with_ctx:
        return y
    return pl.pallas_call(
        _mla_ctx_kernel,
        grid=(b, MLA_HEADS),
        in_specs=[pl.BlockSpec((c, MLA_QK), lambda bi, h: (ctx0 + bi, h)),
                  pl.BlockSpec((MLA_QK, c), lambda bi, h: (h, ctx0 + bi)),
                  pl.BlockSpec((c, MLA_QK), lambda bi, h: (ctx0 + bi, h)),
                  pl.BlockSpec(memory_space=pl.ANY)],
        out_specs=pl.BlockSpec((c, MLA_V), lambda bi, h: (ctx0 + bi, h)),
        out_shape=jax.ShapeDtypeStruct(y.shape, BF16),
        input_output_aliases={3: 0},
        compiler_params=_params(("parallel", "parallel")),
        name="mla_attn_ctx",
    )(q, kt, v, y)


GQA_QB_MAX = 8


def _sink_rows(sink_ref, g, j, rows):
    return jnp.broadcast_to(sink_ref[pl.ds(g * GQA_GROUP + j, 1), :], (rows, LANE))[:, 0:1]


def _gqa_kernel(q_ref, kp_ref, kc_ref, kn_ref, kx_ref, vp_ref, vc_ref, vn_ref, vx_ref, sink_ref, o_ref,
                *, n_lat, qb):
    g = pl.program_id(1)
    n = pl.program_id(2)
    blk = ATT_BLOCK
    kx, vx = kx_ref[...], vx_ref[...]
    kband = jnp.concatenate([kp_ref[...], kc_ref[...], kn_ref[...]], axis=0)
    vband = jnp.concatenate([vp_ref[...], vc_ref[...], vn_ref[...]], axis=0)
    nk = 3 * blk + kx.shape[0]
    ones = jnp.ones((nk, LANE), BF16)
    snk = jnp.concatenate([jnp.broadcast_to(sink_ref[pl.ds(g * GQA_GROUP + j, 1), :], (blk, LANE))
                           for j in range(GQA_GROUP)], axis=0)
    rows = GQA_GROUP * blk
    r = lax.broadcasted_iota(jnp.int32, (rows, nk), 0) % blk
    col = lax.broadcasted_iota(jnp.int32, (rows, nk), 1)
    in_window = (jnp.abs(col - blk - r) <= WINDOW) | (col >= 3 * blk)
    for i in range(qb):
        qs = jnp.concatenate([q_ref[i * blk:(i + 1) * blk, j * LANE:(j + 1) * LANE] for j in range(GQA_GROUP)],
                             axis=0)
        keys = jnp.concatenate([kband[i * blk:(i + 3) * blk, :], kx], axis=0)
        vals = jnp.concatenate([jnp.concatenate([vband[i * blk:(i + 3) * blk, :], vx], axis=0), ones], axis=1)
        block = n * qb + i
        valid = in_window & ((col >= blk) | (block > 0)) & ((col < 2 * blk) | (col >= 3 * blk) | (block < n_lat - 1))
        sc = jnp.where(valid, _dot_nt(qs, keys), -jnp.inf)
        e = snk
        for t in range(nk // LANE):
            e = jnp.maximum(e, sc[:, t * LANE:(t + 1) * LANE])
        m = jnp.max(e, axis=-1, keepdims=True)
        ol = _dot(jnp.exp2((sc - m).astype(BF16)), vals)
        o = ol[:, :LANE] / (ol[:, LANE:] + jnp.exp2(snk - m))
        for j in range(GQA_GROUP):
            o_ref[i * blk:(i + 1) * blk, j * LANE:(j + 1) * LANE] = o[j * blk:(j + 1) * blk, :].astype(BF16)


def _gqa_ctx_kernel(q_ref, kx_ref, vx_ref, sink_ref, y_ref, o_ref):
    del y_ref
    g = pl.program_id(1)
    kx, vx = kx_ref[...], vx_ref[...]
    for j in range(GQA_GROUP):
        sc = _dot_nt(q_ref[:, j * LANE:(j + 1) * LANE], kx)
        snk = _sink_rows(sink_ref, g, j, sc.shape[0])
        m = jnp.maximum(jnp.max(sc, axis=-1, keepdims=True), snk)
        p_c = jnp.exp2(sc - m)
        l = jnp.sum(p_c, axis=-1, keepdims=True) + jnp.exp2(snk - m)
        o_ref[:, j * LANE:(j + 1) * LANE] = (_dot(p_c.astype(BF16), vx) / l).astype(BF16)


def _gqa_attn(gq, gk, px, o_gv, sink, b, s, c, with_ctx):
    mx = b * s
    blk = ATT_BLOCK
    n_lat = s // blk
    qb = max(k for k in (1, 2, 4, 8) if k <= GQA_QB_MAX and n_lat % k == 0)
    nstep = n_lat // qb
    rows = mx + (b * c if with_ctx else 0)
    ctx0 = mx // c
    gvc = o_gv // LANE
    gw = GQA_GROUP * GQA_HEAD_DIM
    big = qb * blk

    def edge(shift, col0):
        def idx(bi, g, n):
            return bi * n_lat + jnp.clip(n * qb + shift, 0, n_lat - 1), col0 + g
        return pl.BlockSpec((blk, LANE), idx)

    main = lambda col0: pl.BlockSpec((big, LANE), lambda bi, g, n: (bi * nstep + n, col0 + g))
    ctx = lambda col0: pl.BlockSpec((c, LANE), lambda bi, g, n: (ctx0 + bi, col0 + g))
    sink_spec = pl.BlockSpec((GQA_HEADS, LANE), lambda *_: (0, 0))
    y = pl.pallas_call(
        functools.partial(_gqa_kernel, n_lat=n_lat, qb=qb),
        grid=(b, GQA_KV_HEADS, nstep),
        in_specs=[pl.BlockSpec((big, gw), lambda bi, g, n: (bi * nstep + n, g)),
                  edge(-1, 0), main(0), edge(qb, 0), ctx(0),
                  edge(-1, gvc), main(gvc), edge(qb, gvc), ctx(gvc), sink_spec],
        out_specs=pl.BlockSpec((big, gw), lambda bi, g, n: (bi * nstep + n, g)),
        out_shape=jax.ShapeDtypeStruct((rows, GQA_HEADS * GQA_HEAD_DIM), BF16),
        compiler_params=_params(("parallel", "parallel", "arbitrary")),
        name="gqa_attn",
    )(gq, gk, gk, gk, gk, px, px, px, px, sink)
    if not with_ctx:
        return y
    return pl.pallas_call(
        _gqa_ctx_kernel,
        grid=(b, GQA_KV_HEADS),
        in_specs=[pl.BlockSpec((c, gw), lambda bi, g: (ctx0 + bi, g)),
                  pl.BlockSpec((c, LANE), lambda bi, g: (ctx0 + bi, g)),
                  pl.BlockSpec((c, LANE), lambda bi, g: (ctx0 + bi, gvc + g)),
                  sink_spec, pl.BlockSpec(memory_space=pl.ANY)],
        out_specs=pl.BlockSpec((c, gw), lambda bi, g: (ctx0 + bi, g)),
        out_shape=jax.ShapeDtypeStruct(y.shape, BF16),
        input_output_aliases={4: 0},
        compiler_params=_params(("parallel", "parallel")),
        name="gqa_attn_ctx",
    )(gq, gk, px, sink, y)


MIX_ROWS = 256


def _mix_out_kernel(gb_ref, gcv_ref, v_ref, gcp_ref, vp_ref, gcn_ref, vn_ref, cw_ref, ym_ref, yg_ref,
                    gc_ref, gm_ref, gg_ref, wc_ref, wm_ref, wg_ref, wo_ref, h_ref, mod_ref, o_ref, *, s, c, mx):
    yc = _conv_tile(gb_ref, gcv_ref, v_ref, gcp_ref, vp_ref, gcn_ref, vn_ref, cw_ref, s=s, c=c, mx=mx)
    gate = lambda ref: jax.nn.sigmoid(ref[...].astype(F32))
    merged = gate(gc_ref) * _dot(yc, wc_ref[...])
    merged += gate(gm_ref) * _dot(ym_ref[...], wm_ref[...])
    merged += gate(gg_ref) * _dot(yg_ref[...], wg_ref[...])
    o_ref[...] = h_ref[...] + mod_ref[0, 5:6, :] * _dot(merged.astype(BF16), wo_ref[...])


def _mix_out(ym, yg, px, conv_w, o_gate, w_bc, w_bm, w_bg, w_out, l, h, mods, rows, s, c, mx, b):
    d = w_out.shape[2]
    bm = _pick(MIX_ROWS, s, c)
    per = s // bm
    assert o_gate % d == 0
    g0 = o_gate // d
    yspec = lambda a: pl.BlockSpec((bm, a.shape[1]), lambda i: (i, 0))
    gspec = lambda k: pl.BlockSpec((bm, d), lambda i: (i, g0 + k))
    wspec = lambda w: pl.BlockSpec((None,) + w.shape[1:], lambda i: (l, 0, 0), pipeline_mode=pl.Buffered(1))
    return pl.pallas_call(
        functools.partial(_mix_out_kernel, s=s, c=c, mx=mx),
        grid=(rows // bm,),
        in_specs=_conv_specs(px, conv_w.shape[1], bm) + [
                  yspec(ym), yspec(yg), gspec(0), gspec(1), gspec(2),
                  wspec(w_bc), wspec(w_bm), wspec(w_bg), wspec(w_out),
                  pl.BlockSpec((bm, d), lambda i: (i, 0)),
                  pl.BlockSpec((1, N_MOD, d), lambda i: (jnp.minimum(i // per, b), 0, 0))],
        out_specs=pl.BlockSpec((bm, d), lambda i: (i, 0)),
        out_shape=jax.ShapeDtypeStruct((rows, d), F32),
        compiler_params=_params(("parallel",)),
        name="mix_out",
    )(px, px, px, px, px, px, px, conv_w, ym, yg, px, px, px, w_bc, w_bm, w_bg, w_out, h, mods)


def _rope_table(s, n_ctx_rows, dim):
    t = jnp.arange(s, dtype=jnp.int32)
    pos = jnp.stack([t // GRID_W, t % GRID_W], axis=1).astype(F32)
    inv = ROPE_BASE ** (-jnp.arange(0, dim, 2, dtype=F32) / dim)
    lane = jnp.arange(LANE)
    ang = pos[:, jnp.minimum(lane // dim, 1)] * inv[lane % (dim // 2)][None, :]
    active = (lane < 2 * dim)[None, :]
    sign = jnp.where((lane % dim) < dim // 2, -1.0, 1.0)[None, :]
    cos = jnp.where(active, jnp.cos(ang), 1.0)
    sin = jnp.where(active, jnp.sin(ang) * sign, 0.0)
    pad = ((0, n_ctx_rows), (0, 0))
    return jnp.pad(cos, pad, constant_values=1.0), jnp.pad(sin, pad)


def _layout(cw, ql, kvl, d):
    lay = {"conv": 0}
    off = 3 * cw
    for name, width in (("gq", GQA_HEADS * GQA_HEAD_DIM), ("cq", ql), ("ckv", kvl),
                        ("gk", GQA_KV_HEADS * GQA_HEAD_DIM), ("gv", GQA_KV_HEADS * GQA_HEAD_DIM),
                        ("kr", LANE)):
        lay[name] = off
        off += width
    q = max(d, 512)
    off = -(-off // q) * q
    lay["gate"] = off
    lay["total"] = off + 3 * d
    return lay


def _pack_kernel(w_ref, o_ref, *, moves, zero):
    for dst, src, width in moves:
        o_ref[dst:dst + width, :] = w_ref[src:src + width, :].astype(BF16)
    o_ref[zero[0]:zero[1], :] = jnp.zeros((zero[1] - zero[0], o_ref.shape[1]), BF16)


def _pack_w_in(w, lay, cw, ql, kvl):
    depth, d, n = w.shape
    w_t = jnp.swapaxes(w, 1, 2)
    o_mla = 3 * cw
    o_gqa = o_mla + ql + kvl + MLA_ROPE
    gqw, gkw = GQA_HEADS * GQA_HEAD_DIM, GQA_KV_HEADS * GQA_HEAD_DIM
    o_gate = o_gqa + gqw + 2 * gkw
    moves = ((0, 0, o_mla), (lay["gq"], o_gqa, gqw), (lay["cq"], o_mla, ql + kvl),
             (lay["gk"], o_gqa + gqw, 2 * gkw), (lay["kr"], o_mla + ql + kvl, MLA_ROPE),
             (lay["gate"], o_gate, n - o_gate))
    cb = _pick(256, d)
    return pl.pallas_call(
        functools.partial(_pack_kernel, moves=moves, zero=(lay["kr"] + MLA_ROPE, lay["gate"])),
        grid=(depth, d // cb),
        in_specs=[pl.BlockSpec((None, n, cb), lambda l, j: (l, 0, j))],
        out_specs=pl.BlockSpec((None, lay["total"], cb), lambda l, j: (l, 0, j)),
        out_shape=jax.ShapeDtypeStruct((depth, lay["total"], d), BF16),
        compiler_params=_params(("parallel", "parallel")),
        name="pack_w_in",
    )(w_t)


def kernel(x, c, ctx, c_ctx, ada_w, ada_b, ffn1_norm, ffn1_w_gu, ffn1_w_down, mix_norm, w_in, conv_w,
           mla_q_norm, mla_w_qb, mla_kv_norm, mla_w_kvb, gqa_sink, w_branch_conv, w_branch_mla,
           w_branch_gqa, w_out, ffn2_norm, ffn2_w_gu, ffn2_w_down, final_norm):
    b, s, d = x.shape
    cl = ctx.shape[1]
    depth = ada_w.shape[0]
    cw = conv_w.shape[-1]
    ql, kvl = mla_q_norm.shape[-1], mla_kv_norm.shape[-1]
    mx, mc = b * s, b * cl
    assert b + 1 <= 8 and s % cl == 0 and cl % ATT_BLOCK == 0

    cvec = jnp.zeros((8, d), F32).at[:b].set(c).at[b].set(c_ctx)
    mods = _mods(cvec, ada_w, ada_b).reshape(depth, 8, N_MOD, d)

    lay = _layout(cw, ql, kvl, d)
    tabs = _rope_table(s, mc, MLA_ROPE // 2) + _rope_table(s, mc, GQA_HEAD_DIM // 2)
    h = None
    bf = _pick(512, ffn1_w_down.shape[1])
    w1_gu, w1_dn = _gu_to_bf16(ffn1_w_gu, bf), ffn1_w_down
    w2_gu, w2_dn = _gu_to_bf16(ffn2_w_gu, bf), ffn2_w_down
    w_bc, w_bm, w_bg = _to_bf16(w_branch_conv), _to_bf16(w_branch_mla), _to_bf16(w_branch_gqa)
    w_o = _to_bf16(w_out)
    w_p = _pack_w_in(w_in, lay, cw, ql, kvl)

    for l in range(depth):
        with_ctx = l < depth - 1
        rows = mx + mc if with_ctx else mx
        wqb_p = jnp.pad(mla_w_qb[l].reshape(ql, MLA_HEADS, MLA_NOPE + MLA_ROPE),
                        ((0, 0), (0, 0), (0, MLA_QK - MLA_NOPE - MLA_ROPE))).reshape(ql, -1).astype(BF16)
        wkv = mla_w_kvb[l].reshape(kvl, MLA_HEADS, MLA_NOPE + MLA_V)
        wkn_t = wkv[:, :, :MLA_NOPE].reshape(kvl, -1).T.astype(BF16)
        wv = wkv[:, :, MLA_NOPE:].reshape(kvl, -1).astype(BF16)
        sink = jnp.broadcast_to((gqa_sink[l].astype(F32) * LOG2E)[:, None], (GQA_HEADS, LANE))

        if l == 0:
            h = _ffn(x.reshape(mx, d), ctx.reshape(mc, d), mx + mc, mods[l], ffn1_norm[l], w1_gu, w1_dn, l, 0,
                     s, b, final_norm, False)
        else:
            h = _ffn(h, None, mx + mc, mods[l], ffn1_norm[l], w1_gu, w1_dn, l, 0, s, b, final_norm, False)
        px = _proj(h, mods[l], mix_norm[l], w_p, l, s, b)
        q, kt, v, gq, gk = _prep(px, lay, mla_q_norm[l], mla_kv_norm[l], wqb_p, wkn_t, wv, tabs, s, mx)
        y_mla = _mla_attn(q, kt, v, b, s, cl, with_ctx)
        y_gqa = _gqa_attn(gq, gk, px, lay["gv"], sink, b, s, cl, with_ctx)
        h = _mix_out(y_mla, y_gqa, px, conv_w[l], lay["gate"], w_bc, w_bm, w_bg, w_o, l, h, mods[l], rows, s, cl,
                     mx, b)
        h = _ffn(h, None, rows, mods[l], ffn2_norm[l], w2_gu, w2_dn, l, 6, s, b, final_norm, not with_ctx)
    return h.reshape(b, s, d)
```

```python
import functools

import jax
import jax.numpy as jnp
from jax import lax
from jax.experimental import pallas as pl
from jax.experimental.pallas import tpu as pltpu

F32 = jnp.float32
BF16 = jnp.bfloat16

GRID_W = 64
N_MOD = 9
EPS = 1e-6
ROPE_BASE = 10000.0
CONV_K = 3
MLA_HEADS = 8
MLA_NOPE = 128
MLA_ROPE = 64
MLA_V = 128
MLA_SCALE = (MLA_NOPE + MLA_ROPE) ** -0.5
GQA_HEADS = 8
GQA_KV_HEADS = 2
GQA_GROUP = GQA_HEADS // GQA_KV_HEADS
GQA_HEAD_DIM = 128
GQA_SCALE = GQA_HEAD_DIM ** -0.5
WINDOW = 128
ATT_BLOCK = 128

LANE = 128
MLA_QK = 2 * LANE
MLA_SUB = 256
LOG2E = 1.4426950408889634
VMEM_LIMIT = 56 << 20
NT_DIMS = (((1,), (1,)), ((), ()))


def _pick(target, *sizes):
    b = target
    while any(s % b for s in sizes):
        b //= 2
        assert b >= 8, (target, sizes)
    return b


def _params(sem):
    return pltpu.CompilerParams(dimension_semantics=sem, vmem_limit_bytes=VMEM_LIMIT)


def _dot(a, b):
    return jnp.dot(a, b, preferred_element_type=F32)


def _dot_nt(a, b):
    return lax.dot_general(a, b, NT_DIMS, preferred_element_type=F32)


def _rms(x, g):
    return x * lax.rsqrt(jnp.mean(x * x, axis=-1, keepdims=True) + EPS) * g


CAST_BLOCK_BYTES = 6 << 20


def _cast_kernel(w_ref, o_ref):
    o_ref[...] = w_ref[...].astype(BF16)


def _to_bf16(w):
    depth, k, n = w.shape
    rows = depth * k
    rb = _pick(max(16, 1 << ((CAST_BLOCK_BYTES // (4 * n)).bit_length() - 1)), rows)
    out = pl.pallas_call(
        _cast_kernel,
        grid=(rows // rb,),
        in_specs=[pl.BlockSpec((rb, n), lambda i: (i, 0))],
        out_specs=pl.BlockSpec((rb, n), lambda i: (i, 0)),
        out_shape=jax.ShapeDtypeStruct((rows, n), BF16),
        compiler_params=_params(("parallel",)),
        name="cast_bf16",
    )(w.reshape(rows, n))
    return out.reshape(depth, k, n)


def _cast_gu_kernel(w_ref, o_ref, *, bf):
    nf = o_ref.shape[0]
    for j in range(nf):
        o_ref[j, :, :bf] = w_ref[:, j * bf:(j + 1) * bf].astype(BF16)
        o_ref[j, :, bf:] = w_ref[:, (nf + j) * bf:(nf + j + 1) * bf].astype(BF16)


def _gu_to_bf16(w, bf):
    depth, k, n = w.shape
    nf = n // (2 * bf)
    rb = _pick(max(16, 1 << ((CAST_BLOCK_BYTES // (4 * n)).bit_length() - 1)), k)
    kb = k // rb
    return pl.pallas_call(
        functools.partial(_cast_gu_kernel, bf=bf),
        grid=(depth, kb),
        in_specs=[pl.BlockSpec((None, rb, n), lambda l, i: (l, i, 0))],
        out_specs=pl.BlockSpec((None, nf, rb, 2 * bf), lambda l, i: (l, 0, i, 0)),
        out_shape=jax.ShapeDtypeStruct((depth, nf, k, 2 * bf), BF16),
        compiler_params=_params(("parallel", "parallel")),
        name="cast_gate_up",
    )(w)


def _mods_kernel(c_ref, w_ref, b_ref, o_ref):
    c = c_ref[...]
    s = (c * jax.nn.sigmoid(c)).astype(BF16)
    o_ref[0] = _dot(s, w_ref[0].astype(BF16)) + b_ref[0]


def _mods(cvec, ada_w, ada_b):
    depth, d, n = ada_w.shape
    bn = _pick(1024, n)
    return pl.pallas_call(
        _mods_kernel,
        grid=(depth, n // bn),
        in_specs=[pl.BlockSpec((8, d), lambda l, j: (0, 0)),
                  pl.BlockSpec((1, d, bn), lambda l, j: (l, 0, j)),
                  pl.BlockSpec((1, 1, bn), lambda l, j: (l, 0, j))],
        out_specs=pl.BlockSpec((1, 8, bn), lambda l, j: (l, 0, j)),
        out_shape=jax.ShapeDtypeStruct((depth, 8, n), F32),
        compiler_params=_params(("parallel", "parallel")),
        name="adaln_mods",
    )(cvec, ada_w, ada_b.reshape(depth, 1, n))


def _norm_mod_rows(h_ref, mod_ref, g_ref, xn_ref, i_shift, i_scale, rc):
    shift = mod_ref[0, i_shift:i_shift + 1, :]
    scale1 = 1.0 + mod_ref[0, i_scale:i_scale + 1, :]
    g = g_ref[...]

    def body(r, carry):
        rows = pl.ds(pl.multiple_of(r * rc, rc), rc)
        xn_ref[rows, :] = (_rms(h_ref[rows, :], g) * scale1 + shift).astype(BF16)
        return carry

    lax.fori_loop(0, h_ref.shape[0] // rc, body, 0)


def _norm_mod_chunk(hn_ref, modn_ref, g_ref, xn_ref, slot, k, ch, i_shift, i_scale):
    r0 = pl.multiple_of(jnp.minimum(k * ch, hn_ref.shape[0] - ch), 16)
    shift = modn_ref[0, i_shift:i_shift + 1, :]
    scale1 = 1.0 + modn_ref[0, i_scale:i_scale + 1, :]
    x = hn_ref[pl.ds(r0, ch), :]
    xn_ref[slot, pl.ds(r0, ch), :] = (_rms(x, g_ref[...]) * scale1 + shift).astype(BF16)


def _chunk_rows(bm, steps):
    return min(bm, -(-bm // (16 * steps)) * 16)


def _mod_spec(d, bm, s, b, ahead=0, last=None):
    per = s // bm

    def idx(i, j):
        t = i if not ahead else jnp.minimum(i + ahead, last)
        return jnp.minimum(t // per, b), 0, 0

    return pl.BlockSpec((1, N_MOD, d), idx)


FFN_UP_ROWS = 1024
FFN_DOWN_ROWS = 256
WEIGHT_CHUNK_ROWS = 512

def _swiglu(gu):
    bf = gu.shape[1] // 2
    gg, uu = gu[:, :bf], gu[:, bf:]
    return (gg * jax.nn.sigmoid(gg) * uu).astype(BF16)


def _up_split_kernel(h_ref, hc_ref, mod_ref, g_ref, w_ref, o_ref, xn_ref, *, i0, rc, n_lat):
    first = pl.program_id(1) == 0
    is_lat = pl.program_id(0) < n_lat

    @pl.when(first & is_lat)
    def _():
        _norm_mod_rows(h_ref, mod_ref, g_ref, xn_ref, i0, i0 + 1, rc)

    @pl.when(first & jnp.logical_not(is_lat))
    def _():
        _norm_mod_rows(hc_ref, mod_ref, g_ref, xn_ref, i0, i0 + 1, rc)

    o_ref[...] = _swiglu(_dot(xn_ref[...], w_ref[...]))


def _up_kernel(h_ref, mod_ref, modn_ref, g_ref, w_ref, o_ref, xn_ref, *, i0, rc, ch):
    i, j = pl.program_id(0), pl.program_id(1)
    slot = i % 2

    @pl.when((j == 0) & (i == 0))
    def _():
        _norm_mod_rows(h_ref, mod_ref, g_ref, xn_ref.at[0], i0, i0 + 1, rc)

    o_ref[...] = _swiglu(_dot(xn_ref[slot], w_ref[...]))
    _norm_mod_chunk(h_ref, modn_ref, g_ref, xn_ref, 1 - slot, jnp.maximum(j - 1, 0), ch, i0, i0 + 1)


def _down_kernel(a_ref, w_ref, h_ref, hc_ref, mod_ref, fg_ref, o_ref, wbf_ref, *, i_gate, n_lat, final, nw):
    t = pl.program_id(0)

    @pl.when(t < nw)
    def _():
        ck = w_ref.shape[0]
        wbf_ref[pl.ds(pl.multiple_of(t * ck, ck), ck), :] = w_ref[...].astype(BF16)

    @pl.when(t >= nw)
    def _():
        h = h_ref[...]
        if n_lat is not None:
            h = jnp.where(t - nw < n_lat, h, hc_ref[...])
        out = h + (0.5 * mod_ref[0, i_gate:i_gate + 1, :]) * _dot(a_ref[...], wbf_ref[...])
        o_ref[...] = _rms(out, fg_ref[...]) if final else out


def _ffn(h, hc, rows, mods, norm_g, w_gu, w_down, l, i0, s, b, final_g, final):
    d = h.shape[1]
    nf, bf = w_gu.shape[1], w_gu.shape[3] // 2
    f = nf * bf
    vec = pl.BlockSpec((1, d), lambda i, j: (0, 0))
    w_spec = pl.BlockSpec((None, None, d, 2 * bf), lambda i, j: (l, j, 0, 0))
    bm = _pick(FFN_UP_ROWS, s, rows) if hc is None else _pick(FFN_UP_ROWS, s, hc.shape[0])
    nt = rows // bm
    up_common = dict(
        grid=(nt, nf),
        out_specs=pl.BlockSpec((bm, bf), lambda i, j: (i, j)),
        out_shape=jax.ShapeDtypeStruct((rows, f), BF16),
        compiler_params=_params(("arbitrary", "arbitrary")),
        name="ffn_up")
    if hc is not None:
        n_lat = h.shape[0] // bm
        act = pl.pallas_call(
            functools.partial(_up_split_kernel, i0=i0, rc=_pick(128, bm), n_lat=n_lat),
            in_specs=[pl.BlockSpec((bm, d), lambda i, j: (jnp.minimum(i, n_lat - 1), 0)),
                      pl.BlockSpec((bm, d), lambda i, j: (jnp.maximum(i - n_lat, 0), 0)),
                      _mod_spec(d, bm, s, b), vec, w_spec],
            scratch_shapes=[pltpu.VMEM((bm, d), BF16)],
            **up_common,
        )(h, hc, mods, norm_g.reshape(1, d), w_gu)
    else:
        assert nf >= 2
        h_idx = lambda i, j: (jnp.where((i == 0) & (j == 0), 0, jnp.minimum(i + 1, nt - 1)), 0)
        act = pl.pallas_call(
            functools.partial(_up_kernel, i0=i0, rc=_pick(128, bm), ch=_chunk_rows(bm, nf - 1)),
            in_specs=[pl.BlockSpec((bm, d), h_idx), _mod_spec(d, bm, s, b), _mod_spec(d, bm, s, b, 1, nt - 1),
                      vec, w_spec],
            scratch_shapes=[pltpu.VMEM((2, bm, d), BF16)],
            **up_common,
        )(h, mods, mods, norm_g.reshape(1, d), w_gu)

    bd = _pick(FFN_DOWN_ROWS, s, rows) if hc is None else _pick(FFN_DOWN_ROWS, s, hc.shape[0])
    per = s // bd
    ck = _pick(WEIGHT_CHUNK_ROWS, f)
    nw = f // ck
    row = lambda t: jnp.maximum(t - nw, 0)
    if hc is None:
        n_lat = None
        h_specs = [pl.BlockSpec((bd, d), lambda t: (row(t), 0)), pl.BlockSpec((1, d), lambda t: (0, 0))]
        hc = norm_g.reshape(1, d)
    else:
        n_lat = h.shape[0] // bd
        h_specs = [pl.BlockSpec((bd, d), lambda t: (jnp.minimum(row(t), n_lat - 1), 0)),
                   pl.BlockSpec((bd, d), lambda t: (jnp.maximum(row(t) - n_lat, 0), 0))]
    return pl.pallas_call(
        functools.partial(_down_kernel, i_gate=i0 + 2, n_lat=n_lat, final=final, nw=nw),
        grid=(nw + rows // bd,),
        in_specs=[pl.BlockSpec((bd, f), lambda t: (row(t), 0)),
                  pl.BlockSpec((None, ck, d), lambda t: (l, jnp.minimum(t, nw - 1), 0))] + h_specs + [
                  pl.BlockSpec((1, N_MOD, d), lambda t: (jnp.minimum(row(t) // per, b), 0, 0)),
                  pl.BlockSpec((1, d), lambda t: (0, 0))],
        out_specs=pl.BlockSpec((bd, d), lambda t: (row(t), 0)),
        out_shape=jax.ShapeDtypeStruct((rows, d), F32),
        scratch_shapes=[pltpu.VMEM((f, d), BF16)],
        compiler_params=_params(("arbitrary",)),
        name="ffn_down",
    )(act, w_down, h, hc, mods, final_g.reshape(1, d))


def _first_norm_kernel(h_ref, mod_ref, g_ref, o_ref, *, rc):
    _norm_mod_rows(h_ref, mod_ref, g_ref, o_ref, 3, 4, rc)


def _proj_kernel(x0_ref, h_ref, modn_ref, g_ref, w_ref, o_ref, xn_ref, *, nch):
    i, j = pl.program_id(0), pl.program_id(1)
    slot = i % 2

    @pl.when((j == 0) & (i == 0))
    def _():
        xn_ref[0] = x0_ref[...]

    o_ref[...] = _dot_nt(xn_ref[slot], w_ref[...]).astype(BF16)
    ch = h_ref.shape[0]
    r0 = pl.multiple_of(jnp.clip(j - 1, 0, nch - 1) * ch, ch)
    x = _rms(h_ref[...], g_ref[...]) * (1.0 + modn_ref[0, 4:5, :]) + modn_ref[0, 3:4, :]
    xn_ref[1 - slot, pl.ds(r0, ch), :] = x.astype(BF16)


def _proj(h, mods, norm_g, w_p, l, s, b):
    rows, d = h.shape
    n = w_p.shape[1]
    bm = _pick(1024, s, rows)
    bn = _pick(2048, n)
    nt, nj = rows // bm, n // bn
    assert nj >= 2
    nch = max(k for k in (1, 2, 4, 8, 16) if k <= nj - 1 and bm % (16 * k) == 0)
    ch = bm // nch
    vec = pl.BlockSpec((1, d), lambda i, j: (0, 0))
    x0 = pl.pallas_call(
        functools.partial(_first_norm_kernel, rc=_pick(128, bm)),
        grid=(1, 1),
        in_specs=[pl.BlockSpec((bm, d), lambda i, j: (0, 0)), _mod_spec(d, bm, s, b), vec],
        out_specs=pl.BlockSpec((bm, d), lambda i, j: (0, 0)),
        out_shape=jax.ShapeDtypeStruct((bm, d), BF16),
        compiler_params=_params(("arbitrary", "arbitrary")),
        name="in_proj_first_norm",
    )(h, mods, norm_g.reshape(1, d))
    h_idx = lambda i, j: (jnp.minimum(i + 1, nt - 1) * nch + jnp.clip(j - 1, 0, nch - 1), 0)
    return pl.pallas_call(
        functools.partial(_proj_kernel, nch=nch),
        grid=(nt, nj),
        in_specs=[pl.BlockSpec((bm, d), lambda i, j: (0, 0), pipeline_mode=pl.Buffered(1)),
                  pl.BlockSpec((ch, d), h_idx), _mod_spec(d, bm, s, b, 1, nt - 1), vec,
                  pl.BlockSpec((None, bn, d), lambda i, j: (l, j, 0))],
        out_specs=pl.BlockSpec((bm, bn), lambda i, j: (i, j)),
        out_shape=jax.ShapeDtypeStruct((rows, n), BF16),
        scratch_shapes=[pltpu.VMEM((2, bm, d), BF16)],
        compiler_params=_params(("arbitrary", "arbitrary")),
        name="in_proj",
    )(x0, h, mods, norm_g.reshape(1, d), w_p)


def _conv_tile(gb_ref, gc_ref, v_ref, gcp_ref, vp_ref, gcn_ref, vn_ref, w_ref, *, s, c, mx):
    bm = gb_ref.shape[0]
    row0 = pl.program_id(0) * bm
    is_lat = row0 < mx
    at_start = jnp.where(is_lat, row0 % s == 0, (row0 - mx) % c == 0)
    at_end = jnp.where(is_lat, (row0 + bm) % s == 0, (row0 - mx + bm) % c == 0)
    cv = gc_ref[...].astype(F32) * v_ref[...].astype(F32)
    hp = (gcp_ref[...].astype(F32) * vp_ref[...].astype(F32))[15:16, :]
    hn = (gcn_ref[...].astype(F32) * vn_ref[...].astype(F32))[0:1, :]
    hp = jnp.where(at_start, 0.0, hp)
    hn = jnp.where(at_end, 0.0, hn)
    rid = lax.broadcasted_iota(jnp.int32, (bm, 1), 0)
    prev = jnp.where(rid == 0, hp, pltpu.roll(cv, 1, 0))
    nxt = jnp.where(rid == bm - 1, hn, pltpu.roll(cv, bm - 1, 0))
    w = w_ref[...]
    y = gb_ref[...].astype(F32) * (prev * w[0:1, :] + cv * w[1:2, :] + nxt * w[2:3, :])
    return y.astype(BF16)


def _conv_specs(px, cw, bm):
    m = px.shape[0]
    hb = bm // 16
    main = lambda col: pl.BlockSpec((bm, cw), lambda i: (i, col))
    prev = lambda col: pl.BlockSpec((16, cw), lambda i: (jnp.maximum(i * hb - 1, 0), col))
    nxt = lambda col: pl.BlockSpec((16, cw), lambda i: (jnp.minimum((i + 1) * hb, m // 16 - 1), col))
    return [main(0), main(1), main(2), prev(1), prev(2), nxt(1), nxt(2), pl.BlockSpec((CONV_K, cw), lambda i: (0, 0))]


def _rope(x, cos, sin, half):
    lane = lax.broadcasted_iota(jnp.int32, x.shape, 1)
    first = (lane % (2 * half)) < half
    rot = jnp.where(first, pltpu.roll(x, LANE - half, 1), pltpu.roll(x, half, 1))
    return x * cos + rot * sin


def _prep_kernel(gq_ref, cq_ref, ckv_ref, gk_ref, kr_ref, qn_ref, kvn_ref, wqb_ref, wknt_ref, wv_ref,
                 mcos_ref, msin_ref, gcos_ref, gsin_ref, q_ref, kt_ref, v_ref, gqo_ref, gko_ref):
    mcos, msin = mcos_ref[...], msin_ref[...]
    gcos, gsin = gcos_ref[...], gsin_ref[...]
    mh = MLA_ROPE // 4
    gh = GQA_HEAD_DIM // 4

    cqn = _rms(cq_ref[...].astype(F32), qn_ref[...]).astype(BF16)
    q = _dot(cqn, wqb_ref[...]) * (MLA_SCALE * LOG2E)
    ckvn = _rms(ckv_ref[...].astype(F32), kvn_ref[...]).astype(BF16)
    knt = _dot_nt(wknt_ref[...], ckvn)
    vv = _dot(ckvn, wv_ref[...])
    krt = _rope(kr_ref[...].astype(F32), mcos, msin, mh).T.astype(BF16)
    ones = jnp.ones((q.shape[0], LANE), BF16)
    for h in range(MLA_HEADS):
        a = h * MLA_QK
        q_ref[:, a:a + LANE] = q[:, a:a + LANE].astype(BF16)
        q_ref[:, a + LANE:a + MLA_QK] = _rope(q[:, a + LANE:a + MLA_QK], mcos, msin, mh).astype(BF16)
        kt_ref[a:a + LANE, :] = knt[h * LANE:(h + 1) * LANE, :].astype(BF16)
        kt_ref[a + LANE:a + MLA_QK, :] = krt
        v_ref[:, a:a + LANE] = vv[:, h * LANE:(h + 1) * LANE].astype(BF16)
        v_ref[:, a + LANE:a + MLA_QK] = ones

    gq = gq_ref[...].astype(F32)
    for h in range(GQA_HEADS):
        a = h * GQA_HEAD_DIM
        gqo_ref[:, a:a + LANE] = (_rope(gq[:, a:a + LANE], gcos, gsin, gh) * (GQA_SCALE * LOG2E)).astype(BF16)
    gk = gk_ref[...].astype(F32)
    for h in range(GQA_KV_HEADS):
        a = h * GQA_HEAD_DIM
        gko_ref[:, a:a + LANE] = _rope(gk[:, a:a + LANE], gcos, gsin, gh).astype(BF16)


def _prep(px, lay, q_norm, kv_norm, wqb_p, wkn_t, wv, tabs, s, mx):
    m = px.shape[0]
    bm = _pick(512, s, m - mx)
    n_lat = mx // bm
    per = s // bm
    ql, kvl = q_norm.shape[0], kv_norm.shape[0]
    gqw, gkw = GQA_HEADS * GQA_HEAD_DIM, GQA_KV_HEADS * GQA_HEAD_DIM

    def col(width, off):
        assert off % width == 0, (width, off)
        return pl.BlockSpec((bm, width), lambda i: (i, off // width))

    const = lambda r, c: pl.BlockSpec((r, c), lambda i: (0, 0))
    tab = pl.BlockSpec((bm, LANE), lambda i: (jnp.where(i < n_lat, i % per, per + i - n_lat), 0))
    row = lambda width: pl.BlockSpec((bm, width), lambda i: (i, 0))
    hq = MLA_HEADS * MLA_QK
    return pl.pallas_call(
        _prep_kernel,
        grid=(m // bm,),
        in_specs=[col(gqw, lay["gq"]), col(ql, lay["cq"]), col(kvl, lay["ckv"]), col(gkw, lay["gk"]),
                  col(LANE, lay["kr"]), const(1, ql), const(1, kvl), const(ql, hq),
                  const(MLA_HEADS * MLA_NOPE, kvl), const(kvl, MLA_HEADS * MLA_V), tab, tab, tab, tab],
        out_specs=[row(hq), pl.BlockSpec((hq, bm), lambda i: (0, i)), row(hq), row(gqw), row(gkw)],
        out_shape=[jax.ShapeDtypeStruct((m, hq), BF16), jax.ShapeDtypeStruct((hq, m), BF16),
                   jax.ShapeDtypeStruct((m, hq), BF16),
                   jax.ShapeDtypeStruct((m, gqw), BF16), jax.ShapeDtypeStruct((m, gkw), BF16)],
        compiler_params=_params(("parallel",)),
        name="attn_prep",
    )(px, px, px, px, px, q_norm.reshape(1, ql), kv_norm.reshape(1, kvl), wqb_p, wkn_t, wv, *tabs)


def _mla_kernel(q_ref, kl_ref, kc_ref, vl_ref, vc_ref, o_ref, *, n_sub):
    kl, kc = kl_ref[...], kc_ref[...]
    subs = [q_ref[i * MLA_SUB:(i + 1) * MLA_SUB, :] for i in range(n_sub)]
    scores = [(_dot(q, kl), _dot(q, kc)) for q in subs]
    for i, (s1, s2) in enumerate(scores):
        m = jnp.maximum(jnp.max(s1, axis=-1, keepdims=True), jnp.max(s2, axis=-1, keepdims=True))
        p1 = jnp.exp2((s1 - m).astype(BF16))
        p2 = jnp.exp2((s2 - m).astype(BF16))
        o = _dot(p1, vl_ref[...]) + _dot(p2, vc_ref[...])
        o_ref[i * MLA_SUB:(i + 1) * MLA_SUB, :] = (o[:, :MLA_V] / o[:, MLA_V:]).astype(BF16)


def _mla_ctx_kernel(q_ref, kc_ref, vc_ref, y_ref, o_ref):
    del y_ref
    s2 = _dot(q_ref[...], kc_ref[...])
    p2 = jnp.exp2(s2 - jnp.max(s2, axis=-1, keepdims=True)).astype(BF16)
    o = _dot(p2, vc_ref[...])
    o_ref[...] = (o[:, :MLA_V] / o[:, MLA_V:]).astype(BF16)


def _mla_attn(q, kt, v, b, s, c, with_ctx):
    mx = b * s
    n_sub = max(d for d in (1, 2, 4, 8) if (s // MLA_SUB) % d == 0)
    bq = n_sub * MLA_SUB
    nq = s // bq
    rows = mx + (b * c if with_ctx else 0)
    ctx0 = mx // c
    y = pl.pallas_call(
        functools.partial(_mla_kernel, n_sub=n_sub),
        grid=(b, MLA_HEADS, nq),
        in_specs=[pl.BlockSpec((bq, MLA_QK), lambda bi, h, qi: (bi * nq + qi, h)),
                  pl.BlockSpec((MLA_QK, s), lambda bi, h, qi: (h, bi)),
                  pl.BlockSpec((MLA_QK, c), lambda bi, h, qi: (h, ctx0 + bi)),
                  pl.BlockSpec((s, MLA_QK), lambda bi, h, qi: (bi, h)),
                  pl.BlockSpec((c, MLA_QK), lambda bi, h, qi: (ctx0 + bi, h))],
        out_specs=pl.BlockSpec((bq, MLA_V), lambda bi, h, qi: (bi * nq + qi, h)),
        out_shape=jax.ShapeDtypeStruct((rows, MLA_HEADS * MLA_V), BF16),
        compiler_params=_params(("parallel", "parallel", "arbitrary")),
        name="mla_attn",
    )(q, kt, kt, v, v)
    if not with_ctx:
        return y
    return pl.pallas_call(
        _mla_ctx_kernel,
        grid=(b, MLA_HEADS),
        in_specs=[pl.BlockSpec((c, MLA_QK), lambda bi, h: (ctx0 + bi, h)),
                  pl.BlockSpec((MLA_QK, c), lambda bi, h: (h, ctx0 + bi)),
                  pl.BlockSpec((c, MLA_QK), lambda bi, h: (ctx0 + bi, h)),
                  pl.BlockSpec(memory_space=pl.ANY)],
        out_specs=pl.BlockSpec((c, MLA_V), lambda bi, h: (ctx0 + bi, h)),
        out_shape=jax.ShapeDtypeStruct(y.shape, BF16),
        input_output_aliases={3: 0},
        compiler_params=_params(("parallel", "parallel")),
        name="mla_attn_ctx",
    )(q, kt, v, y)


GQA_QB_MAX = 16


def _sink_rows(sink_ref, g, j, rows):
    return jnp.broadcast_to(sink_ref[pl.ds(g * GQA_GROUP + j, 1), :], (rows, LANE))[:, 0:1]


def _gqa_kernel(q_ref, kp_ref, kc_ref, kn_ref, kx_ref, vp_ref, vc_ref, vn_ref, vx_ref, sink_ref, o_ref,
                *, n_lat, qb):
    g = pl.program_id(1)
    n = pl.program_id(2)
    blk = ATT_BLOCK
    kx, vx = kx_ref[...], vx_ref[...]
    kband = jnp.concatenate([kp_ref[...], kc_ref[...], kn_ref[...]], axis=0)
    vband = jnp.concatenate([vp_ref[...], vc_ref[...], vn_ref[...]], axis=0)
    nk = 3 * blk + kx.shape[0]
    ones = jnp.ones((nk, LANE), BF16)
    snk = jnp.concatenate([jnp.broadcast_to(sink_ref[pl.ds(g * GQA_GROUP + j, 1), :], (blk, LANE))
                           for j in range(GQA_GROUP)], axis=0)
    rows = GQA_GROUP * blk
    r = lax.broadcasted_iota(jnp.int32, (rows, nk), 0) % blk
    col = lax.broadcasted_iota(jnp.int32, (rows, nk), 1)
    in_window = (jnp.abs(col - blk - r) <= WINDOW) | (col >= 3 * blk)
    for i in range(qb):
        qs = jnp.concatenate([q_ref[i * blk:(i + 1) * blk, j * LANE:(j + 1) * LANE] for j in range(GQA_GROUP)],
                             axis=0)
        keys = jnp.concatenate([kband[i * blk:(i + 3) * blk, :], kx], axis=0)
        vals = jnp.concatenate([jnp.concatenate([vband[i * blk:(i + 3) * blk, :], vx], axis=0), ones], axis=1)
        block = n * qb + i
        valid = in_window & ((col >= blk) | (block > 0)) & ((col < 2 * blk) | (col >= 3 * blk) | (block < n_lat - 1))
        sc = jnp.where(valid, _dot_nt(qs, keys), -jnp.inf)
        e = snk
        for t in range(nk // LANE):
            e = jnp.maximum(e, sc[:, t * LANE:(t + 1) * LANE])
        m = jnp.max(e, axis=-1, keepdims=True)
        ol = _dot(jnp.exp2((sc - m).astype(BF16)), vals)
        o = ol[:, :LANE] / (ol[:, LANE:] + jnp.exp2(snk - m))
        for j in range(GQA_GROUP):
            o_ref[i * blk:(i + 1) * blk, j * LANE:(j + 1) * LANE] = o[j * blk:(j + 1) * blk, :].astype(BF16)


def _gqa_ctx_kernel(q_ref, kx_ref, vx_ref, sink_ref, y_ref, o_ref):
    del y_ref
    g = pl.program_id(1)
    kx, vx = kx_ref[...], vx_ref[...]
    for j in range(GQA_GROUP):
        sc = _dot_nt(q_ref[:, j * LANE:(j + 1) * LANE], kx)
        snk = _sink_rows(sink_ref, g, j, sc.shape[0])
        m = jnp.maximum(jnp.max(sc, axis=-1, keepdims=True), snk)
        p_c = jnp.exp2(sc - m)
        l = jnp.sum(p_c, axis=-1, keepdims=True) + jnp.exp2(snk - m)
        o_ref[:, j * LANE:(j + 1) * LANE] = (_dot(p_c.astype(BF16), vx) / l).astype(BF16)


def _gqa_attn(gq, gk, px, o_gv, sink, b, s, c, with_ctx):
    mx = b * s
    blk = ATT_BLOCK
    n_lat = s // blk
    qb = max(k for k in (1, 2, 4, 8, 16) if k <= GQA_QB_MAX and n_lat % k == 0)
    nstep = n_lat // qb
    rows = mx + (b * c if with_ctx else 0)
    ctx0 = mx // c
    gvc = o_gv // LANE
    gw = GQA_GROUP * GQA_HEAD_DIM
    big = qb * blk

    def edge(shift, col0):
        def idx(bi, g, n):
            return bi * n_lat + jnp.clip(n * qb + shift, 0, n_lat - 1), col0 + g
        return pl.BlockSpec((blk, LANE), idx)

    main = lambda col0: pl.BlockSpec((big, LANE), lambda bi, g, n: (bi * nstep + n, col0 + g))
    ctx = lambda col0: pl.BlockSpec((c, LANE), lambda bi, g, n: (ctx0 + bi, col0 + g))
    sink_spec = pl.BlockSpec((GQA_HEADS, LANE), lambda *_: (0, 0))
    y = pl.pallas_call(
        functools.partial(_gqa_kernel, n_lat=n_lat, qb=qb),
        grid=(b, GQA_KV_HEADS, nstep),
        in_specs=[pl.BlockSpec((big, gw), lambda bi, g, n: (bi * nstep + n, g)),
                  edge(-1, 0), main(0), edge(qb, 0), ctx(0),
                  edge(-1, gvc), main(gvc), edge(qb, gvc), ctx(gvc), sink_spec],
        out_specs=pl.BlockSpec((big, gw), lambda bi, g, n: (bi * nstep + n, g)),
        out_shape=jax.ShapeDtypeStruct((rows, GQA_HEADS * GQA_HEAD_DIM), BF16),
        compiler_params=_params(("parallel", "parallel", "arbitrary")),
        name="gqa_attn",
    )(gq, gk, gk, gk, gk, px, px, px, px, sink)
    if not with_ctx:
        return y
    return pl.pallas_call(
        _gqa_ctx_kernel,
        grid=(b, GQA_KV_HEADS),
        in_specs=[pl.BlockSpec((c, gw), lambda bi, g: (ctx0 + bi, g)),
                  pl.BlockSpec((c, LANE), lambda bi, g: (ctx0 + bi, g)),
                  pl.BlockSpec((c, LANE), lambda bi, g: (ctx0 + bi, gvc + g)),
                  sink_spec, pl.BlockSpec(memory_space=pl.ANY)],
        out_specs=pl.BlockSpec((c, gw), lambda bi, g: (ctx0 + bi, g)),
        out_shape=jax.ShapeDtypeStruct(y.shape, BF16),
        input_output_aliases={4: 0},
        compiler_params=_params(("parallel", "parallel")),
        name="gqa_attn_ctx",
    )(gq, gk, px, sink, y)


MIX_ROWS = 256


def _mix_out_kernel(gb_ref, gcv_ref, v_ref, gcp_ref, vp_ref, gcn_ref, vn_ref, cw_ref, ym_ref, yg_ref,
                    gc_ref, gm_ref, gg_ref, wc_ref, wm_ref, wg_ref, wo_ref, h_ref, mod_ref, o_ref, *, s, c, mx):
    yc = _conv_tile(gb_ref, gcv_ref, v_ref, gcp_ref, vp_ref, gcn_ref, vn_ref, cw_ref, s=s, c=c, mx=mx)
    gate = lambda ref: jax.nn.sigmoid(ref[...].astype(F32))
    merged = gate(gc_ref) * _dot(yc, wc_ref[...])
    merged += gate(gm_ref) * _dot(ym_ref[...], wm_ref[...])
    merged += gate(gg_ref) * _dot(yg_ref[...], wg_ref[...])
    o_ref[...] = h_ref[...] + mod_ref[0, 5:6, :] * _dot(merged.astype(BF16), wo_ref[...])


def _mix_out(ym, yg, px, conv_w, o_gate, w_bc, w_bm, w_bg, w_out, l, h, mods, rows, s, c, mx, b):
    d = w_out.shape[2]
    bm = _pick(MIX_ROWS, s, c)
    per = s // bm
    assert o_gate % d == 0
    g0 = o_gate // d
    yspec = lambda a: pl.BlockSpec((bm, a.shape[1]), lambda i: (i, 0))
    gspec = lambda k: pl.BlockSpec((bm, d), lambda i: (i, g0 + k))
    wspec = lambda w: pl.BlockSpec((None,) + w.shape[1:], lambda i: (l, 0, 0), pipeline_mode=pl.Buffered(1))
    return pl.pallas_call(
        functools.partial(_mix_out_kernel, s=s, c=c, mx=mx),
        grid=(rows // bm,),
        in_specs=_conv_specs(px, conv_w.shape[1], bm) + [
                  yspec(ym), yspec(yg), gspec(0), gspec(1), gspec(2),
                  wspec(w_bc), wspec(w_bm), wspec(w_bg), wspec(w_out),
                  pl.BlockSpec((bm, d), lambda i: (i, 0)),
                  pl.BlockSpec((1, N_MOD, d), lambda i: (jnp.minimum(i // per, b), 0, 0))],
        out_specs=pl.BlockSpec((bm, d), lambda i: (i, 0)),
        out_shape=jax.ShapeDtypeStruct((rows, d), F32),
        compiler_params=_params(("parallel",)),
        name="mix_out",
    )(px, px, px, px, px, px, px, conv_w, ym, yg, px, px, px, w_bc, w_bm, w_bg, w_out, h, mods)


def _rope_table(s, n_ctx_rows, dim):
    t = jnp.arange(s, dtype=jnp.int32)
    pos = jnp.stack([t // GRID_W, t % GRID_W], axis=1).astype(F32)
    inv = ROPE_BASE ** (-jnp.arange(0, dim, 2, dtype=F32) / dim)
    lane = jnp.arange(LANE)
    ang = pos[:, jnp.minimum(lane // dim, 1)] * inv[lane % (dim // 2)][None, :]
    active = (lane < 2 * dim)[None, :]
    sign = jnp.where((lane % dim) < dim // 2, -1.0, 1.0)[None, :]
    cos = jnp.where(active, jnp.cos(ang), 1.0)
    sin = jnp.where(active, jnp.sin(ang) * sign, 0.0)
    pad = ((0, n_ctx_rows), (0, 0))
    return jnp.pad(cos, pad, constant_values=1.0), jnp.pad(sin, pad)


def _layout(cw, ql, kvl, d):
    lay = {"conv": 0}
    off = 3 * cw
    for name, width in (("gq", GQA_HEADS * GQA_HEAD_DIM), ("cq", ql), ("ckv", kvl),
                        ("gk", GQA_KV_HEADS * GQA_HEAD_DIM), ("gv", GQA_KV_HEADS * GQA_HEAD_DIM),
                        ("kr", LANE)):
        lay[name] = off
        off += width
    q = max(d, 512)
    off = -(-off // q) * q
    lay["gate"] = off
    lay["total"] = off + 3 * d
    return lay


def _pack_kernel(w_ref, o_ref, *, moves, zero):
    for dst, src, width in moves:
        o_ref[dst:dst + width, :] = w_ref[src:src + width, :].astype(BF16)
    o_ref[zero[0]:zero[1], :] = jnp.zeros((zero[1] - zero[0], o_ref.shape[1]), BF16)


def _pack_w_in(w, lay, cw, ql, kvl):
    depth, d, n = w.shape
    w_t = jnp.swapaxes(w, 1, 2)
    o_mla = 3 * cw
    o_gqa = o_mla + ql + kvl + MLA_ROPE
    gqw, gkw = GQA_HEADS * GQA_HEAD_DIM, GQA_KV_HEADS * GQA_HEAD_DIM
    o_gate = o_gqa + gqw + 2 * gkw
    moves = ((0, 0, o_mla), (lay["gq"], o_gqa, gqw), (lay["cq"], o_mla, ql + kvl),
             (lay["gk"], o_gqa + gqw, 2 * gkw), (lay["kr"], o_mla + ql + kvl, MLA_ROPE),
             (lay["gate"], o_gate, n - o_gate))
    cb = _pick(256, d)
    return pl.pallas_call(
        functools.partial(_pack_kernel, moves=moves, zero=(lay["kr"] + MLA_ROPE, lay["gate"])),
        grid=(depth, d // cb),
        in_specs=[pl.BlockSpec((None, n, cb), lambda l, j: (l, 0, j))],
        out_specs=pl.BlockSpec((None, lay["total"], cb), lambda l, j: (l, 0, j)),
        out_shape=jax.ShapeDtypeStruct((depth, lay["total"], d), BF16),
        compiler_params=_params(("parallel", "parallel")),
        name="pack_w_in",
    )(w_t)


def kernel(x, c, ctx, c_ctx, ada_w, ada_b, ffn1_norm, ffn1_w_gu, ffn1_w_down, mix_norm, w_in, conv_w,
           mla_q_norm, mla_w_qb, mla_kv_norm, mla_w_kvb, gqa_sink, w_branch_conv, w_branch_mla,
           w_branch_gqa, w_out, ffn2_norm, ffn2_w_gu, ffn2_w_down, final_norm):
    b, s, d = x.shape
    cl = ctx.shape[1]
    depth = ada_w.shape[0]
    cw = conv_w.shape[-1]
    ql, kvl = mla_q_norm.shape[-1], mla_kv_norm.shape[-1]
    mx, mc = b * s, b * cl
    assert b + 1 <= 8 and s % cl == 0 and cl % ATT_BLOCK == 0

    cvec = jnp.zeros((8, d), F32).at[:b].set(c).at[b].set(c_ctx)
    mods = _mods(cvec, ada_w, ada_b).reshape(depth, 8, N_MOD, d)

    lay = _layout(cw, ql, kvl, d)
    tabs = _rope_table(s, mc, MLA_ROPE // 2) + _rope_table(s, mc, GQA_HEAD_DIM // 2)
    h = None
    bf = _pick(512, ffn1_w_down.shape[1])
    w1_gu, w1_dn = _gu_to_bf16(ffn1_w_gu, bf), ffn1_w_down
    w2_gu, w2_dn = _gu_to_bf16(ffn2_w_gu, bf), ffn2_w_down
    w_bc, w_bm, w_bg = _to_bf16(w_branch_conv), _to_bf16(w_branch_mla), _to_bf16(w_branch_gqa)
    w_o = _to_bf16(w_out)
    w_p = _pack_w_in(w_in, lay, cw, ql, kvl)

    for l in range(depth):
        with_ctx = l < depth - 1
        rows = mx + mc if with_ctx else mx
        wqb_p = jnp.pad(mla_w_qb[l].reshape(ql, MLA_HEADS, MLA_NOPE + MLA_ROPE),
                        ((0, 0), (0, 0), (0, MLA_QK - MLA_NOPE - MLA_ROPE))).reshape(ql, -1).astype(BF16)
        wkv = mla_w_kvb[l].reshape(kvl, MLA_HEADS, MLA_NOPE + MLA_V)
        wkn_t = wkv[:, :, :MLA_NOPE].reshape(kvl, -1).T.astype(BF16)
        wv = wkv[:, :, MLA_NOPE:].reshape(kvl, -1).astype(BF16)
        sink = jnp.broadcast_to((gqa_sink[l].astype(F32) * LOG2E)[:, None], (GQA_HEADS, LANE))

        if l == 0:
            h = _ffn(x.reshape(mx, d), ctx.reshape(mc, d), mx + mc, mods[l], ffn1_norm[l], w1_gu, w1_dn, l, 0,
                     s, b, final_norm, False)
        else:
            h = _ffn(h, None, mx + mc, mods[l], ffn1_norm[l], w1_gu, w1_dn, l, 0, s, b, final_norm, False)
        px = _proj(h, mods[l], mix_norm[l], w_p, l, s, b)
        q, kt, v, gq, gk = _prep(px, lay, mla_q_norm[l], mla_kv_norm[l], wqb_p, wkn_t, wv, tabs, s, mx)
        y_mla = _mla_attn(q, kt, v, b, s, cl, with_ctx)
        y_gqa = _gqa_attn(gq, gk, px, lay["gv"], sink, b, s, cl, with_ctx)
        h = _mix_out(y_mla, y_gqa, px, conv_w[l], lay["gate"], w_bc, w_bm, w_bg, w_o, l, h, mods[l], rows, s, cl,
                     mx, b)
        h = _ffn(h, None, rows, mods[l], ffn2_norm[l], w2_gu, w2_dn, l, 6, s, b, final_norm, not with_ctx)
    return h.reshape(b, s, d)
```

```python
import functools

import jax
import jax.numpy as jnp
from jax import lax
from jax.experimental import pallas as pl
from jax.experimental.pallas import tpu as pltpu

F32 = jnp.float32
BF16 = jnp.bfloat16

GRID_W = 64
N_MOD = 9
EPS = 1e-6
ROPE_BASE = 10000.0
CONV_K = 3
MLA_HEADS = 8
MLA_NOPE = 128
MLA_ROPE = 64
MLA_V = 128
MLA_SCALE = (MLA_NOPE + MLA_ROPE) ** -0.5
GQA_HEADS = 8
GQA_KV_HEADS = 2
GQA_GROUP = GQA_HEADS // GQA_KV_HEADS
GQA_HEAD_DIM = 128
GQA_SCALE = GQA_HEAD_DIM ** -0.5
WINDOW = 128
ATT_BLOCK = 128

LANE = 128
MLA_QK = 2 * LANE
MLA_SUB = 256
LOG2E = 1.4426950408889634
VMEM_LIMIT = 56 << 20
NT_DIMS = (((1,), (1,)), ((), ()))


def _pick(target, *sizes):
    b = target
    while any(s % b for s in sizes):
        b //= 2
        assert b >= 8, (target, sizes)
    return b


def _params(sem):
    return pltpu.CompilerParams(dimension_semantics=sem, vmem_limit_bytes=VMEM_LIMIT)


def _dot(a, b):
    return jnp.dot(a, b, preferred_element_type=F32)


def _dot_nt(a, b):
    return lax.dot_general(a, b, NT_DIMS, preferred_element_type=F32)


def _rms(x, g):
    return x * lax.rsqrt(jnp.mean(x * x, axis=-1, keepdims=True) + EPS) * g


CAST_BLOCK_BYTES = 6 << 20


def _cast_kernel(w_ref, o_ref):
    o_ref[...] = w_ref[...].astype(BF16)


def _to_bf16(w):
    depth, k, n = w.shape
    rows = depth * k
    rb = _pick(max(16, 1 << ((CAST_BLOCK_BYTES // (4 * n)).bit_length() - 1)), rows)
    out = pl.pallas_call(
        _cast_kernel,
        grid=(rows // rb,),
        in_specs=[pl.BlockSpec((rb, n), lambda i: (i, 0))],
        out_specs=pl.BlockSpec((rb, n), lambda i: (i, 0)),
        out_shape=jax.ShapeDtypeStruct((rows, n), BF16),
        compiler_params=_params(("parallel",)),
        name="cast_bf16",
    )(w.reshape(rows, n))
    return out.reshape(depth, k, n)


def _cast_gu_kernel(w_ref, o_ref, *, bf):
    nf = o_ref.shape[0]
    for j in range(nf):
        o_ref[j, :, :bf] = w_ref[:, j * bf:(j + 1) * bf].astype(BF16)
        o_ref[j, :, bf:] = w_ref[:, (nf + j) * bf:(nf + j + 1) * bf].astype(BF16)


def _gu_to_bf16(w, bf):
    depth, k, n = w.shape
    nf = n // (2 * bf)
    rb = _pick(max(16, 1 << ((CAST_BLOCK_BYTES // (4 * n)).bit_length() - 1)), k)
    kb = k // rb
    return pl.pallas_call(
        functools.partial(_cast_gu_kernel, bf=bf),
        grid=(depth, kb),
        in_specs=[pl.BlockSpec((None, rb, n), lambda l, i: (l, i, 0))],
        out_specs=pl.BlockSpec((None, nf, rb, 2 * bf), lambda l, i: (l, 0, i, 0)),
        out_shape=jax.ShapeDtypeStruct((depth, nf, k, 2 * bf), BF16),
        compiler_params=_params(("parallel", "parallel")),
        name="cast_gate_up",
    )(w)


def _mods_kernel(c_ref, w_ref, b_ref, o_ref):
    c = c_ref[...]
    s = (c * jax.nn.sigmoid(c)).astype(BF16)
    o_ref[0] = _dot(s, w_ref[0].astype(BF16)) + b_ref[0]


def _mods(cvec, ada_w, ada_b):
    depth, d, n = ada_w.shape
    bn = _pick(1024, n)
    return pl.pallas_call(
        _mods_kernel,
        grid=(depth, n // bn),
        in_specs=[pl.BlockSpec((8, d), lambda l, j: (0, 0)),
                  pl.BlockSpec((1, d, bn), lambda l, j: (l, 0, j)),
                  pl.BlockSpec((1, 1, bn), lambda l, j: (l, 0, j))],
        out_specs=pl.BlockSpec((1, 8, bn), lambda l, j: (l, 0, j)),
        out_shape=jax.ShapeDtypeStruct((depth, 8, n), F32),
        compiler_params=_params(("parallel", "parallel")),
        name="adaln_mods",
    )(cvec, ada_w, ada_b.reshape(depth, 1, n))


def _norm_mod_rows(h_ref, mod_ref, g_ref, xn_ref, i_shift, i_scale, rc):
    shift = mod_ref[0, i_shift:i_shift + 1, :]
    scale1 = 1.0 + mod_ref[0, i_scale:i_scale + 1, :]
    g = g_ref[...]

    def body(r, carry):
        rows = pl.ds(pl.multiple_of(r * rc, rc), rc)
        xn_ref[rows, :] = (_rms(h_ref[rows, :], g) * scale1 + shift).astype(BF16)
        return carry

    lax.fori_loop(0, h_ref.shape[0] // rc, body, 0)


def _norm_mod_chunk(hn_ref, modn_ref, g_ref, xn_ref, slot, k, ch, i_shift, i_scale):
    r0 = pl.multiple_of(jnp.minimum(k * ch, hn_ref.shape[0] - ch), 16)
    shift = modn_ref[0, i_shift:i_shift + 1, :]
    scale1 = 1.0 + modn_ref[0, i_scale:i_scale + 1, :]
    x = hn_ref[pl.ds(r0, ch), :]
    xn_ref[slot, pl.ds(r0, ch), :] = (_rms(x, g_ref[...]) * scale1 + shift).astype(BF16)


def _chunk_rows(bm, steps):
    return min(bm, -(-bm // (16 * steps)) * 16)


def _mod_spec(d, bm, s, b, ahead=0, last=None):
    per = s // bm

    def idx(i, j):
        t = i if not ahead else jnp.minimum(i + ahead, last)
        return jnp.minimum(t // per, b), 0, 0

    return pl.BlockSpec((1, N_MOD, d), idx)


FFN_UP_ROWS = 1024
FFN_DOWN_ROWS = 256
WEIGHT_CHUNK_ROWS = 512
UP_CHUNKS = 4

def _swiglu(gu):
    bf = gu.shape[1] // 2
    gg, uu = gu[:, :bf], gu[:, bf:]
    return (gg * jax.nn.sigmoid(gg) * uu).astype(BF16)


def _up_split_kernel(h_ref, hc_ref, mod_ref, g_ref, w_ref, o_ref, xn_ref, *, i0, rc, n_lat):
    first = pl.program_id(1) == 0
    is_lat = pl.program_id(0) < n_lat

    @pl.when(first & is_lat)
    def _():
        _norm_mod_rows(h_ref, mod_ref, g_ref, xn_ref, i0, i0 + 1, rc)

    @pl.when(first & jnp.logical_not(is_lat))
    def _():
        _norm_mod_rows(hc_ref, mod_ref, g_ref, xn_ref, i0, i0 + 1, rc)

    o_ref[...] = _swiglu(_dot(xn_ref[...], w_ref[...]))


def _up_kernel(h_ref, mod_ref, modn_ref, g_ref, w_ref, o_ref, xn_ref, *, i0, rc, ch):
    i, j = pl.program_id(0), pl.program_id(1)
    slot = i % 2

    @pl.when((j == 0) & (i == 0))
    def _():
        _norm_mod_rows(h_ref, mod_ref, g_ref, xn_ref.at[0], i0, i0 + 1, rc)

    rows = o_ref.shape[0] // UP_CHUNKS
    for r in range(UP_CHUNKS):
        o_ref[r * rows:(r + 1) * rows, :] = _swiglu(_dot(xn_ref[slot, r * rows:(r + 1) * rows, :], w_ref[...]))
    _norm_mod_chunk(h_ref, modn_ref, g_ref, xn_ref, 1 - slot, jnp.maximum(j - 1, 0), ch, i0, i0 + 1)


def _down_kernel(a_ref, w_ref, h_ref, hc_ref, mod_ref, fg_ref, o_ref, wbf_ref, *, i_gate, n_lat, final, nw):
    t = pl.program_id(0)

    @pl.when(t < nw)
    def _():
        ck = w_ref.shape[0]
        wbf_ref[pl.ds(pl.multiple_of(t * ck, ck), ck), :] = w_ref[...].astype(BF16)

    @pl.when(t >= nw)
    def _():
        h = h_ref[...]
        if n_lat is not None:
            h = jnp.where(t - nw < n_lat, h, hc_ref[...])
        out = h + (0.5 * mod_ref[0, i_gate:i_gate + 1, :]) * _dot(a_ref[...], wbf_ref[...])
        o_ref[...] = _rms(out, fg_ref[...]) if final else out


def _ffn(h, hc, rows, mods, norm_g, w_gu, w_down, l, i0, s, b, final_g, final):
    d = h.shape[1]
    nf, bf = w_gu.shape[1], w_gu.shape[3] // 2
    f = nf * bf
    vec = pl.BlockSpec((1, d), lambda i, j: (0, 0))
    w_spec = pl.BlockSpec((None, None, d, 2 * bf), lambda i, j: (l, j, 0, 0))
    bm = _pick(FFN_UP_ROWS, s, rows) if hc is None else _pick(FFN_UP_ROWS, s, hc.shape[0])
    nt = rows // bm
    up_common = dict(
        grid=(nt, nf),
        out_specs=pl.BlockSpec((bm, bf), lambda i, j: (i, j)),
        out_shape=jax.ShapeDtypeStruct((rows, f), BF16),
        compiler_params=_params(("arbitrary", "arbitrary")),
        name="ffn_up")
    if hc is not None:
        n_lat = h.shape[0] // bm
        act = pl.pallas_call(
            functools.partial(_up_split_kernel, i0=i0, rc=_pick(128, bm), n_lat=n_lat),
            in_specs=[pl.BlockSpec((bm, d), lambda i, j: (jnp.minimum(i, n_lat - 1), 0)),
                      pl.BlockSpec((bm, d), lambda i, j: (jnp.maximum(i - n_lat, 0), 0)),
                      _mod_spec(d, bm, s, b), vec, w_spec],
            scratch_shapes=[pltpu.VMEM((bm, d), BF16)],
            **up_common,
        )(h, hc, mods, norm_g.reshape(1, d), w_gu)
    else:
        assert nf >= 2
        h_idx = lambda i, j: (jnp.where((i == 0) & (j == 0), 0, jnp.minimum(i + 1, nt - 1)), 0)
        act = pl.pallas_call(
            functools.partial(_up_kernel, i0=i0, rc=_pick(128, bm), ch=_chunk_rows(bm, nf - 1)),
            in_specs=[pl.BlockSpec((bm, d), h_idx), _mod_spec(d, bm, s, b), _mod_spec(d, bm, s, b, 1, nt - 1),
                      vec, w_spec],
            scratch_shapes=[pltpu.VMEM((2, bm, d), BF16)],
            **up_common,
        )(h, mods, mods, norm_g.reshape(1, d), w_gu)

    bd = _pick(FFN_DOWN_ROWS, s, rows) if hc is None else _pick(FFN_DOWN_ROWS, s, hc.shape[0])
    per = s // bd
    ck = _pick(WEIGHT_CHUNK_ROWS, f)
    nw = f // ck
    row = lambda t: jnp.maximum(t - nw, 0)
    if hc is None:
        n_lat = None
        h_specs = [pl.BlockSpec((bd, d), lambda t: (row(t), 0)), pl.BlockSpec((1, d), lambda t: (0, 0))]
        hc = norm_g.reshape(1, d)
    else:
        n_lat = h.shape[0] // bd
        h_specs = [pl.BlockSpec((bd, d), lambda t: (jnp.minimum(row(t), n_lat - 1), 0)),
                   pl.BlockSpec((bd, d), lambda t: (jnp.maximum(row(t) - n_lat, 0), 0))]
    return pl.pallas_call(
        functools.partial(_down_kernel, i_gate=i0 + 2, n_lat=n_lat, final=final, nw=nw),
        grid=(nw + rows // bd,),
        in_specs=[pl.BlockSpec((bd, f), lambda t: (row(t), 0)),
                  pl.BlockSpec((None, ck, d), lambda t: (l, jnp.minimum(t, nw - 1), 0))] + h_specs + [
                  pl.BlockSpec((1, N_MOD, d), lambda t: (jnp.minimum(row(t) // per, b), 0, 0)),
                  pl.BlockSpec((1, d), lambda t: (0, 0))],
        out_specs=pl.BlockSpec((bd, d), lambda t: (row(t), 0)),
        out_shape=jax.ShapeDtypeStruct((rows, d), F32),
        scratch_shapes=[pltpu.VMEM((f, d), BF16)],
        compiler_params=_params(("arbitrary",)),
        name="ffn_down",
    )(act, w_down, h, hc, mods, final_g.reshape(1, d))


def _first_norm_kernel(h_ref, mod_ref, g_ref, o_ref, *, rc):
    _norm_mod_rows(h_ref, mod_ref, g_ref, o_ref, 3, 4, rc)


def _proj_kernel(x0_ref, h_ref, modn_ref, g_ref, w_ref, o_ref, xn_ref, *, nch):
    i, j = pl.program_id(0), pl.program_id(1)
    slot = i % 2

    @pl.when((j == 0) & (i == 0))
    def _():
        xn_ref[0] = x0_ref[...]

    o_ref[...] = _dot_nt(xn_ref[slot], w_ref[...]).astype(BF16)
    ch = h_ref.shape[0]
    r0 = pl.multiple_of(jnp.clip(j - 1, 0, nch - 1) * ch, ch)
    x = _rms(h_ref[...], g_ref[...]) * (1.0 + modn_ref[0, 4:5, :]) + modn_ref[0, 3:4, :]
    xn_ref[1 - slot, pl.ds(r0, ch), :] = x.astype(BF16)


def _proj(h, mods, norm_g, w_p, l, s, b):
    rows, d = h.shape
    n = w_p.shape[1]
    bm = _pick(1024, s, rows)
    bn = _pick(2048, n)
    nt, nj = rows // bm, n // bn
    assert nj >= 2
    nch = max(k for k in (1, 2, 4, 8, 16) if k <= nj - 1 and bm % (16 * k) == 0)
    ch = bm // nch
    vec = pl.BlockSpec((1, d), lambda i, j: (0, 0))
    x0 = pl.pallas_call(
        functools.partial(_first_norm_kernel, rc=_pick(128, bm)),
        grid=(1, 1),
        in_specs=[pl.BlockSpec((bm, d), lambda i, j: (0, 0)), _mod_spec(d, bm, s, b), vec],
        out_specs=pl.BlockSpec((bm, d), lambda i, j: (0, 0)),
        out_shape=jax.ShapeDtypeStruct((bm, d), BF16),
        compiler_params=_params(("arbitrary", "arbitrary")),
        name="in_proj_first_norm",
    )(h, mods, norm_g.reshape(1, d))
    h_idx = lambda i, j: (jnp.minimum(i + 1, nt - 1) * nch + jnp.clip(j - 1, 0, nch - 1), 0)
    return pl.pallas_call(
        functools.partial(_proj_kernel, nch=nch),
        grid=(nt, nj),
        in_specs=[pl.BlockSpec((bm, d), lambda i, j: (0, 0), pipeline_mode=pl.Buffered(1)),
                  pl.BlockSpec((ch, d), h_idx), _mod_spec(d, bm, s, b, 1, nt - 1), vec,
                  pl.BlockSpec((None, bn, d), lambda i, j: (l, j, 0))],
        out_specs=pl.BlockSpec((bm, bn), lambda i, j: (i, j)),
        out_shape=jax.ShapeDtypeStruct((rows, n), BF16),
        scratch_shapes=[pltpu.VMEM((2, bm, d), BF16)],
        compiler_params=_params(("arbitrary", "arbitrary")),
        name="in_proj",
    )(x0, h, mods, norm_g.reshape(1, d), w_p)


def _conv_tile(gb_ref, gc_ref, v_ref, gcp_ref, vp_ref, gcn_ref, vn_ref, w_ref, *, s, c, mx):
    bm = gb_ref.shape[0]
    row0 = pl.program_id(0) * bm
    is_lat = row0 < mx
    at_start = jnp.where(is_lat, row0 % s == 0, (row0 - mx) % c == 0)
    at_end = jnp.where(is_lat, (row0 + bm) % s == 0, (row0 - mx + bm) % c == 0)
    cv = gc_ref[...].astype(F32) * v_ref[...].astype(F32)
    hp = (gcp_ref[...].astype(F32) * vp_ref[...].astype(F32))[15:16, :]
    hn = (gcn_ref[...].astype(F32) * vn_ref[...].astype(F32))[0:1, :]
    hp = jnp.where(at_start, 0.0, hp)
    hn = jnp.where(at_end, 0.0, hn)
    rid = lax.broadcasted_iota(jnp.int32, (bm, 1), 0)
    prev = jnp.where(rid == 0, hp, pltpu.roll(cv, 1, 0))
    nxt = jnp.where(rid == bm - 1, hn, pltpu.roll(cv, bm - 1, 0))
    w = w_ref[...]
    y = gb_ref[...].astype(F32) * (prev * w[0:1, :] + cv * w[1:2, :] + nxt * w[2:3, :])
    return y.astype(BF16)


def _conv_specs(px, cw, bm):
    m = px.shape[0]
    hb = bm // 16
    main = lambda col: pl.BlockSpec((bm, cw), lambda i: (i, col))
    prev = lambda col: pl.BlockSpec((16, cw), lambda i: (jnp.maximum(i * hb - 1, 0), col))
    nxt = lambda col: pl.BlockSpec((16, cw), lambda i: (jnp.minimum((i + 1) * hb, m // 16 - 1), col))
    return [main(0), main(1), main(2), prev(1), prev(2), nxt(1), nxt(2), pl.BlockSpec((CONV_K, cw), lambda i: (0, 0))]


def _rope(x, cos, sin, half):
    lane = lax.broadcasted_iota(jnp.int32, x.shape, 1)
    first = (lane % (2 * half)) < half
    rot = jnp.where(first, pltpu.roll(x, LANE - half, 1), pltpu.roll(x, half, 1))
    return x * cos + rot * sin


def _prep_kernel(gq_ref, cq_ref, ckv_ref, gk_ref, kr_ref, qn_ref, kvn_ref, wqb_ref, wknt_ref, wv_ref,
                 mcos_ref, msin_ref, gcos_ref, gsin_ref, q_ref, kt_ref, v_ref, gqo_ref, gko_ref):
    mcos, msin = mcos_ref[...], msin_ref[...]
    gcos, gsin = gcos_ref[...], gsin_ref[...]
    mh = MLA_ROPE // 4
    gh = GQA_HEAD_DIM // 4

    cqn = _rms(cq_ref[...].astype(F32), qn_ref[...]).astype(BF16)
    q = _dot(cqn, wqb_ref[...]) * (MLA_SCALE * LOG2E)
    ckvn = _rms(ckv_ref[...].astype(F32), kvn_ref[...]).astype(BF16)
    knt = _dot_nt(wknt_ref[...], ckvn)
    vv = _dot(ckvn, wv_ref[...])
    krt = _rope(kr_ref[...].astype(F32), mcos, msin, mh).T.astype(BF16)
    ones = jnp.ones((q.shape[0], LANE), BF16)
    for h in range(MLA_HEADS):
        a = h * MLA_QK
        q_ref[:, a:a + LANE] = q[:, a:a + LANE].astype(BF16)
        q_ref[:, a + LANE:a + MLA_QK] = _rope(q[:, a + LANE:a + MLA_QK], mcos, msin, mh).astype(BF16)
        kt_ref[a:a + LANE, :] = knt[h * LANE:(h + 1) * LANE, :].astype(BF16)
        kt_ref[a + LANE:a + MLA_QK, :] = krt
        v_ref[:, a:a + LANE] = vv[:, h * LANE:(h + 1) * LANE].astype(BF16)
        v_ref[:, a + LANE:a + MLA_QK] = ones

    gq = gq_ref[...].astype(F32)
    for h in range(GQA_HEADS):
        a = h * GQA_HEAD_DIM
        gqo_ref[:, a:a + LANE] = (_rope(gq[:, a:a + LANE], gcos, gsin, gh) * (GQA_SCALE * LOG2E)).astype(BF16)
    gk = gk_ref[...].astype(F32)
    for h in range(GQA_KV_HEADS):
        a = h * GQA_HEAD_DIM
        gko_ref[:, a:a + LANE] = _rope(gk[:, a:a + LANE], gcos, gsin, gh).astype(BF16)


def _prep(px, lay, q_norm, kv_norm, wqb_p, wkn_t, wv, tabs, s, mx):
    m = px.shape[0]
    bm = _pick(512, s, m - mx)
    n_lat = mx // bm
    per = s // bm
    ql, kvl = q_norm.shape[0], kv_norm.shape[0]
    gqw, gkw = GQA_HEADS * GQA_HEAD_DIM, GQA_KV_HEADS * GQA_HEAD_DIM

    def col(width, off):
        assert off % width == 0, (width, off)
        return pl.BlockSpec((bm, width), lambda i: (i, off // width))

    const = lambda r, c: pl.BlockSpec((r, c), lambda i: (0, 0))
    tab = pl.BlockSpec((bm, LANE), lambda i: (jnp.where(i < n_lat, i % per, per + i - n_lat), 0))
    row = lambda width: pl.BlockSpec((bm, width), lambda i: (i, 0))
    hq = MLA_HEADS * MLA_QK
    return pl.pallas_call(
        _prep_kernel,
        grid=(m // bm,),
        in_specs=[col(gqw, lay["gq"]), col(ql, lay["cq"]), col(kvl, lay["ckv"]), col(gkw, lay["gk"]),
                  col(LANE, lay["kr"]), const(1, ql), const(1, kvl), const(ql, hq),
                  const(MLA_HEADS * MLA_NOPE, kvl), const(kvl, MLA_HEADS * MLA_V), tab, tab, tab, tab],
        out_specs=[row(hq), pl.BlockSpec((hq, bm), lambda i: (0, i)), row(hq), row(gqw), row(gkw)],
        out_shape=[jax.ShapeDtypeStruct((m, hq), BF16), jax.ShapeDtypeStruct((hq, m), BF16),
                   jax.ShapeDtypeStruct((m, hq), BF16),
                   jax.ShapeDtypeStruct((m, gqw), BF16), jax.ShapeDtypeStruct((m, gkw), BF16)],
        compiler_params=_params(("parallel",)),
        name="attn_prep",
    )(px, px, px, px, px, q_norm.reshape(1, ql), kv_norm.reshape(1, kvl), wqb_p, wkn_t, wv, *tabs)


def _mla_kernel(q_ref, kl_ref, kc_ref, vl_ref, vc_ref, o_ref, *, n_sub):
    kl, kc = kl_ref[...], kc_ref[...]
    subs = [q_ref[i * MLA_SUB:(i + 1) * MLA_SUB, :] for i in range(n_sub)]
    scores = [(_dot(q, kl), _dot(q, kc)) for q in subs]
    for i, (s1, s2) in enumerate(scores):
        m = jnp.maximum(jnp.max(s1, axis=-1, keepdims=True), jnp.max(s2, axis=-1, keepdims=True))
        p1 = jnp.exp2((s1 - m).astype(BF16))
        p2 = jnp.exp2((s2 - m).astype(BF16))
        o = _dot(p1, vl_ref[...]) + _dot(p2, vc_ref[...])
        o_ref[i * MLA_SUB:(i + 1) * MLA_SUB, :] = (o[:, :MLA_V] / o[:, MLA_V:]).astype(BF16)


def _mla_ctx_kernel(q_ref, kc_ref, vc_ref, y_ref, o_ref):
    del y_ref
    s2 = _dot(q_ref[...], kc_ref[...])
    p2 = jnp.exp2(s2 - jnp.max(s2, axis=-1, keepdims=True)).astype(BF16)
    o = _dot(p2, vc_ref[...])
    o_ref[...] = (o[:, :MLA_V] / o[:, MLA_V:]).astype(BF16)


def _mla_attn(q, kt, v, b, s, c, with_ctx):
    mx = b * s
    n_sub = max(d for d in (1, 2, 4, 8) if (s // MLA_SUB) % d == 0)
    bq = n_sub * MLA_SUB
    nq = s // bq
    rows = mx + (b * c if with_ctx else 0)
    ctx0 = mx // c
    y = pl.pallas_call(
        functools.partial(_mla_kernel, n_sub=n_sub),
        grid=(b, MLA_HEADS, nq),
        in_specs=[pl.BlockSpec((bq, MLA_QK), lambda bi, h, qi: (bi * nq + qi, h)),
                  pl.BlockSpec((MLA_QK, s), lambda bi, h, qi: (h, bi)),
                  pl.BlockSpec((MLA_QK, c), lambda bi, h, qi: (h, ctx0 + bi)),
                  pl.BlockSpec((s, MLA_QK), lambda bi, h, qi: (bi, h)),
                  pl.BlockSpec((c, MLA_QK), lambda bi, h, qi: (ctx0 + bi, h))],
        out_specs=pl.BlockSpec((bq, MLA_V), lambda bi, h, qi: (bi * nq + qi, h)),
        out_shape=jax.ShapeDtypeStruct((rows, MLA_HEADS * MLA_V), BF16),
        compiler_params=_params(("parallel", "parallel", "arbitrary")),
        name="mla_attn",
    )(q, kt, kt, v, v)
    if not with_ctx:
        return y
    return pl.pallas_call(
        _mla_ctx_kernel,
        grid=(b, MLA_HEADS),
        in_specs=[pl.BlockSpec((c, MLA_QK), lambda bi, h: (ctx0 + bi, h)),
                  pl.BlockSpec((MLA_QK, c), lambda bi, h: (h, ctx0 + bi)),
                  pl.BlockSpec((c, MLA_QK), lambda bi, h: (ctx0 + bi, h)),
                  pl.BlockSpec(memory_space=pl.ANY)],
        out_specs=pl.BlockSpec((c, MLA_V), lambda bi, h: (ctx0 + bi, h)),
        out_shape=jax.ShapeDtypeStruct(y.shape, BF16),
        input_output_aliases={3: 0},
        compiler_params=_params(("parallel", "parallel")),
        name="mla_attn_ctx",
    )(q, kt, v, y)


GQA_QB_MAX = 16


def _sink_rows(sink_ref, g, j, rows):
    return jnp.broadcast_to(sink_ref[pl.ds(g * GQA_GROUP + j, 1), :], (rows, LANE))[:, 0:1]


def _gqa_kernel(q_ref, kp_ref, kc_ref, kn_ref, kx_ref, vp_ref, vc_ref, vn_ref, vx_ref, sink_ref, o_ref,
                *, n_lat, qb):
    g = pl.program_id(1)
    n = pl.program_id(2)
    blk = ATT_BLOCK
    kx, vx = kx_ref[...], vx_ref[...]
    kband = jnp.concatenate([kp_ref[...], kc_ref[...], kn_ref[...]], axis=0)
    vband = jnp.concatenate([vp_ref[...], vc_ref[...], vn_ref[...]], axis=0)
    nk = 3 * blk + kx.shape[0]
    ones = jnp.ones((nk, LANE), BF16)
    snk = jnp.concatenate([jnp.broadcast_to(sink_ref[pl.ds(g * GQA_GROUP + j, 1), :], (blk, LANE))
                           for j in range(GQA_GROUP)], axis=0)
    rows = GQA_GROUP * blk
    r = lax.broadcasted_iota(jnp.int32, (rows, nk), 0) % blk
    col = lax.broadcasted_iota(jnp.int32, (rows, nk), 1)
    in_window = (jnp.abs(col - blk - r) <= WINDOW) | (col >= 3 * blk)
    for i in range(qb):
        qs = jnp.concatenate([q_ref[i * blk:(i + 1) * blk, j * LANE:(j + 1) * LANE] for j in range(GQA_GROUP)],
                             axis=0)
        keys = jnp.concatenate([kband[i * blk:(i + 3) * blk, :], kx], axis=0)
        vals = jnp.concatenate([jnp.concatenate([vband[i * blk:(i + 3) * blk, :], vx], axis=0), ones], axis=1)
        block = n * qb + i
        valid = in_window & ((col >= blk) | (block > 0)) & ((col < 2 * blk) | (col >= 3 * blk) | (block < n_lat - 1))
        sc = jnp.where(valid, _dot_nt(qs, keys), -jnp.inf)
        e = snk
        for t in range(nk // LANE):
            e = jnp.maximum(e, sc[:, t * LANE:(t + 1) * LANE])
        m = jnp.max(e, axis=-1, keepdims=True)
        ol = _dot(jnp.exp2((sc - m).astype(BF16)), vals)
        o = ol[:, :LANE] / (ol[:, LANE:] + jnp.exp2(snk - m))
        for j in range(GQA_GROUP):
            o_ref[i * blk:(i + 1) * blk, j * LANE:(j + 1) * LANE] = o[j * blk:(j + 1) * blk, :].astype(BF16)


def _gqa_ctx_kernel(q_ref, kx_ref, vx_ref, sink_ref, y_ref, o_ref):
    del y_ref
    g = pl.program_id(1)
    kx, vx = kx_ref[...], vx_ref[...]
    for j in range(GQA_GROUP):
        sc = _dot_nt(q_ref[:, j * LANE:(j + 1) * LANE], kx)
        snk = _sink_rows(sink_ref, g, j, sc.shape[0])
        m = jnp.maximum(jnp.max(sc, axis=-1, keepdims=True), snk)
        p_c = jnp.exp2(sc - m)
        l = jnp.sum(p_c, axis=-1, keepdims=True) + jnp.exp2(snk - m)
        o_ref[:, j * LANE:(j + 1) * LANE] = (_dot(p_c.astype(BF16), vx) / l).astype(BF16)


def _gqa_attn(gq, gk, px, o_gv, sink, b, s, c, with_ctx):
    mx = b * s
    blk = ATT_BLOCK
    n_lat = s // blk
    qb = max(k for k in (1, 2, 4, 8, 16) if k <= GQA_QB_MAX and n_lat % k == 0)
    nstep = n_lat // qb
    rows = mx + (b * c if with_ctx else 0)
    ctx0 = mx // c
    gvc = o_gv // LANE
    gw = GQA_GROUP * GQA_HEAD_DIM
    big = qb * blk

    def edge(shift, col0):
        def idx(bi, g, n):
            return bi * n_lat + jnp.clip(n * qb + shift, 0, n_lat - 1), col0 + g
        return pl.BlockSpec((blk, LANE), idx)

    main = lambda col0: pl.BlockSpec((big, LANE), lambda bi, g, n: (bi * nstep + n, col0 + g))
    ctx = lambda col0: pl.BlockSpec((c, LANE), lambda bi, g, n: (ctx0 + bi, col0 + g))
    sink_spec = pl.BlockSpec((GQA_HEADS, LANE), lambda *_: (0, 0))
    y = pl.pallas_call(
        functools.partial(_gqa_kernel, n_lat=n_lat, qb=qb),
        grid=(b, GQA_KV_HEADS, nstep),
        in_specs=[pl.BlockSpec((big, gw), lambda bi, g, n: (bi * nstep + n, g)),
                  edge(-1, 0), main(0), edge(qb, 0), ctx(0),
                  edge(-1, gvc), main(gvc), edge(qb, gvc), ctx(gvc), sink_spec],
        out_specs=pl.BlockSpec((big, gw), lambda bi, g, n: (bi * nstep + n, g)),
        out_shape=jax.ShapeDtypeStruct((rows, GQA_HEADS * GQA_HEAD_DIM), BF16),
        compiler_params=_params(("parallel", "parallel", "arbitrary")),
        name="gqa_attn",
    )(gq, gk, gk, gk, gk, px, px, px, px, sink)
    if not with_ctx:
        return y
    return pl.pallas_call(
        _gqa_ctx_kernel,
        grid=(b, GQA_KV_HEADS),
        in_specs=[pl.BlockSpec((c, gw), lambda bi, g: (ctx0 + bi, g)),
                  pl.BlockSpec((c, LANE), lambda bi, g: (ctx0 + bi, g)),
                  pl.BlockSpec((c, LANE), lambda bi, g: (ctx0 + bi, gvc + g)),
                  sink_spec, pl.BlockSpec(memory_space=pl.ANY)],
        out_specs=pl.BlockSpec((c, gw), lambda bi, g: (ctx0 + bi, g)),
        out_shape=jax.ShapeDtypeStruct(y.shape, BF16),
        input_output_aliases={4: 0},
        compiler_params=_params(("parallel", "parallel")),
        name="gqa_attn_ctx",
    )(gq, gk, px, sink, y)


MIX_ROWS = 256


def _mix_out_kernel(gb_ref, gcv_ref, v_ref, gcp_ref, vp_ref, gcn_ref, vn_ref, cw_ref, ym_ref, yg_ref,
                    gc_ref, gm_ref, gg_ref, wc_ref, wm_ref, wg_ref, wo_ref, h_ref, mod_ref, o_ref, *, s, c, mx):
    yc = _conv_tile(gb_ref, gcv_ref, v_ref, gcp_ref, vp_ref, gcn_ref, vn_ref, cw_ref, s=s, c=c, mx=mx)
    gate = lambda ref: jax.nn.sigmoid(ref[...].astype(F32))
    merged = gate(gc_ref) * _dot(yc, wc_ref[...])
    merged += gate(gm_ref) * _dot(ym_ref[...], wm_ref[...])
    merged += gate(gg_ref) * _dot(yg_ref[...], wg_ref[...])
    o_ref[...] = h_ref[...] + mod_ref[0, 5:6, :] * _dot(merged.astype(BF16), wo_ref[...])


def _mix_out(ym, yg, px, conv_w, o_gate, w_bc, w_bm, w_bg, w_out, l, h, mods, rows, s, c, mx, b):
    d = w_out.shape[2]
    bm = _pick(MIX_ROWS, s, c)
    per = s // bm
    assert o_gate % d == 0
    g0 = o_gate // d
    yspec = lambda a: pl.BlockSpec((bm, a.shape[1]), lambda i: (i, 0))
    gspec = lambda k: pl.BlockSpec((bm, d), lambda i: (i, g0 + k))
    wspec = lambda w: pl.BlockSpec((None,) + w.shape[1:], lambda i: (l, 0, 0), pipeline_mode=pl.Buffered(1))
    return pl.pallas_call(
        functools.partial(_mix_out_kernel, s=s, c=c, mx=mx),
        grid=(rows // bm,),
        in_specs=_conv_specs(px, conv_w.shape[1], bm) + [
                  yspec(ym), yspec(yg), gspec(0), gspec(1), gspec(2),
                  wspec(w_bc), wspec(w_bm), wspec(w_bg), wspec(w_out),
                  pl.BlockSpec((bm, d), lambda i: (i, 0)),
                  pl.BlockSpec((1, N_MOD, d), lambda i: (jnp.minimum(i // per, b), 0, 0))],
        out_specs=pl.BlockSpec((bm, d), lambda i: (i, 0)),
        out_shape=jax.ShapeDtypeStruct((rows, d), F32),
        compiler_params=_params(("parallel",)),
        name="mix_out",
    )(px, px, px, px, px, px, px, conv_w, ym, yg, px, px, px, w_bc, w_bm, w_bg, w_out, h, mods)


def _rope_table(s, n_ctx_rows, dim):
    t = jnp.arange(s, dtype=jnp.int32)
    pos = jnp.stack([t // GRID_W, t % GRID_W], axis=1).astype(F32)
    inv = ROPE_BASE ** (-jnp.arange(0, dim, 2, dtype=F32) / dim)
    lane = jnp.arange(LANE)
    ang = pos[:, jnp.minimum(lane // dim, 1)] * inv[lane % (dim // 2)][None, :]
    active = (lane < 2 * dim)[None, :]
    sign = jnp.where((lane % dim) < dim // 2, -1.0, 1.0)[None, :]
    cos = jnp.where(active, jnp.cos(ang), 1.0)
    sin = jnp.where(active, jnp.sin(ang) * sign, 0.0)
    pad = ((0, n_ctx_rows), (0, 0))
    return jnp.pad(cos, pad, constant_values=1.0), jnp.pad(sin, pad)


def _layout(cw, ql, kvl, d):
    lay = {"conv": 0}
    off = 3 * cw
    for name, width in (("gq", GQA_HEADS * GQA_HEAD_DIM), ("cq", ql), ("ckv", kvl),
                        ("gk", GQA_KV_HEADS * GQA_HEAD_DIM), ("gv", GQA_KV_HEADS * GQA_HEAD_DIM),
                        ("kr", LANE)):
        lay[name] = off
        off += width
    q = max(d, 512)
    off = -(-off // q) * q
    lay["gate"] = off
    lay["total"] = off + 3 * d
    return lay


def _pack_kernel(w_ref, o_ref, *, moves, zero):
    for dst, src, width in moves:
        o_ref[dst:dst + width, :] = w_ref[src:src + width, :].astype(BF16)
    o_ref[zero[0]:zero[1], :] = jnp.zeros((zero[1] - zero[0], o_ref.shape[1]), BF16)


def _pack_w_in(w, lay, cw, ql, kvl):
    depth, d, n = w.shape
    w_t = jnp.swapaxes(w, 1, 2)
    o_mla = 3 * cw
    o_gqa = o_mla + ql + kvl + MLA_ROPE
    gqw, gkw = GQA_HEADS * GQA_HEAD_DIM, GQA_KV_HEADS * GQA_HEAD_DIM
    o_gate = o_gqa + gqw + 2 * gkw
    moves = ((0, 0, o_mla), (lay["gq"], o_gqa, gqw), (lay["cq"], o_mla, ql + kvl),
             (lay["gk"], o_gqa + gqw, 2 * gkw), (lay["kr"], o_mla + ql + kvl, MLA_ROPE),
             (lay["gate"], o_gate, n - o_gate))
    cb = _pick(256, d)
    return pl.pallas_call(
        functools.partial(_pack_kernel, moves=moves, zero=(lay["kr"] + MLA_ROPE, lay["gate"])),
        grid=(depth, d // cb),
        in_specs=[pl.BlockSpec((None, n, cb), lambda l, j: (l, 0, j))],
        out_specs=pl.BlockSpec((None, lay["total"], cb), lambda l, j: (l, 0, j)),
        out_shape=jax.ShapeDtypeStruct((depth, lay["total"], d), BF16),
        compiler_params=_params(("parallel", "parallel")),
        name="pack_w_in",
    )(w_t)


def kernel(x, c, ctx, c_ctx, ada_w, ada_b, ffn1_norm, ffn1_w_gu, ffn1_w_down, mix_norm, w_in, conv_w,
           mla_q_norm, mla_w_qb, mla_kv_norm, mla_w_kvb, gqa_sink, w_branch_conv, w_branch_mla,
           w_branch_gqa, w_out, ffn2_norm, ffn2_w_gu, ffn2_w_down, final_norm):
    b, s, d = x.shape
    cl = ctx.shape[1]
    depth = ada_w.shape[0]
    cw = conv_w.shape[-1]
    ql, kvl = mla_q_norm.shape[-1], mla_kv_norm.shape[-1]
    mx, mc = b * s, b * cl
    assert b + 1 <= 8 and s % cl == 0 and cl % ATT_BLOCK == 0

    cvec = jnp.zeros((8, d), F32).at[:b].set(c).at[b].set(c_ctx)
    mods = _mods(cvec, ada_w, ada_b).reshape(depth, 8, N_MOD, d)

    lay = _layout(cw, ql, kvl, d)
    tabs = _rope_table(s, mc, MLA_ROPE // 2) + _rope_table(s, mc, GQA_HEAD_DIM // 2)
    h = None
    bf = _pick(512, ffn1_w_down.shape[1])
    w1_gu, w1_dn = _gu_to_bf16(ffn1_w_gu, bf), ffn1_w_down
    w2_gu, w2_dn = _gu_to_bf16(ffn2_w_gu, bf), ffn2_w_down
    w_bc, w_bm, w_bg = _to_bf16(w_branch_conv), _to_bf16(w_branch_mla), _to_bf16(w_branch_gqa)
    w_o = _to_bf16(w_out)
    w_p = _pack_w_in(w_in, lay, cw, ql, kvl)

    for l in range(depth):
        with_ctx = l < depth - 1
        rows = mx + mc if with_ctx else mx
        wqb_p = jnp.pad(mla_w_qb[l].reshape(ql, MLA_HEADS, MLA_NOPE + MLA_ROPE),
                        ((0, 0), (0, 0), (0, MLA_QK - MLA_NOPE - MLA_ROPE))).reshape(ql, -1).astype(BF16)
        wkv = mla_w_kvb[l].reshape(kvl, MLA_HEADS, MLA_NOPE + MLA_V)
        wkn_t = wkv[:, :, :MLA_NOPE].reshape(kvl, -1).T.astype(BF16)
        wv = wkv[:, :, MLA_NOPE:].reshape(kvl, -1).astype(BF16)
        sink = jnp.broadcast_to((gqa_sink[l].astype(F32) * LOG2E)[:, None], (GQA_HEADS, LANE))

        if l == 0:
            h = _ffn(x.reshape(mx, d), ctx.reshape(mc, d), mx + mc, mods[l], ffn1_norm[l], w1_gu, w1_dn, l, 0,
                     s, b, final_norm, False)
        else:
            h = _ffn(h, None, mx + mc, mods[l], ffn1_norm[l], w1_gu, w1_dn, l, 0, s, b, final_norm, False)
        px = _proj(h, mods[l], mix_norm[l], w_p, l, s, b)
        q, kt, v, gq, gk = _prep(px, lay, mla_q_norm[l], mla_kv_norm[l], wqb_p, wkn_t, wv, tabs, s, mx)
        y_mla = _mla_attn(q, kt, v, b, s, cl, with_ctx)
        y_gqa = _gqa_attn(gq, gk, px, lay["gv"], sink, b, s, cl, with_ctx)
        h = _mix_out(y_mla, y_gqa, px, conv_w[l], lay["gate"], w_bc, w_bm, w_bg, w_o, l, h, mods[l], rows, s, cl,
                     mx, b)
        h = _ffn(h, None, rows, mods[l], ffn2_norm[l], w2_gu, w2_dn, l, 6, s, b, final_norm, not with_ctx)
    return h.reshape(b, s, d)
```
